```python
import math
import jax
import jax.numpy as jnp
from jax import lax
import numpy as np

D_MODEL = 1024
BATCH = 32
SEQ = 256
DEPTH = 4
DEC_BATCH = 8
DEC_SEQ = 2048
PAST_LEN = 512

GRID_W = 64
ROPE_THETA = 10000.0
NORM_EPS = 1e-6
Q_BLOCK = 128
NEG_INF = -1e30

DIFF_HEADS = 4
DIFF_QK_DIM = 64
DIFF_V_DIM = 2 * DIFF_QK_DIM
MLA_HEADS = 8
MLA_Q_RANK = 384
MLA_KV_RANK = 256
MLA_NOPE_DIM = 64
MLA_ROPE_DIM = 32
MLA_V_DIM = 64
AB_IN_SIZES = (DIFF_HEADS * 2 * DIFF_QK_DIM, DIFF_HEADS * 2 * DIFF_QK_DIM, DIFF_HEADS * DIFF_V_DIM,
               MLA_Q_RANK, MLA_KV_RANK, MLA_ROPE_DIM)
AB_IN = sum(AB_IN_SIZES)
AB_OUT = DIFF_HEADS * DIFF_V_DIM + MLA_HEADS * MLA_V_DIM
SWA_HEADS = 16
SWA_KV_HEADS = 4
SWA_GROUP = SWA_HEADS // SWA_KV_HEADS
SWA_HEAD_DIM = 64
WINDOW = 128
SWA_OUT = SWA_HEADS * SWA_HEAD_DIM
SWA_IN = SWA_OUT + 2 * SWA_KV_HEADS * SWA_HEAD_DIM
N_EXPERTS = 16
N_GROUPS = 4
EXPERTS_PER_GROUP = N_EXPERTS // N_GROUPS
TOP_K = 2
D_EXPERT = 256

N_AB_LAYERS = (DEPTH + 1) // 2
N_SWA_LAYERS = DEPTH // 2

kernel_name = "hybrid_diffusion_prefix_trunk_step"

F32 = jnp.float32


def _split(x, sizes):
    bounds, acc = [], 0
    for s in sizes[:-1]:
        acc += s
        bounds.append(acc)
    return jnp.split(x, bounds, axis=-1)


def rmsnorm(x, g):
    xf = x.astype(F32)
    y = xf * lax.rsqrt(jnp.mean(xf * xf, axis=-1, keepdims=True) + NORM_EPS)
    return (y * g.astype(F32)).astype(x.dtype)


def ada_mod(cond, w, b):
    m = jax.nn.silu(cond) @ w + b
    return jnp.split(m[:, None, :], 6, axis=-1)


def modulate(x, g, shift, scale):
    return rmsnorm(x, g) * (1.0 + scale) + shift


def axial_rope(n_tokens, rot_dim):
    n_rows = n_tokens // GRID_W
    rows = jnp.repeat(jnp.arange(n_rows), GRID_W)
    cols = jnp.tile(jnp.arange(GRID_W), n_rows)
    n_freq = rot_dim // 4
    inv = ROPE_THETA ** (-jnp.arange(n_freq, dtype=F32) / n_freq)
    ang = jnp.concatenate([rows[:, None] * inv, cols[:, None] * inv], axis=-1)
    return jnp.cos(ang), jnp.sin(ang)


def apply_rope(x, cos, sin):
    extra = x.ndim - 3
    cos = cos.reshape(cos.shape[0], *([1] * extra), cos.shape[1]).astype(x.dtype)
    sin = sin.reshape(sin.shape[0], *([1] * extra), sin.shape[1]).astype(x.dtype)
    x1, x2 = x[..., 0::2], x[..., 1::2]
    return jnp.stack([x1 * cos - x2 * sin, x1 * sin + x2 * cos], axis=-1).reshape(x.shape)


def block_attention(q, k, v, sink=None):
    b, sq, hk, r, d = q.shape
    nb = sq // Q_BLOCK
    qb = jnp.moveaxis(q.reshape(b, nb, Q_BLOCK, hk, r, d), 1, 0)
    kf, vf = k.astype(F32), v.astype(F32)
    scale = d ** -0.5

    def one(qi):
        s = jnp.einsum('bqhrd,bkhd->bhrqk', qi.astype(F32), kf) * scale
        if sink is not None:
            sk = jnp.broadcast_to(sink.astype(F32)[None, :, :, None, None], s.shape[:-1] + (1,))
            p = jax.nn.softmax(jnp.concatenate([s, sk], axis=-1), axis=-1)[..., :-1]
        else:
            p = jax.nn.softmax(s, axis=-1)
        return jnp.einsum('bhrqk,bkhe->bqhre', p, vf)

    o = lax.map(one, qb)
    return jnp.moveaxis(o, 0, 1).reshape(b, sq, hk, r, v.shape[-1]).astype(q.dtype)


def window_attention(q, k, v, k_ctx, v_ctx, sink):
    b, n, hk, r, d = q.shape
    nb = n // Q_BLOCK
    band = Q_BLOCK + 2 * WINDOW
    pad = ((0, 0), (WINDOW, WINDOW), (0, 0), (0, 0))
    kp, vp = jnp.pad(k.astype(F32), pad), jnp.pad(v.astype(F32), pad)
    kc, vc = k_ctx.astype(F32), v_ctx.astype(F32)
    qb = jnp.moveaxis(q.reshape(b, nb, Q_BLOCK, hk, r, d), 1, 0)
    scale = d ** -0.5
    sink_f = sink.astype(F32)[None, :, :, None, None]

    def one(args):
        i, qi = args
        start = i * Q_BLOCK
        kb = lax.dynamic_slice_in_dim(kp, start, band, axis=1)
        vb = lax.dynamic_slice_in_dim(vp, start, band, axis=1)
        qpos = start + jnp.arange(Q_BLOCK)
        kpos = start - WINDOW + jnp.arange(band)
        valid = ((jnp.abs(qpos[:, None] - kpos[None, :]) <= WINDOW)
                 & (kpos >= 0)[None, :] & (kpos < n)[None, :])
        qf = qi.astype(F32)
        s_loc = jnp.where(valid, jnp.einsum('bqhrd,bkhd->bhrqk', qf, kb) * scale, NEG_INF)
        s_ctx = jnp.einsum('bqhrd,bkhd->bhrqk', qf, kc) * scale
        s_sink = jnp.broadcast_to(sink_f, s_ctx.shape[:-1] + (1,))
        p = jax.nn.softmax(jnp.concatenate([s_loc, s_ctx, s_sink], axis=-1), axis=-1)
        return (jnp.einsum('bhrqk,bkhe->bqhre', p[..., :band], vb)
                + jnp.einsum('bhrqk,bkhe->bqhre', p[..., band:-1], vc))

    o = lax.map(one, (jnp.arange(nb), qb))
    return jnp.moveaxis(o, 0, 1).reshape(b, n, hk, r, v.shape[-1]).astype(q.dtype)


def diff_lambda_value(lam_p, lambda_init):
    lp = lam_p.astype(F32)
    return jnp.exp(jnp.sum(lp[0] * lp[1])) - jnp.exp(jnp.sum(lp[2] * lp[3])) + lambda_init


def ab_project(h, w_in, q_norm_g, w_qb, kv_norm_g):
    b, s, _ = h.shape
    qa, ka, va, q_lat, ckv, k_rope = _split(h @ w_in, AB_IN_SIZES)
    qa = qa.reshape(b, s, DIFF_HEADS, 2, DIFF_QK_DIM)
    ka = ka.reshape(b, s, DIFF_HEADS, 2, DIFF_QK_DIM)
    va = va.reshape(b, s, DIFF_HEADS, DIFF_V_DIM)
    qb = (rmsnorm(q_lat, q_norm_g) @ w_qb).reshape(b, s, MLA_HEADS, MLA_NOPE_DIM + MLA_ROPE_DIM)
    return qa, ka, va, qb, rmsnorm(ckv, kv_norm_g), k_rope


def ab_mix(qa, ka, va, qb, ckv, k_rope, lam, lambda_init, subln_g, w_kvb, w_out):
    b, s = qa.shape[:2]
    n_keys = ka.shape[1]
    qd = qa.reshape(b, s, 2 * DIFF_HEADS, 1, DIFF_QK_DIM)
    kd = ka.reshape(b, n_keys, 2 * DIFF_HEADS, DIFF_QK_DIM)
    vd = jnp.repeat(va, 2, axis=2)
    od = block_attention(qd, kd, vd).reshape(b, s, DIFF_HEADS, 2, DIFF_V_DIM)
    od = od[:, :, :, 0] - lam.astype(od.dtype) * od[:, :, :, 1]
    od = rmsnorm(od, subln_g) * (1.0 - lambda_init)
    kv = (ckv @ w_kvb).reshape(b, n_keys, MLA_HEADS, MLA_NOPE_DIM + MLA_V_DIM)
    k_nope, vb = kv[..., :MLA_NOPE_DIM], kv[..., MLA_NOPE_DIM:]
    kb = jnp.concatenate(
        [k_nope, jnp.broadcast_to(k_rope[:, :, None, :], (b, n_keys, MLA_HEADS, MLA_ROPE_DIM))], axis=-1)
    ob = block_attention(qb[:, :, :, None, :], kb, vb)
    mixed = jnp.concatenate([od.reshape(b, s, -1), ob.reshape(b, s, -1)], axis=-1)
    return mixed @ w_out


def swa_project(h, w_in):
    b, s, _ = h.shape
    q, k, v = _split(h @ w_in, (SWA_OUT, SWA_KV_HEADS * SWA_HEAD_DIM, SWA_KV_HEADS * SWA_HEAD_DIM))
    return (q.reshape(b, s, SWA_KV_HEADS, SWA_GROUP, SWA_HEAD_DIM),
            k.reshape(b, s, SWA_KV_HEADS, SWA_HEAD_DIM),
            v.reshape(b, s, SWA_KV_HEADS, SWA_HEAD_DIM))


def moe_ffn(h, router_w, router_bias, w_gate, w_up, w_down):
    scores = jax.nn.sigmoid(h.astype(F32) @ router_w.astype(F32))
    sel = (scores + router_bias.astype(F32)).reshape(*scores.shape[:-1], N_GROUPS, EXPERTS_PER_GROUP)
    group_score = lax.top_k(sel, TOP_K)[0].sum(axis=-1)
    gidx = lax.top_k(group_score, 1)[1][..., 0]
    in_group = jnp.einsum('bsge,bsg->bse', sel, jax.nn.one_hot(gidx, N_GROUPS, dtype=F32))
    local = lax.top_k(in_group, TOP_K)[1]
    eidx = gidx[..., None] * EXPERTS_PER_GROUP + local
    w = jnp.take_along_axis(scores, eidx, axis=-1)
    w = w / jnp.sum(w, axis=-1, keepdims=True)
    gates = jnp.einsum('bske,bsk->bse', jax.nn.one_hot(eidx, N_EXPERTS, dtype=F32), w).astype(h.dtype)
    g = jnp.einsum('bsd,edf->bsef', h, w_gate)
    u = jnp.einsum('bsd,edf->bsef', h, w_up)
    a = jax.nn.silu(g) * u * gates[..., None]
    return jnp.einsum('bsef,efd->bsd', a, w_down)


def setup_inputs(seed: int = 0) -> dict:
    key = jax.random.key(seed)
    kit = iter(jax.random.split(key, 40))
    d = D_MODEL

    def nrm(shape, s=1.0):
        return jax.random.normal(next(kit), shape, F32) * s

    def gain(shape):
        return 1.0 + nrm(shape, 0.02)

    return {
        "x_prompt": nrm((BATCH, SEQ, d)),
        "x_sample": nrm((DEC_BATCH, DEC_SEQ, d)),
        "cache_diff_k": nrm((DEC_BATCH, N_AB_LAYERS, PAST_LEN, DIFF_HEADS, 2, DIFF_QK_DIM)),
        "cache_diff_v": nrm((DEC_BATCH, N_AB_LAYERS, PAST_LEN, DIFF_HEADS, DIFF_V_DIM)),
        "cache_mla_ckv": nrm((DEC_BATCH, N_AB_LAYERS, PAST_LEN, MLA_KV_RANK)),
        "cache_mla_krope": nrm((DEC_BATCH, N_AB_LAYERS, PAST_LEN, MLA_ROPE_DIM)),
        "cache_swa_k": nrm((DEC_BATCH, N_SWA_LAYERS, PAST_LEN, SWA_KV_HEADS, SWA_HEAD_DIM)),
        "cache_swa_v": nrm((DEC_BATCH, N_SWA_LAYERS, PAST_LEN, SWA_KV_HEADS, SWA_HEAD_DIM)),
        "c": nrm((DEC_BATCH, d)),
        "c_ctx": nrm((d,)),
        "ada_w": nrm((DEPTH, d, 6 * d), 0.5 * d ** -0.5),
        "ada_b": nrm((DEPTH, 6 * d), 0.02),
        "norm1_g": gain((DEPTH, d)),
        "norm2_g": gain((DEPTH, d)),
        "final_norm_g": gain((d,)),
        "ab_w_in": nrm((N_AB_LAYERS, d, AB_IN), d ** -0.5),
        "diff_lambda": nrm((N_AB_LAYERS, 4, DIFF_QK_DIM), 0.1),
        "diff_subln_g": gain((N_AB_LAYERS, DIFF_V_DIM)),
        "mla_q_norm_g": gain((N_AB_LAYERS, MLA_Q_RANK)),
        "mla_w_qb": nrm((N_AB_LAYERS, MLA_Q_RANK, MLA_HEADS * (MLA_NOPE_DIM + MLA_ROPE_DIM)), MLA_Q_RANK ** -0.5),
        "mla_kv_norm_g": gain((N_AB_LAYERS, MLA_KV_RANK)),
        "mla_w_kvb": nrm((N_AB_LAYERS, MLA_KV_RANK, MLA_HEADS * (MLA_NOPE_DIM + MLA_V_DIM)), MLA_KV_RANK ** -0.5),
        "ab_w_out": nrm((N_AB_LAYERS, AB_OUT, d), AB_OUT ** -0.5),
        "swa_w_in": nrm((N_SWA_LAYERS, d, SWA_IN), d ** -0.5),
        "swa_sink": nrm((N_SWA_LAYERS, SWA_HEADS), 0.5),
        "swa_w_out": nrm((N_SWA_LAYERS, SWA_OUT, d), SWA_OUT ** -0.5),
        "router_w": nrm((d, N_EXPERTS), d ** -0.5),
        "router_bias": nrm((N_EXPERTS,), 0.01),
        "moe_w_gate": nrm((DEPTH, N_EXPERTS, d, D_EXPERT), d ** -0.5),
        "moe_w_up": nrm((DEPTH, N_EXPERTS, d, D_EXPERT), d ** -0.5),
        "moe_w_down": nrm((DEPTH, N_EXPERTS, D_EXPERT, d), D_EXPERT ** -0.5),
    }


def reference(x_prompt, x_sample, cache_diff_k, cache_diff_v, cache_mla_ckv, cache_mla_krope,
              cache_swa_k, cache_swa_v, c, c_ctx, ada_w, ada_b, norm1_g, norm2_g, final_norm_g,
              ab_w_in, diff_lambda, diff_subln_g, mla_q_norm_g, mla_w_qb, mla_kv_norm_g, mla_w_kvb,
              ab_w_out, swa_w_in, swa_sink, swa_w_out, router_w, router_bias,
              moe_w_gate, moe_w_up, moe_w_down):
    n_lat = x_sample.shape[1]
    cos_d, sin_d = axial_rope(n_lat, DIFF_QK_DIM)
    cos_r, sin_r = axial_rope(n_lat, MLA_ROPE_DIM)
    cos_s, sin_s = axial_rope(n_lat, SWA_HEAD_DIM)
    cond_ctx = c_ctx[None, :]
    xp, xs = x_prompt, x_sample
    diff_k_l, diff_v_l, ckv_l, krope_l, swa_k_l, swa_v_l = [], [], [], [], [], []

    for l in range(DEPTH):
        sh1_p, sc1_p, g1_p, sh2_p, sc2_p, g2_p = ada_mod(cond_ctx, ada_w[l], ada_b[l])
        sh1_s, sc1_s, g1_s, sh2_s, sc2_s, g2_s = ada_mod(c, ada_w[l], ada_b[l])
        hp = modulate(xp, norm1_g[l], sh1_p, sc1_p)
        hs = modulate(xs, norm1_g[l], sh1_s, sc1_s)
        j = l // 2
        if l % 2 == 0:
            lambda_init = 0.8 - 0.6 * math.exp(-0.3 * l)
            lam = diff_lambda_value(diff_lambda[j], lambda_init)
            qa, ka, va, qb, ckv, kr = ab_project(hp, ab_w_in[j], mla_q_norm_g[j], mla_w_qb[j], mla_kv_norm_g[j])
            diff_k_l.append(ka)
            diff_v_l.append(va)
            ckv_l.append(ckv)
            krope_l.append(kr)
            mp = ab_mix(qa, ka, va, qb, ckv, kr, lam, lambda_init, diff_subln_g[j], mla_w_kvb[j], ab_w_out[j])
            qa, ka, va, qb, ckv, kr = ab_project(hs, ab_w_in[j], mla_q_norm_g[j], mla_w_qb[j], mla_kv_norm_g[j])
            qa = apply_rope(qa, cos_d, sin_d)
            ka = apply_rope(ka, cos_d, sin_d)
            qb = jnp.concatenate([qb[..., :MLA_NOPE_DIM], apply_rope(qb[..., MLA_NOPE_DIM:], cos_r, sin_r)], axis=-1)
            kr = apply_rope(kr, cos_r, sin_r)
            ms = ab_mix(qa,
                        jnp.concatenate([ka, cache_diff_k[:, j]], axis=1),
                        jnp.concatenate([va, cache_diff_v[:, j]], axis=1),
                        qb,
                        jnp.concatenate([ckv, cache_mla_ckv[:, j]], axis=1),
                        jnp.concatenate([kr, cache_mla_krope[:, j]], axis=1),
                        lam, lambda_init, diff_subln_g[j], mla_w_kvb[j], ab_w_out[j])
        else:
            sink = swa_sink[j].reshape(SWA_KV_HEADS, SWA_GROUP)
            qp, kp, vp = swa_project(hp, swa_w_in[j])
            swa_k_l.append(kp)
            swa_v_l.append(vp)
            op = block_attention(qp, kp, vp, sink)
            mp = op.reshape(op.shape[0], op.shape[1], SWA_OUT) @ swa_w_out[j]
            qs, ks, vs = swa_project(hs, swa_w_in[j])
            qs = apply_rope(qs, cos_s, sin_s)
            ks = apply_rope(ks, cos_s, sin_s)
            o_s = window_attention(qs, ks, vs, cache_swa_k[:, j], cache_swa_v[:, j], sink)
            ms = o_s.reshape(o_s.shape[0], o_s.shape[1], SWA_OUT) @ swa_w_out[j]
        xp = xp + g1_p * mp
        xs = xs + g1_s * ms
        hp = modulate(xp, norm2_g[l], sh2_p, sc2_p)
        hs = modulate(xs, norm2_g[l], sh2_s, sc2_s)
        xp = xp + g2_p * moe_ffn(hp, router_w, router_bias, moe_w_gate[l], moe_w_up[l], moe_w_down[l])
        xs = xs + g2_s * moe_ffn(hs, router_w, router_bias, moe_w_gate[l], moe_w_up[l], moe_w_down[l])

    y_prompt = rmsnorm(xp, final_norm_g)
    y_sample = rmsnorm(xs, final_norm_g)
    new_diff_k = jnp.stack(diff_k_l, axis=1)
    new_diff_v = jnp.stack(diff_v_l, axis=1)
    new_mla_ckv = jnp.stack(ckv_l, axis=1)
    new_mla_krope = jnp.stack(krope_l, axis=1)
    new_swa_k = jnp.stack(swa_k_l, axis=1)
    new_swa_v = jnp.stack(swa_v_l, axis=1)
    return (y_prompt, y_sample, new_diff_k, new_diff_v, new_mla_ckv, new_mla_krope, new_swa_k, new_swa_v)
```

```python
import functools
import math

import jax
import jax.numpy as jnp
import numpy as np
from jax import lax
from jax.experimental import pallas as pl
from jax.experimental.pallas import tpu as pltpu

F32 = jnp.float32
BF16 = jnp.bfloat16

D_MODEL = 1024
GRID_W = 64
ROPE_THETA = 10000.0
NORM_EPS = 1e-6
NEG_INF = -1e30
LANES = 128

DIFF_HEADS = 4
DIFF_QK_DIM = 64
DIFF_V_DIM = 128
MLA_HEADS = 8
MLA_Q_RANK = 384
MLA_KV_RANK = 256
MLA_NOPE_DIM = 64
MLA_ROPE_DIM = 32
MLA_V_DIM = 64
AB_IN = 3 * 512 + MLA_Q_RANK + MLA_KV_RANK + MLA_ROPE_DIM
AB_IN_PAD = 2304
SWA_HEADS = 16
SWA_KV_HEADS = 4
SWA_GROUP = 4
SWA_HEAD_DIM = 64
WINDOW = 128
N_EXPERTS = 16
N_GROUPS = 4
EXPERTS_PER_GROUP = 4
D_EXPERT = 256
COND_ROWS = 16

VMEM_LIMIT = 56 * 1024 * 1024


def _full(shape):
    n = len(shape)
    return pl.BlockSpec(shape, lambda *_: (0,) * n)


def _resident(shape):
    n = len(shape)
    return pl.BlockSpec(shape, lambda *_: (0,) * n, pipeline_mode=pl.Buffered(1))


def _params(n_axes):
    return pltpu.CompilerParams(dimension_semantics=("arbitrary",) * n_axes, vmem_limit_bytes=VMEM_LIMIT)


def _sigmoid(x):
    return 1.0 / (1.0 + jnp.exp(-x))


def _rms(x, g):
    return x * lax.rsqrt(jnp.mean(x * x, axis=-1, keepdims=True) + NORM_EPS) * g


def _modulate(x, g, shift, scale):
    return _rms(x, g) * (1.0 + scale) + shift


def _dot(a, b):
    return jnp.dot(a, b, preferred_element_type=F32)


def _dot_nt(a, b):
    return lax.dot_general(a, b, (((1,), (1,)), ((), ())), preferred_element_type=F32)


def _rope(x, tab_ref):
    c, s_odd, s_even = tab_ref[0], tab_ref[1], tab_ref[2]
    out = []
    for i in range(x.shape[1] // LANES):
        xi = x[:, i * LANES:(i + 1) * LANES]
        out.append(xi * c + pltpu.roll(xi, 1, 1) * s_odd + pltpu.roll(xi, LANES - 1, 1) * s_even)
    return out[0] if len(out) == 1 else jnp.concatenate(out, axis=1)


def _softmax_pv(s_list, v_list, sink=None):
    m = functools.reduce(jnp.maximum, [jnp.max(s, axis=-1, keepdims=True) for s in s_list])
    if sink is not None:
        m = jnp.maximum(m, sink)
    e_list = [jnp.exp(s - m) for s in s_list]
    l = functools.reduce(lambda a, b: a + b, [jnp.sum(e, axis=-1, keepdims=True) for e in e_list])
    if sink is not None:
        l = l + jnp.exp(sink - m)
    o = functools.reduce(lambda a, b: a + b, [_dot(e.astype(BF16), v) for e, v in zip(e_list, v_list)])
    return o / l


def _ada_kernel(cond_ref, w_ref, b_ref, o_ref):
    c = cond_ref[...]
    a = (c * _sigmoid(c)).astype(BF16)
    o_ref[...] = _dot(a, w_ref[...].astype(BF16)) + b_ref[...]


def _ada_all(cond, ada_w, ada_b):
    depth, d, n = ada_w.shape
    tn = 1536
    return pl.pallas_call(
        _ada_kernel,
        grid=(depth, n // tn),
        in_specs=[
            _full((COND_ROWS, d)),
            pl.BlockSpec((None, d, tn), lambda l, j: (l, 0, j)),
            pl.BlockSpec((None, 1, tn), lambda l, j: (l, 0, j)),
        ],
        out_specs=pl.BlockSpec((None, COND_ROWS, tn), lambda l, j: (l, 0, j)),
        out_shape=jax.ShapeDtypeStruct((depth, COND_ROWS, n), F32),
        compiler_params=_params(2),
        name="ada_mod",
    )(cond, ada_w, ada_b.reshape(depth, 1, n))


def _proj_ab_kernel(rope, emit_cache, x_ref, mod_ref, g_ref, w_in_ref, qn_ref, wqb_ref, kvn_ref, wk_ref, wv_ref,
                    e_ref, *rest):
    if rope:
        rope_d, rope_q, rope_k = rest[:3]
        rest = rest[3:]
    qa_o, ka_o, va_o, qb_o, km_o, vm_o = rest[:6]
    mod = mod_ref[0]
    h = _modulate(x_ref[...], g_ref[...], mod[0:1], mod[1:2]).astype(BF16)
    big = _dot(h, w_in_ref[...])
    qa, ka, va = big[:, 0:512], big[:, 512:1024], big[:, 1024:1536]
    q_lat, ckv, kr = big[:, 1536:1920], big[:, 1920:2176], big[:, 2176:2304]
    qb = _dot(_rms(q_lat, qn_ref[...]).astype(BF16), wqb_ref[...])
    ckv_n = _rms(ckv, kvn_ref[...])
    if emit_cache:
        ka32_o, va32_o, ckv32_o, kr32_o = rest[6:10]
        ka32_o[...] = ka
        va32_o[...] = va
        ckv32_o[...] = ckv_n
        kr32_o[...] = kr[:, :MLA_ROPE_DIM]
    if rope:
        qa, ka = _rope(qa, rope_d), _rope(ka, rope_d)
        qb, kr = _rope(qb, rope_q), _rope(kr, rope_k)
    qa = qa * (DIFF_QK_DIM ** -0.5)
    qb = qb * ((MLA_NOPE_DIM + MLA_ROPE_DIM) ** -0.5)
    ckv16 = ckv_n.astype(BF16)
    km = _dot(ckv16, wk_ref[...]) + _dot(kr.astype(BF16), e_ref[...])
    vm = _dot(ckv16, wv_ref[...])
    lo = lax.broadcasted_iota(jnp.int32, (qa.shape[0], LANES), 1) < DIFF_QK_DIM
    for hd in range(DIFF_HEADS):
        sl = slice(hd * LANES, (hd + 1) * LANES)
        qa_o[2 * hd] = jnp.where(lo, qa[:, sl], 0.0).astype(BF16)
        qa_o[2 * hd + 1] = jnp.where(lo, 0.0, qa[:, sl]).astype(BF16)
        ka_o[hd] = ka[:, sl].astype(BF16)
        va_o[hd] = va[:, sl].astype(BF16)
    for hd in range(MLA_HEADS):
        sl = slice(hd * LANES, (hd + 1) * LANES)
        qb_o[hd] = qb[:, sl].astype(BF16)
        km_o[hd] = km[:, sl].astype(BF16)
        vm_o[hd] = vm[:, sl].astype(BF16)


def _proj_ab(x, mods, row_fn, g, wts, rope_tabs, emit_cache, tm):
    b, s, d = x.shape
    rope = rope_tabs is not None
    tok = lambda bi, t: (bi, t, 0)
    head = lambda bi, t: (bi, 0, t, 0)
    in_specs = [
        pl.BlockSpec((None, tm, d), tok),
        pl.BlockSpec((1, 6, d), lambda bi, t: (row_fn(bi), 0, 0)),
        _full((1, d)),
        _full((d, AB_IN_PAD)), _full((1, MLA_Q_RANK)), _full((MLA_Q_RANK, 1024)), _full((1, MLA_KV_RANK)),
        _full((MLA_KV_RANK, 1024)), _full((MLA_KV_RANK, 1024)), _full((LANES, 1024)),
    ]
    args = [x, mods, g, wts["w_in"], wts["qn_g"], wts["w_qb"], wts["kvn_g"], wts["wk"], wts["wv"], wts["e"]]
    if rope:
        in_specs += [pl.BlockSpec((3, tm, LANES), lambda bi, t: (0, t, 0))] * 3
        args += list(rope_tabs)

    def hm(nh):
        return jax.ShapeDtypeStruct((b, nh, s, LANES), BF16), pl.BlockSpec((None, nh, tm, LANES), head)

    outs = [hm(8), hm(4), hm(4), hm(8), hm(8), hm(8)]
    if emit_cache:
        for w in (512, 512, MLA_KV_RANK, MLA_ROPE_DIM):
            outs.append((jax.ShapeDtypeStruct((b, s, w), F32), pl.BlockSpec((None, tm, w), tok)))
    return pl.pallas_call(
        functools.partial(_proj_ab_kernel, rope, emit_cache),
        grid=(b, s // tm),
        in_specs=in_specs,
        out_specs=[o[1] for o in outs],
        out_shape=[o[0] for o in outs],
        compiler_params=_params(2),
        name="proj_ab",
    )(*args)


def _cache_ab_kernel(dk_ref, dv_ref, ckv_ref, kr_ref, wk_ref, wv_ref, e_ref, ck_o, cv_o, km_o, vm_o):
    ckv16 = ckv_ref[...].astype(BF16)
    km = _dot(ckv16, wk_ref[...]) + _dot(kr_ref[...].astype(BF16), e_ref[...])
    vm = _dot(ckv16, wv_ref[...])
    for hd in range(DIFF_HEADS):
        sl = slice(hd * LANES, (hd + 1) * LANES)
        ck_o[hd] = dk_ref[:, sl].astype(BF16)
        cv_o[hd] = dv_ref[:, sl].astype(BF16)
    for hd in range(MLA_HEADS):
        sl = slice(hd * LANES, (hd + 1) * LANES)
        km_o[hd] = km[:, sl].astype(BF16)
        vm_o[hd] = vm[:, sl].astype(BF16)


def _cache_ab(cdk, cdv, cckv, ckr, j, wts):
    b, _, p, _ = cdk.shape
    lay = lambda bi: (bi, j, 0, 0)

    def hm(nh):
        return (jax.ShapeDtypeStruct((b, nh, p, LANES), BF16),
                pl.BlockSpec((None, nh, p, LANES), lambda bi: (bi, 0, 0, 0)))

    outs = [hm(4), hm(4), hm(8), hm(8)]
    return pl.pallas_call(
        _cache_ab_kernel,
        grid=(b,),
        in_specs=[
            pl.BlockSpec((None, None, p, 512), lay), pl.BlockSpec((None, None, p, 512), lay),
            pl.BlockSpec((None, None, p, MLA_KV_RANK), lay), pl.BlockSpec((None, None, p, LANES), lay),
            _full((MLA_KV_RANK, 1024)), _full((MLA_KV_RANK, 1024)), _full((LANES, 1024)),
        ],
        out_specs=[o[1] for o in outs],
        out_shape=[o[0] for o in outs],
        compiler_params=_params(1),
        name="cache_ab",
    )(cdk, cdv, cckv, ckr, wts["wk"], wts["wv"], wts["e"])


def _attn_ab_kernel(n_seg, lambda_init, x_ref, mod_ref, qa_ref, qb_ref, *rest):
    segs = [rest[4 * i:4 * i + 4] for i in range(n_seg)]
    wd_ref, wm_ref, sg_ref, lam_ref, o_ref = rest[4 * n_seg:]
    lp = lam_ref[...]
    lam = (jnp.exp(jnp.sum(lp[0:1] * lp[1:2], axis=-1, keepdims=True))
           - jnp.exp(jnp.sum(lp[2:3] * lp[3:4], axis=-1, keepdims=True)) + lambda_init)
    tq, d = x_ref.shape

    def attend(q, k_refs, v_refs, hd):
        s_list = [_dot_nt(q, k[hd]) for k in k_refs]
        return _softmax_pv(s_list, [v[hd] for v in v_refs])

    def diff_body(hd, acc):
        ks, vs = [sg[0] for sg in segs], [sg[1] for sg in segs]
        o1 = attend(qa_ref[2 * hd], ks, vs, hd)
        o2 = attend(qa_ref[2 * hd + 1], ks, vs, hd)
        od = _rms(o1 - lam * o2, sg_ref[...]) * (1.0 - lambda_init)
        return acc + _dot(od.astype(BF16), wd_ref[hd])

    def mla_body(hd, acc):
        o = attend(qb_ref[hd], [sg[2] for sg in segs], [sg[3] for sg in segs], hd)
        return acc + _dot(o.astype(BF16), wm_ref[hd])

    acc = lax.fori_loop(0, DIFF_HEADS, diff_body, jnp.zeros((tq, d), F32))
    acc = lax.fori_loop(0, MLA_HEADS, mla_body, acc)
    o_ref[...] = x_ref[...] + mod_ref[0][2:3] * acc


def _attn_ab(x, mods, row_fn, q_parts, seg_list, wts, lambda_init, tq):
    b, s, d = x.shape
    qa, qb = q_parts
    in_specs = [
        pl.BlockSpec((None, tq, d), lambda bi, t: (bi, t, 0)),
        pl.BlockSpec((1, 6, d), lambda bi, t: (row_fn(bi), 0, 0)),
        pl.BlockSpec((None, 8, tq, LANES), lambda bi, t: (bi, 0, t, 0)),
        pl.BlockSpec((None, 8, tq, LANES), lambda bi, t: (bi, 0, t, 0)),
    ]
    args = [x, mods, qa, qb]
    for seg in seg_list:
        for arr in seg:
            nh, nk = arr.shape[1], arr.shape[2]
            in_specs.append(pl.BlockSpec((None, nh, nk, LANES), lambda bi, t: (bi, 0, 0, 0),
                                         pipeline_mode=pl.Buffered(1)))
            args.append(arr)
    in_specs += [_full((DIFF_HEADS, LANES, d)), _full((MLA_HEADS, LANES, d)), _full((1, LANES)),
                 _full((4, DIFF_QK_DIM))]
    args += [wts["w_out_d"], wts["w_out_m"], wts["subln_g"], wts["lam"]]
    return pl.pallas_call(
        functools.partial(_attn_ab_kernel, len(seg_list), lambda_init),
        grid=(b, s // tq),
        in_specs=in_specs,
        out_specs=pl.BlockSpec((None, tq, d), lambda bi, t: (bi, t, 0)),
        out_shape=jax.ShapeDtypeStruct((b, s, d), F32),
        compiler_params=_params(2),
        name="attn_ab",
    )(*args)


def _proj_swa_kernel(rope, emit_cache, x_ref, mod_ref, g_ref, w_in_ref, *rest):
    if rope:
        rope_d = rest[0]
        rest = rest[1:]
    q_o, k_o, v_o = rest[:3]
    mod = mod_ref[0]
    h = _modulate(x_ref[...], g_ref[...], mod[0:1], mod[1:2]).astype(BF16)
    big = _dot(h, w_in_ref[...])
    q, k, v = big[:, :1024], big[:, 1024:1280], big[:, 1280:1536]
    if emit_cache:
        k32_o, v32_o = rest[3:5]
        k32_o[...] = k
        v32_o[...] = v
    if rope:
        q, k = _rope(q, rope_d), _rope(k, rope_d)
    q = q * (SWA_HEAD_DIM ** -0.5)
    lo = lax.broadcasted_iota(jnp.int32, (q.shape[0], LANES), 1) < SWA_HEAD_DIM
    for pair in range(2):
        for grp in range(SWA_GROUP):
            ca, cb = (2 * pair) * 2 + grp // 2, (2 * pair + 1) * 2 + grp // 2
            a = q[:, ca * LANES:(ca + 1) * LANES]
            bb = q[:, cb * LANES:(cb + 1) * LANES]
            if grp % 2 == 0:
                bb = pltpu.roll(bb, SWA_HEAD_DIM, 1)
            else:
                a = pltpu.roll(a, SWA_HEAD_DIM, 1)
            q_o[pair * SWA_GROUP + grp] = jnp.where(lo, a, bb).astype(BF16)
    for kvh in range(SWA_KV_HEADS):
        sl = slice((kvh // 2) * LANES, (kvh // 2 + 1) * LANES)
        keep = lo if kvh % 2 == 0 else jnp.logical_not(lo)
        k_o[kvh] = jnp.where(keep, k[:, sl], 0.0).astype(BF16)
        v_o[kvh] = jnp.where(keep, v[:, sl], 0.0).astype(BF16)


def _proj_swa(x, mods, row_fn, g, w_in, rope_tab, emit_cache, tm):
    b, s, d = x.shape
    rope = rope_tab is not None
    tok = lambda bi, t: (bi, t, 0)
    head = lambda bi, t: (bi, 0, t, 0)
    in_specs = [
        pl.BlockSpec((None, tm, d), tok),
        pl.BlockSpec((1, 6, d), lambda bi, t: (row_fn(bi), 0, 0)),
        _full((1, d)), _full((d, 1536)),
    ]
    args = [x, mods, g, w_in]
    if rope:
        in_specs.append(pl.BlockSpec((3, tm, LANES), lambda bi, t: (0, t, 0)))
        args.append(rope_tab)

    def hm(nh):
        return jax.ShapeDtypeStruct((b, nh, s, LANES), BF16), pl.BlockSpec((None, nh, tm, LANES), head)

    outs = [hm(8), hm(4), hm(4)]
    if emit_cache:
        for _ in range(2):
            outs.append((jax.ShapeDtypeStruct((b, s, 256), F32), pl.BlockSpec((None, tm, 256), tok)))
    return pl.pallas_call(
        functools.partial(_proj_swa_kernel, rope, emit_cache),
        grid=(b, s // tm),
        in_specs=in_specs,
        out_specs=[o[1] for o in outs],
        out_shape=[o[0] for o in outs],
        compiler_params=_params(2),
        name="proj_swa",
    )(*args)


def _attn_swa_kernel(windowed, band_w, x_ref, mod_ref, q_ref, k_ref, v_ref, *rest):
    if windowed:
        ck_ref, cv_ref = rest[:2]
        rest = rest[2:]
    sink_ref, w_ref, o_ref = rest
    tq, d = x_ref.shape
    n_keys = k_ref.shape[1]
    rows = SWA_GROUP * tq
    lo_k = lax.broadcasted_iota(jnp.int32, (1, LANES), 1) < SWA_HEAD_DIM
    lo_q = lax.broadcasted_iota(jnp.int32, (tq, LANES), 1) < SWA_HEAD_DIM
    if windowed:
        start = pl.program_id(1) * tq
        bstart = pl.multiple_of(jnp.clip(start - WINDOW, 0, n_keys - band_w), LANES)
        qpos = start + (lax.broadcasted_iota(jnp.int32, (rows, band_w), 0) & (tq - 1))
        kpos = bstart + lax.broadcasted_iota(jnp.int32, (rows, band_w), 1)
        valid = jnp.abs(qpos - kpos) <= WINDOW
    acc = jnp.zeros((tq, d), F32)
    for pair in range(2):
        q4 = jnp.concatenate([q_ref[pair * SWA_GROUP + g] for g in range(SWA_GROUP)], axis=0)
        o4 = None
        for half in range(2):
            kvh = 2 * pair + half
            sink = jnp.concatenate(
                [jnp.full((tq, 1), sink_ref[kvh * SWA_GROUP + g], F32) for g in range(SWA_GROUP)], axis=0)
            if windowed:
                kb = k_ref[kvh, pl.ds(bstart, band_w), :]
                vb = v_ref[kvh, pl.ds(bstart, band_w), :]
                keep = lo_k if half == 0 else jnp.logical_not(lo_k)
                sl = slice(pair * LANES, (pair + 1) * LANES)
                kc = jnp.where(keep, ck_ref[:, sl], 0.0).astype(BF16)
                vc = jnp.where(keep, cv_ref[:, sl], 0.0).astype(BF16)
                s_loc = jnp.where(valid, _dot_nt(q4, kb), NEG_INF)
                o = _softmax_pv([s_loc, _dot_nt(q4, kc)], [vb, vc], sink)
            else:
                o = _softmax_pv([_dot_nt(q4, k_ref[kvh])], [v_ref[kvh]], sink)
            o4 = o if o4 is None else o4 + o
        for half in range(2):
            kvh = 2 * pair + half
            for i in range(2):
                xa = o4[(2 * i) * tq:(2 * i + 1) * tq]
                xb = o4[(2 * i + 1) * tq:(2 * i + 2) * tq]
                if half == 0:
                    xb = pltpu.roll(xb, SWA_HEAD_DIM, 1)
                else:
                    xa = pltpu.roll(xa, SWA_HEAD_DIM, 1)
                slab = jnp.where(lo_q, xa, xb).astype(BF16)
                acc = acc + _dot(slab, w_ref[kvh * 2 + i])
    o_ref[...] = x_ref[...] + mod_ref[0][2:3] * acc


def _attn_swa(x, mods, row_fn, q, k, v, ctx, sink, w_out, tq):
    b, s, d = x.shape
    windowed = ctx is not None
    band_w = min(tq + 2 * WINDOW, s)
    kv_spec = pl.BlockSpec((None, SWA_KV_HEADS, s, LANES), lambda bi, t: (bi, 0, 0, 0))
    in_specs = [
        pl.BlockSpec((None, tq, d), lambda bi, t: (bi, t, 0)),
        pl.BlockSpec((1, 6, d), lambda bi, t: (row_fn(bi), 0, 0)),
        pl.BlockSpec((None, 8, tq, LANES), lambda bi, t: (bi, 0, t, 0)),
        kv_spec, kv_spec,
    ]
    args = [x, mods, q, k, v]
    if windowed:
        ck, cv, j = ctx
        p = ck.shape[2]
        spec = pl.BlockSpec((None, None, p, 256), lambda bi, t: (bi, j, 0, 0))
        in_specs += [spec, spec]
        args += [ck, cv]
    in_specs += [pl.BlockSpec(memory_space=pltpu.SMEM), _full((8, LANES, d))]
    args += [sink, w_out]
    return pl.pallas_call(
        functools.partial(_attn_swa_kernel, windowed, band_w),
        grid=(b, s // tq),
        in_specs=in_specs,
        out_specs=pl.BlockSpec((None, tq, d), lambda bi, t: (bi, t, 0)),
        out_shape=jax.ShapeDtypeStruct((b, s, d), F32),
        compiler_params=_params(2),
        name="attn_swa",
    )(*args)


def _route(scores_t, bias):
    sel_t = scores_t + bias
    sel = [sel_t[e:e + 1] for e in range(N_EXPERTS)]
    raw = [scores_t[e:e + 1] for e in range(N_EXPERTS)]
    gscore = []
    for g in range(N_GROUPS):
        r = sel[g * 4:g * 4 + 4]
        pairs = [r[i] + r[j] for i in range(4) for j in range(i + 1, 4)]
        gscore.append(functools.reduce(jnp.maximum, pairs))
    best, gidx = gscore[0], jnp.zeros_like(gscore[0], dtype=jnp.int32)
    for g in range(1, N_GROUPS):
        take = gscore[g] > best
        best = jnp.where(take, gscore[g], best)
        gidx = jnp.where(take, g, gidx)
    vals = []
    for k in range(EXPERTS_PER_GROUP):
        v = sel[k]
        for g in range(1, N_GROUPS):
            v = jnp.where(gidx == g, sel[g * 4 + k], v)
        vals.append(v)

    def argmax4(vs):
        m, idx = vs[0], jnp.zeros_like(gidx)
        for k in range(1, 4):
            take = vs[k] > m
            m = jnp.where(take, vs[k], m)
            idx = jnp.where(take, k, idx)
        return idx

    i1 = argmax4(vals)
    i2 = argmax4([jnp.where(i1 == k, -jnp.inf, vals[k]) for k in range(4)])
    e1, e2 = gidx * 4 + i1, gidx * 4 + i2
    w1 = functools.reduce(lambda a, b: a + b, [jnp.where(e1 == e, raw[e], 0.0) for e in range(N_EXPERTS)])
    w2 = functools.reduce(lambda a, b: a + b, [jnp.where(e2 == e, raw[e], 0.0) for e in range(N_EXPERTS)])
    den = w1 + w2
    g1, g2 = w1 / den, w2 / den
    rows = [jnp.where(e1 == e, g1, 0.0) + jnp.where(e2 == e, g2, 0.0) for e in range(N_EXPERTS)]
    return jnp.concatenate(rows, axis=0)


def _moe_kernel(final_norm, x_ref, mod_ref, g_ref, rw_ref, rb_ref, wgu_ref, wd_ref, *rest):
    if final_norm:
        fg_ref, o_ref = rest
    else:
        (o_ref,) = rest
    tm, d = x_ref.shape
    mod = mod_ref[0]
    x = x_ref[...]
    h = _modulate(x, g_ref[...], mod[3:4], mod[4:5])
    logits_t = lax.dot_general(rw_ref[...], h, (((1,), (1,)), ((), ())), precision=lax.Precision.HIGHEST,
                               preferred_element_type=F32)
    gates_t = _route(_sigmoid(logits_t), rb_ref[...])
    gates = jnp.concatenate([gates_t, jnp.zeros((LANES - N_EXPERTS, tm), F32)], axis=0).T
    lane = lax.broadcasted_iota(jnp.int32, (tm, LANES), 1)
    h16 = h.astype(BF16)

    def body(e, acc):
        gate = jnp.sum(jnp.where(lane == e, gates, 0.0), axis=-1, keepdims=True)
        gu = _dot(h16, wgu_ref[e])
        g, u = gu[:, :D_EXPERT], gu[:, D_EXPERT:]
        a = g * _sigmoid(g) * u * gate
        return acc + _dot(a.astype(BF16), wd_ref[e])

    acc = lax.fori_loop(0, N_EXPERTS, body, jnp.zeros((tm, d), F32))
    y = x + mod[5:6] * acc
    if final_norm:
        y = _rms(y, fg_ref[...])
    o_ref[...] = y


def _moe(x, mods, row_fn, g, rw_t, rb, wgu, wd, final_g, tm):
    b, s, d = x.shape
    final_norm = final_g is not None
    in_specs = [
        pl.BlockSpec((None, tm, d), lambda bi, t: (bi, t, 0)),
        pl.BlockSpec((1, 6, d), lambda bi, t: (row_fn(bi), 0, 0)),
        _full((1, d)), _full((N_EXPERTS, d)), _full((N_EXPERTS, 1)),
        _resident((N_EXPERTS, d, 2 * D_EXPERT)), _resident((N_EXPERTS, D_EXPERT, d)),
    ]
    args = [x, mods, g, rw_t, rb, wgu, wd]
    if final_norm:
        in_specs.append(_full((1, d)))
        args.append(final_g)
    return pl.pallas_call(
        functools.partial(_moe_kernel, final_norm),
        grid=(b, s // tm),
        in_specs=in_specs,
        out_specs=pl.BlockSpec((None, tm, d), lambda bi, t: (bi, t, 0)),
        out_shape=jax.ShapeDtypeStruct((b, s, d), F32),
        compiler_params=_params(2),
        name="moe",
    )(*args)


def _rope_tables(n_tokens, rot_dim, offset):
    n_rows = n_tokens // GRID_W
    rows = np.repeat(np.arange(n_rows), GRID_W)
    cols = np.tile(np.arange(GRID_W), n_rows)
    n_freq = rot_dim // 4
    inv = jnp.asarray(ROPE_THETA, F32) ** (-jnp.arange(n_freq, dtype=F32) / n_freq)
    ang = jnp.concatenate([jnp.asarray(rows, F32)[:, None] * inv, jnp.asarray(cols, F32)[:, None] * inv], axis=-1)
    cos, sin = jnp.repeat(jnp.cos(ang), 2, axis=-1), jnp.repeat(jnp.sin(ang), 2, axis=-1)
    odd = jnp.asarray(np.arange(rot_dim) % 2 == 1)
    parts = [cos, jnp.where(odd, sin, 0.0), jnp.where(odd, 0.0, -sin)]
    period = 64 if rot_dim == 64 else LANES
    fill = [1.0, 0.0, 0.0]
    out = []
    for p, f in zip(parts, fill):
        slot = jnp.full((n_tokens, period), f, F32).at[:, offset:offset + rot_dim].set(p)
        out.append(jnp.tile(slot, (1, LANES // period)))
    return jnp.stack(out)


def _pad_heads(w, n_heads, lo, hi):
    k = w.shape[0]
    w = w.reshape(k, n_heads, -1)[:, :, lo:hi]
    return jnp.pad(w, ((0, 0), (0, 0), (0, LANES - (hi - lo)))).reshape(k, n_heads * LANES)


def _ab_weights(j, ab_w_in, diff_lambda, diff_subln_g, mla_q_norm_g, mla_w_qb, mla_kv_norm_g, mla_w_kvb, ab_w_out):
    place = np.zeros((LANES, MLA_HEADS * LANES), np.float32)
    for hd in range(MLA_HEADS):
        for r in range(MLA_ROPE_DIM):
            place[r, hd * LANES + MLA_NOPE_DIM + r] = 1.0
    w_out_m = ab_w_out[j][512:].reshape(MLA_HEADS, MLA_V_DIM, D_MODEL)
    return {
        "w_in": jnp.pad(ab_w_in[j], ((0, 0), (0, AB_IN_PAD - AB_IN))).astype(BF16),
        "qn_g": mla_q_norm_g[j][None, :],
        "w_qb": _pad_heads(mla_w_qb[j], MLA_HEADS, 0, MLA_NOPE_DIM + MLA_ROPE_DIM).astype(BF16),
        "kvn_g": mla_kv_norm_g[j][None, :],
        "wk": _pad_heads(mla_w_kvb[j], MLA_HEADS, 0, MLA_NOPE_DIM).astype(BF16),
        "wv": _pad_heads(mla_w_kvb[j], MLA_HEADS, MLA_NOPE_DIM, MLA_NOPE_DIM + MLA_V_DIM).astype(BF16),
        "e": jnp.asarray(place, BF16),
        "w_out_d": ab_w_out[j][:512].reshape(DIFF_HEADS, LANES, D_MODEL).astype(BF16),
        "w_out_m": jnp.pad(w_out_m, ((0, 0), (0, LANES - MLA_V_DIM), (0, 0))).astype(BF16),
        "subln_g": diff_subln_g[j][None, :],
        "lam": diff_lambda[j],
    }


def kernel(x_prompt, x_sample, cache_diff_k, cache_diff_v, cache_mla_ckv, cache_mla_krope, cache_swa_k, cache_swa_v,
           c, c_ctx, ada_w, ada_b, norm1_g, norm2_g, final_norm_g, ab_w_in, diff_lambda, diff_subln_g, mla_q_norm_g,
           mla_w_qb, mla_kv_norm_g, mla_w_kvb, ab_w_out, swa_w_in, swa_sink, swa_w_out, router_w, router_bias,
           moe_w_gate, moe_w_up, moe_w_down):
    bp, sp, d = x_prompt.shape
    bs, ss, _ = x_sample.shape
    depth = ada_w.shape[0]
    past = cache_diff_k.shape[2]
    n_ab = cache_diff_k.shape[1]
    n_swa = cache_swa_k.shape[1]
    assert bs + 1 <= COND_ROWS and d == D_MODEL

    cond = jnp.zeros((COND_ROWS, d), F32).at[:bs].set(c).at[bs].set(c_ctx)
    mods = _ada_all(cond, ada_w, ada_b).reshape(depth, COND_ROWS, 6, d)
    row_s = lambda bi: bi
    row_p = lambda bi: bi * 0 + bs

    rope_d = _rope_tables(ss, DIFF_QK_DIM, 0)
    rope_q = _rope_tables(ss, MLA_ROPE_DIM, MLA_NOPE_DIM)
    rope_k = _rope_tables(ss, MLA_ROPE_DIM, 0)

    cdk = cache_diff_k.reshape(bs, n_ab, past, 512)
    cdv = cache_diff_v.reshape(bs, n_ab, past, 512)
    ckr = jnp.pad(cache_mla_krope, ((0, 0), (0, 0), (0, 0), (0, LANES - MLA_ROPE_DIM)))
    csk = cache_swa_k.reshape(bs, n_swa, past, 256)
    csv = cache_swa_v.reshape(bs, n_swa, past, 256)
    rw_t = router_w.T
    rb = router_bias[:, None]

    tp = min(256, sp)
    ts = min(256, ss)
    tm_s = min(512, ss)

    xp, xs = x_prompt, x_sample
    new = {k: [] for k in ("dk", "dv", "ckv", "kr", "sk", "sv")}
    for l in range(depth):
        j = l // 2
        ml = mods[l]
        g1, g2 = norm1_g[l][None, :], norm2_g[l][None, :]
        if l % 2 == 0:
            lambda_init = 0.8 - 0.6 * math.exp(-0.3 * l)
            wts = _ab_weights(j, ab_w_in, diff_lambda, diff_subln_g, mla_q_norm_g, mla_w_qb, mla_kv_norm_g,
                              mla_w_kvb, ab_w_out)
            qa, ka, va, qb, km, vm, ka32, va32, ckv32, kr32 = _proj_ab(xp, ml, row_p, g1, wts, None, True, tp)
            new["dk"].append(ka32)
            new["dv"].append(va32)
            new["ckv"].append(ckv32)
            new["kr"].append(kr32)
            xp = _attn_ab(xp, ml, row_p, (qa, qb), [(ka, va, km, vm)], wts, lambda_init, tp)
            qa, ka, va, qb, km, vm = _proj_ab(xs, ml, row_s, g1, wts, (rope_d, rope_q, rope_k), False, ts)
            cache_seg = _cache_ab(cdk, cdv, cache_mla_ckv, ckr, j, wts)
            xs = _attn_ab(xs, ml, row_s, (qa, qb), [(ka, va, km, vm), tuple(cache_seg)], wts, lambda_init, ts)
        else:
            w_in = swa_w_in[j].astype(BF16)
            w_out = swa_w_out[j].reshape(8, LANES, d).astype(BF16)
            q, k, v, k32, v32 = _proj_swa(xp, ml, row_p, g1, w_in, None, True, tp)
            new["sk"].append(k32)
            new["sv"].append(v32)
            xp = _attn_swa(xp, ml, row_p, q, k, v, None, swa_sink[j], w_out, tp)
            q, k, v = _proj_swa(xs, ml, row_s, g1, w_in, rope_d, False, ts)
            xs = _attn_swa(xs, ml, row_s, q, k, v, (csk, csv, j), swa_sink[j], w_out, ts)
        wgu = jnp.concatenate([moe_w_gate[l], moe_w_up[l]], axis=-1).astype(BF16)
        wd = moe_w_down[l].astype(BF16)
        fg = final_norm_g[None, :] if l == depth - 1 else None
        xp = _moe(xp, ml, row_p, g2, rw_t, rb, wgu, wd, fg, tp)
        xs = _moe(xs, ml, row_s, g2, rw_t, rb, wgu, wd, fg, tm_s)

    new_diff_k = jnp.stack(new["dk"], axis=1).reshape(bp, n_ab, sp, DIFF_HEADS, 2, DIFF_QK_DIM)
    new_diff_v = jnp.stack(new["dv"], axis=1).reshape(bp, n_ab, sp, DIFF_HEADS, DIFF_V_DIM)
    new_mla_ckv = jnp.stack(new["ckv"], axis=1)
    new_mla_krope = jnp.stack(new["kr"], axis=1)
    new_swa_k = jnp.stack(new["sk"], axis=1).reshape(bp, n_swa, sp, SWA_KV_HEADS, SWA_HEAD_DIM)
    new_swa_v = jnp.stack(new["sv"], axis=1).reshape(bp, n_swa, sp, SWA_KV_HEADS, SWA_HEAD_DIM)
    return (xp, xs, new_diff_k, new_diff_v, new_mla_ckv, new_mla_krope, new_swa_k, new_swa_v)
```

```python
import functools
import math

import jax
import jax.numpy as jnp
import numpy as np
from jax import lax
from jax.experimental import pallas as pl
from jax.experimental.pallas import tpu as pltpu

F32 = jnp.float32
BF16 = jnp.bfloat16

D_MODEL = 1024
GRID_W = 64
ROPE_THETA = 10000.0
NORM_EPS = 1e-6
NEG_INF = -1e30
LANES = 128

DIFF_HEADS = 4
DIFF_QK_DIM = 64
DIFF_V_DIM = 128
MLA_HEADS = 8
MLA_Q_RANK = 384
MLA_KV_RANK = 256
MLA_NOPE_DIM = 64
MLA_ROPE_DIM = 32
MLA_V_DIM = 64
AB_IN = 3 * 512 + MLA_Q_RANK + MLA_KV_RANK + MLA_ROPE_DIM
AB_IN_PAD = 2304
SWA_HEADS = 16
SWA_KV_HEADS = 4
SWA_GROUP = 4
SWA_HEAD_DIM = 64
WINDOW = 128
N_EXPERTS = 16
N_GROUPS = 4
EXPERTS_PER_GROUP = 4
D_EXPERT = 256
COND_ROWS = 16

VMEM_LIMIT = 56 * 1024 * 1024


def _full(shape):
    n = len(shape)
    return pl.BlockSpec(shape, lambda *_: (0,) * n)


def _resident(shape):
    n = len(shape)
    return pl.BlockSpec(shape, lambda *_: (0,) * n, pipeline_mode=pl.Buffered(1))


def _params(n_axes):
    return pltpu.CompilerParams(dimension_semantics=("arbitrary",) * n_axes, vmem_limit_bytes=VMEM_LIMIT)


def _sigmoid(x):
    return 1.0 / (1.0 + jnp.exp(-x))


def _rms(x, g):
    return x * lax.rsqrt(jnp.mean(x * x, axis=-1, keepdims=True) + NORM_EPS) * g


def _modulate(x, g, shift, scale):
    return _rms(x, g) * (1.0 + scale) + shift


def _dot(a, b):
    return jnp.dot(a, b, preferred_element_type=F32)


def _dot_nt(a, b):
    return lax.dot_general(a, b, (((1,), (1,)), ((), ())), preferred_element_type=F32)


def _rope(x, tab_ref):
    c, s_odd, s_even = tab_ref[0], tab_ref[1], tab_ref[2]
    out = []
    for i in range(x.shape[1] // LANES):
        xi = x[:, i * LANES:(i + 1) * LANES]
        out.append(xi * c + pltpu.roll(xi, 1, 1) * s_odd + pltpu.roll(xi, LANES - 1, 1) * s_even)
    return out[0] if len(out) == 1 else jnp.concatenate(out, axis=1)


def _tree(op, xs):
    xs = list(xs)
    while len(xs) > 1:
        xs = [op(xs[i], xs[i + 1]) if i + 1 < len(xs) else xs[i] for i in range(0, len(xs), 2)]
    return xs[0]


def _lane_fold(op, x):
    return _tree(op, [x[:, i * LANES:(i + 1) * LANES] for i in range(x.shape[1] // LANES)])


def _softmax_parts(s_list, sink=None):
    m = jnp.max(_tree(jnp.maximum, [_lane_fold(jnp.maximum, s) for s in s_list]), axis=-1, keepdims=True)
    if sink is not None:
        m = jnp.maximum(m, sink)
    e_list = [jnp.exp(s - m) for s in s_list]
    l = jnp.sum(_tree(jnp.add, [_lane_fold(jnp.add, e) for e in e_list]), axis=-1, keepdims=True)
    if sink is not None:
        l = l + jnp.exp(sink - m)
    return e_list, l


def _softmax_pv(s_list, v_list, sink=None):
    e_list, l = _softmax_parts(s_list, sink)
    o = _tree(jnp.add, [_dot(e.astype(BF16), v) for e, v in zip(e_list, v_list)])
    return o / l


def _ada_kernel(cond_ref, w_ref, b_ref, o_ref):
    c = cond_ref[...]
    a = (c * _sigmoid(c)).astype(BF16)
    o_ref[...] = _dot(a, w_ref[...].astype(BF16)) + b_ref[...]


def _ada_all(cond, ada_w, ada_b):
    depth, d, n = ada_w.shape
    tn = 1536
    return pl.pallas_call(
        _ada_kernel,
        grid=(depth, n // tn),
        in_specs=[
            _full((COND_ROWS, d)),
            pl.BlockSpec((None, d, tn), lambda l, j: (l, 0, j)),
            pl.BlockSpec((None, 1, tn), lambda l, j: (l, 0, j)),
        ],
        out_specs=pl.BlockSpec((None, COND_ROWS, tn), lambda l, j: (l, 0, j)),
        out_shape=jax.ShapeDtypeStruct((depth, COND_ROWS, n), F32),
        compiler_params=_params(2),
        name="ada_mod",
    )(cond, ada_w, ada_b.reshape(depth, 1, n))


def _proj_ab_kernel(rope, emit_cache, x_ref, mod_ref, g_ref, w_in_ref, qn_ref, wqb_ref, kvn_ref, wk_ref, wv_ref,
                    e_ref, *rest):
    if rope:
        rope_d, rope_q, rope_k = rest[:3]
        rest = rest[3:]
    qa_o, ka_o, va_o, qb_o, km_o, vm_o = rest[:6]
    mod = mod_ref[0]
    h = _modulate(x_ref[...], g_ref[...], mod[0:1], mod[1:2]).astype(BF16)
    big = _dot(h, w_in_ref[...])
    qa, ka, va = big[:, 0:512], big[:, 512:1024], big[:, 1024:1536]
    q_lat, ckv, kr = big[:, 1536:1920], big[:, 1920:2176], big[:, 2176:2304]
    qb = _dot(_rms(q_lat, qn_ref[...]).astype(BF16), wqb_ref[...])
    ckv_n = _rms(ckv, kvn_ref[...])
    if emit_cache:
        ka32_o, va32_o, ckv32_o, kr32_o = rest[6:10]
        ka32_o[...] = ka
        va32_o[...] = va
        ckv32_o[...] = ckv_n
        kr32_o[...] = kr[:, :MLA_ROPE_DIM]
    if rope:
        qa, ka = _rope(qa, rope_d), _rope(ka, rope_d)
        qb, kr = _rope(qb, rope_q), _rope(kr, rope_k)
    qa = qa * (DIFF_QK_DIM ** -0.5)
    qb = qb * ((MLA_NOPE_DIM + MLA_ROPE_DIM) ** -0.5)
    ckv16 = ckv_n.astype(BF16)
    km = _dot(ckv16, wk_ref[...]) + _dot(kr.astype(BF16), e_ref[...])
    vm = _dot(ckv16, wv_ref[...])
    lo = lax.broadcasted_iota(jnp.int32, (qa.shape[0], LANES), 1) < DIFF_QK_DIM
    for hd in range(DIFF_HEADS):
        sl = slice(hd * LANES, (hd + 1) * LANES)
        qa_o[2 * hd] = jnp.where(lo, qa[:, sl], 0.0).astype(BF16)
        qa_o[2 * hd + 1] = jnp.where(lo, 0.0, qa[:, sl]).astype(BF16)
        ka_o[hd] = ka[:, sl].astype(BF16)
        va_o[hd] = va[:, sl].astype(BF16)
    for hd in range(MLA_HEADS):
        sl = slice(hd * LANES, (hd + 1) * LANES)
        qb_o[hd] = qb[:, sl].astype(BF16)
        km_o[hd] = km[:, sl].astype(BF16)
        vm_o[hd] = vm[:, sl].astype(BF16)


def _proj_ab(x, mods, row_fn, g, wts, rope_tabs, emit_cache, tm):
    b, s, d = x.shape
    rope = rope_tabs is not None
    tok = lambda bi, t: (bi, t, 0)
    head = lambda bi, t: (bi, 0, t, 0)
    in_specs = [
        pl.BlockSpec((None, tm, d), tok),
        pl.BlockSpec((1, 6, d), lambda bi, t: (row_fn(bi), 0, 0)),
        _full((1, d)),
        _full((d, AB_IN_PAD)), _full((1, MLA_Q_RANK)), _full((MLA_Q_RANK, 1024)), _full((1, MLA_KV_RANK)),
        _full((MLA_KV_RANK, 1024)), _full((MLA_KV_RANK, 1024)), _full((LANES, 1024)),
    ]
    args = [x, mods, g, wts["w_in"], wts["qn_g"], wts["w_qb"], wts["kvn_g"], wts["wk"], wts["wv"], wts["e"]]
    if rope:
        in_specs += [pl.BlockSpec((3, tm, LANES), lambda bi, t: (0, t, 0))] * 3
        args += list(rope_tabs)

    def hm(nh):
        return jax.ShapeDtypeStruct((b, nh, s, LANES), BF16), pl.BlockSpec((None, nh, tm, LANES), head)

    outs = [hm(8), hm(4), hm(4), hm(8), hm(8), hm(8)]
    if emit_cache:
        for w in (512, 512, MLA_KV_RANK, MLA_ROPE_DIM):
            outs.append((jax.ShapeDtypeStruct((b, s, w), F32), pl.BlockSpec((None, tm, w), tok)))
    return pl.pallas_call(
        functools.partial(_proj_ab_kernel, rope, emit_cache),
        grid=(b, s // tm),
        in_specs=in_specs,
        out_specs=[o[1] for o in outs],
        out_shape=[o[0] for o in outs],
        compiler_params=_params(2),
        name="proj_ab",
    )(*args)


def _cache_ab_kernel(dk_ref, dv_ref, ckv_ref, kr_ref, wk_ref, wv_ref, e_ref, ck_o, cv_o, km_o, vm_o):
    ckv16 = ckv_ref[...].astype(BF16)
    km = _dot(ckv16, wk_ref[...]) + _dot(kr_ref[...].astype(BF16), e_ref[...])
    vm = _dot(ckv16, wv_ref[...])
    for hd in range(DIFF_HEADS):
        sl = slice(hd * LANES, (hd + 1) * LANES)
        ck_o[hd] = dk_ref[:, sl].astype(BF16)
        cv_o[hd] = dv_ref[:, sl].astype(BF16)
    for hd in range(MLA_HEADS):
        sl = slice(hd * LANES, (hd + 1) * LANES)
        km_o[hd] = km[:, sl].astype(BF16)
        vm_o[hd] = vm[:, sl].astype(BF16)


def _cache_ab(cdk, cdv, cckv, ckr, j, wts):
    b, _, p, _ = cdk.shape
    lay = lambda bi: (bi, j, 0, 0)

    def hm(nh):
        return (jax.ShapeDtypeStruct((b, nh, p, LANES), BF16),
                pl.BlockSpec((None, nh, p, LANES), lambda bi: (bi, 0, 0, 0)))

    outs = [hm(4), hm(4), hm(8), hm(8)]
    return pl.pallas_call(
        _cache_ab_kernel,
        grid=(b,),
        in_specs=[
            pl.BlockSpec((None, None, p, 512), lay), pl.BlockSpec((None, None, p, 512), lay),
            pl.BlockSpec((None, None, p, MLA_KV_RANK), lay), pl.BlockSpec((None, None, p, LANES), lay),
            _full((MLA_KV_RANK, 1024)), _full((MLA_KV_RANK, 1024)), _full((LANES, 1024)),
        ],
        out_specs=[o[1] for o in outs],
        out_shape=[o[0] for o in outs],
        compiler_params=_params(1),
        name="cache_ab",
    )(cdk, cdv, cckv, ckr, wts["wk"], wts["wv"], wts["e"])


def _head_loop(n_heads, unroll, body, acc):
    if unroll >= n_heads:
        for hd in range(n_heads):
            acc = body(hd, acc)
        return acc

    def trip(i, a):
        for k in range(unroll):
            a = body(i * unroll + k, a)
        return a

    return lax.fori_loop(0, n_heads // unroll, trip, acc)


def _attn_ab_kernel(n_seg, lambda_init, x_ref, mod_ref, qa_ref, qb_ref, *rest):
    segs = [rest[4 * i:4 * i + 4] for i in range(n_seg)]
    wd_ref, wm_ref, sg_ref, lam_ref, o_ref = rest[4 * n_seg:4 * n_seg + 5]
    bufs = rest[4 * n_seg + 5:]
    lp = lam_ref[...]
    lam = (jnp.exp(jnp.sum(lp[0:1] * lp[1:2], axis=-1, keepdims=True))
           - jnp.exp(jnp.sum(lp[2:3] * lp[3:4], axis=-1, keepdims=True)) + lambda_init)
    tq, d = x_ref.shape
    widths = [sg[0].shape[1] for sg in segs]
    offs = [sum(widths[:i]) for i in range(n_seg)]
    units = [("diff", hd) for hd in range(DIFF_HEADS)] + [("mla", hd) for hd in range(MLA_HEADS)]

    def scores(u, buf):
        kind, hd = units[u]
        qs = [qa_ref[2 * hd], qa_ref[2 * hd + 1]] if kind == "diff" else [qb_ref[hd]]
        ks = [sg[0 if kind == "diff" else 2][hd] for sg in segs]
        ms = []
        for i, q in enumerate(qs):
            folds = []
            for k, off, w in zip(ks, offs, widths):
                s = _dot_nt(q, k)
                buf[i, :, off:off + w] = s
                folds.append(_lane_fold(jnp.maximum, s))
            ms.append(jnp.max(_tree(jnp.maximum, folds), axis=-1, keepdims=True))
        return ms

    def exps(buf, i, m):
        e = [jnp.exp(buf[i, :, off:off + w] - m) for off, w in zip(offs, widths)]
        l = jnp.sum(_tree(jnp.add, [_lane_fold(jnp.add, x) for x in e]), axis=-1, keepdims=True)
        return e, l

    def consume(u, buf, ms, acc):
        kind, hd = units[u]
        if kind == "diff":
            vs = [sg[1][hd] for sg in segs]
            (e1, l1), (e2, l2) = exps(buf, 0, ms[0]), exps(buf, 1, ms[1])
            c1, c2 = 1.0 / l1, lam / l2
            o = _tree(jnp.add, [_dot((a * c1 - b * c2).astype(BF16), v) for a, b, v in zip(e1, e2, vs)])
            od = _rms(o, sg_ref[...]) * (1.0 - lambda_init)
            return acc + _dot(od.astype(BF16), wd_ref[hd])
        vs = [sg[3][hd] for sg in segs]
        e, l = exps(buf, 0, ms[0])
        o = _tree(jnp.add, [_dot(x.astype(BF16), v) for x, v in zip(e, vs)]) / l
        return acc + _dot(o.astype(BF16), wm_ref[hd])

    acc = jnp.zeros((tq, d), F32)
    ms = scores(0, bufs[0])
    for u in range(len(units)):
        nxt = scores(u + 1, bufs[(u + 1) % len(bufs)]) if u + 1 < len(units) else None
        acc = consume(u, bufs[u % len(bufs)], ms, acc)
        ms = nxt
    o_ref[...] = x_ref[...] + mod_ref[0][2:3] * acc


def _attn_ab(x, mods, row_fn, q_parts, seg_list, wts, lambda_init, tq, n_bufs):
    b, s, d = x.shape
    qa, qb = q_parts
    in_specs = [
        pl.BlockSpec((None, tq, d), lambda bi, t: (bi, t, 0)),
        pl.BlockSpec((1, 6, d), lambda bi, t: (row_fn(bi), 0, 0)),
        pl.BlockSpec((None, 8, tq, LANES), lambda bi, t: (bi, 0, t, 0)),
        pl.BlockSpec((None, 8, tq, LANES), lambda bi, t: (bi, 0, t, 0)),
    ]
    args = [x, mods, qa, qb]
    for seg in seg_list:
        for arr in seg:
            nh, nk = arr.shape[1], arr.shape[2]
            in_specs.append(pl.BlockSpec((None, nh, nk, LANES), lambda bi, t: (bi, 0, 0, 0),
                                         pipeline_mode=pl.Buffered(1)))
            args.append(arr)
    in_specs += [_full((DIFF_HEADS, LANES, d)), _full((MLA_HEADS, LANES, d)), _full((1, LANES)),
                 _full((4, DIFF_QK_DIM))]
    args += [wts["w_out_d"], wts["w_out_m"], wts["subln_g"], wts["lam"]]
    n_keys = sum(seg[0].shape[2] for seg in seg_list)
    return pl.pallas_call(
        functools.partial(_attn_ab_kernel, len(seg_list), lambda_init),
        grid=(b, s // tq),
        in_specs=in_specs,
        out_specs=pl.BlockSpec((None, tq, d), lambda bi, t: (bi, t, 0)),
        out_shape=jax.ShapeDtypeStruct((b, s, d), F32),
        scratch_shapes=[pltpu.VMEM((2, tq, n_keys), F32) for _ in range(n_bufs)],
        compiler_params=_params(2),
        name="attn_ab",
    )(*args)


def _proj_swa_kernel(rope, emit_cache, x_ref, mod_ref, g_ref, w_in_ref, *rest):
    if rope:
        rope_d = rest[0]
        rest = rest[1:]
    q_o, k_o, v_o = rest[:3]
    mod = mod_ref[0]
    h = _modulate(x_ref[...], g_ref[...], mod[0:1], mod[1:2]).astype(BF16)
    big = _dot(h, w_in_ref[...])
    q, k, v = big[:, :1024], big[:, 1024:1280], big[:, 1280:1536]
    if emit_cache:
        k32_o, v32_o = rest[3:5]
        k32_o[...] = k
        v32_o[...] = v
    if rope:
        q, k = _rope(q, rope_d), _rope(k, rope_d)
    q = q * (SWA_HEAD_DIM ** -0.5)
    lo = lax.broadcasted_iota(jnp.int32, (q.shape[0], LANES), 1) < SWA_HEAD_DIM
    for pair in range(2):
        for grp in range(SWA_GROUP):
            ca, cb = (2 * pair) * 2 + grp // 2, (2 * pair + 1) * 2 + grp // 2
            a = q[:, ca * LANES:(ca + 1) * LANES]
            bb = q[:, cb * LANES:(cb + 1) * LANES]
            if grp % 2 == 0:
                bb = pltpu.roll(bb, SWA_HEAD_DIM, 1)
            else:
                a = pltpu.roll(a, SWA_HEAD_DIM, 1)
            q_o[pair * SWA_GROUP + grp] = jnp.where(lo, a, bb).astype(BF16)
    for kvh in range(SWA_KV_HEADS):
        sl = slice((kvh // 2) * LANES, (kvh // 2 + 1) * LANES)
        keep = lo if kvh % 2 == 0 else jnp.logical_not(lo)
        k_o[kvh] = jnp.where(keep, k[:, sl], 0.0).astype(BF16)
        v_o[kvh] = jnp.where(keep, v[:, sl], 0.0).astype(BF16)


def _proj_swa(x, mods, row_fn, g, w_in, rope_tab, emit_cache, tm):
    b, s, d = x.shape
    rope = rope_tab is not None
    tok = lambda bi, t: (bi, t, 0)
    head = lambda bi, t: (bi, 0, t, 0)
    in_specs = [
        pl.BlockSpec((None, tm, d), tok),
        pl.BlockSpec((1, 6, d), lambda bi, t: (row_fn(bi), 0, 0)),
        _full((1, d)), _full((d, 1536)),
    ]
    args = [x, mods, g, w_in]
    if rope:
        in_specs.append(pl.BlockSpec((3, tm, LANES), lambda bi, t: (0, t, 0)))
        args.append(rope_tab)

    def hm(nh):
        return jax.ShapeDtypeStruct((b, nh, s, LANES), BF16), pl.BlockSpec((None, nh, tm, LANES), head)

    outs = [hm(8), hm(4), hm(4)]
    if emit_cache:
        for _ in range(2):
            outs.append((jax.ShapeDtypeStruct((b, s, 256), F32), pl.BlockSpec((None, tm, 256), tok)))
    return pl.pallas_call(
        functools.partial(_proj_swa_kernel, rope, emit_cache),
        grid=(b, s // tm),
        in_specs=in_specs,
        out_specs=[o[1] for o in outs],
        out_shape=[o[0] for o in outs],
        compiler_params=_params(2),
        name="proj_swa",
    )(*args)


def _attn_swa_kernel(windowed, band_w, x_ref, mod_ref, q_ref, k_ref, v_ref, *rest):
    if windowed:
        ck_ref, cv_ref = rest[:2]
        rest = rest[2:]
    sink_ref, w_ref, o_ref = rest[:3]
    bufs = rest[3:]
    tq, d = x_ref.shape
    n_keys = k_ref.shape[1]
    lo_k = lax.broadcasted_iota(jnp.int32, (1, LANES), 1) < SWA_HEAD_DIM
    lo_q = lax.broadcasted_iota(jnp.int32, (tq, LANES), 1) < SWA_HEAD_DIM
    if windowed:
        start = pl.program_id(1) * tq
        bstart = pl.multiple_of(jnp.clip(start - WINDOW, 0, n_keys - band_w), LANES)
        qpos = start + lax.broadcasted_iota(jnp.int32, (tq, band_w), 0)
        kpos = bstart + lax.broadcasted_iota(jnp.int32, (tq, band_w), 1)
        bias = jnp.where(jnp.abs(qpos - kpos) <= WINDOW, 0.0, NEG_INF)

    def keys_values(kvh):
        if not windowed:
            return [k_ref[kvh]], [v_ref[kvh]]
        keep = lo_k if kvh % 2 == 0 else jnp.logical_not(lo_k)
        sl = slice((kvh // 2) * LANES, (kvh // 2 + 1) * LANES)
        kc = jnp.where(keep, ck_ref[:, sl], 0.0).astype(BF16)
        vc = jnp.where(keep, cv_ref[:, sl], 0.0).astype(BF16)
        return ([k_ref[kvh, pl.ds(bstart, band_w), :], kc], [v_ref[kvh, pl.ds(bstart, band_w), :], vc])

    def scores(kvh, buf):
        pair = kvh // 2
        q4 = jnp.concatenate([q_ref[pair * SWA_GROUP + g] for g in range(SWA_GROUP)], axis=0)
        sink = jnp.concatenate(
            [jnp.full((tq, 1), sink_ref[kvh * SWA_GROUP + g], F32) for g in range(SWA_GROUP)], axis=0)
        folds, off = [], 0
        for i, k in enumerate(keys_values(kvh)[0]):
            s = _dot_nt(q4, k)
            if windowed and i == 0:
                s = jnp.concatenate([s[g * tq:(g + 1) * tq] + bias for g in range(SWA_GROUP)], axis=0)
            buf[:, off:off + s.shape[1]] = s
            folds.append(_lane_fold(jnp.maximum, s))
            off += s.shape[1]
        m = jnp.maximum(jnp.max(_tree(jnp.maximum, folds), axis=-1, keepdims=True), sink)
        return m, sink

    def consume(kvh, buf, m, sink):
        vs = keys_values(kvh)[1]
        e, off = [], 0
        for v in vs:
            e.append(jnp.exp(buf[:, off:off + v.shape[0]] - m))
            off += v.shape[0]
        l = jnp.sum(_tree(jnp.add, [_lane_fold(jnp.add, x) for x in e]), axis=-1, keepdims=True) + jnp.exp(sink - m)
        return _tree(jnp.add, [_dot(x.astype(BF16), v) for x, v in zip(e, vs)]) / l

    acc = jnp.zeros((tq, d), F32)
    nxt = scores(0, bufs[0])
    o4 = None
    for kvh in range(SWA_KV_HEADS):
        cur = nxt
        if kvh + 1 < SWA_KV_HEADS:
            nxt = scores(kvh + 1, bufs[(kvh + 1) % len(bufs)])
        o = consume(kvh, bufs[kvh % len(bufs)], *cur)
        if kvh % 2 == 0:
            o4 = o
            continue
        o4 = o4 + o
        for half in range(2):
            for i in range(2):
                xa = o4[(2 * i) * tq:(2 * i + 1) * tq]
                xb = o4[(2 * i + 1) * tq:(2 * i + 2) * tq]
                if half == 0:
                    xb = pltpu.roll(xb, SWA_HEAD_DIM, 1)
                else:
                    xa = pltpu.roll(xa, SWA_HEAD_DIM, 1)
                slab = jnp.where(lo_q, xa, xb).astype(BF16)
                acc = acc + _dot(slab, w_ref[(kvh - 1 + half) * 2 + i])
    o_ref[...] = x_ref[...] + mod_ref[0][2:3] * acc


def _attn_swa(x, mods, row_fn, q, k, v, ctx, sink, w_out, tq, n_bufs):
    b, s, d = x.shape
    windowed = ctx is not None
    band_w = min(tq + 2 * WINDOW, s)
    kv_spec = pl.BlockSpec((None, SWA_KV_HEADS, s, LANES), lambda bi, t: (bi, 0, 0, 0))
    in_specs = [
        pl.BlockSpec((None, tq, d), lambda bi, t: (bi, t, 0)),
        pl.BlockSpec((1, 6, d), lambda bi, t: (row_fn(bi), 0, 0)),
        pl.BlockSpec((None, 8, tq, LANES), lambda bi, t: (bi, 0, t, 0)),
        kv_spec, kv_spec,
    ]
    args = [x, mods, q, k, v]
    if windowed:
        ck, cv, j = ctx
        p = ck.shape[2]
        spec = pl.BlockSpec((None, None, p, 256), lambda bi, t: (bi, j, 0, 0))
        in_specs += [spec, spec]
        args += [ck, cv]
    in_specs += [pl.BlockSpec(memory_space=pltpu.SMEM), _full((8, LANES, d))]
    args += [sink, w_out]
    n_keys = band_w + ctx[0].shape[2] if windowed else s
    return pl.pallas_call(
        functools.partial(_attn_swa_kernel, windowed, band_w),
        grid=(b, s // tq),
        in_specs=in_specs,
        out_specs=pl.BlockSpec((None, tq, d), lambda bi, t: (bi, t, 0)),
        out_shape=jax.ShapeDtypeStruct((b, s, d), F32),
        scratch_shapes=[pltpu.VMEM((SWA_GROUP * tq, n_keys), F32) for _ in range(n_bufs)],
        compiler_params=_params(2),
        name="attn_swa",
    )(*args)


def _route(scores_t, bias):
    sel_t = scores_t + bias
    sel = [sel_t[e:e + 1] for e in range(N_EXPERTS)]
    raw = [scores_t[e:e + 1] for e in range(N_EXPERTS)]
    gscore = []
    for g in range(N_GROUPS):
        r = sel[g * 4:g * 4 + 4]
        pairs = [r[i] + r[j] for i in range(4) for j in range(i + 1, 4)]
        gscore.append(functools.reduce(jnp.maximum, pairs))
    best, gidx = gscore[0], jnp.zeros_like(gscore[0], dtype=jnp.int32)
    for g in range(1, N_GROUPS):
        take = gscore[g] > best
        best = jnp.where(take, gscore[g], best)
        gidx = jnp.where(take, g, gidx)
    vals = []
    for k in range(EXPERTS_PER_GROUP):
        v = sel[k]
        for g in range(1, N_GROUPS):
            v = jnp.where(gidx == g, sel[g * 4 + k], v)
        vals.append(v)

    def argmax4(vs):
        m, idx = vs[0], jnp.zeros_like(gidx)
        for k in range(1, 4):
            take = vs[k] > m
            m = jnp.where(take, vs[k], m)
            idx = jnp.where(take, k, idx)
        return idx

    i1 = argmax4(vals)
    i2 = argmax4([jnp.where(i1 == k, -jnp.inf, vals[k]) for k in range(4)])
    e1, e2 = gidx * 4 + i1, gidx * 4 + i2
    w1 = functools.reduce(lambda a, b: a + b, [jnp.where(e1 == e, raw[e], 0.0) for e in range(N_EXPERTS)])
    w2 = functools.reduce(lambda a, b: a + b, [jnp.where(e2 == e, raw[e], 0.0) for e in range(N_EXPERTS)])
    den = w1 + w2
    g1, g2 = w1 / den, w2 / den
    rows = [jnp.where(e1 == e, g1, 0.0) + jnp.where(e2 == e, g2, 0.0) for e in range(N_EXPERTS)]
    return jnp.concatenate(rows, axis=0)


def _moe_kernel(final_norm, x_ref, mod_ref, g_ref, rw_ref, rb_ref, wgu_ref, wd_ref, *rest):
    if final_norm:
        fg_ref, o_ref = rest
    else:
        (o_ref,) = rest
    tm, d = x_ref.shape
    mod = mod_ref[0]
    x = x_ref[...]
    h = _modulate(x, g_ref[...], mod[3:4], mod[4:5])
    h16 = h.astype(BF16)
    h_lo = (h - h16.astype(F32)).astype(BF16)
    part = _dot_nt(rw_ref[...], h16)
    logits_t = part[:N_EXPERTS] + part[N_EXPERTS:] + _dot_nt(rw_ref[:N_EXPERTS], h_lo)
    gates_t = _route(_sigmoid(logits_t), rb_ref[...])
    gates = jnp.concatenate([gates_t, jnp.zeros((LANES - N_EXPERTS, tm), F32)], axis=0).T
    lane = lax.broadcasted_iota(jnp.int32, (tm, LANES), 1)

    def body(e, acc):
        gate = jnp.sum(jnp.where(lane == e, gates, 0.0), axis=-1, keepdims=True)
        gu = _dot(h16, wgu_ref[e])
        g, u = gu[:, :D_EXPERT], gu[:, D_EXPERT:]
        a = g * _sigmoid(g) * u * gate
        return acc + _dot(a.astype(BF16), wd_ref[e])

    acc = _head_loop(N_EXPERTS, 2, body, jnp.zeros((tm, d), F32))
    y = x + mod[5:6] * acc
    if final_norm:
        y = _rms(y, fg_ref[...])
    o_ref[...] = y


def _moe(x, mods, row_fn, g, rw_t, rb, wgu, wd, final_g, tm):
    b, s, d = x.shape
    final_norm = final_g is not None
    in_specs = [
        pl.BlockSpec((None, tm, d), lambda bi, t: (bi, t, 0)),
        pl.BlockSpec((1, 6, d), lambda bi, t: (row_fn(bi), 0, 0)),
        _full((1, d)), _full((2 * N_EXPERTS, d)), _full((N_EXPERTS, 1)),
        _resident((N_EXPERTS, d, 2 * D_EXPERT)), _resident((N_EXPERTS, D_EXPERT, d)),
    ]
    args = [x, mods, g, rw_t, rb, wgu, wd]
    if final_norm:
        in_specs.append(_full((1, d)))
        args.append(final_g)
    return pl.pallas_call(
        functools.partial(_moe_kernel, final_norm),
        grid=(b, s // tm),
        in_specs=in_specs,
        out_specs=pl.BlockSpec((None, tm, d), lambda bi, t: (bi, t, 0)),
        out_shape=jax.ShapeDtypeStruct((b, s, d), F32),
        compiler_params=_params(2),
        name="moe",
    )(*args)


def _rope_tables(n_tokens, rot_dim, offset):
    n_rows = n_tokens // GRID_W
    rows = np.repeat(np.arange(n_rows), GRID_W)
    cols = np.tile(np.arange(GRID_W), n_rows)
    n_freq = rot_dim // 4
    inv = jnp.asarray(ROPE_THETA, F32) ** (-jnp.arange(n_freq, dtype=F32) / n_freq)
    ang = jnp.concatenate([jnp.asarray(rows, F32)[:, None] * inv, jnp.asarray(cols, F32)[:, None] * inv], axis=-1)
    cos, sin = jnp.repeat(jnp.cos(ang), 2, axis=-1), jnp.repeat(jnp.sin(ang), 2, axis=-1)
    odd = jnp.asarray(np.arange(rot_dim) % 2 == 1)
    parts = [cos, jnp.where(odd, sin, 0.0), jnp.where(odd, 0.0, -sin)]
    period = 64 if rot_dim == 64 else LANES
    fill = [1.0, 0.0, 0.0]
    out = []
    for p, f in zip(parts, fill):
        slot = jnp.full((n_tokens, period), f, F32).at[:, offset:offset + rot_dim].set(p)
        out.append(jnp.tile(slot, (1, LANES // period)))
    return jnp.stack(out)


def _pad_heads(w, n_heads, lo, hi):
    k = w.shape[0]
    w = w.reshape(k, n_heads, -1)[:, :, lo:hi]
    return jnp.pad(w, ((0, 0), (0, 0), (0, LANES - (hi - lo)))).reshape(k, n_heads * LANES)


def _ab_weights(j, ab_w_in, diff_lambda, diff_subln_g, mla_q_norm_g, mla_w_qb, mla_kv_norm_g, mla_w_kvb, ab_w_out):
    place = np.zeros((LANES, MLA_HEADS * LANES), np.float32)
    for hd in range(MLA_HEADS):
        for r in range(MLA_ROPE_DIM):
            place[r, hd * LANES + MLA_NOPE_DIM + r] = 1.0
    w_out_m = ab_w_out[j][512:].reshape(MLA_HEADS, MLA_V_DIM, D_MODEL)
    return {
        "w_in": jnp.pad(ab_w_in[j], ((0, 0), (0, AB_IN_PAD - AB_IN))).astype(BF16),
        "qn_g": mla_q_norm_g[j][None, :],
        "w_qb": _pad_heads(mla_w_qb[j], MLA_HEADS, 0, MLA_NOPE_DIM + MLA_ROPE_DIM).astype(BF16),
        "kvn_g": mla_kv_norm_g[j][None, :],
        "wk": _pad_heads(mla_w_kvb[j], MLA_HEADS, 0, MLA_NOPE_DIM).astype(BF16),
        "wv": _pad_heads(mla_w_kvb[j], MLA_HEADS, MLA_NOPE_DIM, MLA_NOPE_DIM + MLA_V_DIM).astype(BF16),
        "e": jnp.asarray(place, BF16),
        "w_out_d": ab_w_out[j][:512].reshape(DIFF_HEADS, LANES, D_MODEL).astype(BF16),
        "w_out_m": jnp.pad(w_out_m, ((0, 0), (0, LANES - MLA_V_DIM), (0, 0))).astype(BF16),
        "subln_g": diff_subln_g[j][None, :],
        "lam": diff_lambda[j],
    }


def kernel(x_prompt, x_sample, cache_diff_k, cache_diff_v, cache_mla_ckv, cache_mla_krope, cache_swa_k, cache_swa_v,
           c, c_ctx, ada_w, ada_b, norm1_g, norm2_g, final_norm_g, ab_w_in, diff_lambda, diff_subln_g, mla_q_norm_g,
           mla_w_qb, mla_kv_norm_g, mla_w_kvb, ab_w_out, swa_w_in, swa_sink, swa_w_out, router_w, router_bias,
           moe_w_gate, moe_w_up, moe_w_down):
    bp, sp, d = x_prompt.shape
    bs, ss, _ = x_sample.shape
    depth = ada_w.shape[0]
    past = cache_diff_k.shape[2]
    n_ab = cache_diff_k.shape[1]
    n_swa = cache_swa_k.shape[1]
    assert bs + 1 <= COND_ROWS and d == D_MODEL

    cond = jnp.zeros((COND_ROWS, d), F32).at[:bs].set(c).at[bs].set(c_ctx)
    mods = _ada_all(cond, ada_w, ada_b).reshape(depth, COND_ROWS, 6, d)
    row_s = lambda bi: bi
    row_p = lambda bi: bi * 0 + bs

    rope_d = _rope_tables(ss, DIFF_QK_DIM, 0)
    rope_q = _rope_tables(ss, MLA_ROPE_DIM, MLA_NOPE_DIM)
    rope_k = _rope_tables(ss, MLA_ROPE_DIM, 0)

    cdk = cache_diff_k.reshape(bs, n_ab, past, 512)
    cdv = cache_diff_v.reshape(bs, n_ab, past, 512)
    ckr = jnp.pad(cache_mla_krope, ((0, 0), (0, 0), (0, 0), (0, LANES - MLA_ROPE_DIM)))
    csk = cache_swa_k.reshape(bs, n_swa, past, 256)
    csv = cache_swa_v.reshape(bs, n_swa, past, 256)
    rw_hi = router_w.T.astype(BF16)
    rw_t = jnp.concatenate([rw_hi, (router_w.T - rw_hi.astype(F32)).astype(BF16)], axis=0)
    rb = router_bias[:, None]

    tp = min(256, sp)
    ts = min(256, ss)
    tm_s = min(512, ss)
    tm_p = min(512, bp * sp)

    xp, xs = x_prompt, x_sample
    new = {k: [] for k in ("dk", "dv", "ckv", "kr", "sk", "sv")}
    for l in range(depth):
        j = l // 2
        ml = mods[l]
        g1, g2 = norm1_g[l][None, :], norm2_g[l][None, :]
        if l % 2 == 0:
            lambda_init = 0.8 - 0.6 * math.exp(-0.3 * l)
            wts = _ab_weights(j, ab_w_in, diff_lambda, diff_subln_g, mla_q_norm_g, mla_w_qb, mla_kv_norm_g,
                              mla_w_kvb, ab_w_out)
            qa, ka, va, qb, km, vm, ka32, va32, ckv32, kr32 = _proj_ab(xp, ml, row_p, g1, wts, None, True, tp)
            new["dk"].append(ka32)
            new["dv"].append(va32)
            new["ckv"].append(ckv32)
            new["kr"].append(kr32)
            xp = _attn_ab(xp, ml, row_p, (qa, qb), [(ka, va, km, vm)], wts, lambda_init, tp, DIFF_HEADS + MLA_HEADS)
            qa, ka, va, qb, km, vm = _proj_ab(xs, ml, row_s, g1, wts, (rope_d, rope_q, rope_k), False, ts)
            cache_seg = _cache_ab(cdk, cdv, cache_mla_ckv, ckr, j, wts)
            xs = _attn_ab(xs, ml, row_s, (qa, qb), [(ka, va, km, vm), tuple(cache_seg)], wts, lambda_init, ts, 2)
        else:
            w_in = swa_w_in[j].astype(BF16)
            w_out = swa_w_out[j].reshape(8, LANES, d).astype(BF16)
            q, k, v, k32, v32 = _proj_swa(xp, ml, row_p, g1, w_in, None, True, tp)
            new["sk"].append(k32)
            new["sv"].append(v32)
            xp = _attn_swa(xp, ml, row_p, q, k, v, None, swa_sink[j], w_out, tp, SWA_KV_HEADS)
            q, k, v = _proj_swa(xs, ml, row_s, g1, w_in, rope_d, False, ts)
            xs = _attn_swa(xs, ml, row_s, q, k, v, (csk, csv, j), swa_sink[j], w_out, ts, 2)
        wgu = jnp.concatenate([moe_w_gate[l], moe_w_up[l]], axis=-1).astype(BF16)
        wd = moe_w_down[l].astype(BF16)
        fg = final_norm_g[None, :] if l == depth - 1 else None
        xp = _moe(xp.reshape(1, bp * sp, d), ml, row_p, g2, rw_t, rb, wgu, wd, fg, tm_p).reshape(bp, sp, d)
        xs = _moe(xs, ml, row_s, g2, rw_t, rb, wgu, wd, fg, tm_s)

    new_diff_k = jnp.stack(new["dk"], axis=1).reshape(bp, n_ab, sp, DIFF_HEADS, 2, DIFF_QK_DIM)
    new_diff_v = jnp.stack(new["dv"], axis=1).reshape(bp, n_ab, sp, DIFF_HEADS, DIFF_V_DIM)
    new_mla_ckv = jnp.stack(new["ckv"], axis=1)
    new_mla_krope = jnp.stack(new["kr"], axis=1)
    new_swa_k = jnp.stack(new["sk"], axis=1).reshape(bp, n_swa, sp, SWA_KV_HEADS, SWA_HEAD_DIM)
    new_swa_v = jnp.stack(new["sv"], axis=1).reshape(bp, n_swa, sp, SWA_KV_HEADS, SWA_HEAD_DIM)
    return (xp, xs, new_diff_k, new_diff_v, new_mla_ckv, new_mla_krope, new_swa_k, new_swa_v)
```

```python
import functools
import math

import jax
import jax.numpy as jnp
import numpy as np
from jax import lax
from jax.experimental import pallas as pl
from jax.experimental.pallas import tpu as pltpu

F32 = jnp.float32
BF16 = jnp.bfloat16

D_MODEL = 1024
GRID_W = 64
ROPE_THETA = 10000.0
NORM_EPS = 1e-6
NEG_INF = -1e30
LANES = 128

DIFF_HEADS = 4
DIFF_QK_DIM = 64
DIFF_V_DIM = 128
MLA_HEADS = 8
MLA_Q_RANK = 384
MLA_KV_RANK = 256
MLA_NOPE_DIM = 64
MLA_ROPE_DIM = 32
MLA_V_DIM = 64
AB_IN = 3 * 512 + MLA_Q_RANK + MLA_KV_RANK + MLA_ROPE_DIM
AB_IN_PAD = 2304
SWA_HEADS = 16
SWA_KV_HEADS = 4
SWA_GROUP = 4
SWA_HEAD_DIM = 64
WINDOW = 128
N_EXPERTS = 16
N_GROUPS = 4
EXPERTS_PER_GROUP = 4
D_EXPERT = 256
COND_ROWS = 16

VMEM_LIMIT = 56 * 1024 * 1024


def _full(shape):
    n = len(shape)
    return pl.BlockSpec(shape, lambda *_: (0,) * n)


def _resident(shape):
    n = len(shape)
    return pl.BlockSpec(shape, lambda *_: (0,) * n, pipeline_mode=pl.Buffered(1))


def _params(n_axes):
    return pltpu.CompilerParams(dimension_semantics=("arbitrary",) * n_axes, vmem_limit_bytes=VMEM_LIMIT)


def _sigmoid(x):
    return 1.0 / (1.0 + jnp.exp(-x))


def _rms(x, g):
    return x * lax.rsqrt(jnp.mean(x * x, axis=-1, keepdims=True) + NORM_EPS) * g


def _modulate(x, g, shift, scale):
    return _rms(x, g) * (1.0 + scale) + shift


def _dot(a, b):
    return jnp.dot(a, b, preferred_element_type=F32)


def _dot_nt(a, b):
    return lax.dot_general(a, b, (((1,), (1,)), ((), ())), preferred_element_type=F32)


def _rope(x, tab_ref):
    c, s_odd, s_even = tab_ref[0], tab_ref[1], tab_ref[2]
    out = []
    for i in range(x.shape[1] // LANES):
        xi = x[:, i * LANES:(i + 1) * LANES]
        out.append(xi * c + pltpu.roll(xi, 1, 1) * s_odd + pltpu.roll(xi, LANES - 1, 1) * s_even)
    return out[0] if len(out) == 1 else jnp.concatenate(out, axis=1)


def _tree(op, xs):
    xs = list(xs)
    while len(xs) > 1:
        xs = [op(xs[i], xs[i + 1]) if i + 1 < len(xs) else xs[i] for i in range(0, len(xs), 2)]
    return xs[0]


def _lane_fold(op, x):
    return _tree(op, [x[:, i * LANES:(i + 1) * LANES] for i in range(x.shape[1] // LANES)])


def _softmax_parts(s_list, sink=None):
    m = jnp.max(_tree(jnp.maximum, [_lane_fold(jnp.maximum, s) for s in s_list]), axis=-1, keepdims=True)
    if sink is not None:
        m = jnp.maximum(m, sink)
    e_list = [jnp.exp(s - m) for s in s_list]
    l = jnp.sum(_tree(jnp.add, [_lane_fold(jnp.add, e) for e in e_list]), axis=-1, keepdims=True)
    if sink is not None:
        l = l + jnp.exp(sink - m)
    return e_list, l


def _softmax_pv(s_list, v_list, sink=None):
    e_list, l = _softmax_parts(s_list, sink)
    o = _tree(jnp.add, [_dot(e.astype(BF16), v) for e, v in zip(e_list, v_list)])
    return o / l


def _ada_kernel(cond_ref, w_ref, b_ref, o_ref):
    c = cond_ref[...]
    a = (c * _sigmoid(c)).astype(BF16)
    o_ref[...] = _dot(a, w_ref[...].astype(BF16)) + b_ref[...]


def _ada_all(cond, ada_w, ada_b):
    depth, d, n = ada_w.shape
    tn = 1536
    return pl.pallas_call(
        _ada_kernel,
        grid=(depth, n // tn),
        in_specs=[
            _full((COND_ROWS, d)),
            pl.BlockSpec((None, d, tn), lambda l, j: (l, 0, j)),
            pl.BlockSpec((None, 1, tn), lambda l, j: (l, 0, j)),
        ],
        out_specs=pl.BlockSpec((None, COND_ROWS, tn), lambda l, j: (l, 0, j)),
        out_shape=jax.ShapeDtypeStruct((depth, COND_ROWS, n), F32),
        compiler_params=_params(2),
        name="ada_mod",
    )(cond, ada_w, ada_b.reshape(depth, 1, n))


def _proj_ab_kernel(rope, emit_cache, x_ref, mod_ref, g_ref, w_in_ref, qn_ref, wqb_ref, kvn_ref, wk_ref, wv_ref,
                    e_ref, *rest):
    if rope:
        rope_d, rope_q, rope_k = rest[:3]
        rest = rest[3:]
    qa_o, ka_o, va_o, qb_o, km_o, vm_o = rest[:6]
    mod = mod_ref[0]
    h = _modulate(x_ref[...], g_ref[...], mod[0:1], mod[1:2]).astype(BF16)
    big = _dot(h, w_in_ref[...])
    qa, ka, va = big[:, 0:512], big[:, 512:1024], big[:, 1024:1536]
    q_lat, ckv, kr = big[:, 1536:1920], big[:, 1920:2176], big[:, 2176:2304]
    qb = _dot(_rms(q_lat, qn_ref[...]).astype(BF16), wqb_ref[...])
    ckv_n = _rms(ckv, kvn_ref[...])
    if emit_cache:
        ka32_o, va32_o, ckv32_o, kr32_o = rest[6:10]
        ka32_o[...] = ka
        va32_o[...] = va
        ckv32_o[...] = ckv_n
        kr32_o[...] = kr[:, :MLA_ROPE_DIM]
    if rope:
        qa, ka = _rope(qa, rope_d), _rope(ka, rope_d)
        qb, kr = _rope(qb, rope_q), _rope(kr, rope_k)
    qa = qa * (DIFF_QK_DIM ** -0.5)
    qb = qb * ((MLA_NOPE_DIM + MLA_ROPE_DIM) ** -0.5)
    ckv16 = ckv_n.astype(BF16)
    km = _dot(ckv16, wk_ref[...]) + _dot(kr.astype(BF16), e_ref[...])
    vm = _dot(ckv16, wv_ref[...])
    lo = lax.broadcasted_iota(jnp.int32, (qa.shape[0], LANES), 1) < DIFF_QK_DIM
    for hd in range(DIFF_HEADS):
        sl = slice(hd * LANES, (hd + 1) * LANES)
        qa_o[2 * hd] = jnp.where(lo, qa[:, sl], 0.0).astype(BF16)
        qa_o[2 * hd + 1] = jnp.where(lo, 0.0, qa[:, sl]).astype(BF16)
        ka_o[hd] = ka[:, sl].astype(BF16)
        va_o[hd] = va[:, sl].astype(BF16)
    for hd in range(MLA_HEADS):
        sl = slice(hd * LANES, (hd + 1) * LANES)
        qb_o[hd] = qb[:, sl].astype(BF16)
        km_o[hd] = km[:, sl].astype(BF16)
        vm_o[hd] = vm[:, sl].astype(BF16)


def _proj_ab(x, mods, row_fn, g, wts, rope_tabs, emit_cache, tm):
    b, s, d = x.shape
    rope = rope_tabs is not None
    tok = lambda bi, t: (bi, t, 0)
    head = lambda bi, t: (bi, 0, t, 0)
    in_specs = [
        pl.BlockSpec((None, tm, d), tok),
        pl.BlockSpec((1, 6, d), lambda bi, t: (row_fn(bi), 0, 0)),
        _full((1, d)),
        _full((d, AB_IN_PAD)), _full((1, MLA_Q_RANK)), _full((MLA_Q_RANK, 1024)), _full((1, MLA_KV_RANK)),
        _full((MLA_KV_RANK, 1024)), _full((MLA_KV_RANK, 1024)), _full((LANES, 1024)),
    ]
    args = [x, mods, g, wts["w_in"], wts["qn_g"], wts["w_qb"], wts["kvn_g"], wts["wk"], wts["wv"], wts["e"]]
    if rope:
        in_specs += [pl.BlockSpec((3, tm, LANES), lambda bi, t: (0, t, 0))] * 3
        args += list(rope_tabs)

    def hm(nh):
        return jax.ShapeDtypeStruct((b, nh, s, LANES), BF16), pl.BlockSpec((None, nh, tm, LANES), head)

    outs = [hm(8), hm(4), hm(4), hm(8), hm(8), hm(8)]
    if emit_cache:
        for w in (512, 512, MLA_KV_RANK, MLA_ROPE_DIM):
            outs.append((jax.ShapeDtypeStruct((b, s, w), F32), pl.BlockSpec((None, tm, w), tok)))
    return pl.pallas_call(
        functools.partial(_proj_ab_kernel, rope, emit_cache),
        grid=(b, s // tm),
        in_specs=in_specs,
        out_specs=[o[1] for o in outs],
        out_shape=[o[0] for o in outs],
        compiler_params=_params(2),
        name="proj_ab",
    )(*args)


def _cache_ab_kernel(dk_ref, dv_ref, ckv_ref, kr_ref, wk_ref, wv_ref, e_ref, ck_o, cv_o, km_o, vm_o):
    ckv16 = ckv_ref[...].astype(BF16)
    km = _dot(ckv16, wk_ref[...]) + _dot(kr_ref[...].astype(BF16), e_ref[...])
    vm = _dot(ckv16, wv_ref[...])
    for hd in range(DIFF_HEADS):
        sl = slice(hd * LANES, (hd + 1) * LANES)
        ck_o[hd] = dk_ref[:, sl].astype(BF16)
        cv_o[hd] = dv_ref[:, sl].astype(BF16)
    for hd in range(MLA_HEADS):
        sl = slice(hd * LANES, (hd + 1) * LANES)
        km_o[hd] = km[:, sl].astype(BF16)
        vm_o[hd] = vm[:, sl].astype(BF16)


def _cache_ab(cdk, cdv, cckv, ckr, j, wts):
    b, _, p, _ = cdk.shape
    lay = lambda bi: (bi, j, 0, 0)

    def hm(nh):
        return (jax.ShapeDtypeStruct((b, nh, p, LANES), BF16),
                pl.BlockSpec((None, nh, p, LANES), lambda bi: (bi, 0, 0, 0)))

    outs = [hm(4), hm(4), hm(8), hm(8)]
    return pl.pallas_call(
        _cache_ab_kernel,
        grid=(b,),
        in_specs=[
            pl.BlockSpec((None, None, p, 512), lay), pl.BlockSpec((None, None, p, 512), lay),
            pl.BlockSpec((None, None, p, MLA_KV_RANK), lay), pl.BlockSpec((None, None, p, LANES), lay),
            _full((MLA_KV_RANK, 1024)), _full((MLA_KV_RANK, 1024)), _full((LANES, 1024)),
        ],
        out_specs=[o[1] for o in outs],
        out_shape=[o[0] for o in outs],
        compiler_params=_params(1),
        name="cache_ab",
    )(cdk, cdv, cckv, ckr, wts["wk"], wts["wv"], wts["e"])


def _head_loop(n_heads, unroll, body, acc):
    if unroll >= n_heads:
        for hd in range(n_heads):
            acc = body(hd, acc)
        return acc

    def trip(i, a):
        for k in range(unroll):
            a = body(i * unroll + k, a)
        return a

    return lax.fori_loop(0, n_heads // unroll, trip, acc)


def _attn_ab_kernel(n_seg, lambda_init, x_ref, mod_ref, qa_ref, qb_ref, *rest):
    segs = [rest[4 * i:4 * i + 4] for i in range(n_seg)]
    wd_ref, wm_ref, sg_ref, lam_ref, o_ref = rest[4 * n_seg:4 * n_seg + 5]
    bufs = rest[4 * n_seg + 5:]
    lp = lam_ref[...]
    lam = (jnp.exp(jnp.sum(lp[0:1] * lp[1:2], axis=-1, keepdims=True))
           - jnp.exp(jnp.sum(lp[2:3] * lp[3:4], axis=-1, keepdims=True)) + lambda_init)
    tq, d = x_ref.shape
    widths = [sg[0].shape[1] for sg in segs]
    offs = [sum(widths[:i]) for i in range(n_seg)]
    units = [("diff", hd) for hd in range(DIFF_HEADS)] + [("mla", hd) for hd in range(MLA_HEADS)]

    def scores(u, buf):
        kind, hd = units[u]
        qs = [qa_ref[2 * hd], qa_ref[2 * hd + 1]] if kind == "diff" else [qb_ref[hd]]
        ks = [sg[0 if kind == "diff" else 2][hd] for sg in segs]
        ms = []
        for i, q in enumerate(qs):
            folds = []
            for k, off, w in zip(ks, offs, widths):
                s = _dot_nt(q, k)
                buf[i, :, off:off + w] = s
                folds.append(_lane_fold(jnp.maximum, s))
            ms.append(jnp.max(_tree(jnp.maximum, folds), axis=-1, keepdims=True))
        return ms

    def exps(buf, i, m):
        e = [jnp.exp(buf[i, :, off:off + w] - m) for off, w in zip(offs, widths)]
        l = jnp.sum(_tree(jnp.add, [_lane_fold(jnp.add, x) for x in e]), axis=-1, keepdims=True)
        return e, l

    def consume(u, buf, ms, acc):
        kind, hd = units[u]
        if kind == "diff":
            vs = [sg[1][hd] for sg in segs]
            (e1, l1), (e2, l2) = exps(buf, 0, ms[0]), exps(buf, 1, ms[1])
            c1, c2 = 1.0 / l1, lam / l2
            o = _tree(jnp.add, [_dot((a * c1 - b * c2).astype(BF16), v) for a, b, v in zip(e1, e2, vs)])
            od = _rms(o, sg_ref[...]) * (1.0 - lambda_init)
            return acc + _dot(od.astype(BF16), wd_ref[hd])
        vs = [sg[3][hd] for sg in segs]
        e, l = exps(buf, 0, ms[0])
        o = _tree(jnp.add, [_dot(x.astype(BF16), v) for x, v in zip(e, vs)]) / l
        return acc + _dot(o.astype(BF16), wm_ref[hd])

    acc = jnp.zeros((tq, d), F32)
    ms = scores(0, bufs[0])
    for u in range(len(units)):
        nxt = scores(u + 1, bufs[(u + 1) % len(bufs)]) if u + 1 < len(units) else None
        acc = consume(u, bufs[u % len(bufs)], ms, acc)
        ms = nxt
    o_ref[...] = x_ref[...] + mod_ref[0][2:3] * acc


def _attn_ab(x, mods, row_fn, q_parts, seg_list, wts, lambda_init, tq, n_bufs):
    b, s, d = x.shape
    qa, qb = q_parts
    in_specs = [
        pl.BlockSpec((None, tq, d), lambda bi, t: (bi, t, 0)),
        pl.BlockSpec((1, 6, d), lambda bi, t: (row_fn(bi), 0, 0)),
        pl.BlockSpec((None, 8, tq, LANES), lambda bi, t: (bi, 0, t, 0)),
        pl.BlockSpec((None, 8, tq, LANES), lambda bi, t: (bi, 0, t, 0)),
    ]
    args = [x, mods, qa, qb]
    for seg in seg_list:
        for arr in seg:
            nh, nk = arr.shape[1], arr.shape[2]
            in_specs.append(pl.BlockSpec((None, nh, nk, LANES), lambda bi, t: (bi, 0, 0, 0),
                                         pipeline_mode=pl.Buffered(1)))
            args.append(arr)
    in_specs += [_full((DIFF_HEADS, LANES, d)), _full((MLA_HEADS, LANES, d)), _full((1, LANES)),
                 _full((4, DIFF_QK_DIM))]
    args += [wts["w_out_d"], wts["w_out_m"], wts["subln_g"], wts["lam"]]
    n_keys = sum(seg[0].shape[2] for seg in seg_list)
    return pl.pallas_call(
        functools.partial(_attn_ab_kernel, len(seg_list), lambda_init),
        grid=(b, s // tq),
        in_specs=in_specs,
        out_specs=pl.BlockSpec((None, tq, d), lambda bi, t: (bi, t, 0)),
        out_shape=jax.ShapeDtypeStruct((b, s, d), F32),
        scratch_shapes=[pltpu.VMEM((2, tq, n_keys), F32) for _ in range(n_bufs)],
        compiler_params=_params(2),
        name="attn_ab",
    )(*args)


def _proj_swa_kernel(rope, emit_cache, x_ref, mod_ref, g_ref, w_in_ref, *rest):
    if rope:
        rope_d = rest[0]
        rest = rest[1:]
    q_o, k_o, v_o = rest[:3]
    mod = mod_ref[0]
    h = _modulate(x_ref[...], g_ref[...], mod[0:1], mod[1:2]).astype(BF16)
    big = _dot(h, w_in_ref[...])
    q, k, v = big[:, :1024], big[:, 1024:1280], big[:, 1280:1536]
    if emit_cache:
        k32_o, v32_o = rest[3:5]
        k32_o[...] = k
        v32_o[...] = v
    if rope:
        q, k = _rope(q, rope_d), _rope(k, rope_d)
    q = q * (SWA_HEAD_DIM ** -0.5)
    lo = lax.broadcasted_iota(jnp.int32, (q.shape[0], LANES), 1) < SWA_HEAD_DIM
    for pair in range(2):
        for grp in range(SWA_GROUP):
            ca, cb = (2 * pair) * 2 + grp // 2, (2 * pair + 1) * 2 + grp // 2
            a = q[:, ca * LANES:(ca + 1) * LANES]
            bb = q[:, cb * LANES:(cb + 1) * LANES]
            if grp % 2 == 0:
                bb = pltpu.roll(bb, SWA_HEAD_DIM, 1)
            else:
                a = pltpu.roll(a, SWA_HEAD_DIM, 1)
            q_o[pair * SWA_GROUP + grp] = jnp.where(lo, a, bb).astype(BF16)
    for kvh in range(SWA_KV_HEADS):
        sl = slice((kvh // 2) * LANES, (kvh // 2 + 1) * LANES)
        keep = lo if kvh % 2 == 0 else jnp.logical_not(lo)
        k_o[kvh] = jnp.where(keep, k[:, sl], 0.0).astype(BF16)
        v_o[kvh] = jnp.where(keep, v[:, sl], 0.0).astype(BF16)


def _proj_swa(x, mods, row_fn, g, w_in, rope_tab, emit_cache, tm):
    b, s, d = x.shape
    rope = rope_tab is not None
    tok = lambda bi, t: (bi, t, 0)
    head = lambda bi, t: (bi, 0, t, 0)
    in_specs = [
        pl.BlockSpec((None, tm, d), tok),
        pl.BlockSpec((1, 6, d), lambda bi, t: (row_fn(bi), 0, 0)),
        _full((1, d)), _full((d, 1536)),
    ]
    args = [x, mods, g, w_in]
    if rope:
        in_specs.append(pl.BlockSpec((3, tm, LANES), lambda bi, t: (0, t, 0)))
        args.append(rope_tab)

    def hm(nh):
        return jax.ShapeDtypeStruct((b, nh, s, LANES), BF16), pl.BlockSpec((None, nh, tm, LANES), head)

    outs = [hm(8), hm(4), hm(4)]
    if emit_cache:
        for _ in range(2):
            outs.append((jax.ShapeDtypeStruct((b, s, 256), F32), pl.BlockSpec((None, tm, 256), tok)))
    return pl.pallas_call(
        functools.partial(_proj_swa_kernel, rope, emit_cache),
        grid=(b, s // tm),
        in_specs=in_specs,
        out_specs=[o[1] for o in outs],
        out_shape=[o[0] for o in outs],
        compiler_params=_params(2),
        name="proj_swa",
    )(*args)


def _attn_swa_kernel(windowed, band_w, x_ref, mod_ref, q_ref, k_ref, v_ref, *rest):
    if windowed:
        ck_ref, cv_ref = rest[:2]
        rest = rest[2:]
    sink_ref, w_ref, o_ref = rest[:3]
    bufs = rest[3:]
    tq, d = x_ref.shape
    n_keys = k_ref.shape[1]
    lo_k = lax.broadcasted_iota(jnp.int32, (1, LANES), 1) < SWA_HEAD_DIM
    if windowed:
        start = pl.program_id(1) * tq
        bstart = pl.multiple_of(jnp.clip(start - WINDOW, 0, n_keys - band_w), LANES)
        qpos = start + lax.broadcasted_iota(jnp.int32, (tq, band_w), 0)
        kpos = bstart + lax.broadcasted_iota(jnp.int32, (tq, band_w), 1)
        bias = jnp.where(jnp.abs(qpos - kpos) <= WINDOW, 0.0, NEG_INF)

    def keys_values(kvh):
        if not windowed:
            return [k_ref[kvh]], [v_ref[kvh]]
        keep = lo_k if kvh % 2 == 0 else jnp.logical_not(lo_k)
        sl = slice((kvh // 2) * LANES, (kvh // 2 + 1) * LANES)
        kc = jnp.where(keep, ck_ref[:, sl], 0.0).astype(BF16)
        vc = jnp.where(keep, cv_ref[:, sl], 0.0).astype(BF16)
        return ([k_ref[kvh, pl.ds(bstart, band_w), :], kc], [v_ref[kvh, pl.ds(bstart, band_w), :], vc])

    units = [(kvh, g) for kvh in range(SWA_KV_HEADS) for g in range(SWA_GROUP)]

    def scores(u, buf):
        kvh, g = units[u]
        q = q_ref[(kvh // 2) * SWA_GROUP + g]
        sink = sink_ref[kvh * SWA_GROUP + g]
        folds, off = [], 0
        for i, k in enumerate(keys_values(kvh)[0]):
            s = _dot_nt(q, k)
            if windowed and i == 0:
                s = s + bias
            buf[:, off:off + s.shape[1]] = s
            folds.append(_lane_fold(jnp.maximum, s))
            off += s.shape[1]
        m = jnp.maximum(jnp.max(_tree(jnp.maximum, folds), axis=-1, keepdims=True), sink)
        return m, sink

    def consume(u, buf, m, sink):
        vs = keys_values(units[u][0])[1]
        e, off = [], 0
        for v in vs:
            e.append(jnp.exp(buf[:, off:off + v.shape[0]] - m))
            off += v.shape[0]
        l = jnp.sum(_tree(jnp.add, [_lane_fold(jnp.add, x) for x in e]), axis=-1, keepdims=True) + jnp.exp(sink - m)
        return _tree(jnp.add, [_dot(x.astype(BF16), v) for x, v in zip(e, vs)]) / l

    acc = jnp.zeros((tq, d), F32)
    nxt = scores(0, bufs[0])
    prev = None
    for u, (kvh, g) in enumerate(units):
        cur = nxt
        if u + 1 < len(units):
            nxt = scores(u + 1, bufs[(u + 1) % len(bufs)])
        o = consume(u, bufs[u % len(bufs)], *cur)
        if g % 2 == 0:
            prev = o
            continue
        if kvh % 2 == 0:
            slab = prev + pltpu.roll(o, SWA_HEAD_DIM, 1)
        else:
            slab = pltpu.roll(prev, SWA_HEAD_DIM, 1) + o
        acc = acc + _dot(slab.astype(BF16), w_ref[kvh * 2 + g // 2])
    o_ref[...] = x_ref[...] + mod_ref[0][2:3] * acc


def _attn_swa(x, mods, row_fn, q, k, v, ctx, sink, w_out, tq, n_bufs):
    b, s, d = x.shape
    windowed = ctx is not None
    band_w = min(tq + 2 * WINDOW, s)
    kv_spec = pl.BlockSpec((None, SWA_KV_HEADS, s, LANES), lambda bi, t: (bi, 0, 0, 0))
    in_specs = [
        pl.BlockSpec((None, tq, d), lambda bi, t: (bi, t, 0)),
        pl.BlockSpec((1, 6, d), lambda bi, t: (row_fn(bi), 0, 0)),
        pl.BlockSpec((None, 8, tq, LANES), lambda bi, t: (bi, 0, t, 0)),
        kv_spec, kv_spec,
    ]
    args = [x, mods, q, k, v]
    if windowed:
        ck, cv, j = ctx
        p = ck.shape[2]
        spec = pl.BlockSpec((None, None, p, 256), lambda bi, t: (bi, j, 0, 0))
        in_specs += [spec, spec]
        args += [ck, cv]
    in_specs += [pl.BlockSpec(memory_space=pltpu.SMEM), _full((8, LANES, d))]
    args += [sink, w_out]
    n_keys = band_w + ctx[0].shape[2] if windowed else s
    return pl.pallas_call(
        functools.partial(_attn_swa_kernel, windowed, band_w),
        grid=(b, s // tq),
        in_specs=in_specs,
        out_specs=pl.BlockSpec((None, tq, d), lambda bi, t: (bi, t, 0)),
        out_shape=jax.ShapeDtypeStruct((b, s, d), F32),
        scratch_shapes=[pltpu.VMEM((tq, n_keys), F32) for _ in range(n_bufs)],
        compiler_params=_params(2),
        name="attn_swa",
    )(*args)


def _route(scores_t, bias):
    sel_t = scores_t + bias
    sel = [sel_t[e:e + 1] for e in range(N_EXPERTS)]
    raw = [scores_t[e:e + 1] for e in range(N_EXPERTS)]
    gscore = []
    for g in range(N_GROUPS):
        r = sel[g * 4:g * 4 + 4]
        pairs = [r[i] + r[j] for i in range(4) for j in range(i + 1, 4)]
        gscore.append(functools.reduce(jnp.maximum, pairs))
    best, gidx = gscore[0], jnp.zeros_like(gscore[0], dtype=jnp.int32)
    for g in range(1, N_GROUPS):
        take = gscore[g] > best
        best = jnp.where(take, gscore[g], best)
        gidx = jnp.where(take, g, gidx)
    vals = []
    for k in range(EXPERTS_PER_GROUP):
        v = sel[k]
        for g in range(1, N_GROUPS):
            v = jnp.where(gidx == g, sel[g * 4 + k], v)
        vals.append(v)

    def argmax4(vs):
        m, idx = vs[0], jnp.zeros_like(gidx)
        for k in range(1, 4):
            take = vs[k] > m
            m = jnp.where(take, vs[k], m)
            idx = jnp.where(take, k, idx)
        return idx

    i1 = argmax4(vals)
    i2 = argmax4([jnp.where(i1 == k, -jnp.inf, vals[k]) for k in range(4)])
    e1, e2 = gidx * 4 + i1, gidx * 4 + i2
    w1 = functools.reduce(lambda a, b: a + b, [jnp.where(e1 == e, raw[e], 0.0) for e in range(N_EXPERTS)])
    w2 = functools.reduce(lambda a, b: a + b, [jnp.where(e2 == e, raw[e], 0.0) for e in range(N_EXPERTS)])
    den = w1 + w2
    g1, g2 = w1 / den, w2 / den
    rows = [jnp.where(e1 == e, g1, 0.0) + jnp.where(e2 == e, g2, 0.0) for e in range(N_EXPERTS)]
    return jnp.concatenate(rows, axis=0)


def _moe_kernel(final_norm, x_ref, mod_ref, g_ref, rw_ref, rb_ref, wgu_ref, wd_ref, *rest):
    if final_norm:
        fg_ref, o_ref, gu0_ref, gu1_ref = rest
    else:
        o_ref, gu0_ref, gu1_ref = rest
    tm, d = x_ref.shape
    mod = mod_ref[0]
    x = x_ref[...]
    h = _modulate(x, g_ref[...], mod[3:4], mod[4:5])
    h16 = h.astype(BF16)
    h_lo = (h - h16.astype(F32)).astype(BF16)
    part = _dot_nt(rw_ref[...], h16)
    logits_t = part[:N_EXPERTS] + part[N_EXPERTS:] + _dot_nt(rw_ref[:N_EXPERTS], h_lo)
    gates_t = _route(_sigmoid(logits_t), rb_ref[...])
    gates = jnp.concatenate([gates_t, jnp.zeros((LANES - N_EXPERTS, tm), F32)], axis=0).T
    lane = lax.broadcasted_iota(jnp.int32, (tm, LANES), 1)

    def gate_up(e, buf):
        buf[...] = _dot(h16, wgu_ref[e])

    def down(e, buf, acc):
        gate = jnp.sum(jnp.where(lane == e, gates, 0.0), axis=-1, keepdims=True)
        g, u = buf[:, :D_EXPERT], buf[:, D_EXPERT:]
        a = g * _sigmoid(g) * u * gate
        return acc + _dot(a.astype(BF16), wd_ref[e])

    bufs = (gu0_ref, gu1_ref)
    acc = jnp.zeros((tm, d), F32)
    gate_up(0, bufs[0])
    for e in range(N_EXPERTS):
        if e + 1 < N_EXPERTS:
            gate_up(e + 1, bufs[(e + 1) % 2])
        acc = down(e, bufs[e % 2], acc)
    y = x + mod[5:6] * acc
    if final_norm:
        y = _rms(y, fg_ref[...])
    o_ref[...] = y


def _moe(x, mods, row_fn, g, rw_t, rb, wgu, wd, final_g, tm):
    b, s, d = x.shape
    final_norm = final_g is not None
    in_specs = [
        pl.BlockSpec((None, tm, d), lambda bi, t: (bi, t, 0)),
        pl.BlockSpec((1, 6, d), lambda bi, t: (row_fn(bi), 0, 0)),
        _full((1, d)), _full((2 * N_EXPERTS, d)), _full((N_EXPERTS, 1)),
        _resident((N_EXPERTS, d, 2 * D_EXPERT)), _resident((N_EXPERTS, D_EXPERT, d)),
    ]
    args = [x, mods, g, rw_t, rb, wgu, wd]
    if final_norm:
        in_specs.append(_full((1, d)))
        args.append(final_g)
    return pl.pallas_call(
        functools.partial(_moe_kernel, final_norm),
        grid=(b, s // tm),
        in_specs=in_specs,
        out_specs=pl.BlockSpec((None, tm, d), lambda bi, t: (bi, t, 0)),
        out_shape=jax.ShapeDtypeStruct((b, s, d), F32),
        scratch_shapes=[pltpu.VMEM((tm, 2 * D_EXPERT), F32) for _ in range(2)],
        compiler_params=_params(2),
        name="moe",
    )(*args)


def _rope_tables(n_tokens, rot_dim, offset):
    n_rows = n_tokens // GRID_W
    rows = np.repeat(np.arange(n_rows), GRID_W)
    cols = np.tile(np.arange(GRID_W), n_rows)
    n_freq = rot_dim // 4
    inv = jnp.asarray(ROPE_THETA, F32) ** (-jnp.arange(n_freq, dtype=F32) / n_freq)
    ang = jnp.concatenate([jnp.asarray(rows, F32)[:, None] * inv, jnp.asarray(cols, F32)[:, None] * inv], axis=-1)
    cos, sin = jnp.repeat(jnp.cos(ang), 2, axis=-1), jnp.repeat(jnp.sin(ang), 2, axis=-1)
    odd = jnp.asarray(np.arange(rot_dim) % 2 == 1)
    parts = [cos, jnp.where(odd, sin, 0.0), jnp.where(odd, 0.0, -sin)]
    period = 64 if rot_dim == 64 else LANES
    fill = [1.0, 0.0, 0.0]
    out = []
    for p, f in zip(parts, fill):
        slot = jnp.full((n_tokens, period), f, F32).at[:, offset:offset + rot_dim].set(p)
        out.append(jnp.tile(slot, (1, LANES // period)))
    return jnp.stack(out)


def _pad_heads(w, n_heads, lo, hi):
    k = w.shape[0]
    w = w.reshape(k, n_heads, -1)[:, :, lo:hi]
    return jnp.pad(w, ((0, 0), (0, 0), (0, LANES - (hi - lo)))).reshape(k, n_heads * LANES)


def _ab_weights(j, ab_w_in, diff_lambda, diff_subln_g, mla_q_norm_g, mla_w_qb, mla_kv_norm_g, mla_w_kvb, ab_w_out):
    place = np.zeros((LANES, MLA_HEADS * LANES), np.float32)
    for hd in range(MLA_HEADS):
        for r in range(MLA_ROPE_DIM):
            place[r, hd * LANES + MLA_NOPE_DIM + r] = 1.0
    w_out_m = ab_w_out[j][512:].reshape(MLA_HEADS, MLA_V_DIM, D_MODEL)
    return {
        "w_in": jnp.pad(ab_w_in[j], ((0, 0), (0, AB_IN_PAD - AB_IN))).astype(BF16),
        "qn_g": mla_q_norm_g[j][None, :],
        "w_qb": _pad_heads(mla_w_qb[j], MLA_HEADS, 0, MLA_NOPE_DIM + MLA_ROPE_DIM).astype(BF16),
        "kvn_g": mla_kv_norm_g[j][None, :],
        "wk": _pad_heads(mla_w_kvb[j], MLA_HEADS, 0, MLA_NOPE_DIM).astype(BF16),
        "wv": _pad_heads(mla_w_kvb[j], MLA_HEADS, MLA_NOPE_DIM, MLA_NOPE_DIM + MLA_V_DIM).astype(BF16),
        "e": jnp.asarray(place, BF16),
        "w_out_d": ab_w_out[j][:512].reshape(DIFF_HEADS, LANES, D_MODEL).astype(BF16),
        "w_out_m": jnp.pad(w_out_m, ((0, 0), (0, LANES - MLA_V_DIM), (0, 0))).astype(BF16),
        "subln_g": diff_subln_g[j][None, :],
        "lam": diff_lambda[j],
    }


def kernel(x_prompt, x_sample, cache_diff_k, cache_diff_v, cache_mla_ckv, cache_mla_krope, cache_swa_k, cache_swa_v,
           c, c_ctx, ada_w, ada_b, norm1_g, norm2_g, final_norm_g, ab_w_in, diff_lambda, diff_subln_g, mla_q_norm_g,
           mla_w_qb, mla_kv_norm_g, mla_w_kvb, ab_w_out, swa_w_in, swa_sink, swa_w_out, router_w, router_bias,
           moe_w_gate, moe_w_up, moe_w_down):
    bp, sp, d = x_prompt.shape
    bs, ss, _ = x_sample.shape
    depth = ada_w.shape[0]
    past = cache_diff_k.shape[2]
    n_ab = cache_diff_k.shape[1]
    n_swa = cache_swa_k.shape[1]
    assert bs + 1 <= COND_ROWS and d == D_MODEL

    cond = jnp.zeros((COND_ROWS, d), F32).at[:bs].set(c).at[bs].set(c_ctx)
    mods = _ada_all(cond, ada_w, ada_b).reshape(depth, COND_ROWS, 6, d)
    row_s = lambda bi: bi
    row_p = lambda bi: bi * 0 + bs

    rope_d = _rope_tables(ss, DIFF_QK_DIM, 0)
    rope_q = _rope_tables(ss, MLA_ROPE_DIM, MLA_NOPE_DIM)
    rope_k = _rope_tables(ss, MLA_ROPE_DIM, 0)

    cdk = cache_diff_k.reshape(bs, n_ab, past, 512)
    cdv = cache_diff_v.reshape(bs, n_ab, past, 512)
    ckr = jnp.pad(cache_mla_krope, ((0, 0), (0, 0), (0, 0), (0, LANES - MLA_ROPE_DIM)))
    csk = cache_swa_k.reshape(bs, n_swa, past, 256)
    csv = cache_swa_v.reshape(bs, n_swa, past, 256)
    rw_hi = router_w.T.astype(BF16)
    rw_t = jnp.concatenate([rw_hi, (router_w.T - rw_hi.astype(F32)).astype(BF16)], axis=0)
    rb = router_bias[:, None]

    tp = min(256, sp)
    ts = min(256, ss)
    tm_s = min(512, ss)
    tm_p = min(512, bp * sp)

    xp, xs = x_prompt, x_sample
    new = {k: [] for k in ("dk", "dv", "ckv", "kr", "sk", "sv")}
    for l in range(depth):
        j = l // 2
        ml = mods[l]
        g1, g2 = norm1_g[l][None, :], norm2_g[l][None, :]
        if l % 2 == 0:
            lambda_init = 0.8 - 0.6 * math.exp(-0.3 * l)
            wts = _ab_weights(j, ab_w_in, diff_lambda, diff_subln_g, mla_q_norm_g, mla_w_qb, mla_kv_norm_g,
                              mla_w_kvb, ab_w_out)
            qa, ka, va, qb, km, vm, ka32, va32, ckv32, kr32 = _proj_ab(xp, ml, row_p, g1, wts, None, True, tp)
            new["dk"].append(ka32)
            new["dv"].append(va32)
            new["ckv"].append(ckv32)
            new["kr"].append(kr32)
            xp = _attn_ab(xp, ml, row_p, (qa, qb), [(ka, va, km, vm)], wts, lambda_init, tp, DIFF_HEADS + MLA_HEADS)
            qa, ka, va, qb, km, vm = _proj_ab(xs, ml, row_s, g1, wts, (rope_d, rope_q, rope_k), False, ts)
            cache_seg = _cache_ab(cdk, cdv, cache_mla_ckv, ckr, j, wts)
            xs = _attn_ab(xs, ml, row_s, (qa, qb), [(ka, va, km, vm), tuple(cache_seg)], wts, lambda_init, ts, 2)
        else:
            w_in = swa_w_in[j].astype(BF16)
            w_out = swa_w_out[j].reshape(8, LANES, d).astype(BF16)
            q, k, v, k32, v32 = _proj_swa(xp, ml, row_p, g1, w_in, None, True, tp)
            new["sk"].append(k32)
            new["sv"].append(v32)
            xp = _attn_swa(xp, ml, row_p, q, k, v, None, swa_sink[j], w_out, tp, SWA_HEADS)
            q, k, v = _proj_swa(xs, ml, row_s, g1, w_in, rope_d, False, ts)
            xs = _attn_swa(xs, ml, row_s, q, k, v, (csk, csv, j), swa_sink[j], w_out, ts, 2)
        wgu = jnp.concatenate([moe_w_gate[l], moe_w_up[l]], axis=-1).astype(BF16)
        wd = moe_w_down[l].astype(BF16)
        fg = final_norm_g[None, :] if l == depth - 1 else None
        xp = _moe(xp.reshape(1, bp * sp, d), ml, row_p, g2, rw_t, rb, wgu, wd, fg, tm_p).reshape(bp, sp, d)
        xs = _moe(xs, ml, row_s, g2, rw_t, rb, wgu, wd, fg, tm_s)

    new_diff_k = jnp.stack(new["dk"], axis=1).reshape(bp, n_ab, sp, DIFF_HEADS, 2, DIFF_QK_DIM)
    new_diff_v = jnp.stack(new["dv"], axis=1).reshape(bp, n_ab, sp, DIFF_HEADS, DIFF_V_DIM)
    new_mla_ckv = jnp.stack(new["ckv"], axis=1)
    new_mla_krope = jnp.stack(new["kr"], axis=1)
    new_swa_k = jnp.stack(new["sk"], axis=1).reshape(bp, n_swa, sp, SWA_KV_HEADS, SWA_HEAD_DIM)
    new_swa_v = jnp.stack(new["sv"], axis=1).reshape(bp, n_swa, sp, SWA_KV_HEADS, SWA_HEAD_DIM)
    return (xp, xs, new_diff_k, new_diff_v, new_mla_ckv, new_mla_krope, new_swa_k, new_swa_v)
```

```python
import functools
import math

import jax
import jax.numpy as jnp
import numpy as np
from jax import lax
from jax.experimental import pallas as pl
from jax.experimental.pallas import tpu as pltpu

F32 = jnp.float32
BF16 = jnp.bfloat16

D_MODEL = 1024
GRID_W = 64
ROPE_THETA = 10000.0
NORM_EPS = 1e-6
NEG_INF = -1e30
LOG2E = math.log2(math.e)
LANES = 128

DIFF_HEADS = 4
DIFF_QK_DIM = 64
DIFF_V_DIM = 128
MLA_HEADS = 8
MLA_Q_RANK = 384
MLA_KV_RANK = 256
MLA_NOPE_DIM = 64
MLA_ROPE_DIM = 32
MLA_V_DIM = 64
AB_IN = 3 * 512 + MLA_Q_RANK + MLA_KV_RANK + MLA_ROPE_DIM
AB_IN_PAD = 2304
SWA_HEADS = 16
SWA_KV_HEADS = 4
SWA_GROUP = 4
SWA_HEAD_DIM = 64
WINDOW = 128
N_EXPERTS = 16
N_GROUPS = 4
EXPERTS_PER_GROUP = 4
D_EXPERT = 256
COND_ROWS = 16

VMEM_LIMIT = 56 * 1024 * 1024


def _full(shape):
    n = len(shape)
    return pl.BlockSpec(shape, lambda *_: (0,) * n)


def _resident(shape):
    n = len(shape)
    return pl.BlockSpec(shape, lambda *_: (0,) * n, pipeline_mode=pl.Buffered(1))


def _params(n_axes):
    return pltpu.CompilerParams(dimension_semantics=("arbitrary",) * n_axes, vmem_limit_bytes=VMEM_LIMIT)


def _sigmoid(x):
    return 1.0 / (1.0 + jnp.exp(-x))


def _rms(x, g):
    return x * lax.rsqrt(jnp.mean(x * x, axis=-1, keepdims=True) + NORM_EPS) * g


def _modulate(x, g, shift, scale):
    return _rms(x, g) * (1.0 + scale) + shift


def _dot(a, b):
    return jnp.dot(a, b, preferred_element_type=F32)


def _dot_nt(a, b):
    return lax.dot_general(a, b, (((1,), (1,)), ((), ())), preferred_element_type=F32)


def _rope(x, tab_ref):
    c, s_odd, s_even = tab_ref[0], tab_ref[1], tab_ref[2]
    out = []
    for i in range(x.shape[1] // LANES):
        xi = x[:, i * LANES:(i + 1) * LANES]
        out.append(xi * c + pltpu.roll(xi, 1, 1) * s_odd + pltpu.roll(xi, LANES - 1, 1) * s_even)
    return out[0] if len(out) == 1 else jnp.concatenate(out, axis=1)


def _tree(op, xs):
    xs = list(xs)
    while len(xs) > 1:
        xs = [op(xs[i], xs[i + 1]) if i + 1 < len(xs) else xs[i] for i in range(0, len(xs), 2)]
    return xs[0]


def _lane_fold(op, x):
    return _tree(op, [x[:, i * LANES:(i + 1) * LANES] for i in range(x.shape[1] // LANES)])


def _ada_kernel(cond_ref, w_ref, b_ref, o_ref):
    c = cond_ref[...]
    a = (c * _sigmoid(c)).astype(BF16)
    o_ref[...] = _dot(a, w_ref[...].astype(BF16)) + b_ref[...]


def _ada_all(cond, ada_w, ada_b):
    depth, d, n = ada_w.shape
    tn = 1536
    return pl.pallas_call(
        _ada_kernel,
        grid=(depth, n // tn),
        in_specs=[
            _full((COND_ROWS, d)),
            pl.BlockSpec((None, d, tn), lambda l, j: (l, 0, j)),
            pl.BlockSpec((None, 1, tn), lambda l, j: (l, 0, j)),
        ],
        out_specs=pl.BlockSpec((None, COND_ROWS, tn), lambda l, j: (l, 0, j)),
        out_shape=jax.ShapeDtypeStruct((depth, COND_ROWS, n), F32),
        compiler_params=_params(2),
        name="ada_mod",
    )(cond, ada_w, ada_b.reshape(depth, 1, n))


def _proj_ab_kernel(rope, emit_cache, n_prev, x_ref, mod_ref, g_ref, w_in_ref, qn_ref, wqb_ref, kvn_ref, wk_ref,
                    wv_ref, e_ref, *rest):
    if rope:
        rope_d, rope_q, rope_k = rest[:3]
        rest = rest[3:]
    rest = rest[n_prev:]
    qa_o, ka_o, va_o, qb_o, km_o, vm_o = rest[:6]
    mod = mod_ref[0]
    h = _modulate(x_ref[...], g_ref[...], mod[0:1], mod[1:2]).astype(BF16)
    big = _dot(h, w_in_ref[...])
    qa, ka, va = big[:, 0:512], big[:, 512:1024], big[:, 1024:1536]
    q_lat, ckv, kr = big[:, 1536:1920], big[:, 1920:2176], big[:, 2176:2304]
    qb = _dot(_rms(q_lat, qn_ref[...]).astype(BF16), wqb_ref[...])
    ckv_n = _rms(ckv, kvn_ref[...])
    if emit_cache:
        ka32_o, va32_o, ckv32_o, kr32_o = rest[6:10]
        ka32_o[...] = ka
        va32_o[...] = va
        ckv32_o[...] = ckv_n
        kr32_o[...] = kr[:, :MLA_ROPE_DIM]
    if rope:
        qa, ka = _rope(qa, rope_d), _rope(ka, rope_d)
        qb, kr = _rope(qb, rope_q), _rope(kr, rope_k)
    qa = qa * (DIFF_QK_DIM ** -0.5 * LOG2E)
    qb = qb * ((MLA_NOPE_DIM + MLA_ROPE_DIM) ** -0.5 * LOG2E)
    ckv16 = ckv_n.astype(BF16)
    km = _dot(ckv16, wk_ref[...]) + _dot(kr.astype(BF16), e_ref[...])
    vm = _dot(ckv16, wv_ref[...])
    lo = lax.broadcasted_iota(jnp.int32, (qa.shape[0], LANES), 1) < DIFF_QK_DIM
    for hd in range(DIFF_HEADS):
        sl = slice(hd * LANES, (hd + 1) * LANES)
        qa_o[2 * hd] = jnp.where(lo, qa[:, sl], 0.0).astype(BF16)
        qa_o[2 * hd + 1] = jnp.where(lo, 0.0, qa[:, sl]).astype(BF16)
        ka_o[hd] = ka[:, sl].astype(BF16)
        va_o[hd] = va[:, sl].astype(BF16)
    for hd in range(MLA_HEADS):
        sl = slice(hd * LANES, (hd + 1) * LANES)
        qb_o[hd] = qb[:, sl].astype(BF16)
        km_o[hd] = km[:, sl].astype(BF16)
        vm_o[hd] = vm[:, sl].astype(BF16)


def _cache_outputs(cache, widths, b, s, tm, in_specs, args, n_fixed_outs):
    if cache is None:
        return [], {}, 0
    j, n_layers, prev = cache
    outs = [(jax.ShapeDtypeStruct((b, n_layers, s, w), F32),
             pl.BlockSpec((None, None, tm, w), lambda bi, t: (bi, j, t, 0))) for w in widths]
    aliases = {}
    if prev is not None:
        for i, arr in enumerate(prev):
            aliases[len(args)] = n_fixed_outs + i
            in_specs.append(pl.BlockSpec(memory_space=pl.ANY))
            args.append(arr)
    return outs, aliases, len(aliases)


def _proj_ab(x, mods, row_fn, g, wts, rope_tabs, cache, tm):
    b, s, d = x.shape
    rope = rope_tabs is not None
    tok = lambda bi, t: (bi, t, 0)
    head = lambda bi, t: (bi, 0, t, 0)
    in_specs = [
        pl.BlockSpec((None, tm, d), tok),
        pl.BlockSpec((1, 6, d), lambda bi, t: (row_fn(bi), 0, 0)),
        _full((1, d)),
        _full((d, AB_IN_PAD)), _full((1, MLA_Q_RANK)), _full((MLA_Q_RANK, 1024)), _full((1, MLA_KV_RANK)),
        _full((MLA_KV_RANK, 1024)), _full((MLA_KV_RANK, 1024)), _full((LANES, 1024)),
    ]
    args = [x, mods, g, wts["w_in"], wts["qn_g"], wts["w_qb"], wts["kvn_g"], wts["wk"], wts["wv"], wts["e"]]
    if rope:
        in_specs += [pl.BlockSpec((3, tm, LANES), lambda bi, t: (0, t, 0))] * 3
        args += list(rope_tabs)

    def hm(nh):
        return jax.ShapeDtypeStruct((b, nh, s, LANES), BF16), pl.BlockSpec((None, nh, tm, LANES), head)

    outs = [hm(8), hm(4), hm(4), hm(8), hm(8), hm(8)]
    cache_outs, aliases, n_prev = _cache_outputs(cache, (512, 512, MLA_KV_RANK, MLA_ROPE_DIM), b, s, tm, in_specs,
                                                 args, len(outs))
    outs += cache_outs
    return pl.pallas_call(
        functools.partial(_proj_ab_kernel, rope, cache is not None, n_prev),
        grid=(b, s // tm),
        in_specs=in_specs,
        out_specs=[o[1] for o in outs],
        out_shape=[o[0] for o in outs],
        input_output_aliases=aliases,
        compiler_params=_params(2),
        name="proj_ab",
    )(*args)


def _cache_ab_kernel(dk_ref, dv_ref, ckv_ref, kr_ref, wk_ref, wv_ref, e_ref, ck_o, cv_o, km_o, vm_o):
    ckv16 = ckv_ref[...].astype(BF16)
    km = _dot(ckv16, wk_ref[...]) + _dot(kr_ref[...].astype(BF16), e_ref[...])
    vm = _dot(ckv16, wv_ref[...])
    for hd in range(DIFF_HEADS):
        sl = slice(hd * LANES, (hd + 1) * LANES)
        ck_o[hd] = dk_ref[:, sl].astype(BF16)
        cv_o[hd] = dv_ref[:, sl].astype(BF16)
    for hd in range(MLA_HEADS):
        sl = slice(hd * LANES, (hd + 1) * LANES)
        km_o[hd] = km[:, sl].astype(BF16)
        vm_o[hd] = vm[:, sl].astype(BF16)


def _cache_ab(cdk, cdv, cckv, ckr, j, wts):
    b, _, p, _ = cdk.shape
    lay = lambda bi: (bi, j, 0, 0)

    def hm(nh):
        return (jax.ShapeDtypeStruct((b, nh, p, LANES), BF16),
                pl.BlockSpec((None, nh, p, LANES), lambda bi: (bi, 0, 0, 0)))

    outs = [hm(4), hm(4), hm(8), hm(8)]
    return pl.pallas_call(
        _cache_ab_kernel,
        grid=(b,),
        in_specs=[
            pl.BlockSpec((None, None, p, 512), lay), pl.BlockSpec((None, None, p, 512), lay),
            pl.BlockSpec((None, None, p, MLA_KV_RANK), lay), pl.BlockSpec((None, None, p, LANES), lay),
            _full((MLA_KV_RANK, 1024)), _full((MLA_KV_RANK, 1024)), _full((LANES, 1024)),
        ],
        out_specs=[o[1] for o in outs],
        out_shape=[o[0] for o in outs],
        compiler_params=_params(1),
        name="cache_ab",
    )(cdk, cdv, cckv, ckr, wts["wk"], wts["wv"], wts["e"])


def _attn_ab_kernel(n_seg, lambda_init, x_ref, mod_ref, qa_ref, qb_ref, *rest):
    segs = [rest[4 * i:4 * i + 4] for i in range(n_seg)]
    wd_ref, wm_ref, sg_ref, lam_ref, o_ref = rest[4 * n_seg:4 * n_seg + 5]
    bufs = rest[4 * n_seg + 5:]
    lp = lam_ref[...]
    lam = (jnp.exp(jnp.sum(lp[0:1] * lp[1:2], axis=-1, keepdims=True))
           - jnp.exp(jnp.sum(lp[2:3] * lp[3:4], axis=-1, keepdims=True)) + lambda_init)
    tq, d = x_ref.shape
    widths = [sg[0].shape[1] for sg in segs]
    offs = [sum(widths[:i]) for i in range(n_seg)]
    units = [("diff", hd) for hd in range(DIFF_HEADS)] + [("mla", hd) for hd in range(MLA_HEADS)]

    def scores(u, buf):
        kind, hd = units[u]
        qs = [qa_ref[2 * hd], qa_ref[2 * hd + 1]] if kind == "diff" else [qb_ref[hd]]
        ks = [sg[0 if kind == "diff" else 2][hd] for sg in segs]
        ms = []
        for i, q in enumerate(qs):
            folds = []
            for k, off, w in zip(ks, offs, widths):
                s = _dot_nt(q, k)
                buf[i, :, off:off + w] = s
                folds.append(_lane_fold(jnp.maximum, s))
            ms.append(jnp.max(_tree(jnp.maximum, folds), axis=-1, keepdims=True))
        return ms

    def exps(buf, i, m):
        e = [jnp.exp2(buf[i, :, off:off + w] - m) for off, w in zip(offs, widths)]
        l = jnp.sum(_tree(jnp.add, [_lane_fold(jnp.add, x) for x in e]), axis=-1, keepdims=True)
        return e, l

    def consume(u, buf, ms, acc):
        kind, hd = units[u]
        if kind == "diff":
            vs = [sg[1][hd] for sg in segs]
            (e1, l1), (e2, l2) = exps(buf, 0, ms[0]), exps(buf, 1, ms[1])
            c1, c2 = 1.0 / l1, lam / l2
            o = _tree(jnp.add, [_dot((a * c1 - b * c2).astype(BF16), v) for a, b, v in zip(e1, e2, vs)])
            od = _rms(o, sg_ref[...]) * (1.0 - lambda_init)
            return acc + _dot(od.astype(BF16), wd_ref[hd])
        vs = [sg[3][hd] for sg in segs]
        e, l = exps(buf, 0, ms[0])
        o = _tree(jnp.add, [_dot(x.astype(BF16), v) for x, v in zip(e, vs)]) / l
        return acc + _dot(o.astype(BF16), wm_ref[hd])

    acc = jnp.zeros((tq, d), F32)
    ms = scores(0, bufs[0])
    for u in range(len(units)):
        nxt = scores(u + 1, bufs[(u + 1) % len(bufs)]) if u + 1 < len(units) else None
        acc = consume(u, bufs[u % len(bufs)], ms, acc)
        ms = nxt
    o_ref[...] = x_ref[...] + mod_ref[0][2:3] * acc


def _attn_ab(x, mods, row_fn, q_parts, seg_list, wts, lambda_init, tq, n_bufs):
    b, s, d = x.shape
    qa, qb = q_parts
    in_specs = [
        pl.BlockSpec((None, tq, d), lambda bi, t: (bi, t, 0)),
        pl.BlockSpec((1, 6, d), lambda bi, t: (row_fn(bi), 0, 0)),
        pl.BlockSpec((None, 8, tq, LANES), lambda bi, t: (bi, 0, t, 0)),
        pl.BlockSpec((None, 8, tq, LANES), lambda bi, t: (bi, 0, t, 0)),
    ]
    args = [x, mods, qa, qb]
    kv_mode = dict(pipeline_mode=pl.Buffered(1)) if s // tq > 1 else {}
    for seg in seg_list:
        for arr in seg:
            nh, nk = arr.shape[1], arr.shape[2]
            in_specs.append(pl.BlockSpec((None, nh, nk, LANES), lambda bi, t: (bi, 0, 0, 0), **kv_mode))
            args.append(arr)
    in_specs += [_full((DIFF_HEADS, LANES, d)), _full((MLA_HEADS, LANES, d)), _full((1, LANES)),
                 _full((4, DIFF_QK_DIM))]
    args += [wts["w_out_d"], wts["w_out_m"], wts["subln_g"], wts["lam"]]
    n_keys = sum(seg[0].shape[2] for seg in seg_list)
    return pl.pallas_call(
        functools.partial(_attn_ab_kernel, len(seg_list), lambda_init),
        grid=(b, s // tq),
        in_specs=in_specs,
        out_specs=pl.BlockSpec((None, tq, d), lambda bi, t: (bi, t, 0)),
        out_shape=jax.ShapeDtypeStruct((b, s, d), F32),
        scratch_shapes=[pltpu.VMEM((2, tq, n_keys), F32) for _ in range(n_bufs)],
        compiler_params=_params(2),
        name="attn_ab",
    )(*args)


def _proj_swa_kernel(rope, emit_cache, n_prev, x_ref, mod_ref, g_ref, w_in_ref, *rest):
    if rope:
        rope_d = rest[0]
        rest = rest[1:]
    rest = rest[n_prev:]
    q_o, k_o, v_o = rest[:3]
    mod = mod_ref[0]
    h = _modulate(x_ref[...], g_ref[...], mod[0:1], mod[1:2]).astype(BF16)
    big = _dot(h, w_in_ref[...])
    q, k, v = big[:, :1024], big[:, 1024:1280], big[:, 1280:1536]
    if emit_cache:
        k32_o, v32_o = rest[3:5]
        k32_o[...] = k
        v32_o[...] = v
    if rope:
        q, k = _rope(q, rope_d), _rope(k, rope_d)
    q = q * (SWA_HEAD_DIM ** -0.5 * LOG2E)
    lo = lax.broadcasted_iota(jnp.int32, (q.shape[0], LANES), 1) < SWA_HEAD_DIM
    for pair in range(2):
        for grp in range(SWA_GROUP):
            ca, cb = (2 * pair) * 2 + grp // 2, (2 * pair + 1) * 2 + grp // 2
            a = q[:, ca * LANES:(ca + 1) * LANES]
            bb = q[:, cb * LANES:(cb + 1) * LANES]
            if grp % 2 == 0:
                bb = pltpu.roll(bb, SWA_HEAD_DIM, 1)
            else:
                a = pltpu.roll(a, SWA_HEAD_DIM, 1)
            q_o[pair * SWA_GROUP + grp] = jnp.where(lo, a, bb).astype(BF16)
    for kvh in range(SWA_KV_HEADS):
        sl = slice((kvh // 2) * LANES, (kvh // 2 + 1) * LANES)
        keep = lo if kvh % 2 == 0 else jnp.logical_not(lo)
        k_o[kvh] = jnp.where(keep, k[:, sl], 0.0).astype(BF16)
        v_o[kvh] = jnp.where(keep, v[:, sl], 0.0).astype(BF16)


def _proj_swa(x, mods, row_fn, g, w_in, rope_tab, cache, tm):
    b, s, d = x.shape
    rope = rope_tab is not None
    tok = lambda bi, t: (bi, t, 0)
    head = lambda bi, t: (bi, 0, t, 0)
    in_specs = [
        pl.BlockSpec((None, tm, d), tok),
        pl.BlockSpec((1, 6, d), lambda bi, t: (row_fn(bi), 0, 0)),
        _full((1, d)), _full((d, 1536)),
    ]
    args = [x, mods, g, w_in]
    if rope:
        in_specs.append(pl.BlockSpec((3, tm, LANES), lambda bi, t: (0, t, 0)))
        args.append(rope_tab)

    def hm(nh):
        return jax.ShapeDtypeStruct((b, nh, s, LANES), BF16), pl.BlockSpec((None, nh, tm, LANES), head)

    outs = [hm(8), hm(4), hm(4)]
    cache_outs, aliases, n_prev = _cache_outputs(cache, (256, 256), b, s, tm, in_specs, args, len(outs))
    outs += cache_outs
    return pl.pallas_call(
        functools.partial(_proj_swa_kernel, rope, cache is not None, n_prev),
        grid=(b, s // tm),
        in_specs=in_specs,
        out_specs=[o[1] for o in outs],
        out_shape=[o[0] for o in outs],
        input_output_aliases=aliases,
        compiler_params=_params(2),
        name="proj_swa",
    )(*args)


def _attn_swa_kernel(windowed, band_w, x_ref, mod_ref, q_ref, k_ref, v_ref, *rest):
    if windowed:
        ck_ref, cv_ref = rest[:2]
        rest = rest[2:]
    sink_ref, w_ref, o_ref = rest[:3]
    bufs = rest[3:]
    tq, d = x_ref.shape
    n_keys = k_ref.shape[1]
    lo_k = lax.broadcasted_iota(jnp.int32, (1, LANES), 1) < SWA_HEAD_DIM
    if windowed:
        start = pl.program_id(1) * tq
        bstart = pl.multiple_of(jnp.clip(start - WINDOW, 0, n_keys - band_w), LANES)
        qpos = start + lax.broadcasted_iota(jnp.int32, (tq, band_w), 0)
        kpos = bstart + lax.broadcasted_iota(jnp.int32, (tq, band_w), 1)
        bias = jnp.where(jnp.abs(qpos - kpos) <= WINDOW, 0.0, NEG_INF)

    def keys_values(kvh):
        if not windowed:
            return [k_ref[kvh]], [v_ref[kvh]]
        keep = lo_k if kvh % 2 == 0 else jnp.logical_not(lo_k)
        sl = slice((kvh // 2) * LANES, (kvh // 2 + 1) * LANES)
        kc = jnp.where(keep, ck_ref[:, sl], 0.0).astype(BF16)
        vc = jnp.where(keep, cv_ref[:, sl], 0.0).astype(BF16)
        return ([k_ref[kvh, pl.ds(bstart, band_w), :], kc], [v_ref[kvh, pl.ds(bstart, band_w), :], vc])

    units = [(kvh, g) for kvh in range(SWA_KV_HEADS) for g in range(SWA_GROUP)]

    def scores(u, buf):
        kvh, g = units[u]
        q = q_ref[(kvh // 2) * SWA_GROUP + g]
        sink = sink_ref[kvh * SWA_GROUP + g] * LOG2E
        folds, off = [], 0
        for i, k in enumerate(keys_values(kvh)[0]):
            s = _dot_nt(q, k)
            if windowed and i == 0:
                s = s + bias
            buf[:, off:off + s.shape[1]] = s
            folds.append(_lane_fold(jnp.maximum, s))
            off += s.shape[1]
        m = jnp.maximum(jnp.max(_tree(jnp.maximum, folds), axis=-1, keepdims=True), sink)
        return m, sink

    def consume(u, buf, m, sink):
        vs = keys_values(units[u][0])[1]
        e, off = [], 0
        for v in vs:
            e.append(jnp.exp2(buf[:, off:off + v.shape[0]] - m))
            off += v.shape[0]
        l = jnp.sum(_tree(jnp.add, [_lane_fold(jnp.add, x) for x in e]), axis=-1, keepdims=True) + jnp.exp2(sink - m)
        return _tree(jnp.add, [_dot(x.astype(BF16), v) for x, v in zip(e, vs)]) / l

    acc = jnp.zeros((tq, d), F32)
    nxt = scores(0, bufs[0])
    prev = None
    for u, (kvh, g) in enumerate(units):
        cur = nxt
        if u + 1 < len(units):
            nxt = scores(u + 1, bufs[(u + 1) % len(bufs)])
        o = consume(u, bufs[u % len(bufs)], *cur)
        if g % 2 == 0:
            prev = o
            continue
        if kvh % 2 == 0:
            slab = prev + pltpu.roll(o, SWA_HEAD_DIM, 1)
        else:
            slab = pltpu.roll(prev, SWA_HEAD_DIM, 1) + o
        acc = acc + _dot(slab.astype(BF16), w_ref[kvh * 2 + g // 2])
    o_ref[...] = x_ref[...] + mod_ref[0][2:3] * acc


def _attn_swa(x, mods, row_fn, q, k, v, ctx, sink, w_out, tq, n_bufs):
    b, s, d = x.shape
    windowed = ctx is not None
    band_w = min(tq + 2 * WINDOW, s)
    kv_spec = pl.BlockSpec((None, SWA_KV_HEADS, s, LANES), lambda bi, t: (bi, 0, 0, 0))
    in_specs = [
        pl.BlockSpec((None, tq, d), lambda bi, t: (bi, t, 0)),
        pl.BlockSpec((1, 6, d), lambda bi, t: (row_fn(bi), 0, 0)),
        pl.BlockSpec((None, 8, tq, LANES), lambda bi, t: (bi, 0, t, 0)),
        kv_spec, kv_spec,
    ]
    args = [x, mods, q, k, v]
    if windowed:
        ck, cv, j = ctx
        p = ck.shape[2]
        spec = pl.BlockSpec((None, None, p, 256), lambda bi, t: (bi, j, 0, 0))
        in_specs += [spec, spec]
        args += [ck, cv]
    in_specs += [pl.BlockSpec(memory_space=pltpu.SMEM), _full((8, LANES, d))]
    args += [sink, w_out]
    n_keys = band_w + ctx[0].shape[2] if windowed else s
    return pl.pallas_call(
        functools.partial(_attn_swa_kernel, windowed, band_w),
        grid=(b, s // tq),
        in_specs=in_specs,
        out_specs=pl.BlockSpec((None, tq, d), lambda bi, t: (bi, t, 0)),
        out_shape=jax.ShapeDtypeStruct((b, s, d), F32),
        scratch_shapes=[pltpu.VMEM((tq, n_keys), F32) for _ in range(n_bufs)],
        compiler_params=_params(2),
        name="attn_swa",
    )(*args)


def _route(scores_t, bias):
    sel_t = scores_t + bias
    sel = [sel_t[e:e + 1] for e in range(N_EXPERTS)]
    raw = [scores_t[e:e + 1] for e in range(N_EXPERTS)]
    gscore = []
    for g in range(N_GROUPS):
        r = sel[g * 4:g * 4 + 4]
        pairs = [r[i] + r[j] for i in range(4) for j in range(i + 1, 4)]
        gscore.append(functools.reduce(jnp.maximum, pairs))
    best, gidx = gscore[0], jnp.zeros_like(gscore[0], dtype=jnp.int32)
    for g in range(1, N_GROUPS):
        take = gscore[g] > best
        best = jnp.where(take, gscore[g], best)
        gidx = jnp.where(take, g, gidx)
    vals = []
    for k in range(EXPERTS_PER_GROUP):
        v = sel[k]
        for g in range(1, N_GROUPS):
            v = jnp.where(gidx == g, sel[g * 4 + k], v)
        vals.append(v)

    def argmax4(vs):
        m, idx = vs[0], jnp.zeros_like(gidx)
        for k in range(1, 4):
            take = vs[k] > m
            m = jnp.where(take, vs[k], m)
            idx = jnp.where(take, k, idx)
        return idx

    i1 = argmax4(vals)
    i2 = argmax4([jnp.where(i1 == k, -jnp.inf, vals[k]) for k in range(4)])
    e1, e2 = gidx * 4 + i1, gidx * 4 + i2
    w1 = functools.reduce(lambda a, b: a + b, [jnp.where(e1 == e, raw[e], 0.0) for e in range(N_EXPERTS)])
    w2 = functools.reduce(lambda a, b: a + b, [jnp.where(e2 == e, raw[e], 0.0) for e in range(N_EXPERTS)])
    den = w1 + w2
    g1, g2 = w1 / den, w2 / den
    rows = [jnp.where(e1 == e, g1, 0.0) + jnp.where(e2 == e, g2, 0.0) for e in range(N_EXPERTS)]
    return jnp.concatenate(rows, axis=0)


def _moe_kernel(final_norm, x_ref, mod_ref, g_ref, rw_ref, rb_ref, wgu_ref, wd_ref, *rest):
    if final_norm:
        fg_ref, o_ref, gu0_ref, gu1_ref = rest
    else:
        o_ref, gu0_ref, gu1_ref = rest
    tm, d = x_ref.shape
    mod = mod_ref[0]
    x = x_ref[...]
    h = _modulate(x, g_ref[...], mod[3:4], mod[4:5])
    h16 = h.astype(BF16)
    h_lo = (h - h16.astype(F32)).astype(BF16)
    part = _dot_nt(rw_ref[...], h16)
    logits_t = part[:N_EXPERTS] + part[N_EXPERTS:] + _dot_nt(rw_ref[:N_EXPERTS], h_lo)
    gates_t = _route(_sigmoid(logits_t), rb_ref[...])
    gates = jnp.concatenate([gates_t, jnp.zeros((LANES - N_EXPERTS, tm), F32)], axis=0).T
    lane = lax.broadcasted_iota(jnp.int32, (tm, LANES), 1)

    def gate_up(e, buf):
        buf[...] = _dot(h16, wgu_ref[e])

    def down(e, buf, acc):
        gate = jnp.sum(jnp.where(lane == e, gates, 0.0), axis=-1, keepdims=True)
        g, u = buf[:, :D_EXPERT], buf[:, D_EXPERT:]
        a = g * _sigmoid(g) * u * gate
        return acc + _dot(a.astype(BF16), wd_ref[e])

    bufs = (gu0_ref, gu1_ref)
    acc = jnp.zeros((tm, d), F32)
    gate_up(0, bufs[0])
    for e in range(N_EXPERTS):
        if e + 1 < N_EXPERTS:
            gate_up(e + 1, bufs[(e + 1) % 2])
        acc = down(e, bufs[e % 2], acc)
    y = x + mod[5:6] * acc
    if final_norm:
        y = _rms(y, fg_ref[...])
    o_ref[...] = y


def _moe(x, mods, row_fn, g, rw_t, rb, wgu, wd, final_g, tm):
    b, s, d = x.shape
    final_norm = final_g is not None
    in_specs = [
        pl.BlockSpec((None, tm, d), lambda bi, t: (bi, t, 0)),
        pl.BlockSpec((1, 6, d), lambda bi, t: (row_fn(bi), 0, 0)),
        _full((1, d)), _full((2 * N_EXPERTS, d)), _full((N_EXPERTS, 1)),
        _resident((N_EXPERTS, d, 2 * D_EXPERT)), _resident((N_EXPERTS, D_EXPERT, d)),
    ]
    args = [x, mods, g, rw_t, rb, wgu, wd]
    if final_norm:
        in_specs.append(_full((1, d)))
        args.append(final_g)
    return pl.pallas_call(
        functools.partial(_moe_kernel, final_norm),
        grid=(b, s // tm),
        in_specs=in_specs,
        out_specs=pl.BlockSpec((None, tm, d), lambda bi, t: (bi, t, 0)),
        out_shape=jax.ShapeDtypeStruct((b, s, d), F32),
        scratch_shapes=[pltpu.VMEM((tm, 2 * D_EXPERT), F32) for _ in range(2)],
        compiler_params=_params(2),
        name="moe",
    )(*args)


def _rope_tables(n_tokens, rot_dim, offset):
    n_rows = n_tokens // GRID_W
    rows = np.repeat(np.arange(n_rows), GRID_W)
    cols = np.tile(np.arange(GRID_W), n_rows)
    n_freq = rot_dim // 4
    inv = jnp.asarray(ROPE_THETA, F32) ** (-jnp.arange(n_freq, dtype=F32) / n_freq)
    ang = jnp.concatenate([jnp.asarray(rows, F32)[:, None] * inv, jnp.asarray(cols, F32)[:, None] * inv], axis=-1)
    cos, sin = jnp.repeat(jnp.cos(ang), 2, axis=-1), jnp.repeat(jnp.sin(ang), 2, axis=-1)
    odd = jnp.asarray(np.arange(rot_dim) % 2 == 1)
    parts = [cos, jnp.where(odd, sin, 0.0), jnp.where(odd, 0.0, -sin)]
    period = 64 if rot_dim == 64 else LANES
    fill = [1.0, 0.0, 0.0]
    out = []
    for p, f in zip(parts, fill):
        slot = jnp.full((n_tokens, period), f, F32).at[:, offset:offset + rot_dim].set(p)
        out.append(jnp.tile(slot, (1, LANES // period)))
    return jnp.stack(out)


def _pad_heads(w, n_heads, lo, hi):
    k = w.shape[0]
    w = w.reshape(k, n_heads, -1)[:, :, lo:hi]
    return jnp.pad(w, ((0, 0), (0, 0), (0, LANES - (hi - lo)))).reshape(k, n_heads * LANES)


def _ab_weights(j, ab_w_in, diff_lambda, diff_subln_g, mla_q_norm_g, mla_w_qb, mla_kv_norm_g, mla_w_kvb, ab_w_out):
    place = np.zeros((LANES, MLA_HEADS * LANES), np.float32)
    for hd in range(MLA_HEADS):
        for r in range(MLA_ROPE_DIM):
            place[r, hd * LANES + MLA_NOPE_DIM + r] = 1.0
    w_out_m = ab_w_out[j][512:].reshape(MLA_HEADS, MLA_V_DIM, D_MODEL)
    return {
        "w_in": jnp.pad(ab_w_in[j], ((0, 0), (0, AB_IN_PAD - AB_IN))).astype(BF16),
        "qn_g": mla_q_norm_g[j][None, :],
        "w_qb": _pad_heads(mla_w_qb[j], MLA_HEADS, 0, MLA_NOPE_DIM + MLA_ROPE_DIM).astype(BF16),
        "kvn_g": mla_kv_norm_g[j][None, :],
        "wk": _pad_heads(mla_w_kvb[j], MLA_HEADS, 0, MLA_NOPE_DIM).astype(BF16),
        "wv": _pad_heads(mla_w_kvb[j], MLA_HEADS, MLA_NOPE_DIM, MLA_NOPE_DIM + MLA_V_DIM).astype(BF16),
        "e": jnp.asarray(place, BF16),
        "w_out_d": ab_w_out[j][:512].reshape(DIFF_HEADS, LANES, D_MODEL).astype(BF16),
        "w_out_m": jnp.pad(w_out_m, ((0, 0), (0, LANES - MLA_V_DIM), (0, 0))).astype(BF16),
        "subln_g": diff_subln_g[j][None, :],
        "lam": diff_lambda[j],
    }


def kernel(x_prompt, x_sample, cache_diff_k, cache_diff_v, cache_mla_ckv, cache_mla_krope, cache_swa_k, cache_swa_v,
           c, c_ctx, ada_w, ada_b, norm1_g, norm2_g, final_norm_g, ab_w_in, diff_lambda, diff_subln_g, mla_q_norm_g,
           mla_w_qb, mla_kv_norm_g, mla_w_kvb, ab_w_out, swa_w_in, swa_sink, swa_w_out, router_w, router_bias,
           moe_w_gate, moe_w_up, moe_w_down):
    bp, sp, d = x_prompt.shape
    bs, ss, _ = x_sample.shape
    depth = ada_w.shape[0]
    past = cache_diff_k.shape[2]
    n_ab = cache_diff_k.shape[1]
    n_swa = cache_swa_k.shape[1]
    assert bs + 1 <= COND_ROWS and d == D_MODEL

    cond = jnp.zeros((COND_ROWS, d), F32).at[:bs].set(c).at[bs].set(c_ctx)
    mods = _ada_all(cond, ada_w, ada_b).reshape(depth, COND_ROWS, 6, d)
    row_s = lambda bi: bi
    row_p = lambda bi: bi * 0 + bs

    rope_d = _rope_tables(ss, DIFF_QK_DIM, 0)
    rope_q = _rope_tables(ss, MLA_ROPE_DIM, MLA_NOPE_DIM)
    rope_k = _rope_tables(ss, MLA_ROPE_DIM, 0)

    cdk = cache_diff_k.reshape(bs, n_ab, past, 512)
    cdv = cache_diff_v.reshape(bs, n_ab, past, 512)
    ckr = jnp.pad(cache_mla_krope, ((0, 0), (0, 0), (0, 0), (0, LANES - MLA_ROPE_DIM)))
    csk = cache_swa_k.reshape(bs, n_swa, past, 256)
    csv = cache_swa_v.reshape(bs, n_swa, past, 256)
    rw_hi = router_w.T.astype(BF16)
    rw_t = jnp.concatenate([rw_hi, (router_w.T - rw_hi.astype(F32)).astype(BF16)], axis=0)
    rb = router_bias[:, None]

    tp = min(256, sp)
    ts = min(256, ss)
    tm_s = min(512, ss)
    tm_p = min(512, bp * sp)

    xp, xs = x_prompt, x_sample
    new_ab, new_swa = None, None
    for l in range(depth):
        j = l // 2
        ml = mods[l]
        g1, g2 = norm1_g[l][None, :], norm2_g[l][None, :]
        if l % 2 == 0:
            lambda_init = 0.8 - 0.6 * math.exp(-0.3 * l)
            wts = _ab_weights(j, ab_w_in, diff_lambda, diff_subln_g, mla_q_norm_g, mla_w_qb, mla_kv_norm_g,
                              mla_w_kvb, ab_w_out)
            qa, ka, va, qb, km, vm, *new_ab = _proj_ab(xp, ml, row_p, g1, wts, None, (j, n_ab, new_ab), tp)
            xp = _attn_ab(xp, ml, row_p, (qa, qb), [(ka, va, km, vm)], wts, lambda_init, tp, DIFF_HEADS + MLA_HEADS)
            qa, ka, va, qb, km, vm = _proj_ab(xs, ml, row_s, g1, wts, (rope_d, rope_q, rope_k), None, ts)
            cache_seg = _cache_ab(cdk, cdv, cache_mla_ckv, ckr, j, wts)
            xs = _attn_ab(xs, ml, row_s, (qa, qb), [(ka, va, km, vm), tuple(cache_seg)], wts, lambda_init, ts, 2)
        else:
            w_in = swa_w_in[j].astype(BF16)
            w_out = swa_w_out[j].reshape(8, LANES, d).astype(BF16)
            q, k, v, *new_swa = _proj_swa(xp, ml, row_p, g1, w_in, None, (j, n_swa, new_swa), tp)
            xp = _attn_swa(xp, ml, row_p, q, k, v, None, swa_sink[j], w_out, tp, SWA_HEADS)
            q, k, v = _proj_swa(xs, ml, row_s, g1, w_in, rope_d, None, ts)
            xs = _attn_swa(xs, ml, row_s, q, k, v, (csk, csv, j), swa_sink[j], w_out, ts, 2)
        wgu = jnp.concatenate([moe_w_gate[l], moe_w_up[l]], axis=-1).astype(BF16)
        wd = moe_w_down[l].astype(BF16)
        fg = final_norm_g[None, :] if l == depth - 1 else None
        xp = _moe(xp.reshape(1, bp * sp, d), ml, row_p, g2, rw_t, rb, wgu, wd, fg, tm_p).reshape(bp, sp, d)
        xs = _moe(xs, ml, row_s, g2, rw_t, rb, wgu, wd, fg, tm_s)

    new_diff_k = new_ab[0].reshape(bp, n_ab, sp, DIFF_HEADS, 2, DIFF_QK_DIM)
    new_diff_v = new_ab[1].reshape(bp, n_ab, sp, DIFF_HEADS, DIFF_V_DIM)
    new_mla_ckv, new_mla_krope = new_ab[2], new_ab[3]
    new_swa_k = new_swa[0].reshape(bp, n_swa, sp, SWA_KV_HEADS, SWA_HEAD_DIM)
    new_swa_v = new_swa[1].reshape(bp, n_swa, sp, SWA_KV_HEADS, SWA_HEAD_DIM)
    return (xp, xs, new_diff_k, new_diff_v, new_mla_ckv, new_mla_krope, new_swa_k, new_swa_v)
```

```python
import functools
import math

import jax
import jax.numpy as jnp
import numpy as np
from jax import lax
from jax.experimental import pallas as pl
from jax.experimental.pallas import tpu as pltpu
from jax.experimental.pallas import tpu_sc as plsc

F32 = jnp.float32
BF16 = jnp.bfloat16

D_MODEL = 1024
GRID_W = 64
ROPE_THETA = 10000.0
NORM_EPS = 1e-6
NEG_INF = -1e30
LOG2E = math.log2(math.e)
LANES = 128

DIFF_HEADS = 4
DIFF_QK_DIM = 64
DIFF_V_DIM = 128
MLA_HEADS = 8
MLA_Q_RANK = 384
MLA_KV_RANK = 256
MLA_NOPE_DIM = 64
MLA_ROPE_DIM = 32
MLA_V_DIM = 64
AB_IN = 3 * 512 + MLA_Q_RANK + MLA_KV_RANK + MLA_ROPE_DIM
AB_IN_PAD = 2304
SWA_HEADS = 16
SWA_KV_HEADS = 4
SWA_GROUP = 4
SWA_HEAD_DIM = 64
WINDOW = 128
N_EXPERTS = 16
N_GROUPS = 4
EXPERTS_PER_GROUP = 4
D_EXPERT = 256
COND_ROWS = 16

VMEM_LIMIT = 56 * 1024 * 1024


def _full(shape):
    n = len(shape)
    return pl.BlockSpec(shape, lambda *_: (0,) * n)


def _resident(shape):
    n = len(shape)
    return pl.BlockSpec(shape, lambda *_: (0,) * n, pipeline_mode=pl.Buffered(1))


def _params(n_axes):
    return pltpu.CompilerParams(dimension_semantics=("arbitrary",) * n_axes, vmem_limit_bytes=VMEM_LIMIT)


def _sigmoid(x):
    return 1.0 / (1.0 + jnp.exp(-x))


def _rms(x, g):
    return x * lax.rsqrt(jnp.mean(x * x, axis=-1, keepdims=True) + NORM_EPS) * g


def _modulate(x, g, shift, scale):
    return _rms(x, g) * (1.0 + scale) + shift


def _dot(a, b):
    return jnp.dot(a, b, preferred_element_type=F32)


def _dot_nt(a, b):
    return lax.dot_general(a, b, (((1,), (1,)), ((), ())), preferred_element_type=F32)


def _rope(x, tab_ref):
    c, s_odd, s_even = tab_ref[0], tab_ref[1], tab_ref[2]
    out = []
    for i in range(x.shape[1] // LANES):
        xi = x[:, i * LANES:(i + 1) * LANES]
        out.append(xi * c + pltpu.roll(xi, 1, 1) * s_odd + pltpu.roll(xi, LANES - 1, 1) * s_even)
    return out[0] if len(out) == 1 else jnp.concatenate(out, axis=1)


def _tree(op, xs):
    xs = list(xs)
    while len(xs) > 1:
        xs = [op(xs[i], xs[i + 1]) if i + 1 < len(xs) else xs[i] for i in range(0, len(xs), 2)]
    return xs[0]


def _lane_fold(op, x):
    return _tree(op, [x[:, i * LANES:(i + 1) * LANES] for i in range(x.shape[1] // LANES)])


def _ada_kernel(cond_ref, w_ref, b_ref, o_ref):
    c = cond_ref[...]
    a = (c * _sigmoid(c)).astype(BF16)
    o_ref[...] = _dot(a, w_ref[...].astype(BF16)) + b_ref[...]


def _ada_all(cond, ada_w, ada_b):
    depth, d, n = ada_w.shape
    tn = 1536
    return pl.pallas_call(
        _ada_kernel,
        grid=(depth, n // tn),
        in_specs=[
            _full((COND_ROWS, d)),
            pl.BlockSpec((None, d, tn), lambda l, j: (l, 0, j)),
            pl.BlockSpec((None, 1, tn), lambda l, j: (l, 0, j)),
        ],
        out_specs=pl.BlockSpec((None, COND_ROWS, tn), lambda l, j: (l, 0, j)),
        out_shape=jax.ShapeDtypeStruct((depth, COND_ROWS, n), F32),
        compiler_params=_params(2),
        name="ada_mod",
    )(cond, ada_w, ada_b.reshape(depth, 1, n))


def _proj_ab_kernel(rope, emit_cache, n_prev, x_ref, mod_ref, g_ref, w_in_ref, qn_ref, wqb_ref, kvn_ref, wk_ref,
                    wv_ref, e_ref, *rest):
    if rope:
        rope_d, rope_q, rope_k = rest[:3]
        rest = rest[3:]
    rest = rest[n_prev:]
    qa_o, ka_o, va_o, qb_o, km_o, vm_o = rest[:6]
    mod = mod_ref[0]
    h = _modulate(x_ref[...], g_ref[...], mod[0:1], mod[1:2]).astype(BF16)
    big = _dot(h, w_in_ref[...])
    qa, ka, va = big[:, 0:512], big[:, 512:1024], big[:, 1024:1536]
    q_lat, ckv, kr = big[:, 1536:1920], big[:, 1920:2176], big[:, 2176:2304]
    qb = _dot(_rms(q_lat, qn_ref[...]).astype(BF16), wqb_ref[...])
    ckv_n = _rms(ckv, kvn_ref[...])
    if emit_cache:
        ka32_o, va32_o, ckv32_o, kr32_o = rest[6:10]
        ka32_o[...] = ka
        va32_o[...] = va
        ckv32_o[...] = ckv_n
        kr32_o[...] = kr[:, :MLA_ROPE_DIM]
    if rope:
        qa, ka = _rope(qa, rope_d), _rope(ka, rope_d)
        qb, kr = _rope(qb, rope_q), _rope(kr, rope_k)
    qa = qa * (DIFF_QK_DIM ** -0.5 * LOG2E)
    qb = qb * ((MLA_NOPE_DIM + MLA_ROPE_DIM) ** -0.5 * LOG2E)
    ckv16 = ckv_n.astype(BF16)
    km = _dot(ckv16, wk_ref[...]) + _dot(kr.astype(BF16), e_ref[...])
    vm = _dot(ckv16, wv_ref[...])
    lo = lax.broadcasted_iota(jnp.int32, (qa.shape[0], LANES), 1) < DIFF_QK_DIM
    for hd in range(DIFF_HEADS):
        sl = slice(hd * LANES, (hd + 1) * LANES)
        qa_o[2 * hd] = jnp.where(lo, qa[:, sl], 0.0).astype(BF16)
        qa_o[2 * hd + 1] = jnp.where(lo, 0.0, qa[:, sl]).astype(BF16)
        ka_o[hd] = ka[:, sl].astype(BF16)
        va_o[hd] = va[:, sl].astype(BF16)
    for hd in range(MLA_HEADS):
        sl = slice(hd * LANES, (hd + 1) * LANES)
        qb_o[hd] = qb[:, sl].astype(BF16)
        km_o[hd] = km[:, sl].astype(BF16)
        vm_o[hd] = vm[:, sl].astype(BF16)


def _cache_outputs(cache, widths, b, s, tm, in_specs, args, n_fixed_outs):
    if cache is None:
        return [], {}, 0
    j, n_layers, prev = cache
    outs = [(jax.ShapeDtypeStruct((b, n_layers, s, w), F32),
             pl.BlockSpec((None, None, tm, w), lambda bi, t: (bi, j, t, 0))) for w in widths]
    aliases = {}
    if prev is not None:
        for i, arr in enumerate(prev):
            aliases[len(args)] = n_fixed_outs + i
            in_specs.append(pl.BlockSpec(memory_space=pl.ANY))
            args.append(arr)
    return outs, aliases, len(aliases)


def _proj_ab(x, mods, row_fn, g, wts, rope_tabs, cache, tm):
    b, s, d = x.shape
    rope = rope_tabs is not None
    tok = lambda bi, t: (bi, t, 0)
    head = lambda bi, t: (bi, 0, t, 0)
    in_specs = [
        pl.BlockSpec((None, tm, d), tok),
        pl.BlockSpec((1, 6, d), lambda bi, t: (row_fn(bi), 0, 0)),
        _full((1, d)),
        _full((d, AB_IN_PAD)), _full((1, MLA_Q_RANK)), _full((MLA_Q_RANK, 1024)), _full((1, MLA_KV_RANK)),
        _full((MLA_KV_RANK, 1024)), _full((MLA_KV_RANK, 1024)), _full((LANES, 1024)),
    ]
    args = [x, mods, g, wts["w_in"], wts["qn_g"], wts["w_qb"], wts["kvn_g"], wts["wk"], wts["wv"], wts["e"]]
    if rope:
        in_specs += [pl.BlockSpec((3, tm, LANES), lambda bi, t: (0, t, 0))] * 3
        args += list(rope_tabs)

    def hm(nh):
        return jax.ShapeDtypeStruct((b, nh, s, LANES), BF16), pl.BlockSpec((None, nh, tm, LANES), head)

    outs = [hm(8), hm(4), hm(4), hm(8), hm(8), hm(8)]
    cache_outs, aliases, n_prev = _cache_outputs(cache, (512, 512, MLA_KV_RANK, MLA_ROPE_DIM), b, s, tm, in_specs,
                                                 args, len(outs))
    outs += cache_outs
    return pl.pallas_call(
        functools.partial(_proj_ab_kernel, rope, cache is not None, n_prev),
        grid=(b, s // tm),
        in_specs=in_specs,
        out_specs=[o[1] for o in outs],
        out_shape=[o[0] for o in outs],
        input_output_aliases=aliases,
        compiler_params=_params(2),
        name="proj_ab",
    )(*args)


def _cache_ab_kernel(dk_ref, dv_ref, ckv_ref, kr_ref, wk_ref, wv_ref, e_ref, ck_o, cv_o, km_o, vm_o):
    ckv16 = ckv_ref[...].astype(BF16)
    km = _dot(ckv16, wk_ref[...]) + _dot(kr_ref[...].astype(BF16), e_ref[...])
    vm = _dot(ckv16, wv_ref[...])
    for hd in range(DIFF_HEADS):
        sl = slice(hd * LANES, (hd + 1) * LANES)
        ck_o[hd] = dk_ref[:, sl].astype(BF16)
        cv_o[hd] = dv_ref[:, sl].astype(BF16)
    for hd in range(MLA_HEADS):
        sl = slice(hd * LANES, (hd + 1) * LANES)
        km_o[hd] = km[:, sl].astype(BF16)
        vm_o[hd] = vm[:, sl].astype(BF16)


def _cache_ab(cdk, cdv, cckv, ckr, j, wts):
    b, _, p, _ = cdk.shape
    lay = lambda bi: (bi, j, 0, 0)

    def hm(nh):
        return (jax.ShapeDtypeStruct((b, nh, p, LANES), BF16),
                pl.BlockSpec((None, nh, p, LANES), lambda bi: (bi, 0, 0, 0)))

    outs = [hm(4), hm(4), hm(8), hm(8)]
    return pl.pallas_call(
        _cache_ab_kernel,
        grid=(b,),
        in_specs=[
            pl.BlockSpec((None, None, p, 512), lay), pl.BlockSpec((None, None, p, 512), lay),
            pl.BlockSpec((None, None, p, MLA_KV_RANK), lay), pl.BlockSpec((None, None, p, LANES), lay),
            _full((MLA_KV_RANK, 1024)), _full((MLA_KV_RANK, 1024)), _full((LANES, 1024)),
        ],
        out_specs=[o[1] for o in outs],
        out_shape=[o[0] for o in outs],
        compiler_params=_params(1),
        name="cache_ab",
    )(cdk, cdv, cckv, ckr, wts["wk"], wts["wv"], wts["e"])


def _attn_ab_kernel(n_seg, lambda_init, x_ref, mod_ref, qa_ref, qb_ref, *rest):
    segs = [rest[4 * i:4 * i + 4] for i in range(n_seg)]
    wd_ref, wm_ref, sg_ref, lam_ref, o_ref = rest[4 * n_seg:4 * n_seg + 5]
    bufs = rest[4 * n_seg + 5:]
    lp = lam_ref[...]
    lam = (jnp.exp(jnp.sum(lp[0:1] * lp[1:2], axis=-1, keepdims=True))
           - jnp.exp(jnp.sum(lp[2:3] * lp[3:4], axis=-1, keepdims=True)) + lambda_init)
    tq, d = x_ref.shape
    widths = [sg[0].shape[1] for sg in segs]
    offs = [sum(widths[:i]) for i in range(n_seg)]
    units = [("diff", hd) for hd in range(DIFF_HEADS)] + [("mla", hd) for hd in range(MLA_HEADS)]

    def scores(u, buf):
        kind, hd = units[u]
        qs = [qa_ref[2 * hd], qa_ref[2 * hd + 1]] if kind == "diff" else [qb_ref[hd]]
        ks = [sg[0 if kind == "diff" else 2][hd] for sg in segs]
        ms = []
        for i, q in enumerate(qs):
            folds = []
            for k, off, w in zip(ks, offs, widths):
                s = _dot_nt(q, k)
                buf[i, :, off:off + w] = s
                folds.append(_lane_fold(jnp.maximum, s))
            ms.append(jnp.max(_tree(jnp.maximum, folds), axis=-1, keepdims=True))
        return ms

    def exps(buf, i, m):
        e = [jnp.exp2(buf[i, :, off:off + w] - m) for off, w in zip(offs, widths)]
        l = jnp.sum(_tree(jnp.add, [_lane_fold(jnp.add, x) for x in e]), axis=-1, keepdims=True)
        return e, l

    def consume(u, buf, ms, acc):
        kind, hd = units[u]
        if kind == "diff":
            vs = [sg[1][hd] for sg in segs]
            (e1, l1), (e2, l2) = exps(buf, 0, ms[0]), exps(buf, 1, ms[1])
            c1, c2 = 1.0 / l1, lam / l2
            o = _tree(jnp.add, [_dot((a * c1 - b * c2).astype(BF16), v) for a, b, v in zip(e1, e2, vs)])
            od = _rms(o, sg_ref[...]) * (1.0 - lambda_init)
            return acc + _dot(od.astype(BF16), wd_ref[hd])
        vs = [sg[3][hd] for sg in segs]
        e, l = exps(buf, 0, ms[0])
        o = _tree(jnp.add, [_dot(x.astype(BF16), v) for x, v in zip(e, vs)]) / l
        return acc + _dot(o.astype(BF16), wm_ref[hd])

    acc = jnp.zeros((tq, d), F32)
    ms = scores(0, bufs[0])
    for u in range(len(units)):
        nxt = scores(u + 1, bufs[(u + 1) % len(bufs)]) if u + 1 < len(units) else None
        acc = consume(u, bufs[u % len(bufs)], ms, acc)
        ms = nxt
    o_ref[...] = x_ref[...] + mod_ref[0][2:3] * acc


def _attn_ab(x, mods, row_fn, q_parts, seg_list, wts, lambda_init, tq, n_bufs):
    b, s, d = x.shape
    qa, qb = q_parts
    in_specs = [
        pl.BlockSpec((None, tq, d), lambda bi, t: (bi, t, 0)),
        pl.BlockSpec((1, 6, d), lambda bi, t: (row_fn(bi), 0, 0)),
        pl.BlockSpec((None, 8, tq, LANES), lambda bi, t: (bi, 0, t, 0)),
        pl.BlockSpec((None, 8, tq, LANES), lambda bi, t: (bi, 0, t, 0)),
    ]
    args = [x, mods, qa, qb]
    kv_mode = dict(pipeline_mode=pl.Buffered(1)) if s // tq > 1 else {}
    for seg in seg_list:
        for arr in seg:
            nh, nk = arr.shape[1], arr.shape[2]
            in_specs.append(pl.BlockSpec((None, nh, nk, LANES), lambda bi, t: (bi, 0, 0, 0), **kv_mode))
            args.append(arr)
    in_specs += [_full((DIFF_HEADS, LANES, d)), _full((MLA_HEADS, LANES, d)), _full((1, LANES)),
                 _full((4, DIFF_QK_DIM))]
    args += [wts["w_out_d"], wts["w_out_m"], wts["subln_g"], wts["lam"]]
    n_keys = sum(seg[0].shape[2] for seg in seg_list)
    return pl.pallas_call(
        functools.partial(_attn_ab_kernel, len(seg_list), lambda_init),
        grid=(b, s // tq),
        in_specs=in_specs,
        out_specs=pl.BlockSpec((None, tq, d), lambda bi, t: (bi, t, 0)),
        out_shape=jax.ShapeDtypeStruct((b, s, d), F32),
        scratch_shapes=[pltpu.VMEM((2, tq, n_keys), F32) for _ in range(n_bufs)],
        compiler_params=_params(2),
        name="attn_ab",
    )(*args)


def _proj_swa_kernel(rope, emit_cache, n_prev, x_ref, mod_ref, g_ref, w_in_ref, *rest):
    if rope:
        rope_d = rest[0]
        rest = rest[1:]
    rest = rest[n_prev:]
    q_o, k_o, v_o = rest[:3]
    mod = mod_ref[0]
    h = _modulate(x_ref[...], g_ref[...], mod[0:1], mod[1:2]).astype(BF16)
    big = _dot(h, w_in_ref[...])
    q, k, v = big[:, :1024], big[:, 1024:1280], big[:, 1280:1536]
    if emit_cache:
        k32_o, v32_o = rest[3:5]
        k32_o[...] = k
        v32_o[...] = v
    if rope:
        q, k = _rope(q, rope_d), _rope(k, rope_d)
    q = q * (SWA_HEAD_DIM ** -0.5 * LOG2E)
    lo = lax.broadcasted_iota(jnp.int32, (q.shape[0], LANES), 1) < SWA_HEAD_DIM
    for pair in range(2):
        for grp in range(SWA_GROUP):
            ca, cb = (2 * pair) * 2 + grp // 2, (2 * pair + 1) * 2 + grp // 2
            a = q[:, ca * LANES:(ca + 1) * LANES]
            bb = q[:, cb * LANES:(cb + 1) * LANES]
            if grp % 2 == 0:
                bb = pltpu.roll(bb, SWA_HEAD_DIM, 1)
            else:
                a = pltpu.roll(a, SWA_HEAD_DIM, 1)
            q_o[pair * SWA_GROUP + grp] = jnp.where(lo, a, bb).astype(BF16)
    for kvh in range(SWA_KV_HEADS):
        sl = slice((kvh // 2) * LANES, (kvh // 2 + 1) * LANES)
        keep = lo if kvh % 2 == 0 else jnp.logical_not(lo)
        k_o[kvh] = jnp.where(keep, k[:, sl], 0.0).astype(BF16)
        v_o[kvh] = jnp.where(keep, v[:, sl], 0.0).astype(BF16)


def _proj_swa(x, mods, row_fn, g, w_in, rope_tab, cache, tm):
    b, s, d = x.shape
    rope = rope_tab is not None
    tok = lambda bi, t: (bi, t, 0)
    head = lambda bi, t: (bi, 0, t, 0)
    in_specs = [
        pl.BlockSpec((None, tm, d), tok),
        pl.BlockSpec((1, 6, d), lambda bi, t: (row_fn(bi), 0, 0)),
        _full((1, d)), _full((d, 1536)),
    ]
    args = [x, mods, g, w_in]
    if rope:
        in_specs.append(pl.BlockSpec((3, tm, LANES), lambda bi, t: (0, t, 0)))
        args.append(rope_tab)

    def hm(nh):
        return jax.ShapeDtypeStruct((b, nh, s, LANES), BF16), pl.BlockSpec((None, nh, tm, LANES), head)

    outs = [hm(8), hm(4), hm(4)]
    cache_outs, aliases, n_prev = _cache_outputs(cache, (256, 256), b, s, tm, in_specs, args, len(outs))
    outs += cache_outs
    return pl.pallas_call(
        functools.partial(_proj_swa_kernel, rope, cache is not None, n_prev),
        grid=(b, s // tm),
        in_specs=in_specs,
        out_specs=[o[1] for o in outs],
        out_shape=[o[0] for o in outs],
        input_output_aliases=aliases,
        compiler_params=_params(2),
        name="proj_swa",
    )(*args)


def _attn_swa_kernel(windowed, band_w, x_ref, mod_ref, q_ref, k_ref, v_ref, *rest):
    if windowed:
        ck_ref, cv_ref = rest[:2]
        rest = rest[2:]
    sink_ref, w_ref, o_ref = rest[:3]
    bufs = rest[3:]
    tq, d = x_ref.shape
    n_keys = k_ref.shape[1]
    lo_k = lax.broadcasted_iota(jnp.int32, (1, LANES), 1) < SWA_HEAD_DIM
    if windowed:
        start = pl.program_id(1) * tq
        bstart = pl.multiple_of(jnp.clip(start - WINDOW, 0, n_keys - band_w), LANES)
        qpos = start + lax.broadcasted_iota(jnp.int32, (tq, band_w), 0)
        kpos = bstart + lax.broadcasted_iota(jnp.int32, (tq, band_w), 1)
        bias = jnp.where(jnp.abs(qpos - kpos) <= WINDOW, 0.0, NEG_INF)

    def keys_values(kvh):
        if not windowed:
            return [k_ref[kvh]], [v_ref[kvh]]
        keep = lo_k if kvh % 2 == 0 else jnp.logical_not(lo_k)
        sl = slice((kvh // 2) * LANES, (kvh // 2 + 1) * LANES)
        kc = jnp.where(keep, ck_ref[:, sl], 0.0).astype(BF16)
        vc = jnp.where(keep, cv_ref[:, sl], 0.0).astype(BF16)
        return ([k_ref[kvh, pl.ds(bstart, band_w), :], kc], [v_ref[kvh, pl.ds(bstart, band_w), :], vc])

    units = [(kvh, g) for kvh in range(SWA_KV_HEADS) for g in range(SWA_GROUP)]

    def scores(u, buf):
        kvh, g = units[u]
        q = q_ref[(kvh // 2) * SWA_GROUP + g]
        sink = sink_ref[kvh * SWA_GROUP + g] * LOG2E
        folds, off = [], 0
        for i, k in enumerate(keys_values(kvh)[0]):
            s = _dot_nt(q, k)
            if windowed and i == 0:
                s = s + bias
            buf[:, off:off + s.shape[1]] = s
            folds.append(_lane_fold(jnp.maximum, s))
            off += s.shape[1]
        m = jnp.maximum(jnp.max(_tree(jnp.maximum, folds), axis=-1, keepdims=True), sink)
        return m, sink

    def consume(u, buf, m, sink):
        vs = keys_values(units[u][0])[1]
        e, off = [], 0
        for v in vs:
            e.append(jnp.exp2(buf[:, off:off + v.shape[0]] - m))
            off += v.shape[0]
        l = jnp.sum(_tree(jnp.add, [_lane_fold(jnp.add, x) for x in e]), axis=-1, keepdims=True) + jnp.exp2(sink - m)
        return _tree(jnp.add, [_dot(x.astype(BF16), v) for x, v in zip(e, vs)]) / l

    acc = jnp.zeros((tq, d), F32)
    nxt = scores(0, bufs[0])
    prev = None
    for u, (kvh, g) in enumerate(units):
        cur = nxt
        if u + 1 < len(units):
            nxt = scores(u + 1, bufs[(u + 1) % len(bufs)])
        o = consume(u, bufs[u % len(bufs)], *cur)
        if g % 2 == 0:
            prev = o
            continue
        if kvh % 2 == 0:
            slab = prev + pltpu.roll(o, SWA_HEAD_DIM, 1)
        else:
            slab = pltpu.roll(prev, SWA_HEAD_DIM, 1) + o
        acc = acc + _dot(slab.astype(BF16), w_ref[kvh * 2 + g // 2])
    o_ref[...] = x_ref[...] + mod_ref[0][2:3] * acc


def _attn_swa(x, mods, row_fn, q, k, v, ctx, sink, w_out, tq, n_bufs):
    b, s, d = x.shape
    windowed = ctx is not None
    band_w = min(tq + 2 * WINDOW, s)
    kv_spec = pl.BlockSpec((None, SWA_KV_HEADS, s, LANES), lambda bi, t: (bi, 0, 0, 0))
    in_specs = [
        pl.BlockSpec((None, tq, d), lambda bi, t: (bi, t, 0)),
        pl.BlockSpec((1, 6, d), lambda bi, t: (row_fn(bi), 0, 0)),
        pl.BlockSpec((None, 8, tq, LANES), lambda bi, t: (bi, 0, t, 0)),
        kv_spec, kv_spec,
    ]
    args = [x, mods, q, k, v]
    if windowed:
        ck, cv, j = ctx
        p = ck.shape[2]
        spec = pl.BlockSpec((None, None, p, 256), lambda bi, t: (bi, j, 0, 0))
        in_specs += [spec, spec]
        args += [ck, cv]
    in_specs += [pl.BlockSpec(memory_space=pltpu.SMEM), _full((8, LANES, d))]
    args += [sink, w_out]
    n_keys = band_w + ctx[0].shape[2] if windowed else s
    return pl.pallas_call(
        functools.partial(_attn_swa_kernel, windowed, band_w),
        grid=(b, s // tq),
        in_specs=in_specs,
        out_specs=pl.BlockSpec((None, tq, d), lambda bi, t: (bi, t, 0)),
        out_shape=jax.ShapeDtypeStruct((b, s, d), F32),
        scratch_shapes=[pltpu.VMEM((tq, n_keys), F32) for _ in range(n_bufs)],
        compiler_params=_params(2),
        name="attn_swa",
    )(*args)


def _route(scores_t, bias):
    sel_t = scores_t + bias
    sel = [sel_t[e:e + 1] for e in range(N_EXPERTS)]
    raw = [scores_t[e:e + 1] for e in range(N_EXPERTS)]
    gscore = []
    for g in range(N_GROUPS):
        r = sel[g * 4:g * 4 + 4]
        pairs = [r[i] + r[j] for i in range(4) for j in range(i + 1, 4)]
        gscore.append(functools.reduce(jnp.maximum, pairs))
    best, gidx = gscore[0], jnp.zeros_like(gscore[0], dtype=jnp.int32)
    for g in range(1, N_GROUPS):
        take = gscore[g] > best
        best = jnp.where(take, gscore[g], best)
        gidx = jnp.where(take, g, gidx)
    vals = []
    for k in range(EXPERTS_PER_GROUP):
        v = sel[k]
        for g in range(1, N_GROUPS):
            v = jnp.where(gidx == g, sel[g * 4 + k], v)
        vals.append(v)

    def argmax4(vs):
        m, idx = vs[0], jnp.zeros_like(gidx)
        for k in range(1, 4):
            take = vs[k] > m
            m = jnp.where(take, vs[k], m)
            idx = jnp.where(take, k, idx)
        return idx

    i1 = argmax4(vals)
    i2 = argmax4([jnp.where(i1 == k, -jnp.inf, vals[k]) for k in range(4)])
    e1, e2 = gidx * 4 + i1, gidx * 4 + i2
    w1 = functools.reduce(lambda a, b: a + b, [jnp.where(e1 == e, raw[e], 0.0) for e in range(N_EXPERTS)])
    w2 = functools.reduce(lambda a, b: a + b, [jnp.where(e2 == e, raw[e], 0.0) for e in range(N_EXPERTS)])
    den = w1 + w2
    return e1, e2, w1 / den, w2 / den


def _gate_rows(e1, e2, g1, g2):
    rows = [jnp.where(e1 == e, g1, 0.0) + jnp.where(e2 == e, g2, 0.0) for e in range(N_EXPERTS)]
    return jnp.concatenate(rows, axis=0)


def _moe_kernel(final_norm, x_ref, mod_ref, g_ref, rw_ref, rb_ref, wgu_ref, wd_ref, *rest):
    if final_norm:
        fg_ref, o_ref, gu0_ref, gu1_ref = rest
    else:
        o_ref, gu0_ref, gu1_ref = rest
    tm, d = x_ref.shape
    mod = mod_ref[0]
    x = x_ref[...]
    h = _modulate(x, g_ref[...], mod[3:4], mod[4:5])
    h16 = h.astype(BF16)
    h_lo = (h - h16.astype(F32)).astype(BF16)
    part = _dot_nt(rw_ref[...], h16)
    logits_t = part[:N_EXPERTS] + part[N_EXPERTS:] + _dot_nt(rw_ref[:N_EXPERTS], h_lo)
    gates_t = _gate_rows(*_route(_sigmoid(logits_t), rb_ref[...]))
    gates = jnp.concatenate([gates_t, jnp.zeros((LANES - N_EXPERTS, tm), F32)], axis=0).T
    lane = lax.broadcasted_iota(jnp.int32, (tm, LANES), 1)

    def gate_up(e, buf):
        buf[...] = _dot(h16, wgu_ref[e])

    def down(e, buf, acc):
        gate = jnp.sum(jnp.where(lane == e, gates, 0.0), axis=-1, keepdims=True)
        g, u = buf[:, :D_EXPERT], buf[:, D_EXPERT:]
        a = g * _sigmoid(g) * u * gate
        return acc + _dot(a.astype(BF16), wd_ref[e])

    bufs = (gu0_ref, gu1_ref)
    acc = jnp.zeros((tm, d), F32)
    gate_up(0, bufs[0])
    for e in range(N_EXPERTS):
        if e + 1 < N_EXPERTS:
            gate_up(e + 1, bufs[(e + 1) % 2])
        acc = down(e, bufs[e % 2], acc)
    y = x + mod[5:6] * acc
    if final_norm:
        y = _rms(y, fg_ref[...])
    o_ref[...] = y


def _moe(x, mods, row_fn, g, rw_t, rb, wgu, wd, final_g, tm):
    b, s, d = x.shape
    final_norm = final_g is not None
    in_specs = [
        pl.BlockSpec((None, tm, d), lambda bi, t: (bi, t, 0)),
        pl.BlockSpec((1, 6, d), lambda bi, t: (row_fn(bi), 0, 0)),
        _full((1, d)), _full((2 * N_EXPERTS, d)), _full((N_EXPERTS, 1)),
        _resident((N_EXPERTS, d, 2 * D_EXPERT)), _resident((N_EXPERTS, D_EXPERT, d)),
    ]
    args = [x, mods, g, rw_t, rb, wgu, wd]
    if final_norm:
        in_specs.append(_full((1, d)))
        args.append(final_g)
    return pl.pallas_call(
        functools.partial(_moe_kernel, final_norm),
        grid=(b, s // tm),
        in_specs=in_specs,
        out_specs=pl.BlockSpec((None, tm, d), lambda bi, t: (bi, t, 0)),
        out_shape=jax.ShapeDtypeStruct((b, s, d), F32),
        scratch_shapes=[pltpu.VMEM((tm, 2 * D_EXPERT), F32) for _ in range(2)],
        compiler_params=_params(2),
        name="moe",
    )(*args)


SC_WINDOW = 128
SC_ROW = 128
HALF_WORD = -65536


def _pack_bf16_pairs(x):
    n = x.shape[1] // 2
    lo = pltpu.bitcast(x[:, :n].astype(BF16).astype(F32), jnp.int32)
    hi = pltpu.bitcast(x[:, n:].astype(BF16).astype(F32), jnp.int32)
    return lax.shift_right_logical(lo, jnp.int32(16)) | (hi & jnp.int32(HALF_WORD))


def _unpack_bf16_pairs(p):
    lo = pltpu.bitcast(lax.shift_left(p, jnp.int32(16)), F32)
    hi = pltpu.bitcast(p & jnp.int32(HALF_WORD), F32)
    return jnp.concatenate([lo, hi], axis=1)


def _pick(idx, rows):
    return _tree(jnp.add, [jnp.where(idx == e, rows[e], 0.0) for e in range(N_EXPERTS)])


def _moe_route_kernel(x_ref, mod_ref, g_ref, rw_ref, rb_ref, tri_ref, hp_o, eidx_o, rank_o, gcol_o, cnt_o, seen):
    @pl.when(jnp.logical_and(pl.program_id(0) == 0, pl.program_id(1) == 0))
    def _():
        seen[...] = jnp.zeros_like(seen)

    tm = x_ref.shape[0]
    mod = mod_ref[0]
    h = _modulate(x_ref[...], g_ref[...], mod[3:4], mod[4:5])
    h16 = h.astype(BF16)
    hp_o[...] = _pack_bf16_pairs(h)
    h_lo = (h - h16.astype(F32)).astype(BF16)
    part = _dot_nt(rw_ref[...], h16)
    logits_t = part[:N_EXPERTS] + part[N_EXPERTS:] + _dot_nt(rw_ref[:N_EXPERTS], h_lo)
    e1, e2, g1, g2 = _route(_sigmoid(logits_t), rb_ref[...])
    hot = jnp.concatenate([jnp.where(jnp.logical_or(e1 == e, e2 == e), 1.0, 0.0) for e in range(N_EXPERTS)], axis=0)
    before = _dot(hot.astype(BF16), tri_ref[...]) + seen[...]
    rows = [before[e:e + 1] for e in range(N_EXPERTS)]
    eidx_o[0], eidx_o[1] = e1, e2
    rank_o[0], rank_o[1] = _pick(e1, rows).astype(jnp.int32), _pick(e2, rows).astype(jnp.int32)
    gcol_o[...] = jnp.concatenate([g1, g2, jnp.zeros((LANES - 2, tm), F32)], axis=0).T
    seen[...] = seen[...] + jnp.sum(hot, axis=-1, keepdims=True)
    cnt_o[...] = seen[...]


def _moe_slots_kernel(mb, eidx_ref, rank_ref, cnt_ref, slot_o, be_o):
    cnt = cnt_ref[...]
    padded = jnp.ceil(cnt / mb) * mb
    starts, ends, run = [], [], jnp.zeros((1, 1), F32)
    for e in range(N_EXPERTS):
        starts.append(run)
        run = run + padded[e:e + 1]
        ends.append(run)
    for k in range(2):
        for i in range(eidx_ref.shape[1]):
            slot_o[k, i] = rank_ref[k, i] + _pick(eidx_ref[k, i], starts).astype(jnp.int32)
    first_row = lax.broadcasted_iota(jnp.int32, be_o.shape, 1).astype(F32) * mb
    be = _tree(jnp.add, [jnp.where(ends[e] <= first_row, 1.0, 0.0) for e in range(N_EXPERTS)])
    be_o[...] = jnp.minimum(be, N_EXPERTS - 1.0).astype(jnp.int32)


def _moe_ffn_kernel(be_ref, hs_ref, wgu_ref, wd_ref, ys_o):
    del be_ref
    h = _unpack_bf16_pairs(hs_ref[...]).astype(BF16)
    gu = _dot(h, wgu_ref[...])
    g, u = gu[:, :D_EXPERT], gu[:, D_EXPERT:]
    a = g * _sigmoid(g) * u
    ys_o[...] = _pack_bf16_pairs(_dot(a.astype(BF16), wd_ref[...]))


def _moe_combine_kernel(final_norm, x_ref, mod_ref, z0_ref, z1_ref, gcol_ref, *rest):
    if final_norm:
        fg_ref, o_ref = rest
    else:
        (o_ref,) = rest
    gc = gcol_ref[...]
    lane = lax.broadcasted_iota(jnp.int32, gc.shape, 1)
    g1 = jnp.sum(jnp.where(lane == 0, gc, 0.0), axis=-1, keepdims=True)
    g2 = jnp.sum(jnp.where(lane == 1, gc, 0.0), axis=-1, keepdims=True)
    y = g1 * _unpack_bf16_pairs(z0_ref[...]) + g2 * _unpack_bf16_pairs(z1_ref[...])
    out = x_ref[...] + mod_ref[0][5:6] * y
    if final_norm:
        out = _rms(out, fg_ref[...])
    o_ref[...] = out


def _sc_mesh():
    return plsc.VectorSubcoreMesh(core_axis_name="core", subcore_axis_name="subcore")


def _sc_scatter_rows(rows, slot0, slot1, n_out):
    n, w = rows.shape

    @pl.kernel(out_type=jax.ShapeDtypeStruct((n_out, w), rows.dtype), mesh=_sc_mesh(), scratch_types=[])
    def scatter(x_hbm, i0_hbm, i1_hbm, o_hbm):
        def body(x_vmem, i0_vmem, i1_vmem):
            pltpu.sync_copy(x_vmem, o_hbm.at[i0_vmem.at[0]])
            pltpu.sync_copy(x_vmem, o_hbm.at[i1_vmem.at[0]])

        pltpu.emit_pipeline(
            body,
            grid=(n // SC_WINDOW,),
            in_specs=[pl.BlockSpec((SC_WINDOW, w), lambda i: (i, 0)),
                      pl.BlockSpec((1, SC_WINDOW), lambda i: (0, i)),
                      pl.BlockSpec((1, SC_WINDOW), lambda i: (0, i))],
            out_specs=[],
            core_axis_name=("core", "subcore"),
            dimension_semantics=(pltpu.PARALLEL,),
        )(x_hbm, i0_hbm, i1_hbm)

    return scatter(rows, slot0, slot1)


def _sc_gather_rows(rows, idx):
    n, w = idx.shape[1], rows.shape[1]

    @pl.kernel(out_type=jax.ShapeDtypeStruct((n, w), rows.dtype), mesh=_sc_mesh(), scratch_types=[])
    def gather(x_hbm, i_hbm, o_hbm):
        def body(i_vmem, o_vmem):
            pltpu.sync_copy(x_hbm.at[i_vmem.at[0]], o_vmem)

        pltpu.emit_pipeline(
            body,
            grid=(n // SC_WINDOW,),
            in_specs=[pl.BlockSpec((1, SC_WINDOW), lambda i: (0, i))],
            out_specs=[pl.BlockSpec((SC_WINDOW, w), lambda i: (i, 0))],
            core_axis_name=("core", "subcore"),
            dimension_semantics=(pltpu.PARALLEL,),
        )(i_hbm, o_hbm)

    return gather(rows, idx)


def _moe_routed(x, mods, row_fn, g, rw_t, rb, wgu, wd, final_g, tm, mb):
    b, s, d = x.shape
    n, nt = b * s, (b * s) // tm
    per_b = s // tm
    half = d // 2
    tile = lambda bi, t: (0, bi * per_b + t, 0, 0)
    tri = jnp.asarray(np.triu(np.ones((tm, tm), np.float32), 1), BF16)
    hp, eidx, rank, gcol, cnt = pl.pallas_call(
        _moe_route_kernel,
        grid=(b, per_b),
        in_specs=[
            pl.BlockSpec((None, tm, d), lambda bi, t: (bi, t, 0)),
            pl.BlockSpec((1, 6, d), lambda bi, t: (row_fn(bi), 0, 0)),
            _full((1, d)), _full((2 * N_EXPERTS, d)), _full((N_EXPERTS, 1)), _full((tm, tm)),
        ],
        out_specs=[
            pl.BlockSpec((None, tm, half), lambda bi, t: (bi, t, 0)),
            pl.BlockSpec((2, None, 1, tm), tile), pl.BlockSpec((2, None, 1, tm), tile),
            pl.BlockSpec((None, tm, LANES), lambda bi, t: (bi, t, 0)),
            _full((N_EXPERTS, 1)),
        ],
        out_shape=[
            jax.ShapeDtypeStruct((b, s, half), jnp.int32),
            jax.ShapeDtypeStruct((2, nt, 1, tm), jnp.int32), jax.ShapeDtypeStruct((2, nt, 1, tm), jnp.int32),
            jax.ShapeDtypeStruct((b, s, LANES), F32),
            jax.ShapeDtypeStruct((N_EXPERTS, 1), F32),
        ],
        scratch_shapes=[pltpu.VMEM((N_EXPERTS, 1), F32)],
        compiler_params=_params(2),
        name="moe_route",
    )(x, mods, g, rw_t, rb, tri)

    n_rows = 2 * n + N_EXPERTS * mb
    nb = n_rows // mb
    nb_pad = -(-nb // LANES) * LANES
    slots, be = pl.pallas_call(
        functools.partial(_moe_slots_kernel, mb),
        out_shape=[jax.ShapeDtypeStruct((2, nt, 1, tm), jnp.int32), jax.ShapeDtypeStruct((1, nb_pad), jnp.int32)],
        name="moe_slots",
    )(eidx, rank, cnt)
    split = half // SC_ROW
    slots = (slots.reshape(2, n, 1) * split + jnp.arange(split, dtype=jnp.int32)).reshape(2, n * split)

    hs = _sc_scatter_rows(hp.reshape(n * split, SC_ROW), slots[0:1], slots[1:2], n_rows * split)
    hs = hs.reshape(n_rows, half)
    ys = pl.pallas_call(
        _moe_ffn_kernel,
        grid_spec=pltpu.PrefetchScalarGridSpec(
            num_scalar_prefetch=1,
            grid=(nb,),
            in_specs=[
                pl.BlockSpec((mb, half), lambda i, be_r: (i, 0)),
                pl.BlockSpec((None, d, 2 * D_EXPERT), lambda i, be_r: (be_r[i], 0, 0)),
                pl.BlockSpec((None, D_EXPERT, d), lambda i, be_r: (be_r[i], 0, 0)),
            ],
            out_specs=pl.BlockSpec((mb, half), lambda i, be_r: (i, 0)),
        ),
        out_shape=jax.ShapeDtypeStruct((n_rows, half), jnp.int32),
        compiler_params=_params(1),
        name="moe_ffn",
    )(be.reshape(nb_pad), hs, wgu, wd)
    z = _sc_gather_rows(ys.reshape(n_rows * split, SC_ROW), slots.reshape(1, 2 * n * split)).reshape(2, b, s, half)

    final_norm = final_g is not None
    in_specs = [
        pl.BlockSpec((None, tm, d), lambda bi, t: (bi, t, 0)),
        pl.BlockSpec((1, 6, d), lambda bi, t: (row_fn(bi), 0, 0)),
        pl.BlockSpec((None, None, tm, half), lambda bi, t: (0, bi, t, 0)),
        pl.BlockSpec((None, None, tm, half), lambda bi, t: (1, bi, t, 0)),
        pl.BlockSpec((None, tm, LANES), lambda bi, t: (bi, t, 0)),
    ]
    args = [x, mods, z, z, gcol]
    if final_norm:
        in_specs.append(_full((1, d)))
        args.append(final_g)
    return pl.pallas_call(
        functools.partial(_moe_combine_kernel, final_norm),
        grid=(b, per_b),
        in_specs=in_specs,
        out_specs=pl.BlockSpec((None, tm, d), lambda bi, t: (bi, t, 0)),
        out_shape=jax.ShapeDtypeStruct((b, s, d), F32),
        compiler_params=_params(2),
        name="moe_combine",
    )(*args)


def _rope_tables(n_tokens, rot_dim, offset):
    n_rows = n_tokens // GRID_W
    rows = np.repeat(np.arange(n_rows), GRID_W)
    cols = np.tile(np.arange(GRID_W), n_rows)
    n_freq = rot_dim // 4
    inv = jnp.asarray(ROPE_THETA, F32) ** (-jnp.arange(n_freq, dtype=F32) / n_freq)
    ang = jnp.concatenate([jnp.asarray(rows, F32)[:, None] * inv, jnp.asarray(cols, F32)[:, None] * inv], axis=-1)
    cos, sin = jnp.repeat(jnp.cos(ang), 2, axis=-1), jnp.repeat(jnp.sin(ang), 2, axis=-1)
    odd = jnp.asarray(np.arange(rot_dim) % 2 == 1)
    parts = [cos, jnp.where(odd, sin, 0.0), jnp.where(odd, 0.0, -sin)]
    period = 64 if rot_dim == 64 else LANES
    fill = [1.0, 0.0, 0.0]
    out = []
    for p, f in zip(parts, fill):
        slot = jnp.full((n_tokens, period), f, F32).at[:, offset:offset + rot_dim].set(p)
        out.append(jnp.tile(slot, (1, LANES // period)))
    return jnp.stack(out)


def _pad_heads(w, n_heads, lo, hi):
    k = w.shape[0]
    w = w.reshape(k, n_heads, -1)[:, :, lo:hi]
    return jnp.pad(w, ((0, 0), (0, 0), (0, LANES - (hi - lo)))).reshape(k, n_heads * LANES)


def _ab_weights(j, ab_w_in, diff_lambda, diff_subln_g, mla_q_norm_g, mla_w_qb, mla_kv_norm_g, mla_w_kvb, ab_w_out):
    place = np.zeros((LANES, MLA_HEADS * LANES), np.float32)
    for hd in range(MLA_HEADS):
        for r in range(MLA_ROPE_DIM):
            place[r, hd * LANES + MLA_NOPE_DIM + r] = 1.0
    w_out_m = ab_w_out[j][512:].reshape(MLA_HEADS, MLA_V_DIM, D_MODEL)
    return {
        "w_in": jnp.pad(ab_w_in[j], ((0, 0), (0, AB_IN_PAD - AB_IN))).astype(BF16),
        "qn_g": mla_q_norm_g[j][None, :],
        "w_qb": _pad_heads(mla_w_qb[j], MLA_HEADS, 0, MLA_NOPE_DIM + MLA_ROPE_DIM).astype(BF16),
        "kvn_g": mla_kv_norm_g[j][None, :],
        "wk": _pad_heads(mla_w_kvb[j], MLA_HEADS, 0, MLA_NOPE_DIM).astype(BF16),
        "wv": _pad_heads(mla_w_kvb[j], MLA_HEADS, MLA_NOPE_DIM, MLA_NOPE_DIM + MLA_V_DIM).astype(BF16),
        "e": jnp.asarray(place, BF16),
        "w_out_d": ab_w_out[j][:512].reshape(DIFF_HEADS, LANES, D_MODEL).astype(BF16),
        "w_out_m": jnp.pad(w_out_m, ((0, 0), (0, LANES - MLA_V_DIM), (0, 0))).astype(BF16),
        "subln_g": diff_subln_g[j][None, :],
        "lam": diff_lambda[j],
    }


def kernel(x_prompt, x_sample, cache_diff_k, cache_diff_v, cache_mla_ckv, cache_mla_krope, cache_swa_k, cache_swa_v,
           c, c_ctx, ada_w, ada_b, norm1_g, norm2_g, final_norm_g, ab_w_in, diff_lambda, diff_subln_g, mla_q_norm_g,
           mla_w_qb, mla_kv_norm_g, mla_w_kvb, ab_w_out, swa_w_in, swa_sink, swa_w_out, router_w, router_bias,
           moe_w_gate, moe_w_up, moe_w_down):
    bp, sp, d = x_prompt.shape
    bs, ss, _ = x_sample.shape
    depth = ada_w.shape[0]
    past = cache_diff_k.shape[2]
    n_ab = cache_diff_k.shape[1]
    n_swa = cache_swa_k.shape[1]
    assert bs + 1 <= COND_ROWS and d == D_MODEL

    cond = jnp.zeros((COND_ROWS, d), F32).at[:bs].set(c).at[bs].set(c_ctx)
    mods = _ada_all(cond, ada_w, ada_b).reshape(depth, COND_ROWS, 6, d)
    row_s = lambda bi: bi
    row_p = lambda bi: bi * 0 + bs

    rope_d = _rope_tables(ss, DIFF_QK_DIM, 0)
    rope_q = _rope_tables(ss, MLA_ROPE_DIM, MLA_NOPE_DIM)
    rope_k = _rope_tables(ss, MLA_ROPE_DIM, 0)

    cdk = cache_diff_k.reshape(bs, n_ab, past, 512)
    cdv = cache_diff_v.reshape(bs, n_ab, past, 512)
    ckr = jnp.pad(cache_mla_krope, ((0, 0), (0, 0), (0, 0), (0, LANES - MLA_ROPE_DIM)))
    csk = cache_swa_k.reshape(bs, n_swa, past, 256)
    csv = cache_swa_v.reshape(bs, n_swa, past, 256)
    rw_hi = router_w.T.astype(BF16)
    rw_t = jnp.concatenate([rw_hi, (router_w.T - rw_hi.astype(F32)).astype(BF16)], axis=0)
    rb = router_bias[:, None]

    tp = min(256, sp)
    ts = min(256, ss)
    tm_s = min(512, ss)
    tm_p = min(512, bp * sp)

    xp, xs = x_prompt, x_sample
    new_ab, new_swa = None, None
    for l in range(depth):
        j = l // 2
        ml = mods[l]
        g1, g2 = norm1_g[l][None, :], norm2_g[l][None, :]
        if l % 2 == 0:
            lambda_init = 0.8 - 0.6 * math.exp(-0.3 * l)
            wts = _ab_weights(j, ab_w_in, diff_lambda, diff_subln_g, mla_q_norm_g, mla_w_qb, mla_kv_norm_g,
                              mla_w_kvb, ab_w_out)
            qa, ka, va, qb, km, vm, *new_ab = _proj_ab(xp, ml, row_p, g1, wts, None, (j, n_ab, new_ab), tp)
            xp = _attn_ab(xp, ml, row_p, (qa, qb), [(ka, va, km, vm)], wts, lambda_init, tp, DIFF_HEADS + MLA_HEADS)
            qa, ka, va, qb, km, vm = _proj_ab(xs, ml, row_s, g1, wts, (rope_d, rope_q, rope_k), None, ts)
            cache_seg = _cache_ab(cdk, cdv, cache_mla_ckv, ckr, j, wts)
            xs = _attn_ab(xs, ml, row_s, (qa, qb), [(ka, va, km, vm), tuple(cache_seg)], wts, lambda_init, ts, 2)
        else:
            w_in = swa_w_in[j].astype(BF16)
            w_out = swa_w_out[j].reshape(8, LANES, d).astype(BF16)
            q, k, v, *new_swa = _proj_swa(xp, ml, row_p, g1, w_in, None, (j, n_swa, new_swa), tp)
            xp = _attn_swa(xp, ml, row_p, q, k, v, None, swa_sink[j], w_out, tp, SWA_HEADS)
            q, k, v = _proj_swa(xs, ml, row_s, g1, w_in, rope_d, None, ts)
            xs = _attn_swa(xs, ml, row_s, q, k, v, (csk, csv, j), swa_sink[j], w_out, ts, 2)
        wgu = jnp.concatenate([moe_w_gate[l], moe_w_up[l]], axis=-1).astype(BF16)
        wd = moe_w_down[l].astype(BF16)
        fg = final_norm_g[None, :] if l == depth - 1 else None
        xp = _moe_routed(xp.reshape(1, bp * sp, d), ml, row_p, g2, rw_t, rb, wgu, wd, fg, tm_p, 256).reshape(bp, sp, d)
        xs = _moe_routed(xs, ml, row_s, g2, rw_t, rb, wgu, wd, fg, tm_s, 512)

    new_diff_k = new_ab[0].reshape(bp, n_ab, sp, DIFF_HEADS, 2, DIFF_QK_DIM)
    new_diff_v = new_ab[1].reshape(bp, n_ab, sp, DIFF_HEADS, DIFF_V_DIM)
    new_mla_ckv, new_mla_krope = new_ab[2], new_ab[3]
    new_swa_k = new_swa[0].reshape(bp, n_swa, sp, SWA_KV_HEADS, SWA_HEAD_DIM)
    new_swa_v = new_swa[1].reshape(bp, n_swa, sp, SWA_KV_HEADS, SWA_HEAD_DIM)
    return (xp, xs, new_diff_k, new_diff_v, new_mla_ckv, new_mla_krope, new_swa_k, new_swa_v)
```

```python
import functools
import math

import jax
import jax.numpy as jnp
import numpy as np
from jax import lax
from jax.experimental import pallas as pl
from jax.experimental.pallas import tpu as pltpu
from jax.experimental.pallas import tpu_sc as plsc

F32 = jnp.float32
BF16 = jnp.bfloat16

D_MODEL = 1024
GRID_W = 64
ROPE_THETA = 10000.0
NORM_EPS = 1e-6
NEG_INF = -1e30
LOG2E = math.log2(math.e)
LANES = 128

DIFF_HEADS = 4
DIFF_QK_DIM = 64
DIFF_V_DIM = 128
MLA_HEADS = 8
MLA_Q_RANK = 384
MLA_KV_RANK = 256
MLA_NOPE_DIM = 64
MLA_ROPE_DIM = 32
MLA_V_DIM = 64
AB_IN = 3 * 512 + MLA_Q_RANK + MLA_KV_RANK + MLA_ROPE_DIM
AB_IN_PAD = 2304
SWA_HEADS = 16
SWA_KV_HEADS = 4
SWA_GROUP = 4
SWA_HEAD_DIM = 64
WINDOW = 128
N_EXPERTS = 16
N_GROUPS = 4
EXPERTS_PER_GROUP = 4
D_EXPERT = 256
COND_ROWS = 16

VMEM_LIMIT = 56 * 1024 * 1024


def _full(shape):
    n = len(shape)
    return pl.BlockSpec(shape, lambda *_: (0,) * n)


def _resident(shape):
    n = len(shape)
    return pl.BlockSpec(shape, lambda *_: (0,) * n, pipeline_mode=pl.Buffered(1))


def _params(n_axes):
    return pltpu.CompilerParams(dimension_semantics=("arbitrary",) * n_axes, vmem_limit_bytes=VMEM_LIMIT)


def _sigmoid(x):
    return 1.0 / (1.0 + jnp.exp(-x))


def _rms(x, g):
    return x * lax.rsqrt(jnp.mean(x * x, axis=-1, keepdims=True) + NORM_EPS) * g


def _modulate(x, g, shift, scale):
    return _rms(x, g) * (1.0 + scale) + shift


def _dot(a, b):
    return jnp.dot(a, b, preferred_element_type=F32)


def _dot_nt(a, b):
    return lax.dot_general(a, b, (((1,), (1,)), ((), ())), preferred_element_type=F32)


def _rope(x, tab_ref):
    c, s_odd, s_even = tab_ref[0], tab_ref[1], tab_ref[2]
    out = []
    for i in range(x.shape[1] // LANES):
        xi = x[:, i * LANES:(i + 1) * LANES]
        out.append(xi * c + pltpu.roll(xi, 1, 1) * s_odd + pltpu.roll(xi, LANES - 1, 1) * s_even)
    return out[0] if len(out) == 1 else jnp.concatenate(out, axis=1)


def _tree(op, xs):
    xs = list(xs)
    while len(xs) > 1:
        xs = [op(xs[i], xs[i + 1]) if i + 1 < len(xs) else xs[i] for i in range(0, len(xs), 2)]
    return xs[0]


def _lane_fold(op, x):
    return _tree(op, [x[:, i * LANES:(i + 1) * LANES] for i in range(x.shape[1] // LANES)])


def _ada_kernel(cond_ref, w_ref, b_ref, o_ref):
    c = cond_ref[...]
    a = (c * _sigmoid(c)).astype(BF16)
    o_ref[...] = _dot(a, w_ref[...].astype(BF16)) + b_ref[...]


def _ada_all(cond, ada_w, ada_b):
    depth, d, n = ada_w.shape
    tn = 1536
    return pl.pallas_call(
        _ada_kernel,
        grid=(depth, n // tn),
        in_specs=[
            _full((COND_ROWS, d)),
            pl.BlockSpec((None, d, tn), lambda l, j: (l, 0, j)),
            pl.BlockSpec((None, 1, tn), lambda l, j: (l, 0, j)),
        ],
        out_specs=pl.BlockSpec((None, COND_ROWS, tn), lambda l, j: (l, 0, j)),
        out_shape=jax.ShapeDtypeStruct((depth, COND_ROWS, n), F32),
        compiler_params=_params(2),
        name="ada_mod",
    )(cond, ada_w, ada_b.reshape(depth, 1, n))


def _proj_ab_kernel(rope, emit_cache, n_prev, x_ref, mod_ref, g_ref, w_in_ref, qn_ref, wqb_ref, kvn_ref, wk_ref,
                    wv_ref, e_ref, *rest):
    if rope:
        rope_d, rope_q, rope_k = rest[:3]
        rest = rest[3:]
    rest = rest[n_prev:]
    qa_o, ka_o, va_o, qb_o, km_o, vm_o = rest[:6]
    mod = mod_ref[0]
    h = _modulate(x_ref[...], g_ref[...], mod[0:1], mod[1:2]).astype(BF16)
    big = _dot(h, w_in_ref[...])
    qa, ka, va = big[:, 0:512], big[:, 512:1024], big[:, 1024:1536]
    q_lat, ckv, kr = big[:, 1536:1920], big[:, 1920:2176], big[:, 2176:2304]
    qb = _dot(_rms(q_lat, qn_ref[...]).astype(BF16), wqb_ref[...])
    ckv_n = _rms(ckv, kvn_ref[...])
    if emit_cache:
        ka32_o, va32_o, ckv32_o, kr32_o = rest[6:10]
        ka32_o[...] = ka
        va32_o[...] = va
        ckv32_o[...] = ckv_n
        kr32_o[...] = kr[:, :MLA_ROPE_DIM]
    if rope:
        qa, ka = _rope(qa, rope_d), _rope(ka, rope_d)
        qb, kr = _rope(qb, rope_q), _rope(kr, rope_k)
    qa = qa * (DIFF_QK_DIM ** -0.5 * LOG2E)
    qb = qb * ((MLA_NOPE_DIM + MLA_ROPE_DIM) ** -0.5 * LOG2E)
    ckv16 = ckv_n.astype(BF16)
    km = _dot(ckv16, wk_ref[...]) + _dot(kr.astype(BF16), e_ref[...])
    vm = _dot(ckv16, wv_ref[...])
    lo = lax.broadcasted_iota(jnp.int32, (qa.shape[0], LANES), 1) < DIFF_QK_DIM
    for hd in range(DIFF_HEADS):
        sl = slice(hd * LANES, (hd + 1) * LANES)
        qa_o[2 * hd] = jnp.where(lo, qa[:, sl], 0.0).astype(BF16)
        qa_o[2 * hd + 1] = jnp.where(lo, 0.0, qa[:, sl]).astype(BF16)
        ka_o[hd] = ka[:, sl].astype(BF16)
        va_o[hd] = va[:, sl].astype(BF16)
    for hd in range(MLA_HEADS):
        sl = slice(hd * LANES, (hd + 1) * LANES)
        qb_o[hd] = qb[:, sl].astype(BF16)
        km_o[hd] = km[:, sl].astype(BF16)
        vm_o[hd] = vm[:, sl].astype(BF16)


def _cache_outputs(cache, widths, b, s, tm, in_specs, args, n_fixed_outs):
    if cache is None:
        return [], {}, 0
    j, n_layers, prev = cache
    outs = [(jax.ShapeDtypeStruct((b, n_layers, s, w), F32),
             pl.BlockSpec((None, None, tm, w), lambda bi, t: (bi, j, t, 0))) for w in widths]
    aliases = {}
    if prev is not None:
        for i, arr in enumerate(prev):
            aliases[len(args)] = n_fixed_outs + i
            in_specs.append(pl.BlockSpec(memory_space=pl.ANY))
            args.append(arr)
    return outs, aliases, len(aliases)


def _proj_ab(x, mods, row_fn, g, wts, rope_tabs, cache, tm):
    b, s, d = x.shape
    rope = rope_tabs is not None
    tok = lambda bi, t: (bi, t, 0)
    head = lambda bi, t: (bi, 0, t, 0)
    in_specs = [
        pl.BlockSpec((None, tm, d), tok),
        pl.BlockSpec((1, 6, d), lambda bi, t: (row_fn(bi), 0, 0)),
        _full((1, d)),
        _full((d, AB_IN_PAD)), _full((1, MLA_Q_RANK)), _full((MLA_Q_RANK, 1024)), _full((1, MLA_KV_RANK)),
        _full((MLA_KV_RANK, 1024)), _full((MLA_KV_RANK, 1024)), _full((LANES, 1024)),
    ]
    args = [x, mods, g, wts["w_in"], wts["qn_g"], wts["w_qb"], wts["kvn_g"], wts["wk"], wts["wv"], wts["e"]]
    if rope:
        in_specs += [pl.BlockSpec((3, tm, LANES), lambda bi, t: (0, t, 0))] * 3
        args += list(rope_tabs)

    def hm(nh):
        return jax.ShapeDtypeStruct((b, nh, s, LANES), BF16), pl.BlockSpec((None, nh, tm, LANES), head)

    outs = [hm(8), hm(4), hm(4), hm(8), hm(8), hm(8)]
    cache_outs, aliases, n_prev = _cache_outputs(cache, (512, 512, MLA_KV_RANK, MLA_ROPE_DIM), b, s, tm, in_specs,
                                                 args, len(outs))
    outs += cache_outs
    return pl.pallas_call(
        functools.partial(_proj_ab_kernel, rope, cache is not None, n_prev),
        grid=(b, s // tm),
        in_specs=in_specs,
        out_specs=[o[1] for o in outs],
        out_shape=[o[0] for o in outs],
        input_output_aliases=aliases,
        compiler_params=_params(2),
        name="proj_ab",
    )(*args)


def _cache_ab_kernel(dk_ref, dv_ref, ckv_ref, kr_ref, wk_ref, wv_ref, e_ref, ck_o, cv_o, km_o, vm_o):
    ckv16 = ckv_ref[...].astype(BF16)
    km = _dot(ckv16, wk_ref[...]) + _dot(kr_ref[...].astype(BF16), e_ref[...])
    vm = _dot(ckv16, wv_ref[...])
    for hd in range(DIFF_HEADS):
        sl = slice(hd * LANES, (hd + 1) * LANES)
        ck_o[hd] = dk_ref[:, sl].astype(BF16)
        cv_o[hd] = dv_ref[:, sl].astype(BF16)
    for hd in range(MLA_HEADS):
        sl = slice(hd * LANES, (hd + 1) * LANES)
        km_o[hd] = km[:, sl].astype(BF16)
        vm_o[hd] = vm[:, sl].astype(BF16)


def _cache_ab(cdk, cdv, cckv, ckr, j, wts):
    b, _, p, _ = cdk.shape
    lay = lambda bi: (bi, j, 0, 0)

    def hm(nh):
        return (jax.ShapeDtypeStruct((b, nh, p, LANES), BF16),
                pl.BlockSpec((None, nh, p, LANES), lambda bi: (bi, 0, 0, 0)))

    outs = [hm(4), hm(4), hm(8), hm(8)]
    return pl.pallas_call(
        _cache_ab_kernel,
        grid=(b,),
        in_specs=[
            pl.BlockSpec((None, None, p, 512), lay), pl.BlockSpec((None, None, p, 512), lay),
            pl.BlockSpec((None, None, p, MLA_KV_RANK), lay), pl.BlockSpec((None, None, p, LANES), lay),
            _full((MLA_KV_RANK, 1024)), _full((MLA_KV_RANK, 1024)), _full((LANES, 1024)),
        ],
        out_specs=[o[1] for o in outs],
        out_shape=[o[0] for o in outs],
        compiler_params=_params(1),
        name="cache_ab",
    )(cdk, cdv, cckv, ckr, wts["wk"], wts["wv"], wts["e"])


def _attn_ab_kernel(n_seg, lambda_init, x_ref, mod_ref, qa_ref, qb_ref, *rest):
    segs = [rest[4 * i:4 * i + 4] for i in range(n_seg)]
    wd_ref, wm_ref, sg_ref, lam_ref, o_ref = rest[4 * n_seg:4 * n_seg + 5]
    bufs = rest[4 * n_seg + 5:]
    lp = lam_ref[...]
    lam = (jnp.exp(jnp.sum(lp[0:1] * lp[1:2], axis=-1, keepdims=True))
           - jnp.exp(jnp.sum(lp[2:3] * lp[3:4], axis=-1, keepdims=True)) + lambda_init)
    tq, d = x_ref.shape
    widths = [sg[0].shape[1] for sg in segs]
    offs = [sum(widths[:i]) for i in range(n_seg)]
    units = [("diff", hd) for hd in range(DIFF_HEADS)] + [("mla", hd) for hd in range(MLA_HEADS)]

    def scores(u, buf):
        kind, hd = units[u]
        qs = [qa_ref[2 * hd], qa_ref[2 * hd + 1]] if kind == "diff" else [qb_ref[hd]]
        ks = [sg[0 if kind == "diff" else 2][hd] for sg in segs]
        ms = []
        for i, q in enumerate(qs):
            folds = []
            for k, off, w in zip(ks, offs, widths):
                s = _dot_nt(q, k)
                buf[i, :, off:off + w] = s
                folds.append(_lane_fold(jnp.maximum, s))
            ms.append(jnp.max(_tree(jnp.maximum, folds), axis=-1, keepdims=True))
        return ms

    def exps(buf, i, m):
        e = [jnp.exp2(buf[i, :, off:off + w] - m) for off, w in zip(offs, widths)]
        l = jnp.sum(_tree(jnp.add, [_lane_fold(jnp.add, x) for x in e]), axis=-1, keepdims=True)
        return e, l

    def consume(u, buf, ms, acc):
        kind, hd = units[u]
        if kind == "diff":
            vs = [sg[1][hd] for sg in segs]
            (e1, l1), (e2, l2) = exps(buf, 0, ms[0]), exps(buf, 1, ms[1])
            c1, c2 = 1.0 / l1, lam / l2
            o = _tree(jnp.add, [_dot((a * c1 - b * c2).astype(BF16), v) for a, b, v in zip(e1, e2, vs)])
            od = _rms(o, sg_ref[...]) * (1.0 - lambda_init)
            return acc + _dot(od.astype(BF16), wd_ref[hd])
        vs = [sg[3][hd] for sg in segs]
        e, l = exps(buf, 0, ms[0])
        o = _tree(jnp.add, [_dot(x.astype(BF16), v) for x, v in zip(e, vs)]) / l
        return acc + _dot(o.astype(BF16), wm_ref[hd])

    acc = jnp.zeros((tq, d), F32)
    ms = scores(0, bufs[0])
    for u in range(len(units)):
        nxt = scores(u + 1, bufs[(u + 1) % len(bufs)]) if u + 1 < len(units) else None
        acc = consume(u, bufs[u % len(bufs)], ms, acc)
        ms = nxt
    o_ref[...] = x_ref[...] + mod_ref[0][2:3] * acc


def _attn_ab(x, mods, row_fn, q_parts, seg_list, wts, lambda_init, tq, n_bufs):
    b, s, d = x.shape
    qa, qb = q_parts
    in_specs = [
        pl.BlockSpec((None, tq, d), lambda bi, t: (bi, t, 0)),
        pl.BlockSpec((1, 6, d), lambda bi, t: (row_fn(bi), 0, 0)),
        pl.BlockSpec((None, 8, tq, LANES), lambda bi, t: (bi, 0, t, 0)),
        pl.BlockSpec((None, 8, tq, LANES), lambda bi, t: (bi, 0, t, 0)),
    ]
    args = [x, mods, qa, qb]
    kv_mode = dict(pipeline_mode=pl.Buffered(1)) if s // tq > 1 else {}
    for seg in seg_list:
        for arr in seg:
            nh, nk = arr.shape[1], arr.shape[2]
            in_specs.append(pl.BlockSpec((None, nh, nk, LANES), lambda bi, t: (bi, 0, 0, 0), **kv_mode))
            args.append(arr)
    in_specs += [_full((DIFF_HEADS, LANES, d)), _full((MLA_HEADS, LANES, d)), _full((1, LANES)),
                 _full((4, DIFF_QK_DIM))]
    args += [wts["w_out_d"], wts["w_out_m"], wts["subln_g"], wts["lam"]]
    n_keys = sum(seg[0].shape[2] for seg in seg_list)
    return pl.pallas_call(
        functools.partial(_attn_ab_kernel, len(seg_list), lambda_init),
        grid=(b, s // tq),
        in_specs=in_specs,
        out_specs=pl.BlockSpec((None, tq, d), lambda bi, t: (bi, t, 0)),
        out_shape=jax.ShapeDtypeStruct((b, s, d), F32),
        scratch_shapes=[pltpu.VMEM((2, tq, n_keys), F32) for _ in range(n_bufs)],
        compiler_params=_params(2),
        name="attn_ab",
    )(*args)


def _proj_swa_kernel(rope, emit_cache, n_prev, x_ref, mod_ref, g_ref, w_in_ref, *rest):
    if rope:
        rope_d = rest[0]
        rest = rest[1:]
    rest = rest[n_prev:]
    q_o, k_o, v_o = rest[:3]
    mod = mod_ref[0]
    h = _modulate(x_ref[...], g_ref[...], mod[0:1], mod[1:2]).astype(BF16)
    big = _dot(h, w_in_ref[...])
    q, k, v = big[:, :1024], big[:, 1024:1280], big[:, 1280:1536]
    if emit_cache:
        k32_o, v32_o = rest[3:5]
        k32_o[...] = k
        v32_o[...] = v
    if rope:
        q, k = _rope(q, rope_d), _rope(k, rope_d)
    q = q * (SWA_HEAD_DIM ** -0.5 * LOG2E)
    lo = lax.broadcasted_iota(jnp.int32, (q.shape[0], LANES), 1) < SWA_HEAD_DIM
    for pair in range(2):
        for grp in range(SWA_GROUP):
            ca, cb = (2 * pair) * 2 + grp // 2, (2 * pair + 1) * 2 + grp // 2
            a = q[:, ca * LANES:(ca + 1) * LANES]
            bb = q[:, cb * LANES:(cb + 1) * LANES]
            if grp % 2 == 0:
                bb = pltpu.roll(bb, SWA_HEAD_DIM, 1)
            else:
                a = pltpu.roll(a, SWA_HEAD_DIM, 1)
            q_o[pair * SWA_GROUP + grp] = jnp.where(lo, a, bb).astype(BF16)
    for kvh in range(SWA_KV_HEADS):
        sl = slice((kvh // 2) * LANES, (kvh // 2 + 1) * LANES)
        keep = lo if kvh % 2 == 0 else jnp.logical_not(lo)
        k_o[kvh] = jnp.where(keep, k[:, sl], 0.0).astype(BF16)
        v_o[kvh] = jnp.where(keep, v[:, sl], 0.0).astype(BF16)


def _proj_swa(x, mods, row_fn, g, w_in, rope_tab, cache, tm):
    b, s, d = x.shape
    rope = rope_tab is not None
    tok = lambda bi, t: (bi, t, 0)
    head = lambda bi, t: (bi, 0, t, 0)
    in_specs = [
        pl.BlockSpec((None, tm, d), tok),
        pl.BlockSpec((1, 6, d), lambda bi, t: (row_fn(bi), 0, 0)),
        _full((1, d)), _full((d, 1536)),
    ]
    args = [x, mods, g, w_in]
    if rope:
        in_specs.append(pl.BlockSpec((3, tm, LANES), lambda bi, t: (0, t, 0)))
        args.append(rope_tab)

    def hm(nh):
        return jax.ShapeDtypeStruct((b, nh, s, LANES), BF16), pl.BlockSpec((None, nh, tm, LANES), head)

    outs = [hm(8), hm(4), hm(4)]
    cache_outs, aliases, n_prev = _cache_outputs(cache, (256, 256), b, s, tm, in_specs, args, len(outs))
    outs += cache_outs
    return pl.pallas_call(
        functools.partial(_proj_swa_kernel, rope, cache is not None, n_prev),
        grid=(b, s // tm),
        in_specs=in_specs,
        out_specs=[o[1] for o in outs],
        out_shape=[o[0] for o in outs],
        input_output_aliases=aliases,
        compiler_params=_params(2),
        name="proj_swa",
    )(*args)


def _attn_swa_kernel(windowed, band_w, x_ref, mod_ref, q_ref, k_ref, v_ref, *rest):
    if windowed:
        ck_ref, cv_ref = rest[:2]
        rest = rest[2:]
    sink_ref, w_ref, o_ref = rest[:3]
    bufs = rest[3:]
    tq, d = x_ref.shape
    n_keys = k_ref.shape[1]
    lo_k = lax.broadcasted_iota(jnp.int32, (1, LANES), 1) < SWA_HEAD_DIM
    if windowed:
        start = pl.program_id(1) * tq
        bstart = pl.multiple_of(jnp.clip(start - WINDOW, 0, n_keys - band_w), LANES)
        qpos = start + lax.broadcasted_iota(jnp.int32, (tq, band_w), 0)
        kpos = bstart + lax.broadcasted_iota(jnp.int32, (tq, band_w), 1)
        bias = jnp.where(jnp.abs(qpos - kpos) <= WINDOW, 0.0, NEG_INF)

    def keys_values(kvh):
        if not windowed:
            return [k_ref[kvh]], [v_ref[kvh]]
        keep = lo_k if kvh % 2 == 0 else jnp.logical_not(lo_k)
        sl = slice((kvh // 2) * LANES, (kvh // 2 + 1) * LANES)
        kc = jnp.where(keep, ck_ref[:, sl], 0.0).astype(BF16)
        vc = jnp.where(keep, cv_ref[:, sl], 0.0).astype(BF16)
        return ([k_ref[kvh, pl.ds(bstart, band_w), :], kc], [v_ref[kvh, pl.ds(bstart, band_w), :], vc])

    units = [(kvh, g) for kvh in range(SWA_KV_HEADS) for g in range(SWA_GROUP)]

    def scores(u, buf):
        kvh, g = units[u]
        q = q_ref[(kvh // 2) * SWA_GROUP + g]
        sink = sink_ref[kvh * SWA_GROUP + g] * LOG2E
        folds, off = [], 0
        for i, k in enumerate(keys_values(kvh)[0]):
            s = _dot_nt(q, k)
            if windowed and i == 0:
                s = s + bias
            buf[:, off:off + s.shape[1]] = s
            folds.append(_lane_fold(jnp.maximum, s))
            off += s.shape[1]
        m = jnp.maximum(jnp.max(_tree(jnp.maximum, folds), axis=-1, keepdims=True), sink)
        return m, sink

    def consume(u, buf, m, sink):
        vs = keys_values(units[u][0])[1]
        e, off = [], 0
        for v in vs:
            e.append(jnp.exp2(buf[:, off:off + v.shape[0]] - m))
            off += v.shape[0]
        l = jnp.sum(_tree(jnp.add, [_lane_fold(jnp.add, x) for x in e]), axis=-1, keepdims=True) + jnp.exp2(sink - m)
        return _tree(jnp.add, [_dot(x.astype(BF16), v) for x, v in zip(e, vs)]) / l

    acc = jnp.zeros((tq, d), F32)
    nxt = scores(0, bufs[0])
    prev = None
    for u, (kvh, g) in enumerate(units):
        cur = nxt
        if u + 1 < len(units):
            nxt = scores(u + 1, bufs[(u + 1) % len(bufs)])
        o = consume(u, bufs[u % len(bufs)], *cur)
        if g % 2 == 0:
            prev = o
            continue
        if kvh % 2 == 0:
            slab = prev + pltpu.roll(o, SWA_HEAD_DIM, 1)
        else:
            slab = pltpu.roll(prev, SWA_HEAD_DIM, 1) + o
        acc = acc + _dot(slab.astype(BF16), w_ref[kvh * 2 + g // 2])
    o_ref[...] = x_ref[...] + mod_ref[0][2:3] * acc


def _attn_swa(x, mods, row_fn, q, k, v, ctx, sink, w_out, tq, n_bufs):
    b, s, d = x.shape
    windowed = ctx is not None
    band_w = min(tq + 2 * WINDOW, s)
    kv_spec = pl.BlockSpec((None, SWA_KV_HEADS, s, LANES), lambda bi, t: (bi, 0, 0, 0))
    in_specs = [
        pl.BlockSpec((None, tq, d), lambda bi, t: (bi, t, 0)),
        pl.BlockSpec((1, 6, d), lambda bi, t: (row_fn(bi), 0, 0)),
        pl.BlockSpec((None, 8, tq, LANES), lambda bi, t: (bi, 0, t, 0)),
        kv_spec, kv_spec,
    ]
    args = [x, mods, q, k, v]
    if windowed:
        ck, cv, j = ctx
        p = ck.shape[2]
        spec = pl.BlockSpec((None, None, p, 256), lambda bi, t: (bi, j, 0, 0))
        in_specs += [spec, spec]
        args += [ck, cv]
    in_specs += [pl.BlockSpec(memory_space=pltpu.SMEM), _full((8, LANES, d))]
    args += [sink, w_out]
    n_keys = band_w + ctx[0].shape[2] if windowed else s
    return pl.pallas_call(
        functools.partial(_attn_swa_kernel, windowed, band_w),
        grid=(b, s // tq),
        in_specs=in_specs,
        out_specs=pl.BlockSpec((None, tq, d), lambda bi, t: (bi, t, 0)),
        out_shape=jax.ShapeDtypeStruct((b, s, d), F32),
        scratch_shapes=[pltpu.VMEM((tq, n_keys), F32) for _ in range(n_bufs)],
        compiler_params=_params(2),
        name="attn_swa",
    )(*args)


def _route(scores_t, bias):
    sel_t = scores_t + bias
    sel = [sel_t[e:e + 1] for e in range(N_EXPERTS)]
    raw = [scores_t[e:e + 1] for e in range(N_EXPERTS)]
    gscore = []
    for g in range(N_GROUPS):
        r = sel[g * 4:g * 4 + 4]
        pairs = [r[i] + r[j] for i in range(4) for j in range(i + 1, 4)]
        gscore.append(functools.reduce(jnp.maximum, pairs))
    best, gidx = gscore[0], jnp.zeros_like(gscore[0], dtype=jnp.int32)
    for g in range(1, N_GROUPS):
        take = gscore[g] > best
        best = jnp.where(take, gscore[g], best)
        gidx = jnp.where(take, g, gidx)
    vals = []
    for k in range(EXPERTS_PER_GROUP):
        v = sel[k]
        for g in range(1, N_GROUPS):
            v = jnp.where(gidx == g, sel[g * 4 + k], v)
        vals.append(v)

    def argmax4(vs):
        m, idx = vs[0], jnp.zeros_like(gidx)
        for k in range(1, 4):
            take = vs[k] > m
            m = jnp.where(take, vs[k], m)
            idx = jnp.where(take, k, idx)
        return idx

    i1 = argmax4(vals)
    i2 = argmax4([jnp.where(i1 == k, -jnp.inf, vals[k]) for k in range(4)])
    e1, e2 = gidx * 4 + i1, gidx * 4 + i2
    w1 = functools.reduce(lambda a, b: a + b, [jnp.where(e1 == e, raw[e], 0.0) for e in range(N_EXPERTS)])
    w2 = functools.reduce(lambda a, b: a + b, [jnp.where(e2 == e, raw[e], 0.0) for e in range(N_EXPERTS)])
    den = w1 + w2
    return e1, e2, w1 / den, w2 / den


def _gate_rows(e1, e2, g1, g2):
    rows = [jnp.where(e1 == e, g1, 0.0) + jnp.where(e2 == e, g2, 0.0) for e in range(N_EXPERTS)]
    return jnp.concatenate(rows, axis=0)


def _moe_kernel(final_norm, x_ref, mod_ref, g_ref, rw_ref, rb_ref, wgu_ref, wd_ref, *rest):
    if final_norm:
        fg_ref, o_ref, gu0_ref, gu1_ref = rest
    else:
        o_ref, gu0_ref, gu1_ref = rest
    tm, d = x_ref.shape
    mod = mod_ref[0]
    x = x_ref[...]
    h = _modulate(x, g_ref[...], mod[3:4], mod[4:5])
    h16 = h.astype(BF16)
    h_lo = (h - h16.astype(F32)).astype(BF16)
    part = _dot_nt(rw_ref[...], h16)
    logits_t = part[:N_EXPERTS] + part[N_EXPERTS:] + _dot_nt(rw_ref[:N_EXPERTS], h_lo)
    gates_t = _gate_rows(*_route(_sigmoid(logits_t), rb_ref[...]))
    gates = jnp.concatenate([gates_t, jnp.zeros((LANES - N_EXPERTS, tm), F32)], axis=0).T
    lane = lax.broadcasted_iota(jnp.int32, (tm, LANES), 1)

    def gate_up(e, buf):
        buf[...] = _dot(h16, wgu_ref[e])

    def down(e, buf, acc):
        gate = jnp.sum(jnp.where(lane == e, gates, 0.0), axis=-1, keepdims=True)
        g, u = buf[:, :D_EXPERT], buf[:, D_EXPERT:]
        a = g * _sigmoid(g) * u * gate
        return acc + _dot(a.astype(BF16), wd_ref[e])

    bufs = (gu0_ref, gu1_ref)
    acc = jnp.zeros((tm, d), F32)
    gate_up(0, bufs[0])
    for e in range(N_EXPERTS):
        if e + 1 < N_EXPERTS:
            gate_up(e + 1, bufs[(e + 1) % 2])
        acc = down(e, bufs[e % 2], acc)
    y = x + mod[5:6] * acc
    if final_norm:
        y = _rms(y, fg_ref[...])
    o_ref[...] = y


def _moe(x, mods, row_fn, g, rw_t, rb, wgu, wd, final_g, tm):
    b, s, d = x.shape
    final_norm = final_g is not None
    in_specs = [
        pl.BlockSpec((None, tm, d), lambda bi, t: (bi, t, 0)),
        pl.BlockSpec((1, 6, d), lambda bi, t: (row_fn(bi), 0, 0)),
        _full((1, d)), _full((2 * N_EXPERTS, d)), _full((N_EXPERTS, 1)),
        _resident((N_EXPERTS, d, 2 * D_EXPERT)), _resident((N_EXPERTS, D_EXPERT, d)),
    ]
    args = [x, mods, g, rw_t, rb, wgu, wd]
    if final_norm:
        in_specs.append(_full((1, d)))
        args.append(final_g)
    return pl.pallas_call(
        functools.partial(_moe_kernel, final_norm),
        grid=(b, s // tm),
        in_specs=in_specs,
        out_specs=pl.BlockSpec((None, tm, d), lambda bi, t: (bi, t, 0)),
        out_shape=jax.ShapeDtypeStruct((b, s, d), F32),
        scratch_shapes=[pltpu.VMEM((tm, 2 * D_EXPERT), F32) for _ in range(2)],
        compiler_params=_params(2),
        name="moe",
    )(*args)


SC_WINDOW = 128
SC_ROW = 128
N_PLANES = (D_MODEL // 2) // SC_ROW
HALF_WORD = -65536


def _to_planes(packed, out_ref):
    for j in range(N_PLANES):
        out_ref[j] = packed[:, j * SC_ROW:(j + 1) * SC_ROW]


def _from_planes(ref):
    return jnp.concatenate([ref[j] for j in range(N_PLANES)], axis=1)


def _pack_bf16_pairs(x):
    n = x.shape[1] // 2
    lo = pltpu.bitcast(x[:, :n].astype(BF16).astype(F32), jnp.int32)
    hi = pltpu.bitcast(x[:, n:].astype(BF16).astype(F32), jnp.int32)
    return lax.shift_right_logical(lo, jnp.int32(16)) | (hi & jnp.int32(HALF_WORD))


def _unpack_bf16_pairs(p):
    lo = pltpu.bitcast(lax.shift_left(p, jnp.int32(16)), F32)
    hi = pltpu.bitcast(p & jnp.int32(HALF_WORD), F32)
    return jnp.concatenate([lo, hi], axis=1)


def _pick(idx, rows):
    return _tree(jnp.add, [jnp.where(idx == e, rows[e], 0.0) for e in range(N_EXPERTS)])


def _moe_route_kernel(x_ref, mod_ref, g_ref, rw_ref, rb_ref, tri_ref, hp_o, eidx_o, rank_o, gcol_o, cnt_o, seen):
    @pl.when(jnp.logical_and(pl.program_id(0) == 0, pl.program_id(1) == 0))
    def _():
        seen[...] = jnp.zeros_like(seen)

    tm = x_ref.shape[0]
    mod = mod_ref[0]
    h = _modulate(x_ref[...], g_ref[...], mod[3:4], mod[4:5])
    h16 = h.astype(BF16)
    _to_planes(_pack_bf16_pairs(h), hp_o)
    h_lo = (h - h16.astype(F32)).astype(BF16)
    part = _dot_nt(rw_ref[...], h16)
    logits_t = part[:N_EXPERTS] + part[N_EXPERTS:] + _dot_nt(rw_ref[:N_EXPERTS], h_lo)
    e1, e2, g1, g2 = _route(_sigmoid(logits_t), rb_ref[...])
    hot = jnp.concatenate([jnp.where(jnp.logical_or(e1 == e, e2 == e), 1.0, 0.0) for e in range(N_EXPERTS)], axis=0)
    before = _dot(hot.astype(BF16), tri_ref[...]) + seen[...]
    rows = [before[e:e + 1] for e in range(N_EXPERTS)]
    eidx_o[0], eidx_o[1] = e1, e2
    rank_o[0], rank_o[1] = _pick(e1, rows).astype(jnp.int32), _pick(e2, rows).astype(jnp.int32)
    gcol_o[...] = jnp.concatenate([g1, g2, jnp.zeros((LANES - 2, tm), F32)], axis=0).T
    seen[...] = seen[...] + jnp.sum(hot, axis=-1, keepdims=True)
    cnt_o[...] = seen[...]


def _moe_slots_kernel(mb, n_rows, eidx_ref, rank_ref, cnt_ref, idx_o, be_o):
    cnt = cnt_ref[...]
    padded = jnp.ceil(cnt / mb) * mb
    starts, ends, run = [], [], jnp.zeros((1, 1), F32)
    for e in range(N_EXPERTS):
        starts.append(run)
        run = run + padded[e:e + 1]
        ends.append(run)
    n_tiles, per_tile = eidx_ref.shape[1], eidx_ref.shape[3] // SC_WINDOW
    for k in range(2):
        for i in range(n_tiles):
            slot = rank_ref[k, i] + _pick(eidx_ref[k, i], starts).astype(jnp.int32)
            for j in range(N_PLANES):
                for q in range(per_tile):
                    row = ((k * N_PLANES + j) * n_tiles + i) * per_tile + q
                    idx_o[row:row + 1, :] = slot[:, q * SC_WINDOW:(q + 1) * SC_WINDOW] + j * n_rows
    first_row = lax.broadcasted_iota(jnp.int32, be_o.shape, 1).astype(F32) * mb
    be = _tree(jnp.add, [jnp.where(ends[e] <= first_row, 1.0, 0.0) for e in range(N_EXPERTS)])
    be_o[...] = jnp.minimum(be, N_EXPERTS - 1.0).astype(jnp.int32)


def _moe_ffn_kernel(be_ref, hs_ref, wgu_ref, wd_ref, ys_o):
    del be_ref
    h = _unpack_bf16_pairs(_from_planes(hs_ref)).astype(BF16)
    gu = _dot(h, wgu_ref[...])
    g, u = gu[:, :D_EXPERT], gu[:, D_EXPERT:]
    a = g * _sigmoid(g) * u
    _to_planes(_pack_bf16_pairs(_dot(a.astype(BF16), wd_ref[...])), ys_o)


def _moe_combine_kernel(final_norm, x_ref, mod_ref, z0_ref, z1_ref, gcol_ref, *rest):
    if final_norm:
        fg_ref, o_ref = rest
    else:
        (o_ref,) = rest
    gc = gcol_ref[...]
    lane = lax.broadcasted_iota(jnp.int32, gc.shape, 1)
    g1 = jnp.sum(jnp.where(lane == 0, gc, 0.0), axis=-1, keepdims=True)
    g2 = jnp.sum(jnp.where(lane == 1, gc, 0.0), axis=-1, keepdims=True)
    y = g1 * _unpack_bf16_pairs(_from_planes(z0_ref)) + g2 * _unpack_bf16_pairs(_from_planes(z1_ref))
    out = x_ref[...] + mod_ref[0][5:6] * y
    if final_norm:
        out = _rms(out, fg_ref[...])
    o_ref[...] = out


def _sc_mesh():
    return plsc.VectorSubcoreMesh(core_axis_name="core", subcore_axis_name="subcore")


def _sc_scatter_rows(rows, idx, n_out):
    n, w = rows.shape
    steps = n // SC_WINDOW

    @pl.kernel(out_type=jax.ShapeDtypeStruct((n_out, w), rows.dtype), mesh=_sc_mesh(), scratch_types=[])
    def scatter(x_hbm, i0_hbm, i1_hbm, o_hbm):
        def body(x_vmem, i0_vmem, i1_vmem):
            pltpu.sync_copy(x_vmem, o_hbm.at[i0_vmem.at[0]])
            pltpu.sync_copy(x_vmem, o_hbm.at[i1_vmem.at[0]])

        pltpu.emit_pipeline(
            body,
            grid=(steps,),
            in_specs=[pl.BlockSpec((SC_WINDOW, w), lambda i: (i, 0)),
                      pl.BlockSpec((1, SC_WINDOW), lambda i: (i, 0)),
                      pl.BlockSpec((1, SC_WINDOW), lambda i: (i + steps, 0))],
            out_specs=[],
            core_axis_name=("core", "subcore"),
            dimension_semantics=(pltpu.PARALLEL,),
        )(x_hbm, i0_hbm, i1_hbm)

    return scatter(rows, idx, idx)


def _sc_gather_rows(rows, idx):
    steps, w = idx.shape[0], rows.shape[1]

    @pl.kernel(out_type=jax.ShapeDtypeStruct((steps * SC_WINDOW, w), rows.dtype), mesh=_sc_mesh(),
               scratch_types=[])
    def gather(x_hbm, i_hbm, o_hbm):
        def body(i_vmem, o_vmem):
            pltpu.sync_copy(x_hbm.at[i_vmem.at[0]], o_vmem)

        pltpu.emit_pipeline(
            body,
            grid=(steps,),
            in_specs=[pl.BlockSpec((1, SC_WINDOW), lambda i: (i, 0))],
            out_specs=[pl.BlockSpec((SC_WINDOW, w), lambda i: (i, 0))],
            core_axis_name=("core", "subcore"),
            dimension_semantics=(pltpu.PARALLEL,),
        )(i_hbm, o_hbm)

    return gather(rows, idx)


def _moe_routed(x, mods, row_fn, g, rw_t, rb, wgu, wd, final_g, tm, mb):
    b, s, d = x.shape
    n, nt = b * s, (b * s) // tm
    per_b = s // tm
    half = d // 2
    tile = lambda bi, t: (0, bi * per_b + t, 0, 0)
    tri = jnp.asarray(np.triu(np.ones((tm, tm), np.float32), 1), BF16)
    hp, eidx, rank, gcol, cnt = pl.pallas_call(
        _moe_route_kernel,
        grid=(b, per_b),
        in_specs=[
            pl.BlockSpec((None, tm, d), lambda bi, t: (bi, t, 0)),
            pl.BlockSpec((1, 6, d), lambda bi, t: (row_fn(bi), 0, 0)),
            _full((1, d)), _full((2 * N_EXPERTS, d)), _full((N_EXPERTS, 1)), _full((tm, tm)),
        ],
        out_specs=[
            pl.BlockSpec((N_PLANES, tm, SC_ROW), lambda bi, t: (0, bi * per_b + t, 0)),
            pl.BlockSpec((2, None, 1, tm), tile), pl.BlockSpec((2, None, 1, tm), tile),
            pl.BlockSpec((None, tm, LANES), lambda bi, t: (bi, t, 0)),
            _full((N_EXPERTS, 1)),
        ],
        out_shape=[
            jax.ShapeDtypeStruct((N_PLANES, n, SC_ROW), jnp.int32),
            jax.ShapeDtypeStruct((2, nt, 1, tm), jnp.int32), jax.ShapeDtypeStruct((2, nt, 1, tm), jnp.int32),
            jax.ShapeDtypeStruct((b, s, LANES), F32),
            jax.ShapeDtypeStruct((N_EXPERTS, 1), F32),
        ],
        scratch_shapes=[pltpu.VMEM((N_EXPERTS, 1), F32)],
        compiler_params=_params(2),
        name="moe_route",
    )(x, mods, g, rw_t, rb, tri)

    n_rows = 2 * n + N_EXPERTS * mb
    nb = n_rows // mb
    nb_pad = -(-nb // LANES) * LANES
    idx, be = pl.pallas_call(
        functools.partial(_moe_slots_kernel, mb, n_rows),
        out_shape=[jax.ShapeDtypeStruct((2 * N_PLANES * n // SC_WINDOW, SC_WINDOW), jnp.int32),
                   jax.ShapeDtypeStruct((1, nb_pad), jnp.int32)],
        name="moe_slots",
    )(eidx, rank, cnt)

    hs = _sc_scatter_rows(hp.reshape(N_PLANES * n, SC_ROW), idx, N_PLANES * n_rows)
    plane_block = pl.BlockSpec((N_PLANES, mb, SC_ROW), lambda i, be_r: (0, i, 0))
    ys = pl.pallas_call(
        _moe_ffn_kernel,
        grid_spec=pltpu.PrefetchScalarGridSpec(
            num_scalar_prefetch=1,
            grid=(nb,),
            in_specs=[
                plane_block,
                pl.BlockSpec((None, d, 2 * D_EXPERT), lambda i, be_r: (be_r[i], 0, 0)),
                pl.BlockSpec((None, D_EXPERT, d), lambda i, be_r: (be_r[i], 0, 0)),
            ],
            out_specs=plane_block,
        ),
        out_shape=jax.ShapeDtypeStruct((N_PLANES, n_rows, SC_ROW), jnp.int32),
        compiler_params=_params(1),
        name="moe_ffn",
    )(be.reshape(nb_pad), hs.reshape(N_PLANES, n_rows, SC_ROW), wgu, wd)
    z = _sc_gather_rows(ys.reshape(N_PLANES * n_rows, SC_ROW), idx).reshape(2, N_PLANES, b, s, SC_ROW)

    final_norm = final_g is not None
    in_specs = [
        pl.BlockSpec((None, tm, d), lambda bi, t: (bi, t, 0)),
        pl.BlockSpec((1, 6, d), lambda bi, t: (row_fn(bi), 0, 0)),
        pl.BlockSpec((None, N_PLANES, None, tm, SC_ROW), lambda bi, t: (0, 0, bi, t, 0)),
        pl.BlockSpec((None, N_PLANES, None, tm, SC_ROW), lambda bi, t: (1, 0, bi, t, 0)),
        pl.BlockSpec((None, tm, LANES), lambda bi, t: (bi, t, 0)),
    ]
    args = [x, mods, z, z, gcol]
    if final_norm:
        in_specs.append(_full((1, d)))
        args.append(final_g)
    return pl.pallas_call(
        functools.partial(_moe_combine_kernel, final_norm),
        grid=(b, per_b),
        in_specs=in_specs,
        out_specs=pl.BlockSpec((None, tm, d), lambda bi, t: (bi, t, 0)),
        out_shape=jax.ShapeDtypeStruct((b, s, d), F32),
        compiler_params=_params(2),
        name="moe_combine",
    )(*args)


def _rope_tables(n_tokens, rot_dim, offset):
    n_rows = n_tokens // GRID_W
    rows = np.repeat(np.arange(n_rows), GRID_W)
    cols = np.tile(np.arange(GRID_W), n_rows)
    n_freq = rot_dim // 4
    inv = jnp.asarray(ROPE_THETA, F32) ** (-jnp.arange(n_freq, dtype=F32) / n_freq)
    ang = jnp.concatenate([jnp.asarray(rows, F32)[:, None] * inv, jnp.asarray(cols, F32)[:, None] * inv], axis=-1)
    cos, sin = jnp.repeat(jnp.cos(ang), 2, axis=-1), jnp.repeat(jnp.sin(ang), 2, axis=-1)
    odd = jnp.asarray(np.arange(rot_dim) % 2 == 1)
    parts = [cos, jnp.where(odd, sin, 0.0), jnp.where(odd, 0.0, -sin)]
    period = 64 if rot_dim == 64 else LANES
    fill = [1.0, 0.0, 0.0]
    out = []
    for p, f in zip(parts, fill):
        slot = jnp.full((n_tokens, period), f, F32).at[:, offset:offset + rot_dim].set(p)
        out.append(jnp.tile(slot, (1, LANES // period)))
    return jnp.stack(out)


def _pad_heads(w, n_heads, lo, hi):
    k = w.shape[0]
    w = w.reshape(k, n_heads, -1)[:, :, lo:hi]
    return jnp.pad(w, ((0, 0), (0, 0), (0, LANES - (hi - lo)))).reshape(k, n_heads * LANES)


def _ab_weights(j, ab_w_in, diff_lambda, diff_subln_g, mla_q_norm_g, mla_w_qb, mla_kv_norm_g, mla_w_kvb, ab_w_out):
    place = np.zeros((LANES, MLA_HEADS * LANES), np.float32)
    for hd in range(MLA_HEADS):
        for r in range(MLA_ROPE_DIM):
            place[r, hd * LANES + MLA_NOPE_DIM + r] = 1.0
    w_out_m = ab_w_out[j][512:].reshape(MLA_HEADS, MLA_V_DIM, D_MODEL)
    return {
        "w_in": jnp.pad(ab_w_in[j], ((0, 0), (0, AB_IN_PAD - AB_IN))).astype(BF16),
        "qn_g": mla_q_norm_g[j][None, :],
        "w_qb": _pad_heads(mla_w_qb[j], MLA_HEADS, 0, MLA_NOPE_DIM + MLA_ROPE_DIM).astype(BF16),
        "kvn_g": mla_kv_norm_g[j][None, :],
        "wk": _pad_heads(mla_w_kvb[j], MLA_HEADS, 0, MLA_NOPE_DIM).astype(BF16),
        "wv": _pad_heads(mla_w_kvb[j], MLA_HEADS, MLA_NOPE_DIM, MLA_NOPE_DIM + MLA_V_DIM).astype(BF16),
        "e": jnp.asarray(place, BF16),
        "w_out_d": ab_w_out[j][:512].reshape(DIFF_HEADS, LANES, D_MODEL).astype(BF16),
        "w_out_m": jnp.pad(w_out_m, ((0, 0), (0, LANES - MLA_V_DIM), (0, 0))).astype(BF16),
        "subln_g": diff_subln_g[j][None, :],
        "lam": diff_lambda[j],
    }


def kernel(x_prompt, x_sample, cache_diff_k, cache_diff_v, cache_mla_ckv, cache_mla_krope, cache_swa_k, cache_swa_v,
           c, c_ctx, ada_w, ada_b, norm1_g, norm2_g, final_norm_g, ab_w_in, diff_lambda, diff_subln_g, mla_q_norm_g,
           mla_w_qb, mla_kv_norm_g, mla_w_kvb, ab_w_out, swa_w_in, swa_sink, swa_w_out, router_w, router_bias,
           moe_w_gate, moe_w_up, moe_w_down):
    bp, sp, d = x_prompt.shape
    bs, ss, _ = x_sample.shape
    depth = ada_w.shape[0]
    past = cache_diff_k.shape[2]
    n_ab = cache_diff_k.shape[1]
    n_swa = cache_swa_k.shape[1]
    assert bs + 1 <= COND_ROWS and d == D_MODEL

    cond = jnp.zeros((COND_ROWS, d), F32).at[:bs].set(c).at[bs].set(c_ctx)
    mods = _ada_all(cond, ada_w, ada_b).reshape(depth, COND_ROWS, 6, d)
    row_s = lambda bi: bi
    row_p = lambda bi: bi * 0 + bs

    rope_d = _rope_tables(ss, DIFF_QK_DIM, 0)
    rope_q = _rope_tables(ss, MLA_ROPE_DIM, MLA_NOPE_DIM)
    rope_k = _rope_tables(ss, MLA_ROPE_DIM, 0)

    cdk = cache_diff_k.reshape(bs, n_ab, past, 512)
    cdv = cache_diff_v.reshape(bs, n_ab, past, 512)
    ckr = jnp.pad(cache_mla_krope, ((0, 0), (0, 0), (0, 0), (0, LANES - MLA_ROPE_DIM)))
    csk = cache_swa_k.reshape(bs, n_swa, past, 256)
    csv = cache_swa_v.reshape(bs, n_swa, past, 256)
    rw_hi = router_w.T.astype(BF16)
    rw_t = jnp.concatenate([rw_hi, (router_w.T - rw_hi.astype(F32)).astype(BF16)], axis=0)
    rb = router_bias[:, None]

    tp = min(256, sp)
    ts = min(256, ss)
    tm_s = min(512, ss)
    tm_p = min(512, bp * sp)

    xp, xs = x_prompt, x_sample
    new_ab, new_swa = None, None
    for l in range(depth):
        j = l // 2
        ml = mods[l]
        g1, g2 = norm1_g[l][None, :], norm2_g[l][None, :]
        if l % 2 == 0:
            lambda_init = 0.8 - 0.6 * math.exp(-0.3 * l)
            wts = _ab_weights(j, ab_w_in, diff_lambda, diff_subln_g, mla_q_norm_g, mla_w_qb, mla_kv_norm_g,
                              mla_w_kvb, ab_w_out)
            qa, ka, va, qb, km, vm, *new_ab = _proj_ab(xp, ml, row_p, g1, wts, None, (j, n_ab, new_ab), tp)
            xp = _attn_ab(xp, ml, row_p, (qa, qb), [(ka, va, km, vm)], wts, lambda_init, tp, DIFF_HEADS + MLA_HEADS)
            qa, ka, va, qb, km, vm = _proj_ab(xs, ml, row_s, g1, wts, (rope_d, rope_q, rope_k), None, ts)
            cache_seg = _cache_ab(cdk, cdv, cache_mla_ckv, ckr, j, wts)
            xs = _attn_ab(xs, ml, row_s, (qa, qb), [(ka, va, km, vm), tuple(cache_seg)], wts, lambda_init, ts, 2)
        else:
            w_in = swa_w_in[j].astype(BF16)
            w_out = swa_w_out[j].reshape(8, LANES, d).astype(BF16)
            q, k, v, *new_swa = _proj_swa(xp, ml, row_p, g1, w_in, None, (j, n_swa, new_swa), tp)
            xp = _attn_swa(xp, ml, row_p, q, k, v, None, swa_sink[j], w_out, tp, SWA_HEADS)
            q, k, v = _proj_swa(xs, ml, row_s, g1, w_in, rope_d, None, ts)
            xs = _attn_swa(xs, ml, row_s, q, k, v, (csk, csv, j), swa_sink[j], w_out, ts, 2)
        wgu = jnp.concatenate([moe_w_gate[l], moe_w_up[l]], axis=-1).astype(BF16)
        wd = moe_w_down[l].astype(BF16)
        fg = final_norm_g[None, :] if l == depth - 1 else None
        xp = _moe_routed(xp.reshape(1, bp * sp, d), ml, row_p, g2, rw_t, rb, wgu, wd, fg, tm_p, 256).reshape(bp, sp, d)
        xs = _moe_routed(xs, ml, row_s, g2, rw_t, rb, wgu, wd, fg, tm_s, 512)

    new_diff_k = new_ab[0].reshape(bp, n_ab, sp, DIFF_HEADS, 2, DIFF_QK_DIM)
    new_diff_v = new_ab[1].reshape(bp, n_ab, sp, DIFF_HEADS, DIFF_V_DIM)
    new_mla_ckv, new_mla_krope = new_ab[2], new_ab[3]
    new_swa_k = new_swa[0].reshape(bp, n_swa, sp, SWA_KV_HEADS, SWA_HEAD_DIM)
    new_swa_v = new_swa[1].reshape(bp, n_swa, sp, SWA_KV_HEADS, SWA_HEAD_DIM)
    return (xp, xs, new_diff_k, new_diff_v, new_mla_ckv, new_mla_krope, new_swa_k, new_swa_v)
```

```python
import functools
import math

import jax
import jax.numpy as jnp
import numpy as np
from jax import lax
from jax.experimental import pallas as pl
from jax.experimental.pallas import tpu as pltpu
from jax.experimental.pallas import tpu_sc as plsc

F32 = jnp.float32
BF16 = jnp.bfloat16

D_MODEL = 1024
GRID_W = 64
ROPE_THETA = 10000.0
NORM_EPS = 1e-6
NEG_INF = -1e30
LOG2E = math.log2(math.e)
LANES = 128

DIFF_HEADS = 4
DIFF_QK_DIM = 64
DIFF_V_DIM = 128
MLA_HEADS = 8
MLA_Q_RANK = 384
MLA_KV_RANK = 256
MLA_NOPE_DIM = 64
MLA_ROPE_DIM = 32
MLA_V_DIM = 64
AB_IN = 3 * 512 + MLA_Q_RANK + MLA_KV_RANK + MLA_ROPE_DIM
AB_IN_PAD = 2304
SWA_HEADS = 16
SWA_KV_HEADS = 4
SWA_GROUP = 4
SWA_HEAD_DIM = 64
WINDOW = 128
N_EXPERTS = 16
N_GROUPS = 4
EXPERTS_PER_GROUP = 4
D_EXPERT = 256
COND_ROWS = 16

VMEM_LIMIT = 56 * 1024 * 1024


def _full(shape):
    n = len(shape)
    return pl.BlockSpec(shape, lambda *_: (0,) * n)


def _params(n_axes):
    return pltpu.CompilerParams(dimension_semantics=("arbitrary",) * n_axes, vmem_limit_bytes=VMEM_LIMIT)


def _sigmoid(x):
    return 1.0 / (1.0 + jnp.exp(-x))


def _rms(x, g):
    return x * lax.rsqrt(jnp.mean(x * x, axis=-1, keepdims=True) + NORM_EPS) * g


def _modulate(x, g, shift, scale):
    return _rms(x, g) * (1.0 + scale) + shift


def _dot(a, b):
    return jnp.dot(a, b, preferred_element_type=F32)


def _dot_nt(a, b):
    return lax.dot_general(a, b, (((1,), (1,)), ((), ())), preferred_element_type=F32)


def _rope(x, tab_ref):
    c, s_odd, s_even = tab_ref[0], tab_ref[1], tab_ref[2]
    out = []
    for i in range(x.shape[1] // LANES):
        xi = x[:, i * LANES:(i + 1) * LANES]
        out.append(xi * c + pltpu.roll(xi, 1, 1) * s_odd + pltpu.roll(xi, LANES - 1, 1) * s_even)
    return out[0] if len(out) == 1 else jnp.concatenate(out, axis=1)


def _tree(op, xs):
    xs = list(xs)
    while len(xs) > 1:
        xs = [op(xs[i], xs[i + 1]) if i + 1 < len(xs) else xs[i] for i in range(0, len(xs), 2)]
    return xs[0]


def _lane_fold(op, x):
    return _tree(op, [x[:, i * LANES:(i + 1) * LANES] for i in range(x.shape[1] // LANES)])


def _ada_kernel(cond_ref, w_ref, b_ref, o_ref):
    c = cond_ref[...]
    a = (c * _sigmoid(c)).astype(BF16)
    o_ref[...] = _dot(a, w_ref[...].astype(BF16)) + b_ref[...]


def _ada_all(cond, ada_w, ada_b):
    depth, d, n = ada_w.shape
    tn = 1536
    return pl.pallas_call(
        _ada_kernel,
        grid=(depth, n // tn),
        in_specs=[
            _full((COND_ROWS, d)),
            pl.BlockSpec((None, d, tn), lambda l, j: (l, 0, j)),
            pl.BlockSpec((None, 1, tn), lambda l, j: (l, 0, j)),
        ],
        out_specs=pl.BlockSpec((None, COND_ROWS, tn), lambda l, j: (l, 0, j)),
        out_shape=jax.ShapeDtypeStruct((depth, COND_ROWS, n), F32),
        compiler_params=_params(2),
        name="ada_mod",
    )(cond, ada_w, ada_b.reshape(depth, 1, n))


def _proj_ab_kernel(rope, emit_cache, n_prev, x_ref, mod_ref, g_ref, w_in_ref, qn_ref, wqb_ref, kvn_ref, wk_ref,
                    wv_ref, e_ref, *rest):
    if rope:
        rope_d, rope_q, rope_k = rest[:3]
        rest = rest[3:]
    rest = rest[n_prev:]
    qa_o, ka_o, va_o, qb_o, km_o, vm_o = rest[:6]
    mod = mod_ref[0]
    h = _modulate(x_ref[...], g_ref[...], mod[0:1], mod[1:2]).astype(BF16)
    big = _dot(h, w_in_ref[...])
    qa, ka, va = big[:, 0:512], big[:, 512:1024], big[:, 1024:1536]
    q_lat, ckv, kr = big[:, 1536:1920], big[:, 1920:2176], big[:, 2176:2304]
    qb = _dot(_rms(q_lat, qn_ref[...]).astype(BF16), wqb_ref[...])
    ckv_n = _rms(ckv, kvn_ref[...])
    if emit_cache:
        ka32_o, va32_o, ckv32_o, kr32_o = rest[6:10]
        ka32_o[...] = ka
        va32_o[...] = va
        ckv32_o[...] = ckv_n
        kr32_o[...] = kr[:, :MLA_ROPE_DIM]
    if rope:
        qa, ka = _rope(qa, rope_d), _rope(ka, rope_d)
        qb, kr = _rope(qb, rope_q), _rope(kr, rope_k)
    qa = qa * (DIFF_QK_DIM ** -0.5 * LOG2E)
    qb = qb * ((MLA_NOPE_DIM + MLA_ROPE_DIM) ** -0.5 * LOG2E)
    ckv16 = ckv_n.astype(BF16)
    km = _dot(ckv16, wk_ref[...]) + _dot(kr.astype(BF16), e_ref[...])
    vm = _dot(ckv16, wv_ref[...])
    lo = lax.broadcasted_iota(jnp.int32, (qa.shape[0], LANES), 1) < DIFF_QK_DIM
    for hd in range(DIFF_HEADS):
        sl = slice(hd * LANES, (hd + 1) * LANES)
        qa_o[2 * hd] = jnp.where(lo, qa[:, sl], 0.0).astype(BF16)
        qa_o[2 * hd + 1] = jnp.where(lo, 0.0, qa[:, sl]).astype(BF16)
        ka_o[hd] = ka[:, sl].astype(BF16)
        va_o[hd] = va[:, sl].astype(BF16)
    for hd in range(MLA_HEADS):
        sl = slice(hd * LANES, (hd + 1) * LANES)
        qb_o[hd] = qb[:, sl].astype(BF16)
        km_o[hd] = km[:, sl].astype(BF16)
        vm_o[hd] = vm[:, sl].astype(BF16)


def _cache_outputs(cache, widths, b, s, tm, in_specs, args, n_fixed_outs):
    if cache is None:
        return [], {}, 0
    j, n_layers, prev = cache
    outs = [(jax.ShapeDtypeStruct((b, n_layers, s, w), F32),
             pl.BlockSpec((None, None, tm, w), lambda bi, t: (bi, j, t, 0))) for w in widths]
    aliases = {}
    if prev is not None:
        for i, arr in enumerate(prev):
            aliases[len(args)] = n_fixed_outs + i
            in_specs.append(pl.BlockSpec(memory_space=pl.ANY))
            args.append(arr)
    return outs, aliases, len(aliases)


def _proj_ab(x, mods, row_fn, g, wts, rope_tabs, cache, tm):
    b, s, d = x.shape
    rope = rope_tabs is not None
    tok = lambda bi, t: (bi, t, 0)
    head = lambda bi, t: (bi, 0, t, 0)
    in_specs = [
        pl.BlockSpec((None, tm, d), tok),
        pl.BlockSpec((1, 6, d), lambda bi, t: (row_fn(bi), 0, 0)),
        _full((1, d)),
        _full((d, AB_IN_PAD)), _full((1, MLA_Q_RANK)), _full((MLA_Q_RANK, 1024)), _full((1, MLA_KV_RANK)),
        _full((MLA_KV_RANK, 1024)), _full((MLA_KV_RANK, 1024)), _full((LANES, 1024)),
    ]
    args = [x, mods, g, wts["w_in"], wts["qn_g"], wts["w_qb"], wts["kvn_g"], wts["wk"], wts["wv"], wts["e"]]
    if rope:
        in_specs += [pl.BlockSpec((3, tm, LANES), lambda bi, t: (0, t, 0))] * 3
        args += list(rope_tabs)

    def hm(nh):
        return jax.ShapeDtypeStruct((b, nh, s, LANES), BF16), pl.BlockSpec((None, nh, tm, LANES), head)

    outs = [hm(8), hm(4), hm(4), hm(8), hm(8), hm(8)]
    cache_outs, aliases, n_prev = _cache_outputs(cache, (512, 512, MLA_KV_RANK, MLA_ROPE_DIM), b, s, tm, in_specs,
                                                 args, len(outs))
    outs += cache_outs
    return pl.pallas_call(
        functools.partial(_proj_ab_kernel, rope, cache is not None, n_prev),
        grid=(b, s // tm),
        in_specs=in_specs,
        out_specs=[o[1] for o in outs],
        out_shape=[o[0] for o in outs],
        input_output_aliases=aliases,
        compiler_params=_params(2),
        name="proj_ab",
    )(*args)


def _cache_ab_kernel(dk_ref, dv_ref, ckv_ref, kr_ref, wk_ref, wv_ref, e_ref, ck_o, cv_o, km_o, vm_o):
    ckv16 = ckv_ref[...].astype(BF16)
    km = _dot(ckv16, wk_ref[...]) + _dot(kr_ref[...].astype(BF16), e_ref[...])
    vm = _dot(ckv16, wv_ref[...])
    for hd in range(DIFF_HEADS):
        sl = slice(hd * LANES, (hd + 1) * LANES)
        ck_o[hd] = dk_ref[:, sl].astype(BF16)
        cv_o[hd] = dv_ref[:, sl].astype(BF16)
    for hd in range(MLA_HEADS):
        sl = slice(hd * LANES, (hd + 1) * LANES)
        km_o[hd] = km[:, sl].astype(BF16)
        vm_o[hd] = vm[:, sl].astype(BF16)


def _cache_ab(cdk, cdv, cckv, ckr, j, wts):
    b, _, p, _ = cdk.shape
    lay = lambda bi: (bi, j, 0, 0)

    def hm(nh):
        return (jax.ShapeDtypeStruct((b, nh, p, LANES), BF16),
                pl.BlockSpec((None, nh, p, LANES), lambda bi: (bi, 0, 0, 0)))

    outs = [hm(4), hm(4), hm(8), hm(8)]
    return pl.pallas_call(
        _cache_ab_kernel,
        grid=(b,),
        in_specs=[
            pl.BlockSpec((None, None, p, 512), lay), pl.BlockSpec((None, None, p, 512), lay),
            pl.BlockSpec((None, None, p, MLA_KV_RANK), lay), pl.BlockSpec((None, None, p, LANES), lay),
            _full((MLA_KV_RANK, 1024)), _full((MLA_KV_RANK, 1024)), _full((LANES, 1024)),
        ],
        out_specs=[o[1] for o in outs],
        out_shape=[o[0] for o in outs],
        compiler_params=_params(1),
        name="cache_ab",
    )(cdk, cdv, cckv, ckr, wts["wk"], wts["wv"], wts["e"])


def _attn_ab_kernel(n_seg, lambda_init, x_ref, mod_ref, qa_ref, qb_ref, *rest):
    segs = [rest[4 * i:4 * i + 4] for i in range(n_seg)]
    wd_ref, wm_ref, sg_ref, lam_ref, o_ref = rest[4 * n_seg:4 * n_seg + 5]
    bufs = rest[4 * n_seg + 5:]
    lp = lam_ref[...]
    lam = (jnp.exp(jnp.sum(lp[0:1] * lp[1:2], axis=-1, keepdims=True))
           - jnp.exp(jnp.sum(lp[2:3] * lp[3:4], axis=-1, keepdims=True)) + lambda_init)
    tq, d = x_ref.shape
    widths = [sg[0].shape[1] for sg in segs]
    offs = [sum(widths[:i]) for i in range(n_seg)]
    units = [("diff", hd) for hd in range(DIFF_HEADS)] + [("mla", hd) for hd in range(MLA_HEADS)]

    def scores(u, buf):
        kind, hd = units[u]
        qs = [qa_ref[2 * hd], qa_ref[2 * hd + 1]] if kind == "diff" else [qb_ref[hd]]
        ks = [sg[0 if kind == "diff" else 2][hd] for sg in segs]
        ms = []
        for i, q in enumerate(qs):
            folds = []
            for k, off, w in zip(ks, offs, widths):
                s = _dot_nt(q, k)
                buf[i, :, off:off + w] = s
                folds.append(_lane_fold(jnp.maximum, s))
            ms.append(jnp.max(_tree(jnp.maximum, folds), axis=-1, keepdims=True))
        return ms

    def exps(buf, i, m):
        e = [jnp.exp2(buf[i, :, off:off + w] - m) for off, w in zip(offs, widths)]
        l = jnp.sum(_tree(jnp.add, [_lane_fold(jnp.add, x) for x in e]), axis=-1, keepdims=True)
        return e, l

    def consume(u, buf, ms, acc):
        kind, hd = units[u]
        if kind == "diff":
            vs = [sg[1][hd] for sg in segs]
            (e1, l1), (e2, l2) = exps(buf, 0, ms[0]), exps(buf, 1, ms[1])
            c1, c2 = 1.0 / l1, lam / l2
            o = _tree(jnp.add, [_dot((a * c1 - b * c2).astype(BF16), v) for a, b, v in zip(e1, e2, vs)])
            od = _rms(o, sg_ref[...]) * (1.0 - lambda_init)
            return acc + _dot(od.astype(BF16), wd_ref[hd])
        vs = [sg[3][hd] for sg in segs]
        e, l = exps(buf, 0, ms[0])
        o = _tree(jnp.add, [_dot(x.astype(BF16), v) for x, v in zip(e, vs)]) / l
        return acc + _dot(o.astype(BF16), wm_ref[hd])

    acc = jnp.zeros((tq, d), F32)
    ms = scores(0, bufs[0])
    for u in range(len(units)):
        nxt = scores(u + 1, bufs[(u + 1) % len(bufs)]) if u + 1 < len(units) else None
        acc = consume(u, bufs[u % len(bufs)], ms, acc)
        ms = nxt
    o_ref[...] = x_ref[...] + mod_ref[0][2:3] * acc


def _attn_ab(x, mods, row_fn, q_parts, seg_list, wts, lambda_init, tq, n_bufs):
    b, s, d = x.shape
    qa, qb = q_parts
    in_specs = [
        pl.BlockSpec((None, tq, d), lambda bi, t: (bi, t, 0)),
        pl.BlockSpec((1, 6, d), lambda bi, t: (row_fn(bi), 0, 0)),
        pl.BlockSpec((None, 8, tq, LANES), lambda bi, t: (bi, 0, t, 0)),
        pl.BlockSpec((None, 8, tq, LANES), lambda bi, t: (bi, 0, t, 0)),
    ]
    args = [x, mods, qa, qb]
    kv_mode = dict(pipeline_mode=pl.Buffered(1)) if s // tq > 1 else {}
    for seg in seg_list:
        for arr in seg:
            nh, nk = arr.shape[1], arr.shape[2]
            in_specs.append(pl.BlockSpec((None, nh, nk, LANES), lambda bi, t: (bi, 0, 0, 0), **kv_mode))
            args.append(arr)
    in_specs += [_full((DIFF_HEADS, LANES, d)), _full((MLA_HEADS, LANES, d)), _full((1, LANES)),
                 _full((4, DIFF_QK_DIM))]
    args += [wts["w_out_d"], wts["w_out_m"], wts["subln_g"], wts["lam"]]
    n_keys = sum(seg[0].shape[2] for seg in seg_list)
    return pl.pallas_call(
        functools.partial(_attn_ab_kernel, len(seg_list), lambda_init),
        grid=(b, s // tq),
        in_specs=in_specs,
        out_specs=pl.BlockSpec((None, tq, d), lambda bi, t: (bi, t, 0)),
        out_shape=jax.ShapeDtypeStruct((b, s, d), F32),
        scratch_shapes=[pltpu.VMEM((2, tq, n_keys), F32) for _ in range(n_bufs)],
        compiler_params=_params(2),
        name="attn_ab",
    )(*args)


def _proj_swa_kernel(rope, emit_cache, n_prev, x_ref, mod_ref, g_ref, w_in_ref, *rest):
    if rope:
        rope_d = rest[0]
        rest = rest[1:]
    rest = rest[n_prev:]
    q_o, k_o, v_o = rest[:3]
    mod = mod_ref[0]
    h = _modulate(x_ref[...], g_ref[...], mod[0:1], mod[1:2]).astype(BF16)
    big = _dot(h, w_in_ref[...])
    q, k, v = big[:, :1024], big[:, 1024:1280], big[:, 1280:1536]
    if emit_cache:
        k32_o, v32_o = rest[3:5]
        k32_o[...] = k
        v32_o[...] = v
    if rope:
        q, k = _rope(q, rope_d), _rope(k, rope_d)
    q = q * (SWA_HEAD_DIM ** -0.5 * LOG2E)
    lo = lax.broadcasted_iota(jnp.int32, (q.shape[0], LANES), 1) < SWA_HEAD_DIM
    for pair in range(2):
        for grp in range(SWA_GROUP):
            ca, cb = (2 * pair) * 2 + grp // 2, (2 * pair + 1) * 2 + grp // 2
            a = q[:, ca * LANES:(ca + 1) * LANES]
            bb = q[:, cb * LANES:(cb + 1) * LANES]
            if grp % 2 == 0:
                bb = pltpu.roll(bb, SWA_HEAD_DIM, 1)
            else:
                a = pltpu.roll(a, SWA_HEAD_DIM, 1)
            q_o[pair * SWA_GROUP + grp] = jnp.where(lo, a, bb).astype(BF16)
    for kvh in range(SWA_KV_HEADS):
        sl = slice((kvh // 2) * LANES, (kvh // 2 + 1) * LANES)
        keep = lo if kvh % 2 == 0 else jnp.logical_not(lo)
        k_o[kvh] = jnp.where(keep, k[:, sl], 0.0).astype(BF16)
        v_o[kvh] = jnp.where(keep, v[:, sl], 0.0).astype(BF16)


def _proj_swa(x, mods, row_fn, g, w_in, rope_tab, cache, tm):
    b, s, d = x.shape
    rope = rope_tab is not None
    tok = lambda bi, t: (bi, t, 0)
    head = lambda bi, t: (bi, 0, t, 0)
    in_specs = [
        pl.BlockSpec((None, tm, d), tok),
        pl.BlockSpec((1, 6, d), lambda bi, t: (row_fn(bi), 0, 0)),
        _full((1, d)), _full((d, 1536)),
    ]
    args = [x, mods, g, w_in]
    if rope:
        in_specs.append(pl.BlockSpec((3, tm, LANES), lambda bi, t: (0, t, 0)))
        args.append(rope_tab)

    def hm(nh):
        return jax.ShapeDtypeStruct((b, nh, s, LANES), BF16), pl.BlockSpec((None, nh, tm, LANES), head)

    outs = [hm(8), hm(4), hm(4)]
    cache_outs, aliases, n_prev = _cache_outputs(cache, (256, 256), b, s, tm, in_specs, args, len(outs))
    outs += cache_outs
    return pl.pallas_call(
        functools.partial(_proj_swa_kernel, rope, cache is not None, n_prev),
        grid=(b, s // tm),
        in_specs=in_specs,
        out_specs=[o[1] for o in outs],
        out_shape=[o[0] for o in outs],
        input_output_aliases=aliases,
        compiler_params=_params(2),
        name="proj_swa",
    )(*args)


def _attn_swa_kernel(windowed, band_w, x_ref, mod_ref, q_ref, k_ref, v_ref, *rest):
    if windowed:
        ck_ref, cv_ref = rest[:2]
        rest = rest[2:]
    sink_ref, w_ref, o_ref = rest[:3]
    bufs = rest[3:]
    tq, d = x_ref.shape
    n_keys = k_ref.shape[1]
    lo_k = lax.broadcasted_iota(jnp.int32, (1, LANES), 1) < SWA_HEAD_DIM
    if windowed:
        start = pl.program_id(1) * tq
        bstart = pl.multiple_of(jnp.clip(start - WINDOW, 0, n_keys - band_w), LANES)
        qpos = start + lax.broadcasted_iota(jnp.int32, (tq, band_w), 0)
        kpos = bstart + lax.broadcasted_iota(jnp.int32, (tq, band_w), 1)
        bias = jnp.where(jnp.abs(qpos - kpos) <= WINDOW, 0.0, NEG_INF)

    def keys_values(kvh):
        if not windowed:
            return [k_ref[kvh]], [v_ref[kvh]]
        keep = lo_k if kvh % 2 == 0 else jnp.logical_not(lo_k)
        sl = slice((kvh // 2) * LANES, (kvh // 2 + 1) * LANES)
        kc = jnp.where(keep, ck_ref[:, sl], 0.0).astype(BF16)
        vc = jnp.where(keep, cv_ref[:, sl], 0.0).astype(BF16)
        return ([k_ref[kvh, pl.ds(bstart, band_w), :], kc], [v_ref[kvh, pl.ds(bstart, band_w), :], vc])

    units = [(kvh, g) for kvh in range(SWA_KV_HEADS) for g in range(SWA_GROUP)]

    def scores(u, buf):
        kvh, g = units[u]
        q = q_ref[(kvh // 2) * SWA_GROUP + g]
        sink = sink_ref[kvh * SWA_GROUP + g] * LOG2E
        folds, off = [], 0
        for i, k in enumerate(keys_values(kvh)[0]):
            s = _dot_nt(q, k)
            if windowed and i == 0:
                s = s + bias
            buf[:, off:off + s.shape[1]] = s
            folds.append(_lane_fold(jnp.maximum, s))
            off += s.shape[1]
        m = jnp.maximum(jnp.max(_tree(jnp.maximum, folds), axis=-1, keepdims=True), sink)
        return m, sink

    def consume(u, buf, m, sink):
        vs = keys_values(units[u][0])[1]
        e, off = [], 0
        for v in vs:
            e.append(jnp.exp2(buf[:, off:off + v.shape[0]] - m))
            off += v.shape[0]
        l = jnp.sum(_tree(jnp.add, [_lane_fold(jnp.add, x) for x in e]), axis=-1, keepdims=True) + jnp.exp2(sink - m)
        return _tree(jnp.add, [_dot(x.astype(BF16), v) for x, v in zip(e, vs)]) / l

    acc = jnp.zeros((tq, d), F32)
    nxt = scores(0, bufs[0])
    prev = None
    for u, (kvh, g) in enumerate(units):
        cur = nxt
        if u + 1 < len(units):
            nxt = scores(u + 1, bufs[(u + 1) % len(bufs)])
        o = consume(u, bufs[u % len(bufs)], *cur)
        if g % 2 == 0:
            prev = o
            continue
        if kvh % 2 == 0:
            slab = prev + pltpu.roll(o, SWA_HEAD_DIM, 1)
        else:
            slab = pltpu.roll(prev, SWA_HEAD_DIM, 1) + o
        acc = acc + _dot(slab.astype(BF16), w_ref[kvh * 2 + g // 2])
    o_ref[...] = x_ref[...] + mod_ref[0][2:3] * acc


def _attn_swa(x, mods, row_fn, q, k, v, ctx, sink, w_out, tq, n_bufs):
    b, s, d = x.shape
    windowed = ctx is not None
    band_w = min(tq + 2 * WINDOW, s)
    kv_spec = pl.BlockSpec((None, SWA_KV_HEADS, s, LANES), lambda bi, t: (bi, 0, 0, 0))
    in_specs = [
        pl.BlockSpec((None, tq, d), lambda bi, t: (bi, t, 0)),
        pl.BlockSpec((1, 6, d), lambda bi, t: (row_fn(bi), 0, 0)),
        pl.BlockSpec((None, 8, tq, LANES), lambda bi, t: (bi, 0, t, 0)),
        kv_spec, kv_spec,
    ]
    args = [x, mods, q, k, v]
    if windowed:
        ck, cv, j = ctx
        p = ck.shape[2]
        spec = pl.BlockSpec((None, None, p, 256), lambda bi, t: (bi, j, 0, 0))
        in_specs += [spec, spec]
        args += [ck, cv]
    in_specs += [pl.BlockSpec(memory_space=pltpu.SMEM), _full((8, LANES, d))]
    args += [sink, w_out]
    n_keys = band_w + ctx[0].shape[2] if windowed else s
    return pl.pallas_call(
        functools.partial(_attn_swa_kernel, windowed, band_w),
        grid=(b, s // tq),
        in_specs=in_specs,
        out_specs=pl.BlockSpec((None, tq, d), lambda bi, t: (bi, t, 0)),
        out_shape=jax.ShapeDtypeStruct((b, s, d), F32),
        scratch_shapes=[pltpu.VMEM((tq, n_keys), F32) for _ in range(n_bufs)],
        compiler_params=_params(2),
        name="attn_swa",
    )(*args)


def _route(scores_t, bias):
    sel_t = scores_t + bias
    sel = [sel_t[e:e + 1] for e in range(N_EXPERTS)]
    raw = [scores_t[e:e + 1] for e in range(N_EXPERTS)]
    gscore = []
    for g in range(N_GROUPS):
        r = sel[g * 4:g * 4 + 4]
        pairs = [r[i] + r[j] for i in range(4) for j in range(i + 1, 4)]
        gscore.append(functools.reduce(jnp.maximum, pairs))
    best, gidx = gscore[0], jnp.zeros_like(gscore[0], dtype=jnp.int32)
    for g in range(1, N_GROUPS):
        take = gscore[g] > best
        best = jnp.where(take, gscore[g], best)
        gidx = jnp.where(take, g, gidx)
    vals = []
    for k in range(EXPERTS_PER_GROUP):
        v = sel[k]
        for g in range(1, N_GROUPS):
            v = jnp.where(gidx == g, sel[g * 4 + k], v)
        vals.append(v)

    def argmax4(vs):
        m, idx = vs[0], jnp.zeros_like(gidx)
        for k in range(1, 4):
            take = vs[k] > m
            m = jnp.where(take, vs[k], m)
            idx = jnp.where(take, k, idx)
        return idx

    i1 = argmax4(vals)
    i2 = argmax4([jnp.where(i1 == k, -jnp.inf, vals[k]) for k in range(4)])
    e1, e2 = gidx * 4 + i1, gidx * 4 + i2
    w1 = functools.reduce(lambda a, b: a + b, [jnp.where(e1 == e, raw[e], 0.0) for e in range(N_EXPERTS)])
    w2 = functools.reduce(lambda a, b: a + b, [jnp.where(e2 == e, raw[e], 0.0) for e in range(N_EXPERTS)])
    den = w1 + w2
    return e1, e2, w1 / den, w2 / den


SC_WINDOW = 128
SC_ROW = 128
N_PLANES = (D_MODEL // 2) // SC_ROW
MOE_ROW_BLOCK = 512
HALF_WORD = -65536


def _to_planes(packed, out_ref):
    for j in range(N_PLANES):
        out_ref[j] = packed[:, j * SC_ROW:(j + 1) * SC_ROW]


def _from_planes(ref):
    return jnp.concatenate([ref[j] for j in range(N_PLANES)], axis=1)


def _pack_bf16_pairs(x):
    n = x.shape[1] // 2
    lo = pltpu.bitcast(x[:, :n].astype(BF16).astype(F32), jnp.int32)
    hi = pltpu.bitcast(x[:, n:].astype(BF16).astype(F32), jnp.int32)
    return lax.shift_right_logical(lo, jnp.int32(16)) | (hi & jnp.int32(HALF_WORD))


def _unpack_bf16_pairs(p):
    lo = pltpu.bitcast(lax.shift_left(p, jnp.int32(16)), F32)
    hi = pltpu.bitcast(p & jnp.int32(HALF_WORD), F32)
    return jnp.concatenate([lo, hi], axis=1)


def _pick(idx, rows):
    return _tree(jnp.add, [jnp.where(idx == e, rows[e], 0.0) for e in range(N_EXPERTS)])


def _moe_route_kernel(x_ref, mod_ref, g_ref, rw_ref, rb_ref, tri_ref, hp_o, eidx_o, rank_o, gcol_o, cnt_o, seen):
    @pl.when(jnp.logical_and(pl.program_id(0) == 0, pl.program_id(1) == 0))
    def _():
        seen[...] = jnp.zeros_like(seen)

    tm = x_ref.shape[0]
    mod = mod_ref[0]
    h = _modulate(x_ref[...], g_ref[...], mod[3:4], mod[4:5])
    h16 = h.astype(BF16)
    _to_planes(_pack_bf16_pairs(h), hp_o)
    h_lo = (h - h16.astype(F32)).astype(BF16)
    part = _dot_nt(rw_ref[...], h16)
    logits_t = part[:N_EXPERTS] + part[N_EXPERTS:] + _dot_nt(rw_ref[:N_EXPERTS], h_lo)
    e1, e2, g1, g2 = _route(_sigmoid(logits_t), rb_ref[...])
    hot = jnp.concatenate([jnp.where(jnp.logical_or(e1 == e, e2 == e), 1.0, 0.0) for e in range(N_EXPERTS)], axis=0)
    before = _dot(hot.astype(BF16), tri_ref[...]) + seen[...]
    rows = [before[e:e + 1] for e in range(N_EXPERTS)]
    eidx_o[0], eidx_o[1] = e1, e2
    rank_o[0], rank_o[1] = _pick(e1, rows).astype(jnp.int32), _pick(e2, rows).astype(jnp.int32)
    gcol_o[...] = jnp.concatenate([g1, g2, jnp.zeros((LANES - 2, tm), F32)], axis=0).T
    seen[...] = seen[...] + jnp.sum(hot, axis=-1, keepdims=True)
    cnt_o[...] = seen[...]


def _moe_slots_kernel(mb, n_rows, eidx_ref, rank_ref, cnt_ref, idx_o, be_o):
    cnt = cnt_ref[...]
    padded = jnp.ceil(cnt / mb) * mb
    starts, ends, run = [], [], jnp.zeros((1, 1), F32)
    for e in range(N_EXPERTS):
        starts.append(run)
        run = run + padded[e:e + 1]
        ends.append(run)
    n_tiles, per_tile = eidx_ref.shape[1], eidx_ref.shape[3] // SC_WINDOW
    for k in range(2):
        for i in range(n_tiles):
            slot = rank_ref[k, i] + _pick(eidx_ref[k, i], starts).astype(jnp.int32)
            for j in range(N_PLANES):
                for q in range(per_tile):
                    row = ((k * N_PLANES + j) * n_tiles + i) * per_tile + q
                    idx_o[row:row + 1, :] = slot[:, q * SC_WINDOW:(q + 1) * SC_WINDOW] + j * n_rows
    first_row = lax.broadcasted_iota(jnp.int32, be_o.shape, 1).astype(F32) * mb
    be = _tree(jnp.add, [jnp.where(ends[e] <= first_row, 1.0, 0.0) for e in range(N_EXPERTS)])
    be_o[...] = jnp.minimum(be, N_EXPERTS - 1.0).astype(jnp.int32)


def _moe_ffn_kernel(be_ref, hs_ref, wg_ref, wu_ref, wd_ref, ys_o, wgu16, wd16):
    i = pl.program_id(0)

    @pl.when(jnp.logical_or(i == 0, be_ref[i] != be_ref[jnp.maximum(i - 1, 0)]))
    def _():
        wgu16[:, :D_EXPERT] = wg_ref[...].astype(BF16)
        wgu16[:, D_EXPERT:] = wu_ref[...].astype(BF16)
        wd16[...] = wd_ref[...].astype(BF16)

    h = _unpack_bf16_pairs(_from_planes(hs_ref)).astype(BF16)
    gu = _dot(h, wgu16[...])
    g, u = gu[:, :D_EXPERT], gu[:, D_EXPERT:]
    a = g * _sigmoid(g) * u
    _to_planes(_pack_bf16_pairs(_dot(a.astype(BF16), wd16[...])), ys_o)


def _moe_combine_kernel(final_norm, x_ref, mod_ref, z0_ref, z1_ref, gcol_ref, *rest):
    if final_norm:
        fg_ref, o_ref = rest
    else:
        (o_ref,) = rest
    gc = gcol_ref[...]
    lane = lax.broadcasted_iota(jnp.int32, gc.shape, 1)
    g1 = jnp.sum(jnp.where(lane == 0, gc, 0.0), axis=-1, keepdims=True)
    g2 = jnp.sum(jnp.where(lane == 1, gc, 0.0), axis=-1, keepdims=True)
    y = g1 * _unpack_bf16_pairs(_from_planes(z0_ref)) + g2 * _unpack_bf16_pairs(_from_planes(z1_ref))
    out = x_ref[...] + mod_ref[0][5:6] * y
    if final_norm:
        out = _rms(out, fg_ref[...])
    o_ref[...] = out


def _sc_mesh():
    return plsc.VectorSubcoreMesh(core_axis_name="core", subcore_axis_name="subcore")


def _sc_scatter_rows(rows, idx, n_out):
    n, w = rows.shape
    steps = n // SC_WINDOW

    @pl.kernel(out_type=jax.ShapeDtypeStruct((n_out, w), rows.dtype), mesh=_sc_mesh(), scratch_types=[])
    def scatter(x_hbm, i0_hbm, i1_hbm, o_hbm):
        def body(x_vmem, i0_vmem, i1_vmem):
            pltpu.sync_copy(x_vmem, o_hbm.at[i0_vmem.at[0]])
            pltpu.sync_copy(x_vmem, o_hbm.at[i1_vmem.at[0]])

        pltpu.emit_pipeline(
            body,
            grid=(steps,),
            in_specs=[pl.BlockSpec((SC_WINDOW, w), lambda i: (i, 0)),
                      pl.BlockSpec((1, SC_WINDOW), lambda i: (i, 0)),
                      pl.BlockSpec((1, SC_WINDOW), lambda i: (i + steps, 0))],
            out_specs=[],
            core_axis_name=("core", "subcore"),
            dimension_semantics=(pltpu.PARALLEL,),
        )(x_hbm, i0_hbm, i1_hbm)

    return scatter(rows, idx, idx)


def _sc_gather_rows(rows, idx):
    steps, w = idx.shape[0], rows.shape[1]

    @pl.kernel(out_type=jax.ShapeDtypeStruct((steps * SC_WINDOW, w), rows.dtype), mesh=_sc_mesh(),
               scratch_types=[])
    def gather(x_hbm, i_hbm, o_hbm):
        def body(i_vmem, o_vmem):
            pltpu.sync_copy(x_hbm.at[i_vmem.at[0]], o_vmem)

        pltpu.emit_pipeline(
            body,
            grid=(steps,),
            in_specs=[pl.BlockSpec((1, SC_WINDOW), lambda i: (i, 0))],
            out_specs=[pl.BlockSpec((SC_WINDOW, w), lambda i: (i, 0))],
            core_axis_name=("core", "subcore"),
            dimension_semantics=(pltpu.PARALLEL,),
        )(i_hbm, o_hbm)

    return gather(rows, idx)


def _moe_bucket(x, mods, row_fn, g, rw_t, rb, tm, mb):
    b, s, d = x.shape
    n, nt = b * s, (b * s) // tm
    per_b = s // tm
    tile = lambda bi, t: (0, bi * per_b + t, 0, 0)
    tri = jnp.asarray(np.triu(np.ones((tm, tm), np.float32), 1), BF16)
    hp, eidx, rank, gcol, cnt = pl.pallas_call(
        _moe_route_kernel,
        grid=(b, per_b),
        in_specs=[
            pl.BlockSpec((None, tm, d), lambda bi, t: (bi, t, 0)),
            pl.BlockSpec((1, 6, d), lambda bi, t: (row_fn(bi), 0, 0)),
            _full((1, d)), _full((2 * N_EXPERTS, d)), _full((N_EXPERTS, 1)), _full((tm, tm)),
        ],
        out_specs=[
            pl.BlockSpec((N_PLANES, tm, SC_ROW), lambda bi, t: (0, bi * per_b + t, 0)),
            pl.BlockSpec((2, None, 1, tm), tile), pl.BlockSpec((2, None, 1, tm), tile),
            pl.BlockSpec((None, tm, LANES), lambda bi, t: (bi, t, 0)),
            _full((N_EXPERTS, 1)),
        ],
        out_shape=[
            jax.ShapeDtypeStruct((N_PLANES, n, SC_ROW), jnp.int32),
            jax.ShapeDtypeStruct((2, nt, 1, tm), jnp.int32), jax.ShapeDtypeStruct((2, nt, 1, tm), jnp.int32),
            jax.ShapeDtypeStruct((b, s, LANES), F32),
            jax.ShapeDtypeStruct((N_EXPERTS, 1), F32),
        ],
        scratch_shapes=[pltpu.VMEM((N_EXPERTS, 1), F32)],
        compiler_params=_params(2),
        name="moe_route",
    )(x, mods, g, rw_t, rb, tri)

    n_rows = 2 * n + N_EXPERTS * mb
    nb = n_rows // mb
    nb_pad = -(-nb // LANES) * LANES
    idx, be = pl.pallas_call(
        functools.partial(_moe_slots_kernel, mb, n_rows),
        out_shape=[jax.ShapeDtypeStruct((2 * N_PLANES * n // SC_WINDOW, SC_WINDOW), jnp.int32),
                   jax.ShapeDtypeStruct((1, nb_pad), jnp.int32)],
        name="moe_slots",
    )(eidx, rank, cnt)

    hs = _sc_scatter_rows(hp.reshape(N_PLANES * n, SC_ROW), idx, N_PLANES * n_rows)
    return hs.reshape(N_PLANES, n_rows, SC_ROW), idx, be.reshape(nb_pad), gcol


def _moe_experts(bucket, layer, w_gate, w_up, w_down, mb, out_dims):
    hs, idx, be, gcol = bucket
    n_rows, d = hs.shape[1], w_gate.shape[2]
    plane_block = pl.BlockSpec((N_PLANES, mb, SC_ROW), lambda i, be_r: (0, i, 0))
    ys = pl.pallas_call(
        _moe_ffn_kernel,
        grid_spec=pltpu.PrefetchScalarGridSpec(
            num_scalar_prefetch=1,
            grid=(n_rows // mb,),
            in_specs=[
                plane_block,
                pl.BlockSpec((None, None, d, D_EXPERT), lambda i, be_r: (layer, be_r[i], 0, 0)),
                pl.BlockSpec((None, None, d, D_EXPERT), lambda i, be_r: (layer, be_r[i], 0, 0)),
                pl.BlockSpec((None, None, D_EXPERT, d), lambda i, be_r: (layer, be_r[i], 0, 0)),
            ],
            out_specs=plane_block,
            scratch_shapes=[pltpu.VMEM((d, 2 * D_EXPERT), BF16), pltpu.VMEM((D_EXPERT, d), BF16)],
        ),
        out_shape=jax.ShapeDtypeStruct((N_PLANES, n_rows, SC_ROW), jnp.int32),
        compiler_params=_params(1),
        name="moe_ffn",
    )(be, hs, w_gate, w_up, w_down)
    z = _sc_gather_rows(ys.reshape(N_PLANES * n_rows, SC_ROW), idx)
    return z.reshape(2, N_PLANES, *out_dims, SC_ROW), gcol


def _moe_combine(x, mods, row_fn, gathered, final_g, tm):
    z, gcol = gathered
    b, s, d = x.shape
    per_b = s // tm
    final_norm = final_g is not None
    in_specs = [
        pl.BlockSpec((None, tm, d), lambda bi, t: (bi, t, 0)),
        pl.BlockSpec((1, 6, d), lambda bi, t: (row_fn(bi), 0, 0)),
        pl.BlockSpec((None, N_PLANES, None, tm, SC_ROW), lambda bi, t: (0, 0, bi, t, 0)),
        pl.BlockSpec((None, N_PLANES, None, tm, SC_ROW), lambda bi, t: (1, 0, bi, t, 0)),
        pl.BlockSpec((None, tm, LANES), lambda bi, t: (bi, t, 0)),
    ]
    args = [x, mods, z, z, gcol]
    if final_norm:
        in_specs.append(_full((1, d)))
        args.append(final_g)
    return pl.pallas_call(
        functools.partial(_moe_combine_kernel, final_norm),
        grid=(b, per_b),
        in_specs=in_specs,
        out_specs=pl.BlockSpec((None, tm, d), lambda bi, t: (bi, t, 0)),
        out_shape=jax.ShapeDtypeStruct((b, s, d), F32),
        compiler_params=_params(2),
        name="moe_combine",
    )(*args)


def _rope_tables(n_tokens, rot_dim, offset):
    n_rows = n_tokens // GRID_W
    rows = np.repeat(np.arange(n_rows), GRID_W)
    cols = np.tile(np.arange(GRID_W), n_rows)
    n_freq = rot_dim // 4
    inv = jnp.asarray(ROPE_THETA, F32) ** (-jnp.arange(n_freq, dtype=F32) / n_freq)
    ang = jnp.concatenate([jnp.asarray(rows, F32)[:, None] * inv, jnp.asarray(cols, F32)[:, None] * inv], axis=-1)
    cos, sin = jnp.repeat(jnp.cos(ang), 2, axis=-1), jnp.repeat(jnp.sin(ang), 2, axis=-1)
    odd = jnp.asarray(np.arange(rot_dim) % 2 == 1)
    parts = [cos, jnp.where(odd, sin, 0.0), jnp.where(odd, 0.0, -sin)]
    period = 64 if rot_dim == 64 else LANES
    fill = [1.0, 0.0, 0.0]
    out = []
    for p, f in zip(parts, fill):
        slot = jnp.full((n_tokens, period), f, F32).at[:, offset:offset + rot_dim].set(p)
        out.append(jnp.tile(slot, (1, LANES // period)))
    return jnp.stack(out)


def _pad_heads(w, n_heads, lo, hi):
    k = w.shape[0]
    w = w.reshape(k, n_heads, -1)[:, :, lo:hi]
    return jnp.pad(w, ((0, 0), (0, 0), (0, LANES - (hi - lo)))).reshape(k, n_heads * LANES)


def _ab_weights(j, ab_w_in, diff_lambda, diff_subln_g, mla_q_norm_g, mla_w_qb, mla_kv_norm_g, mla_w_kvb, ab_w_out):
    place = np.zeros((LANES, MLA_HEADS * LANES), np.float32)
    for hd in range(MLA_HEADS):
        for r in range(MLA_ROPE_DIM):
            place[r, hd * LANES + MLA_NOPE_DIM + r] = 1.0
    w_out_m = ab_w_out[j][512:].reshape(MLA_HEADS, MLA_V_DIM, D_MODEL)
    return {
        "w_in": jnp.pad(ab_w_in[j], ((0, 0), (0, AB_IN_PAD - AB_IN))).astype(BF16),
        "qn_g": mla_q_norm_g[j][None, :],
        "w_qb": _pad_heads(mla_w_qb[j], MLA_HEADS, 0, MLA_NOPE_DIM + MLA_ROPE_DIM).astype(BF16),
        "kvn_g": mla_kv_norm_g[j][None, :],
        "wk": _pad_heads(mla_w_kvb[j], MLA_HEADS, 0, MLA_NOPE_DIM).astype(BF16),
        "wv": _pad_heads(mla_w_kvb[j], MLA_HEADS, MLA_NOPE_DIM, MLA_NOPE_DIM + MLA_V_DIM).astype(BF16),
        "e": jnp.asarray(place, BF16),
        "w_out_d": ab_w_out[j][:512].reshape(DIFF_HEADS, LANES, D_MODEL).astype(BF16),
        "w_out_m": jnp.pad(w_out_m, ((0, 0), (0, LANES - MLA_V_DIM), (0, 0))).astype(BF16),
        "subln_g": diff_subln_g[j][None, :],
        "lam": diff_lambda[j],
    }


def kernel(x_prompt, x_sample, cache_diff_k, cache_diff_v, cache_mla_ckv, cache_mla_krope, cache_swa_k, cache_swa_v,
           c, c_ctx, ada_w, ada_b, norm1_g, norm2_g, final_norm_g, ab_w_in, diff_lambda, diff_subln_g, mla_q_norm_g,
           mla_w_qb, mla_kv_norm_g, mla_w_kvb, ab_w_out, swa_w_in, swa_sink, swa_w_out, router_w, router_bias,
           moe_w_gate, moe_w_up, moe_w_down):
    bp, sp, d = x_prompt.shape
    bs, ss, _ = x_sample.shape
    depth = ada_w.shape[0]
    past = cache_diff_k.shape[2]
    n_ab = cache_diff_k.shape[1]
    n_swa = cache_swa_k.shape[1]
    assert bs + 1 <= COND_ROWS and d == D_MODEL

    cond = jnp.zeros((COND_ROWS, d), F32).at[:bs].set(c).at[bs].set(c_ctx)
    mods = _ada_all(cond, ada_w, ada_b).reshape(depth, COND_ROWS, 6, d)
    row_s = lambda bi: bi
    row_p = lambda bi: bi * 0 + bs

    rope_d = _rope_tables(ss, DIFF_QK_DIM, 0)
    rope_q = _rope_tables(ss, MLA_ROPE_DIM, MLA_NOPE_DIM)
    rope_k = _rope_tables(ss, MLA_ROPE_DIM, 0)

    cdk = cache_diff_k.reshape(bs, n_ab, past, 512)
    cdv = cache_diff_v.reshape(bs, n_ab, past, 512)
    ckr = jnp.pad(cache_mla_krope, ((0, 0), (0, 0), (0, 0), (0, LANES - MLA_ROPE_DIM)))
    csk = cache_swa_k.reshape(bs, n_swa, past, 256)
    csv = cache_swa_v.reshape(bs, n_swa, past, 256)
    rw_hi = router_w.T.astype(BF16)
    rw_t = jnp.concatenate([rw_hi, (router_w.T - rw_hi.astype(F32)).astype(BF16)], axis=0)
    rb = router_bias[:, None]

    tp = min(256, sp)
    ts = min(256, ss)
    tm_s = min(512, ss)
    tm_p = min(512, bp * sp)

    xp, xs = x_prompt, x_sample
    new_ab, new_swa = None, None
    for l in range(depth):
        j = l // 2
        ml = mods[l]
        g1, g2 = norm1_g[l][None, :], norm2_g[l][None, :]
        if l % 2 == 0:
            lambda_init = 0.8 - 0.6 * math.exp(-0.3 * l)
            wts = _ab_weights(j, ab_w_in, diff_lambda, diff_subln_g, mla_q_norm_g, mla_w_qb, mla_kv_norm_g,
                              mla_w_kvb, ab_w_out)
            qa, ka, va, qb, km, vm, *new_ab = _proj_ab(xp, ml, row_p, g1, wts, None, (j, n_ab, new_ab), tp)
            xp = _attn_ab(xp, ml, row_p, (qa, qb), [(ka, va, km, vm)], wts, lambda_init, tp, DIFF_HEADS + MLA_HEADS)
            qa, ka, va, qb, km, vm = _proj_ab(xs, ml, row_s, g1, wts, (rope_d, rope_q, rope_k), None, ts)
            cache_seg = _cache_ab(cdk, cdv, cache_mla_ckv, ckr, j, wts)
            xs = _attn_ab(xs, ml, row_s, (qa, qb), [(ka, va, km, vm), tuple(cache_seg)], wts, lambda_init, ts, 2)
        else:
            w_in = swa_w_in[j].astype(BF16)
            w_out = swa_w_out[j].reshape(8, LANES, d).astype(BF16)
            q, k, v, *new_swa = _proj_swa(xp, ml, row_p, g1, w_in, None, (j, n_swa, new_swa), tp)
            xp = _attn_swa(xp, ml, row_p, q, k, v, None, swa_sink[j], w_out, tp, SWA_HEADS)
            q, k, v = _proj_swa(xs, ml, row_s, g1, w_in, rope_d, None, ts)
            xs = _attn_swa(xs, ml, row_s, q, k, v, (csk, csv, j), swa_sink[j], w_out, ts, 2)
        fg = final_norm_g[None, :] if l == depth - 1 else None
        xp = xp.reshape(1, bp * sp, d)
        bucket_p = _moe_bucket(xp, ml, row_p, g2, rw_t, rb, tm_p, MOE_ROW_BLOCK)
        bucket_s = _moe_bucket(xs, ml, row_s, g2, rw_t, rb, tm_s, MOE_ROW_BLOCK)
        got_p = _moe_experts(bucket_p, l, moe_w_gate, moe_w_up, moe_w_down, MOE_ROW_BLOCK, (1, bp * sp))
        got_s = _moe_experts(bucket_s, l, moe_w_gate, moe_w_up, moe_w_down, MOE_ROW_BLOCK, (bs, ss))
        xp = _moe_combine(xp, ml, row_p, got_p, fg, tm_p).reshape(bp, sp, d)
        xs = _moe_combine(xs, ml, row_s, got_s, fg, tm_s)

    new_diff_k = new_ab[0].reshape(bp, n_ab, sp, DIFF_HEADS, 2, DIFF_QK_DIM)
    new_diff_v = new_ab[1].reshape(bp, n_ab, sp, DIFF_HEADS, DIFF_V_DIM)
    new_mla_ckv, new_mla_krope = new_ab[2], new_ab[3]
    new_swa_k = new_swa[0].reshape(bp, n_swa, sp, SWA_KV_HEADS, SWA_HEAD_DIM)
    new_swa_v = new_swa[1].reshape(bp, n_swa, sp, SWA_KV_HEADS, SWA_HEAD_DIM)
    return (xp, xs, new_diff_k, new_diff_v, new_mla_ckv, new_mla_krope, new_swa_k, new_swa_v)
```

```python
import functools
import math

import jax
import jax.numpy as jnp
import numpy as np
from jax import lax
from jax.experimental import pallas as pl
from jax.experimental.pallas import tpu as pltpu
from jax.experimental.pallas import tpu_sc as plsc

F32 = jnp.float32
BF16 = jnp.bfloat16

D_MODEL = 1024
GRID_W = 64
ROPE_THETA = 10000.0
NORM_EPS = 1e-6
NEG_INF = -1e30
LOG2E = math.log2(math.e)
LANES = 128

DIFF_HEADS = 4
DIFF_QK_DIM = 64
DIFF_V_DIM = 128
MLA_HEADS = 8
MLA_Q_RANK = 384
MLA_KV_RANK = 256
MLA_NOPE_DIM = 64
MLA_ROPE_DIM = 32
MLA_V_DIM = 64
AB_IN = 3 * 512 + MLA_Q_RANK + MLA_KV_RANK + MLA_ROPE_DIM
AB_IN_PAD = 2304
SWA_HEADS = 16
SWA_KV_HEADS = 4
SWA_GROUP = 4
SWA_HEAD_DIM = 64
WINDOW = 128
N_EXPERTS = 16
N_GROUPS = 4
EXPERTS_PER_GROUP = 4
D_EXPERT = 256
COND_ROWS = 16

VMEM_LIMIT = 56 * 1024 * 1024


def _full(shape):
    n = len(shape)
    return pl.BlockSpec(shape, lambda *_: (0,) * n)


def _params(n_axes):
    return pltpu.CompilerParams(dimension_semantics=("arbitrary",) * n_axes, vmem_limit_bytes=VMEM_LIMIT)


def _sigmoid(x):
    return 1.0 / (1.0 + jnp.exp(-x))


def _rms(x, g):
    return x * lax.rsqrt(jnp.mean(x * x, axis=-1, keepdims=True) + NORM_EPS) * g


def _modulate(x, g, shift, scale):
    return _rms(x, g) * (1.0 + scale) + shift


def _dot(a, b):
    return jnp.dot(a, b, preferred_element_type=F32)


def _dot_nt(a, b):
    return lax.dot_general(a, b, (((1,), (1,)), ((), ())), preferred_element_type=F32)


def _rope(x, tab_ref):
    c, s_odd, s_even = tab_ref[0], tab_ref[1], tab_ref[2]
    out = []
    for i in range(x.shape[1] // LANES):
        xi = x[:, i * LANES:(i + 1) * LANES]
        out.append(xi * c + pltpu.roll(xi, 1, 1) * s_odd + pltpu.roll(xi, LANES - 1, 1) * s_even)
    return out[0] if len(out) == 1 else jnp.concatenate(out, axis=1)


def _tree(op, xs):
    xs = list(xs)
    while len(xs) > 1:
        xs = [op(xs[i], xs[i + 1]) if i + 1 < len(xs) else xs[i] for i in range(0, len(xs), 2)]
    return xs[0]


def _lane_fold(op, x):
    return _tree(op, [x[:, i * LANES:(i + 1) * LANES] for i in range(x.shape[1] // LANES)])


def _ada_kernel(cond_ref, w_ref, b_ref, o_ref):
    c = cond_ref[...]
    a = (c * _sigmoid(c)).astype(BF16)
    o_ref[...] = _dot(a, w_ref[...].astype(BF16)) + b_ref[...]


def _ada_all(cond, ada_w, ada_b):
    depth, d, n = ada_w.shape
    tn = 1536
    return pl.pallas_call(
        _ada_kernel,
        grid=(depth, n // tn),
        in_specs=[
            _full((COND_ROWS, d)),
            pl.BlockSpec((None, d, tn), lambda l, j: (l, 0, j)),
            pl.BlockSpec((None, 1, tn), lambda l, j: (l, 0, j)),
        ],
        out_specs=pl.BlockSpec((None, COND_ROWS, tn), lambda l, j: (l, 0, j)),
        out_shape=jax.ShapeDtypeStruct((depth, COND_ROWS, n), F32),
        compiler_params=_params(2),
        name="ada_mod",
    )(cond, ada_w, ada_b.reshape(depth, 1, n))


def _proj_ab_kernel(rope, emit_cache, n_prev, fused, x_ref, mod_ref, g_ref, w_in_ref, qn_ref, wqb_ref, kvn_ref,
                    wk_ref, wv_ref, e_ref, *rest):
    if rope:
        rope_d, rope_q, rope_k = rest[:3]
        rest = rest[3:]
    x = x_ref[...]
    if fused:
        x = _moe_residual(x, *rest[:4])
        rest[-1][...] = x
        rest = rest[4:]
    rest = rest[n_prev:]
    qa_o, ka_o, va_o, qb_o, km_o, vm_o = rest[:6]
    mod = mod_ref[0]
    h = _modulate(x, g_ref[...], mod[0:1], mod[1:2]).astype(BF16)
    big = _dot(h, w_in_ref[...])
    qa, ka, va = big[:, 0:512], big[:, 512:1024], big[:, 1024:1536]
    q_lat, ckv, kr = big[:, 1536:1920], big[:, 1920:2176], big[:, 2176:2304]
    qb = _dot(_rms(q_lat, qn_ref[...]).astype(BF16), wqb_ref[...])
    ckv_n = _rms(ckv, kvn_ref[...])
    if emit_cache:
        ka32_o, va32_o, ckv32_o, kr32_o = rest[6:10]
        ka32_o[...] = ka
        va32_o[...] = va
        ckv32_o[...] = ckv_n
        kr32_o[...] = kr[:, :MLA_ROPE_DIM]
    if rope:
        qa, ka = _rope(qa, rope_d), _rope(ka, rope_d)
        qb, kr = _rope(qb, rope_q), _rope(kr, rope_k)
    qa = qa * (DIFF_QK_DIM ** -0.5 * LOG2E)
    qb = qb * ((MLA_NOPE_DIM + MLA_ROPE_DIM) ** -0.5 * LOG2E)
    ckv16 = ckv_n.astype(BF16)
    km = _dot(ckv16, wk_ref[...]) + _dot(kr.astype(BF16), e_ref[...])
    vm = _dot(ckv16, wv_ref[...])
    lo = lax.broadcasted_iota(jnp.int32, (qa.shape[0], LANES), 1) < DIFF_QK_DIM
    for hd in range(DIFF_HEADS):
        sl = slice(hd * LANES, (hd + 1) * LANES)
        qa_o[2 * hd] = jnp.where(lo, qa[:, sl], 0.0).astype(BF16)
        qa_o[2 * hd + 1] = jnp.where(lo, 0.0, qa[:, sl]).astype(BF16)
        ka_o[hd] = ka[:, sl].astype(BF16)
        va_o[hd] = va[:, sl].astype(BF16)
    for hd in range(MLA_HEADS):
        sl = slice(hd * LANES, (hd + 1) * LANES)
        qb_o[hd] = qb[:, sl].astype(BF16)
        km_o[hd] = km[:, sl].astype(BF16)
        vm_o[hd] = vm[:, sl].astype(BF16)


def _cache_outputs(cache, widths, b, s, tm, in_specs, args, n_fixed_outs):
    if cache is None:
        return [], {}, 0
    j, n_layers, prev = cache
    outs = [(jax.ShapeDtypeStruct((b, n_layers, s, w), F32),
             pl.BlockSpec((None, None, tm, w), lambda bi, t: (bi, j, t, 0))) for w in widths]
    aliases = {}
    if prev is not None:
        for i, arr in enumerate(prev):
            aliases[len(args)] = n_fixed_outs + i
            in_specs.append(pl.BlockSpec(memory_space=pl.ANY))
            args.append(arr)
    return outs, aliases, len(aliases)


def _proj_ab(x, mods, row_fn, g, wts, rope_tabs, cache, tm, resid=None):
    b, s, d = x.shape
    rope = rope_tabs is not None
    tok = lambda bi, t: (bi, t, 0)
    head = lambda bi, t: (bi, 0, t, 0)
    in_specs = [
        pl.BlockSpec((None, tm, d), tok),
        pl.BlockSpec((1, 6, d), lambda bi, t: (row_fn(bi), 0, 0)),
        _full((1, d)),
        _full((d, AB_IN_PAD)), _full((1, MLA_Q_RANK)), _full((MLA_Q_RANK, 1024)), _full((1, MLA_KV_RANK)),
        _full((MLA_KV_RANK, 1024)), _full((MLA_KV_RANK, 1024)), _full((LANES, 1024)),
    ]
    args = [x, mods, g, wts["w_in"], wts["qn_g"], wts["w_qb"], wts["kvn_g"], wts["wk"], wts["wv"], wts["e"]]
    if rope:
        in_specs += [pl.BlockSpec((3, tm, LANES), lambda bi, t: (0, t, 0))] * 3
        args += list(rope_tabs)
    if resid is not None:
        _residual_specs(resid, row_fn, tm, d, in_specs, args)

    def hm(nh):
        return jax.ShapeDtypeStruct((b, nh, s, LANES), BF16), pl.BlockSpec((None, nh, tm, LANES), head)

    outs = [hm(8), hm(4), hm(4), hm(8), hm(8), hm(8)]
    cache_outs, aliases, n_prev = _cache_outputs(cache, (512, 512, MLA_KV_RANK, MLA_ROPE_DIM), b, s, tm, in_specs,
                                                 args, len(outs))
    outs += cache_outs
    if resid is not None:
        outs.append((jax.ShapeDtypeStruct((b, s, d), F32), pl.BlockSpec((None, tm, d), tok)))
    return pl.pallas_call(
        functools.partial(_proj_ab_kernel, rope, cache is not None, n_prev, resid is not None),
        grid=(b, s // tm),
        in_specs=in_specs,
        out_specs=[o[1] for o in outs],
        out_shape=[o[0] for o in outs],
        input_output_aliases=aliases,
        compiler_params=_params(2),
        name="proj_ab",
    )(*args)


def _cache_ab_kernel(dk_ref, dv_ref, ckv_ref, kr_ref, wk_ref, wv_ref, e_ref, ck_o, cv_o, km_o, vm_o):
    ckv16 = ckv_ref[...].astype(BF16)
    km = _dot(ckv16, wk_ref[...]) + _dot(kr_ref[...].astype(BF16), e_ref[...])
    vm = _dot(ckv16, wv_ref[...])
    for hd in range(DIFF_HEADS):
        sl = slice(hd * LANES, (hd + 1) * LANES)
        ck_o[hd] = dk_ref[:, sl].astype(BF16)
        cv_o[hd] = dv_ref[:, sl].astype(BF16)
    for hd in range(MLA_HEADS):
        sl = slice(hd * LANES, (hd + 1) * LANES)
        km_o[hd] = km[:, sl].astype(BF16)
        vm_o[hd] = vm[:, sl].astype(BF16)


def _cache_ab(cdk, cdv, cckv, ckr, j, wts):
    b, _, p, _ = cdk.shape
    lay = lambda bi: (bi, j, 0, 0)

    def hm(nh):
        return (jax.ShapeDtypeStruct((b, nh, p, LANES), BF16),
                pl.BlockSpec((None, nh, p, LANES), lambda bi: (bi, 0, 0, 0)))

    outs = [hm(4), hm(4), hm(8), hm(8)]
    return pl.pallas_call(
        _cache_ab_kernel,
        grid=(b,),
        in_specs=[
            pl.BlockSpec((None, None, p, 512), lay), pl.BlockSpec((None, None, p, 512), lay),
            pl.BlockSpec((None, None, p, MLA_KV_RANK), lay), pl.BlockSpec((None, None, p, LANES), lay),
            _full((MLA_KV_RANK, 1024)), _full((MLA_KV_RANK, 1024)), _full((LANES, 1024)),
        ],
        out_specs=[o[1] for o in outs],
        out_shape=[o[0] for o in outs],
        compiler_params=_params(1),
        name="cache_ab",
    )(cdk, cdv, cckv, ckr, wts["wk"], wts["wv"], wts["e"])


def _attn_ab_kernel(n_seg, lambda_init, x_ref, mod_ref, qa_ref, qb_ref, *rest):
    segs = [rest[4 * i:4 * i + 4] for i in range(n_seg)]
    wd_ref, wm_ref, sg_ref, lam_ref, o_ref = rest[4 * n_seg:4 * n_seg + 5]
    bufs = rest[4 * n_seg + 5:]
    lp = lam_ref[...]
    lam = (jnp.exp(jnp.sum(lp[0:1] * lp[1:2], axis=-1, keepdims=True))
           - jnp.exp(jnp.sum(lp[2:3] * lp[3:4], axis=-1, keepdims=True)) + lambda_init)
    tq, d = x_ref.shape
    widths = [sg[0].shape[1] for sg in segs]
    offs = [sum(widths[:i]) for i in range(n_seg)]
    units = [("diff", hd) for hd in range(DIFF_HEADS)] + [("mla", hd) for hd in range(MLA_HEADS)]

    def scores(u, buf):
        kind, hd = units[u]
        qs = [qa_ref[2 * hd], qa_ref[2 * hd + 1]] if kind == "diff" else [qb_ref[hd]]
        ks = [sg[0 if kind == "diff" else 2][hd] for sg in segs]
        ms = []
        for i, q in enumerate(qs):
            folds = []
            for k, off, w in zip(ks, offs, widths):
                s = _dot_nt(q, k)
                buf[i, :, off:off + w] = s
                folds.append(_lane_fold(jnp.maximum, s))
            ms.append(jnp.max(_tree(jnp.maximum, folds), axis=-1, keepdims=True))
        return ms

    def exps(buf, i, m):
        e = [jnp.exp2(buf[i, :, off:off + w] - m) for off, w in zip(offs, widths)]
        l = jnp.sum(_tree(jnp.add, [_lane_fold(jnp.add, x) for x in e]), axis=-1, keepdims=True)
        return e, l

    def consume(u, buf, ms, acc):
        kind, hd = units[u]
        if kind == "diff":
            vs = [sg[1][hd] for sg in segs]
            (e1, l1), (e2, l2) = exps(buf, 0, ms[0]), exps(buf, 1, ms[1])
            c1, c2 = 1.0 / l1, lam / l2
            o = _tree(jnp.add, [_dot((a * c1 - b * c2).astype(BF16), v) for a, b, v in zip(e1, e2, vs)])
            od = _rms(o, sg_ref[...]) * (1.0 - lambda_init)
            return acc + _dot(od.astype(BF16), wd_ref[hd])
        vs = [sg[3][hd] for sg in segs]
        e, l = exps(buf, 0, ms[0])
        o = _tree(jnp.add, [_dot(x.astype(BF16), v) for x, v in zip(e, vs)]) / l
        return acc + _dot(o.astype(BF16), wm_ref[hd])

    acc = jnp.zeros((tq, d), F32)
    ms = scores(0, bufs[0])
    for u in range(len(units)):
        nxt = scores(u + 1, bufs[(u + 1) % len(bufs)]) if u + 1 < len(units) else None
        acc = consume(u, bufs[u % len(bufs)], ms, acc)
        ms = nxt
    o_ref[...] = x_ref[...] + mod_ref[0][2:3] * acc


def _attn_ab(x, mods, row_fn, q_parts, seg_list, wts, lambda_init, tq, n_bufs):
    b, s, d = x.shape
    qa, qb = q_parts
    in_specs = [
        pl.BlockSpec((None, tq, d), lambda bi, t: (bi, t, 0)),
        pl.BlockSpec((1, 6, d), lambda bi, t: (row_fn(bi), 0, 0)),
        pl.BlockSpec((None, 8, tq, LANES), lambda bi, t: (bi, 0, t, 0)),
        pl.BlockSpec((None, 8, tq, LANES), lambda bi, t: (bi, 0, t, 0)),
    ]
    args = [x, mods, qa, qb]
    kv_mode = dict(pipeline_mode=pl.Buffered(1)) if s // tq > 1 else {}
    for seg in seg_list:
        for arr in seg:
            nh, nk = arr.shape[1], arr.shape[2]
            in_specs.append(pl.BlockSpec((None, nh, nk, LANES), lambda bi, t: (bi, 0, 0, 0), **kv_mode))
            args.append(arr)
    in_specs += [_full((DIFF_HEADS, LANES, d)), _full((MLA_HEADS, LANES, d)), _full((1, LANES)),
                 _full((4, DIFF_QK_DIM))]
    args += [wts["w_out_d"], wts["w_out_m"], wts["subln_g"], wts["lam"]]
    n_keys = sum(seg[0].shape[2] for seg in seg_list)
    return pl.pallas_call(
        functools.partial(_attn_ab_kernel, len(seg_list), lambda_init),
        grid=(b, s // tq),
        in_specs=in_specs,
        out_specs=pl.BlockSpec((None, tq, d), lambda bi, t: (bi, t, 0)),
        out_shape=jax.ShapeDtypeStruct((b, s, d), F32),
        scratch_shapes=[pltpu.VMEM((2, tq, n_keys), F32) for _ in range(n_bufs)],
        compiler_params=_params(2),
        name="attn_ab",
    )(*args)


def _proj_swa_kernel(rope, emit_cache, n_prev, fused, x_ref, mod_ref, g_ref, w_in_ref, *rest):
    if rope:
        rope_d = rest[0]
        rest = rest[1:]
    x = x_ref[...]
    if fused:
        x = _moe_residual(x, *rest[:4])
        rest[-1][...] = x
        rest = rest[4:]
    rest = rest[n_prev:]
    q_o, k_o, v_o = rest[:3]
    mod = mod_ref[0]
    h = _modulate(x, g_ref[...], mod[0:1], mod[1:2]).astype(BF16)
    big = _dot(h, w_in_ref[...])
    q, k, v = big[:, :1024], big[:, 1024:1280], big[:, 1280:1536]
    if emit_cache:
        k32_o, v32_o = rest[3:5]
        k32_o[...] = k
        v32_o[...] = v
    if rope:
        q, k = _rope(q, rope_d), _rope(k, rope_d)
    q = q * (SWA_HEAD_DIM ** -0.5 * LOG2E)
    lo = lax.broadcasted_iota(jnp.int32, (q.shape[0], LANES), 1) < SWA_HEAD_DIM
    for pair in range(2):
        for grp in range(SWA_GROUP):
            ca, cb = (2 * pair) * 2 + grp // 2, (2 * pair + 1) * 2 + grp // 2
            a = q[:, ca * LANES:(ca + 1) * LANES]
            bb = q[:, cb * LANES:(cb + 1) * LANES]
            if grp % 2 == 0:
                bb = pltpu.roll(bb, SWA_HEAD_DIM, 1)
            else:
                a = pltpu.roll(a, SWA_HEAD_DIM, 1)
            q_o[pair * SWA_GROUP + grp] = jnp.where(lo, a, bb).astype(BF16)
    for kvh in range(SWA_KV_HEADS):
        sl = slice((kvh // 2) * LANES, (kvh // 2 + 1) * LANES)
        keep = lo if kvh % 2 == 0 else jnp.logical_not(lo)
        k_o[kvh] = jnp.where(keep, k[:, sl], 0.0).astype(BF16)
        v_o[kvh] = jnp.where(keep, v[:, sl], 0.0).astype(BF16)


def _proj_swa(x, mods, row_fn, g, w_in, rope_tab, cache, tm, resid=None):
    b, s, d = x.shape
    rope = rope_tab is not None
    tok = lambda bi, t: (bi, t, 0)
    head = lambda bi, t: (bi, 0, t, 0)
    in_specs = [
        pl.BlockSpec((None, tm, d), tok),
        pl.BlockSpec((1, 6, d), lambda bi, t: (row_fn(bi), 0, 0)),
        _full((1, d)), _full((d, 1536)),
    ]
    args = [x, mods, g, w_in]
    if rope:
        in_specs.append(pl.BlockSpec((3, tm, LANES), lambda bi, t: (0, t, 0)))
        args.append(rope_tab)
    if resid is not None:
        _residual_specs(resid, row_fn, tm, d, in_specs, args)

    def hm(nh):
        return jax.ShapeDtypeStruct((b, nh, s, LANES), BF16), pl.BlockSpec((None, nh, tm, LANES), head)

    outs = [hm(8), hm(4), hm(4)]
    cache_outs, aliases, n_prev = _cache_outputs(cache, (256, 256), b, s, tm, in_specs, args, len(outs))
    outs += cache_outs
    if resid is not None:
        outs.append((jax.ShapeDtypeStruct((b, s, d), F32), pl.BlockSpec((None, tm, d), tok)))
    return pl.pallas_call(
        functools.partial(_proj_swa_kernel, rope, cache is not None, n_prev, resid is not None),
        grid=(b, s // tm),
        in_specs=in_specs,
        out_specs=[o[1] for o in outs],
        out_shape=[o[0] for o in outs],
        input_output_aliases=aliases,
        compiler_params=_params(2),
        name="proj_swa",
    )(*args)


def _attn_swa_kernel(windowed, band_w, x_ref, mod_ref, q_ref, k_ref, v_ref, *rest):
    if windowed:
        ck_ref, cv_ref = rest[:2]
        rest = rest[2:]
    sink_ref, w_ref, o_ref = rest[:3]
    bufs = rest[3:]
    tq, d = x_ref.shape
    n_keys = k_ref.shape[1]
    lo_k = lax.broadcasted_iota(jnp.int32, (1, LANES), 1) < SWA_HEAD_DIM
    if windowed:
        start = pl.program_id(1) * tq
        bstart = pl.multiple_of(jnp.clip(start - WINDOW, 0, n_keys - band_w), LANES)
        qpos = start + lax.broadcasted_iota(jnp.int32, (tq, band_w), 0)
        kpos = bstart + lax.broadcasted_iota(jnp.int32, (tq, band_w), 1)
        bias = jnp.where(jnp.abs(qpos - kpos) <= WINDOW, 0.0, NEG_INF)

    def keys_values(kvh):
        if not windowed:
            return [k_ref[kvh]], [v_ref[kvh]]
        keep = lo_k if kvh % 2 == 0 else jnp.logical_not(lo_k)
        sl = slice((kvh // 2) * LANES, (kvh // 2 + 1) * LANES)
        kc = jnp.where(keep, ck_ref[:, sl], 0.0).astype(BF16)
        vc = jnp.where(keep, cv_ref[:, sl], 0.0).astype(BF16)
        return ([k_ref[kvh, pl.ds(bstart, band_w), :], kc], [v_ref[kvh, pl.ds(bstart, band_w), :], vc])

    units = [(kvh, g) for kvh in range(SWA_KV_HEADS) for g in range(SWA_GROUP)]

    def scores(u, buf):
        kvh, g = units[u]
        q = q_ref[(kvh // 2) * SWA_GROUP + g]
        sink = sink_ref[kvh * SWA_GROUP + g] * LOG2E
        folds, off = [], 0
        for i, k in enumerate(keys_values(kvh)[0]):
            s = _dot_nt(q, k)
            if windowed and i == 0:
                s = s + bias
            buf[:, off:off + s.shape[1]] = s
            folds.append(_lane_fold(jnp.maximum, s))
            off += s.shape[1]
        m = jnp.maximum(jnp.max(_tree(jnp.maximum, folds), axis=-1, keepdims=True), sink)
        return m, sink

    def consume(u, buf, m, sink):
        vs = keys_values(units[u][0])[1]
        e, off = [], 0
        for v in vs:
            e.append(jnp.exp2(buf[:, off:off + v.shape[0]] - m))
            off += v.shape[0]
        l = jnp.sum(_tree(jnp.add, [_lane_fold(jnp.add, x) for x in e]), axis=-1, keepdims=True) + jnp.exp2(sink - m)
        return _tree(jnp.add, [_dot(x.astype(BF16), v) for x, v in zip(e, vs)]) / l

    acc = jnp.zeros((tq, d), F32)
    nxt = scores(0, bufs[0])
    prev = None
    for u, (kvh, g) in enumerate(units):
        cur = nxt
        if u + 1 < len(units):
            nxt = scores(u + 1, bufs[(u + 1) % len(bufs)])
        o = consume(u, bufs[u % len(bufs)], *cur)
        if g % 2 == 0:
            prev = o
            continue
        if kvh % 2 == 0:
            slab = prev + pltpu.roll(o, SWA_HEAD_DIM, 1)
        else:
            slab = pltpu.roll(prev, SWA_HEAD_DIM, 1) + o
        acc = acc + _dot(slab.astype(BF16), w_ref[kvh * 2 + g // 2])
    o_ref[...] = x_ref[...] + mod_ref[0][2:3] * acc


def _attn_swa(x, mods, row_fn, q, k, v, ctx, sink, w_out, tq, n_bufs):
    b, s, d = x.shape
    windowed = ctx is not None
    band_w = min(tq + 2 * WINDOW, s)
    kv_spec = pl.BlockSpec((None, SWA_KV_HEADS, s, LANES), lambda bi, t: (bi, 0, 0, 0))
    in_specs = [
        pl.BlockSpec((None, tq, d), lambda bi, t: (bi, t, 0)),
        pl.BlockSpec((1, 6, d), lambda bi, t: (row_fn(bi), 0, 0)),
        pl.BlockSpec((None, 8, tq, LANES), lambda bi, t: (bi, 0, t, 0)),
        kv_spec, kv_spec,
    ]
    args = [x, mods, q, k, v]
    if windowed:
        ck, cv, j = ctx
        p = ck.shape[2]
        spec = pl.BlockSpec((None, None, p, 256), lambda bi, t: (bi, j, 0, 0))
        in_specs += [spec, spec]
        args += [ck, cv]
    in_specs += [pl.BlockSpec(memory_space=pltpu.SMEM), _full((8, LANES, d))]
    args += [sink, w_out]
    n_keys = band_w + ctx[0].shape[2] if windowed else s
    return pl.pallas_call(
        functools.partial(_attn_swa_kernel, windowed, band_w),
        grid=(b, s // tq),
        in_specs=in_specs,
        out_specs=pl.BlockSpec((None, tq, d), lambda bi, t: (bi, t, 0)),
        out_shape=jax.ShapeDtypeStruct((b, s, d), F32),
        scratch_shapes=[pltpu.VMEM((tq, n_keys), F32) for _ in range(n_bufs)],
        compiler_params=_params(2),
        name="attn_swa",
    )(*args)


def _route(scores_t, bias):
    sel_t = scores_t + bias
    sel = [sel_t[e:e + 1] for e in range(N_EXPERTS)]
    raw = [scores_t[e:e + 1] for e in range(N_EXPERTS)]
    gscore = []
    for g in range(N_GROUPS):
        r = sel[g * 4:g * 4 + 4]
        pairs = [r[i] + r[j] for i in range(4) for j in range(i + 1, 4)]
        gscore.append(functools.reduce(jnp.maximum, pairs))
    best, gidx = gscore[0], jnp.zeros_like(gscore[0], dtype=jnp.int32)
    for g in range(1, N_GROUPS):
        take = gscore[g] > best
        best = jnp.where(take, gscore[g], best)
        gidx = jnp.where(take, g, gidx)
    vals = []
    for k in range(EXPERTS_PER_GROUP):
        v = sel[k]
        for g in range(1, N_GROUPS):
            v = jnp.where(gidx == g, sel[g * 4 + k], v)
        vals.append(v)

    def argmax4(vs):
        m, idx = vs[0], jnp.zeros_like(gidx)
        for k in range(1, 4):
            take = vs[k] > m
            m = jnp.where(take, vs[k], m)
            idx = jnp.where(take, k, idx)
        return idx

    i1 = argmax4(vals)
    i2 = argmax4([jnp.where(i1 == k, -jnp.inf, vals[k]) for k in range(4)])
    e1, e2 = gidx * 4 + i1, gidx * 4 + i2
    w1 = functools.reduce(lambda a, b: a + b, [jnp.where(e1 == e, raw[e], 0.0) for e in range(N_EXPERTS)])
    w2 = functools.reduce(lambda a, b: a + b, [jnp.where(e2 == e, raw[e], 0.0) for e in range(N_EXPERTS)])
    den = w1 + w2
    return e1, e2, w1 / den, w2 / den


SC_WINDOW = 128
SC_ROW = 128
N_PLANES = (D_MODEL // 2) // SC_ROW
MOE_ROW_BLOCK = 512
HALF_WORD = -65536


def _to_planes(packed, out_ref):
    for j in range(N_PLANES):
        out_ref[j] = packed[:, j * SC_ROW:(j + 1) * SC_ROW]


def _from_planes(ref):
    return jnp.concatenate([ref[j] for j in range(N_PLANES)], axis=1)


def _pack_bf16_pairs(x):
    n = x.shape[1] // 2
    lo = pltpu.bitcast(x[:, :n].astype(BF16).astype(F32), jnp.int32)
    hi = pltpu.bitcast(x[:, n:].astype(BF16).astype(F32), jnp.int32)
    return lax.shift_right_logical(lo, jnp.int32(16)) | (hi & jnp.int32(HALF_WORD))


def _unpack_bf16_pairs(p):
    lo = pltpu.bitcast(lax.shift_left(p, jnp.int32(16)), F32)
    hi = pltpu.bitcast(p & jnp.int32(HALF_WORD), F32)
    return jnp.concatenate([lo, hi], axis=1)


def _pick(idx, rows):
    return _tree(jnp.add, [jnp.where(idx == e, rows[e], 0.0) for e in range(N_EXPERTS)])


def _moe_route_kernel(x_ref, mod_ref, g_ref, rw_ref, rb_ref, tri_ref, hp_o, eidx_o, rank_o, gcol_o, cnt_o, seen):
    @pl.when(jnp.logical_and(pl.program_id(0) == 0, pl.program_id(1) == 0))
    def _():
        seen[...] = jnp.zeros_like(seen)

    tm = x_ref.shape[0]
    mod = mod_ref[0]
    h = _modulate(x_ref[...], g_ref[...], mod[3:4], mod[4:5])
    h16 = h.astype(BF16)
    _to_planes(_pack_bf16_pairs(h), hp_o)
    h_lo = (h - h16.astype(F32)).astype(BF16)
    part = _dot_nt(rw_ref[...], h16)
    logits_t = part[:N_EXPERTS] + part[N_EXPERTS:] + _dot_nt(rw_ref[:N_EXPERTS], h_lo)
    e1, e2, g1, g2 = _route(_sigmoid(logits_t), rb_ref[...])
    hot = jnp.concatenate([jnp.where(jnp.logical_or(e1 == e, e2 == e), 1.0, 0.0) for e in range(N_EXPERTS)], axis=0)
    before = _dot(hot.astype(BF16), tri_ref[...]) + seen[...]
    rows = [before[e:e + 1] for e in range(N_EXPERTS)]
    eidx_o[0], eidx_o[1] = e1, e2
    rank_o[0], rank_o[1] = _pick(e1, rows).astype(jnp.int32), _pick(e2, rows).astype(jnp.int32)
    gcol_o[...] = jnp.concatenate([g1, g2, jnp.zeros((LANES - 2, tm), F32)], axis=0).T
    seen[...] = seen[...] + jnp.sum(hot, axis=-1, keepdims=True)
    cnt_o[...] = seen[...]


def _moe_slots_kernel(mb, n_rows, eidx_ref, rank_ref, cnt_ref, idx_o, be_o):
    cnt = cnt_ref[...]
    padded = jnp.ceil(cnt / mb) * mb
    starts, ends, run = [], [], jnp.zeros((1, 1), F32)
    for e in range(N_EXPERTS):
        starts.append(run)
        run = run + padded[e:e + 1]
        ends.append(run)
    n_tiles, per_tile = eidx_ref.shape[1], eidx_ref.shape[3] // SC_WINDOW
    for k in range(2):
        for i in range(n_tiles):
            slot = rank_ref[k, i] + _pick(eidx_ref[k, i], starts).astype(jnp.int32)
            for j in range(N_PLANES):
                for q in range(per_tile):
                    row = ((k * N_PLANES + j) * n_tiles + i) * per_tile + q
                    idx_o[row:row + 1, :] = slot[:, q * SC_WINDOW:(q + 1) * SC_WINDOW] + j * n_rows
    first_row = lax.broadcasted_iota(jnp.int32, be_o.shape, 1).astype(F32) * mb
    be = _tree(jnp.add, [jnp.where(ends[e] <= first_row, 1.0, 0.0) for e in range(N_EXPERTS)])
    be_o[...] = jnp.minimum(be, N_EXPERTS - 1.0).astype(jnp.int32)


def _moe_ffn_kernel(be_ref, hs_ref, wg_ref, wu_ref, wd_ref, ys_o, wgu16, wd16):
    i = pl.program_id(0)

    @pl.when(jnp.logical_or(i == 0, be_ref[i] != be_ref[jnp.maximum(i - 1, 0)]))
    def _():
        wgu16[:, :D_EXPERT] = wg_ref[...].astype(BF16)
        wgu16[:, D_EXPERT:] = wu_ref[...].astype(BF16)
        wd16[...] = wd_ref[...].astype(BF16)

    h = _unpack_bf16_pairs(_from_planes(hs_ref)).astype(BF16)
    gu = _dot(h, wgu16[...])
    g, u = gu[:, :D_EXPERT], gu[:, D_EXPERT:]
    a = g * _sigmoid(g) * u
    _to_planes(_pack_bf16_pairs(_dot(a.astype(BF16), wd16[...])), ys_o)


def _moe_residual(x, z0_ref, z1_ref, gcol_ref, mod_ref):
    gc = gcol_ref[...]
    lane = lax.broadcasted_iota(jnp.int32, gc.shape, 1)
    g1 = jnp.sum(jnp.where(lane == 0, gc, 0.0), axis=-1, keepdims=True)
    g2 = jnp.sum(jnp.where(lane == 1, gc, 0.0), axis=-1, keepdims=True)
    y = g1 * _unpack_bf16_pairs(_from_planes(z0_ref)) + g2 * _unpack_bf16_pairs(_from_planes(z1_ref))
    return x + mod_ref[0][5:6] * y


def _moe_combine_kernel(x_ref, z0_ref, z1_ref, gcol_ref, mod_ref, fg_ref, o_ref):
    o_ref[...] = _rms(_moe_residual(x_ref[...], z0_ref, z1_ref, gcol_ref, mod_ref), fg_ref[...])


def _residual_specs(resid, row_fn, tm, d, in_specs, args):
    (z, gcol), mods_prev = resid
    for k in range(2):
        in_specs.append(pl.BlockSpec((None, N_PLANES, None, tm, SC_ROW), lambda bi, t, k=k: (k, 0, bi, t, 0)))
        args.append(z)
    in_specs += [pl.BlockSpec((None, tm, LANES), lambda bi, t: (bi, t, 0)),
                 pl.BlockSpec((1, 6, d), lambda bi, t: (row_fn(bi), 0, 0))]
    args += [gcol, mods_prev]


def _sc_mesh():
    return plsc.VectorSubcoreMesh(core_axis_name="core", subcore_axis_name="subcore")


def _sc_scatter_rows(rows, idx, n_out):
    n, w = rows.shape
    steps = n // SC_WINDOW

    @pl.kernel(out_type=jax.ShapeDtypeStruct((n_out, w), rows.dtype), mesh=_sc_mesh(), scratch_types=[])
    def scatter(x_hbm, i0_hbm, i1_hbm, o_hbm):
        def body(x_vmem, i0_vmem, i1_vmem):
            pltpu.sync_copy(x_vmem, o_hbm.at[i0_vmem.at[0]])
            pltpu.sync_copy(x_vmem, o_hbm.at[i1_vmem.at[0]])

        pltpu.emit_pipeline(
            body,
            grid=(steps,),
            in_specs=[pl.BlockSpec((SC_WINDOW, w), lambda i: (i, 0)),
                      pl.BlockSpec((1, SC_WINDOW), lambda i: (i, 0)),
                      pl.BlockSpec((1, SC_WINDOW), lambda i: (i + steps, 0))],
            out_specs=[],
            core_axis_name=("core", "subcore"),
            dimension_semantics=(pltpu.PARALLEL,),
        )(x_hbm, i0_hbm, i1_hbm)

    return scatter(rows, idx, idx)


def _sc_gather_rows(rows, idx):
    steps, w = idx.shape[0], rows.shape[1]

    @pl.kernel(out_type=jax.ShapeDtypeStruct((steps * SC_WINDOW, w), rows.dtype), mesh=_sc_mesh(),
               scratch_types=[])
    def gather(x_hbm, i_hbm, o_hbm):
        def body(i_vmem, o_vmem):
            pltpu.sync_copy(x_hbm.at[i_vmem.at[0]], o_vmem)

        pltpu.emit_pipeline(
            body,
            grid=(steps,),
            in_specs=[pl.BlockSpec((1, SC_WINDOW), lambda i: (i, 0))],
            out_specs=[pl.BlockSpec((SC_WINDOW, w), lambda i: (i, 0))],
            core_axis_name=("core", "subcore"),
            dimension_semantics=(pltpu.PARALLEL,),
        )(i_hbm, o_hbm)

    return gather(rows, idx)


def _moe_bucket(x, mods, row_fn, g, rw_t, rb, tm, mb):
    b, s, d = x.shape
    n, nt = b * s, (b * s) // tm
    per_b = s // tm
    tile = lambda bi, t: (0, bi * per_b + t, 0, 0)
    tri = jnp.asarray(np.triu(np.ones((tm, tm), np.float32), 1), BF16)
    hp, eidx, rank, gcol, cnt = pl.pallas_call(
        _moe_route_kernel,
        grid=(b, per_b),
        in_specs=[
            pl.BlockSpec((None, tm, d), lambda bi, t: (bi, t, 0)),
            pl.BlockSpec((1, 6, d), lambda bi, t: (row_fn(bi), 0, 0)),
            _full((1, d)), _full((2 * N_EXPERTS, d)), _full((N_EXPERTS, 1)), _full((tm, tm)),
        ],
        out_specs=[
            pl.BlockSpec((N_PLANES, tm, SC_ROW), lambda bi, t: (0, bi * per_b + t, 0)),
            pl.BlockSpec((2, None, 1, tm), tile), pl.BlockSpec((2, None, 1, tm), tile),
            pl.BlockSpec((None, tm, LANES), lambda bi, t: (bi, t, 0)),
            _full((N_EXPERTS, 1)),
        ],
        out_shape=[
            jax.ShapeDtypeStruct((N_PLANES, n, SC_ROW), jnp.int32),
            jax.ShapeDtypeStruct((2, nt, 1, tm), jnp.int32), jax.ShapeDtypeStruct((2, nt, 1, tm), jnp.int32),
            jax.ShapeDtypeStruct((b, s, LANES), F32),
            jax.ShapeDtypeStruct((N_EXPERTS, 1), F32),
        ],
        scratch_shapes=[pltpu.VMEM((N_EXPERTS, 1), F32)],
        compiler_params=_params(2),
        name="moe_route",
    )(x, mods, g, rw_t, rb, tri)

    n_rows = 2 * n + N_EXPERTS * mb
    nb = n_rows // mb
    nb_pad = -(-nb // LANES) * LANES
    idx, be = pl.pallas_call(
        functools.partial(_moe_slots_kernel, mb, n_rows),
        out_shape=[jax.ShapeDtypeStruct((2 * N_PLANES * n // SC_WINDOW, SC_WINDOW), jnp.int32),
                   jax.ShapeDtypeStruct((1, nb_pad), jnp.int32)],
        name="moe_slots",
    )(eidx, rank, cnt)

    hs = _sc_scatter_rows(hp.reshape(N_PLANES * n, SC_ROW), idx, N_PLANES * n_rows)
    return hs.reshape(N_PLANES, n_rows, SC_ROW), idx, be.reshape(nb_pad), gcol


def _moe_experts(bucket, layer, w_gate, w_up, w_down, mb, out_dims):
    hs, idx, be, gcol = bucket
    n_rows, d = hs.shape[1], w_gate.shape[2]
    plane_block = pl.BlockSpec((N_PLANES, mb, SC_ROW), lambda i, be_r: (0, i, 0))
    ys = pl.pallas_call(
        _moe_ffn_kernel,
        grid_spec=pltpu.PrefetchScalarGridSpec(
            num_scalar_prefetch=1,
            grid=(n_rows // mb,),
            in_specs=[
                plane_block,
                pl.BlockSpec((None, None, d, D_EXPERT), lambda i, be_r: (layer, be_r[i], 0, 0)),
                pl.BlockSpec((None, None, d, D_EXPERT), lambda i, be_r: (layer, be_r[i], 0, 0)),
                pl.BlockSpec((None, None, D_EXPERT, d), lambda i, be_r: (layer, be_r[i], 0, 0)),
            ],
            out_specs=plane_block,
            scratch_shapes=[pltpu.VMEM((d, 2 * D_EXPERT), BF16), pltpu.VMEM((D_EXPERT, d), BF16)],
        ),
        out_shape=jax.ShapeDtypeStruct((N_PLANES, n_rows, SC_ROW), jnp.int32),
        compiler_params=_params(1),
        name="moe_ffn",
    )(be, hs, w_gate, w_up, w_down)
    z = _sc_gather_rows(ys.reshape(N_PLANES * n_rows, SC_ROW), idx)
    return z.reshape(2, N_PLANES, *out_dims, SC_ROW), gcol.reshape(*out_dims, LANES)


def _moe_combine(x, mods, row_fn, gathered, final_g, tm):
    b, s, d = x.shape
    in_specs = [pl.BlockSpec((None, tm, d), lambda bi, t: (bi, t, 0))]
    args = [x]
    _residual_specs((gathered, mods), row_fn, tm, d, in_specs, args)
    in_specs.append(_full((1, d)))
    args.append(final_g)
    return pl.pallas_call(
        _moe_combine_kernel,
        grid=(b, s // tm),
        in_specs=in_specs,
        out_specs=pl.BlockSpec((None, tm, d), lambda bi, t: (bi, t, 0)),
        out_shape=jax.ShapeDtypeStruct((b, s, d), F32),
        compiler_params=_params(2),
        name="moe_combine",
    )(*args)


def _rope_tables(n_tokens, rot_dim, offset):
    n_rows = n_tokens // GRID_W
    rows = np.repeat(np.arange(n_rows), GRID_W)
    cols = np.tile(np.arange(GRID_W), n_rows)
    n_freq = rot_dim // 4
    inv = jnp.asarray(ROPE_THETA, F32) ** (-jnp.arange(n_freq, dtype=F32) / n_freq)
    ang = jnp.concatenate([jnp.asarray(rows, F32)[:, None] * inv, jnp.asarray(cols, F32)[:, None] * inv], axis=-1)
    cos, sin = jnp.repeat(jnp.cos(ang), 2, axis=-1), jnp.repeat(jnp.sin(ang), 2, axis=-1)
    odd = jnp.asarray(np.arange(rot_dim) % 2 == 1)
    parts = [cos, jnp.where(odd, sin, 0.0), jnp.where(odd, 0.0, -sin)]
    period = 64 if rot_dim == 64 else LANES
    fill = [1.0, 0.0, 0.0]
    out = []
    for p, f in zip(parts, fill):
        slot = jnp.full((n_tokens, period), f, F32).at[:, offset:offset + rot_dim].set(p)
        out.append(jnp.tile(slot, (1, LANES // period)))
    return jnp.stack(out)


def _pad_heads(w, n_heads, lo, hi):
    k = w.shape[0]
    w = w.reshape(k, n_heads, -1)[:, :, lo:hi]
    return jnp.pad(w, ((0, 0), (0, 0), (0, LANES - (hi - lo)))).reshape(k, n_heads * LANES)


def _ab_weights(j, ab_w_in, diff_lambda, diff_subln_g, mla_q_norm_g, mla_w_qb, mla_kv_norm_g, mla_w_kvb, ab_w_out):
    place = np.zeros((LANES, MLA_HEADS * LANES), np.float32)
    for hd in range(MLA_HEADS):
        for r in range(MLA_ROPE_DIM):
            place[r, hd * LANES + MLA_NOPE_DIM + r] = 1.0
    w_out_m = ab_w_out[j][512:].reshape(MLA_HEADS, MLA_V_DIM, D_MODEL)
    return {
        "w_in": jnp.pad(ab_w_in[j], ((0, 0), (0, AB_IN_PAD - AB_IN))).astype(BF16),
        "qn_g": mla_q_norm_g[j][None, :],
        "w_qb": _pad_heads(mla_w_qb[j], MLA_HEADS, 0, MLA_NOPE_DIM + MLA_ROPE_DIM).astype(BF16),
        "kvn_g": mla_kv_norm_g[j][None, :],
        "wk": _pad_heads(mla_w_kvb[j], MLA_HEADS, 0, MLA_NOPE_DIM).astype(BF16),
        "wv": _pad_heads(mla_w_kvb[j], MLA_HEADS, MLA_NOPE_DIM, MLA_NOPE_DIM + MLA_V_DIM).astype(BF16),
        "e": jnp.asarray(place, BF16),
        "w_out_d": ab_w_out[j][:512].reshape(DIFF_HEADS, LANES, D_MODEL).astype(BF16),
        "w_out_m": jnp.pad(w_out_m, ((0, 0), (0, LANES - MLA_V_DIM), (0, 0))).astype(BF16),
        "subln_g": diff_subln_g[j][None, :],
        "lam": diff_lambda[j],
    }


def kernel(x_prompt, x_sample, cache_diff_k, cache_diff_v, cache_mla_ckv, cache_mla_krope, cache_swa_k, cache_swa_v,
           c, c_ctx, ada_w, ada_b, norm1_g, norm2_g, final_norm_g, ab_w_in, diff_lambda, diff_subln_g, mla_q_norm_g,
           mla_w_qb, mla_kv_norm_g, mla_w_kvb, ab_w_out, swa_w_in, swa_sink, swa_w_out, router_w, router_bias,
           moe_w_gate, moe_w_up, moe_w_down):
    bp, sp, d = x_prompt.shape
    bs, ss, _ = x_sample.shape
    depth = ada_w.shape[0]
    past = cache_diff_k.shape[2]
    n_ab = cache_diff_k.shape[1]
    n_swa = cache_swa_k.shape[1]
    assert bs + 1 <= COND_ROWS and d == D_MODEL

    cond = jnp.zeros((COND_ROWS, d), F32).at[:bs].set(c).at[bs].set(c_ctx)
    mods = _ada_all(cond, ada_w, ada_b).reshape(depth, COND_ROWS, 6, d)
    row_s = lambda bi: bi
    row_p = lambda bi: bi * 0 + bs

    rope_d = _rope_tables(ss, DIFF_QK_DIM, 0)
    rope_q = _rope_tables(ss, MLA_ROPE_DIM, MLA_NOPE_DIM)
    rope_k = _rope_tables(ss, MLA_ROPE_DIM, 0)

    cdk = cache_diff_k.reshape(bs, n_ab, past, 512)
    cdv = cache_diff_v.reshape(bs, n_ab, past, 512)
    ckr = jnp.pad(cache_mla_krope, ((0, 0), (0, 0), (0, 0), (0, LANES - MLA_ROPE_DIM)))
    csk = cache_swa_k.reshape(bs, n_swa, past, 256)
    csv = cache_swa_v.reshape(bs, n_swa, past, 256)
    rw_hi = router_w.T.astype(BF16)
    rw_t = jnp.concatenate([rw_hi, (router_w.T - rw_hi.astype(F32)).astype(BF16)], axis=0)
    rb = router_bias[:, None]

    tp = min(256, sp)
    ts = min(256, ss)
    tm_s = min(512, ss)
    tm_p = min(512, bp * sp)

    xp, xs = x_prompt, x_sample
    new_ab, new_swa = None, None
    pend_p = pend_s = None

    def project(fn, x, pend, *a):
        outs = list(fn(x, *a, pend))
        return (outs.pop() if pend is not None else x), outs

    for l in range(depth):
        j = l // 2
        ml = mods[l]
        g1, g2 = norm1_g[l][None, :], norm2_g[l][None, :]
        if l % 2 == 0:
            lambda_init = 0.8 - 0.6 * math.exp(-0.3 * l)
            wts = _ab_weights(j, ab_w_in, diff_lambda, diff_subln_g, mla_q_norm_g, mla_w_qb, mla_kv_norm_g,
                              mla_w_kvb, ab_w_out)
            xp, (qa, ka, va, qb, km, vm, *new_ab) = project(_proj_ab, xp, pend_p, ml, row_p, g1, wts, None,
                                                             (j, n_ab, new_ab), tp)
            xp = _attn_ab(xp, ml, row_p, (qa, qb), [(ka, va, km, vm)], wts, lambda_init, tp, DIFF_HEADS + MLA_HEADS)
            xs, (qa, ka, va, qb, km, vm) = project(_proj_ab, xs, pend_s, ml, row_s, g1, wts,
                                                   (rope_d, rope_q, rope_k), None, ts)
            cache_seg = _cache_ab(cdk, cdv, cache_mla_ckv, ckr, j, wts)
            xs = _attn_ab(xs, ml, row_s, (qa, qb), [(ka, va, km, vm), tuple(cache_seg)], wts, lambda_init, ts, 2)
        else:
            w_in = swa_w_in[j].astype(BF16)
            w_out = swa_w_out[j].reshape(8, LANES, d).astype(BF16)
            xp, (q, k, v, *new_swa) = project(_proj_swa, xp, pend_p, ml, row_p, g1, w_in, None,
                                              (j, n_swa, new_swa), tp)
            xp = _attn_swa(xp, ml, row_p, q, k, v, None, swa_sink[j], w_out, tp, SWA_HEADS)
            xs, (q, k, v) = project(_proj_swa, xs, pend_s, ml, row_s, g1, w_in, rope_d, None, ts)
            xs = _attn_swa(xs, ml, row_s, q, k, v, (csk, csv, j), swa_sink[j], w_out, ts, 2)
        bucket_p = _moe_bucket(xp.reshape(1, bp * sp, d), ml, row_p, g2, rw_t, rb, tm_p, MOE_ROW_BLOCK)
        bucket_s = _moe_bucket(xs, ml, row_s, g2, rw_t, rb, tm_s, MOE_ROW_BLOCK)
        got_p = _moe_experts(bucket_p, l, moe_w_gate, moe_w_up, moe_w_down, MOE_ROW_BLOCK, (bp, sp))
        got_s = _moe_experts(bucket_s, l, moe_w_gate, moe_w_up, moe_w_down, MOE_ROW_BLOCK, (bs, ss))
        pend_p, pend_s = (got_p, ml), (got_s, ml)

    xp = _moe_combine(xp, pend_p[1], row_p, pend_p[0], final_norm_g[None, :], tp)
    xs = _moe_combine(xs, pend_s[1], row_s, pend_s[0], final_norm_g[None, :], ts)

    new_diff_k = new_ab[0].reshape(bp, n_ab, sp, DIFF_HEADS, 2, DIFF_QK_DIM)
    new_diff_v = new_ab[1].reshape(bp, n_ab, sp, DIFF_HEADS, DIFF_V_DIM)
    new_mla_ckv, new_mla_krope = new_ab[2], new_ab[3]
    new_swa_k = new_swa[0].reshape(bp, n_swa, sp, SWA_KV_HEADS, SWA_HEAD_DIM)
    new_swa_v = new_swa[1].reshape(bp, n_swa, sp, SWA_KV_HEADS, SWA_HEAD_DIM)
    return (xp, xs, new_diff_k, new_diff_v, new_mla_ckv, new_mla_krope, new_swa_k, new_swa_v)
```

```python
import functools
import math

import jax
import jax.numpy as jnp
import numpy as np
from jax import lax
from jax.experimental import pallas as pl
from jax.experimental.pallas import tpu as pltpu
from jax.experimental.pallas import tpu_sc as plsc

F32 = jnp.float32
BF16 = jnp.bfloat16

D_MODEL = 1024
GRID_W = 64
ROPE_THETA = 10000.0
NORM_EPS = 1e-6
NEG_INF = -1e30
LOG2E = math.log2(math.e)
LANES = 128

DIFF_HEADS = 4
DIFF_QK_DIM = 64
DIFF_V_DIM = 128
MLA_HEADS = 8
MLA_Q_RANK = 384
MLA_KV_RANK = 256
MLA_NOPE_DIM = 64
MLA_ROPE_DIM = 32
MLA_V_DIM = 64
AB_IN = 3 * 512 + MLA_Q_RANK + MLA_KV_RANK + MLA_ROPE_DIM
AB_IN_PAD = 2304
SWA_HEADS = 16
SWA_KV_HEADS = 4
SWA_GROUP = 4
SWA_HEAD_DIM = 64
WINDOW = 128
N_EXPERTS = 16
N_GROUPS = 4
EXPERTS_PER_GROUP = 4
D_EXPERT = 256
COND_ROWS = 16

VMEM_LIMIT = 56 * 1024 * 1024


def _full(shape):
    n = len(shape)
    return pl.BlockSpec(shape, lambda *_: (0,) * n)


def _params(n_axes):
    return pltpu.CompilerParams(dimension_semantics=("arbitrary",) * n_axes, vmem_limit_bytes=VMEM_LIMIT)


def _sigmoid(x):
    return 1.0 / (1.0 + jnp.exp(-x))


def _rms(x, g):
    return x * lax.rsqrt(jnp.mean(x * x, axis=-1, keepdims=True) + NORM_EPS) * g


def _modulate(x, g, shift, scale):
    return _rms(x, g) * (1.0 + scale) + shift


def _dot(a, b):
    return jnp.dot(a, b, preferred_element_type=F32)


def _dot_nt(a, b):
    return lax.dot_general(a, b, (((1,), (1,)), ((), ())), preferred_element_type=F32)


def _rope(x, tab_ref):
    c, s_odd, s_even = tab_ref[0], tab_ref[1], tab_ref[2]
    out = []
    for i in range(x.shape[1] // LANES):
        xi = x[:, i * LANES:(i + 1) * LANES]
        out.append(xi * c + pltpu.roll(xi, 1, 1) * s_odd + pltpu.roll(xi, LANES - 1, 1) * s_even)
    return out[0] if len(out) == 1 else jnp.concatenate(out, axis=1)


def _tree(op, xs):
    xs = list(xs)
    while len(xs) > 1:
        xs = [op(xs[i], xs[i + 1]) if i + 1 < len(xs) else xs[i] for i in range(0, len(xs), 2)]
    return xs[0]


def _lane_fold(op, x):
    return _tree(op, [x[:, i * LANES:(i + 1) * LANES] for i in range(x.shape[1] // LANES)])


def _ada_kernel(cond_ref, w_ref, b_ref, o_ref):
    c = cond_ref[...]
    a = (c * _sigmoid(c)).astype(BF16)
    o_ref[...] = _dot(a, w_ref[...].astype(BF16)) + b_ref[...]


def _ada_all(cond, ada_w, ada_b):
    depth, d, n = ada_w.shape
    tn = 1536
    return pl.pallas_call(
        _ada_kernel,
        grid=(depth, n // tn),
        in_specs=[
            _full((COND_ROWS, d)),
            pl.BlockSpec((None, d, tn), lambda l, j: (l, 0, j)),
            pl.BlockSpec((None, 1, tn), lambda l, j: (l, 0, j)),
        ],
        out_specs=pl.BlockSpec((None, COND_ROWS, tn), lambda l, j: (l, 0, j)),
        out_shape=jax.ShapeDtypeStruct((depth, COND_ROWS, n), F32),
        compiler_params=_params(2),
        name="ada_mod",
    )(cond, ada_w, ada_b.reshape(depth, 1, n))


def _proj_ab_kernel(rope, emit_cache, n_prev, fused, x_ref, mod_ref, g_ref, w_in_ref, qn_ref, wqb_ref, kvn_ref,
                    wk_ref, wv_ref, e_ref, *rest):
    if rope:
        rope_d, rope_q, rope_k = rest[:3]
        rest = rest[3:]
    x = x_ref[...]
    if fused:
        x = _moe_residual(x, *rest[:4])
        rest[-1][...] = x
        rest = rest[4:]
    rest = rest[n_prev:]
    qa_o, ka_o, va_o, qb_o, km_o, vm_o = rest[:6]
    mod = mod_ref[0]
    h = _modulate(x, g_ref[...], mod[0:1], mod[1:2]).astype(BF16)
    big = _dot(h, w_in_ref[...])
    qa, ka, va = big[:, 0:512], big[:, 512:1024], big[:, 1024:1536]
    q_lat, ckv, kr = big[:, 1536:1920], big[:, 1920:2176], big[:, 2176:2304]
    qb = _dot(_rms(q_lat, qn_ref[...]).astype(BF16), wqb_ref[...])
    ckv_n = _rms(ckv, kvn_ref[...])
    if emit_cache:
        ka32_o, va32_o, ckv32_o, kr32_o = rest[6:10]
        ka32_o[...] = ka
        va32_o[...] = va
        ckv32_o[...] = ckv_n
        kr32_o[...] = kr[:, :MLA_ROPE_DIM]
    if rope:
        qa, ka = _rope(qa, rope_d), _rope(ka, rope_d)
        qb, kr = _rope(qb, rope_q), _rope(kr, rope_k)
    qa = qa * (DIFF_QK_DIM ** -0.5 * LOG2E)
    qb = qb * ((MLA_NOPE_DIM + MLA_ROPE_DIM) ** -0.5 * LOG2E)
    ckv16 = ckv_n.astype(BF16)
    km = _dot(ckv16, wk_ref[...]) + _dot(kr.astype(BF16), e_ref[...])
    vm = _dot(ckv16, wv_ref[...])
    lo = lax.broadcasted_iota(jnp.int32, (qa.shape[0], LANES), 1) < DIFF_QK_DIM
    for hd in range(DIFF_HEADS):
        sl = slice(hd * LANES, (hd + 1) * LANES)
        qa_o[2 * hd] = jnp.where(lo, qa[:, sl], 0.0).astype(BF16)
        qa_o[2 * hd + 1] = jnp.where(lo, 0.0, qa[:, sl]).astype(BF16)
        ka_o[hd] = ka[:, sl].astype(BF16)
        va_o[hd] = va[:, sl].astype(BF16)
    for hd in range(MLA_HEADS):
        sl = slice(hd * LANES, (hd + 1) * LANES)
        qb_o[hd] = qb[:, sl].astype(BF16)
        km_o[hd] = km[:, sl].astype(BF16)
        vm_o[hd] = vm[:, sl].astype(BF16)


def _cache_outputs(cache, widths, b, s, tm, in_specs, args, n_fixed_outs):
    if cache is None:
        return [], {}, 0
    j, n_layers, prev = cache
    outs = [(jax.ShapeDtypeStruct((b, n_layers, s, w), F32),
             pl.BlockSpec((None, None, tm, w), lambda bi, t: (bi, j, t, 0))) for w in widths]
    aliases = {}
    if prev is not None:
        for i, arr in enumerate(prev):
            aliases[len(args)] = n_fixed_outs + i
            in_specs.append(pl.BlockSpec(memory_space=pl.ANY))
            args.append(arr)
    return outs, aliases, len(aliases)


def _proj_ab(x, mods, row_fn, g, wts, rope_tabs, cache, tm, resid=None):
    b, s, d = x.shape
    rope = rope_tabs is not None
    tok = lambda bi, t: (bi, t, 0)
    head = lambda bi, t: (bi, 0, t, 0)
    in_specs = [
        pl.BlockSpec((None, tm, d), tok),
        pl.BlockSpec((1, 6, d), lambda bi, t: (row_fn(bi), 0, 0)),
        _full((1, d)),
        _full((d, AB_IN_PAD)), _full((1, MLA_Q_RANK)), _full((MLA_Q_RANK, 1024)), _full((1, MLA_KV_RANK)),
        _full((MLA_KV_RANK, 1024)), _full((MLA_KV_RANK, 1024)), _full((LANES, 1024)),
    ]
    args = [x, mods, g, wts["w_in"], wts["qn_g"], wts["w_qb"], wts["kvn_g"], wts["wk"], wts["wv"], wts["e"]]
    if rope:
        in_specs += [pl.BlockSpec((3, tm, LANES), lambda bi, t: (0, t, 0))] * 3
        args += list(rope_tabs)
    if resid is not None:
        _residual_specs(resid, row_fn, tm, d, in_specs, args)

    def hm(nh):
        return jax.ShapeDtypeStruct((b, nh, s, LANES), BF16), pl.BlockSpec((None, nh, tm, LANES), head)

    outs = [hm(8), hm(4), hm(4), hm(8), hm(8), hm(8)]
    cache_outs, aliases, n_prev = _cache_outputs(cache, (512, 512, MLA_KV_RANK, MLA_ROPE_DIM), b, s, tm, in_specs,
                                                 args, len(outs))
    outs += cache_outs
    if resid is not None:
        outs.append((jax.ShapeDtypeStruct((b, s, d), F32), pl.BlockSpec((None, tm, d), tok)))
    return pl.pallas_call(
        functools.partial(_proj_ab_kernel, rope, cache is not None, n_prev, resid is not None),
        grid=(b, s // tm),
        in_specs=in_specs,
        out_specs=[o[1] for o in outs],
        out_shape=[o[0] for o in outs],
        input_output_aliases=aliases,
        compiler_params=_params(2),
        name="proj_ab",
    )(*args)


def _cache_ab_kernel(dk_ref, dv_ref, ckv_ref, kr_ref, wk_ref, wv_ref, e_ref, ck_o, cv_o, km_o, vm_o):
    ckv16 = ckv_ref[...].astype(BF16)
    km = _dot(ckv16, wk_ref[...]) + _dot(kr_ref[...].astype(BF16), e_ref[...])
    vm = _dot(ckv16, wv_ref[...])
    for hd in range(DIFF_HEADS):
        sl = slice(hd * LANES, (hd + 1) * LANES)
        ck_o[hd] = dk_ref[:, sl].astype(BF16)
        cv_o[hd] = dv_ref[:, sl].astype(BF16)
    for hd in range(MLA_HEADS):
        sl = slice(hd * LANES, (hd + 1) * LANES)
        km_o[hd] = km[:, sl].astype(BF16)
        vm_o[hd] = vm[:, sl].astype(BF16)


def _cache_ab(cdk, cdv, cckv, ckr, j, wts):
    b, _, p, _ = cdk.shape
    lay = lambda bi: (bi, j, 0, 0)

    def hm(nh):
        return (jax.ShapeDtypeStruct((b, nh, p, LANES), BF16),
                pl.BlockSpec((None, nh, p, LANES), lambda bi: (bi, 0, 0, 0)))

    outs = [hm(4), hm(4), hm(8), hm(8)]
    return pl.pallas_call(
        _cache_ab_kernel,
        grid=(b,),
        in_specs=[
            pl.BlockSpec((None, None, p, 512), lay), pl.BlockSpec((None, None, p, 512), lay),
            pl.BlockSpec((None, None, p, MLA_KV_RANK), lay), pl.BlockSpec((None, None, p, LANES), lay),
            _full((MLA_KV_RANK, 1024)), _full((MLA_KV_RANK, 1024)), _full((LANES, 1024)),
        ],
        out_specs=[o[1] for o in outs],
        out_shape=[o[0] for o in outs],
        compiler_params=_params(1),
        name="cache_ab",
    )(cdk, cdv, cckv, ckr, wts["wk"], wts["wv"], wts["e"])


def _attn_ab_kernel(n_seg, lambda_init, x_ref, mod_ref, qa_ref, qb_ref, *rest):
    segs = [rest[4 * i:4 * i + 4] for i in range(n_seg)]
    wd_ref, wm_ref, sg_ref, lam_ref, o_ref = rest[4 * n_seg:4 * n_seg + 5]
    bufs = rest[4 * n_seg + 5:]
    lp = lam_ref[...]
    lam = (jnp.exp(jnp.sum(lp[0:1] * lp[1:2], axis=-1, keepdims=True))
           - jnp.exp(jnp.sum(lp[2:3] * lp[3:4], axis=-1, keepdims=True)) + lambda_init)
    tq, d = x_ref.shape
    widths = [sg[0].shape[1] for sg in segs]
    offs = [sum(widths[:i]) for i in range(n_seg)]
    units = [("diff", hd) for hd in range(DIFF_HEADS)] + [("mla", hd) for hd in range(MLA_HEADS)]

    def scores(u, buf):
        kind, hd = units[u]
        qs = [qa_ref[2 * hd], qa_ref[2 * hd + 1]] if kind == "diff" else [qb_ref[hd]]
        ks = [sg[0 if kind == "diff" else 2][hd] for sg in segs]
        ms = []
        for i, q in enumerate(qs):
            folds = []
            for k, off, w in zip(ks, offs, widths):
                s = _dot_nt(q, k)
                buf[i, :, off:off + w] = s
                folds.append(_lane_fold(jnp.maximum, s))
            ms.append(jnp.max(_tree(jnp.maximum, folds), axis=-1, keepdims=True))
        return ms

    def exps(buf, i, m):
        e = [jnp.exp2(buf[i, :, off:off + w] - m) for off, w in zip(offs, widths)]
        l = jnp.sum(_tree(jnp.add, [_lane_fold(jnp.add, x) for x in e]), axis=-1, keepdims=True)
        return e, l

    def consume(u, buf, ms, acc):
        kind, hd = units[u]
        if kind == "diff":
            vs = [sg[1][hd] for sg in segs]
            (e1, l1), (e2, l2) = exps(buf, 0, ms[0]), exps(buf, 1, ms[1])
            c1, c2 = 1.0 / l1, lam / l2
            o = _tree(jnp.add, [_dot((a * c1 - b * c2).astype(BF16), v) for a, b, v in zip(e1, e2, vs)])
            od = _rms(o, sg_ref[...]) * (1.0 - lambda_init)
            return acc + _dot(od.astype(BF16), wd_ref[hd])
        vs = [sg[3][hd] for sg in segs]
        e, l = exps(buf, 0, ms[0])
        o = _tree(jnp.add, [_dot(x.astype(BF16), v) for x, v in zip(e, vs)]) / l
        return acc + _dot(o.astype(BF16), wm_ref[hd])

    acc = jnp.zeros((tq, d), F32)
    ms = scores(0, bufs[0])
    for u in range(len(units)):
        nxt = scores(u + 1, bufs[(u + 1) % len(bufs)]) if u + 1 < len(units) else None
        acc = consume(u, bufs[u % len(bufs)], ms, acc)
        ms = nxt
    o_ref[...] = x_ref[...] + mod_ref[0][2:3] * acc


def _attn_ab(x, mods, row_fn, q_parts, seg_list, wts, lambda_init, tq, n_bufs):
    b, s, d = x.shape
    qa, qb = q_parts
    in_specs = [
        pl.BlockSpec((None, tq, d), lambda bi, t: (bi, t, 0)),
        pl.BlockSpec((1, 6, d), lambda bi, t: (row_fn(bi), 0, 0)),
        pl.BlockSpec((None, 8, tq, LANES), lambda bi, t: (bi, 0, t, 0)),
        pl.BlockSpec((None, 8, tq, LANES), lambda bi, t: (bi, 0, t, 0)),
    ]
    args = [x, mods, qa, qb]
    kv_mode = dict(pipeline_mode=pl.Buffered(1)) if s // tq > 1 else {}
    for seg in seg_list:
        for arr in seg:
            nh, nk = arr.shape[1], arr.shape[2]
            in_specs.append(pl.BlockSpec((None, nh, nk, LANES), lambda bi, t: (bi, 0, 0, 0), **kv_mode))
            args.append(arr)
    in_specs += [_full((DIFF_HEADS, LANES, d)), _full((MLA_HEADS, LANES, d)), _full((1, LANES)),
                 _full((4, DIFF_QK_DIM))]
    args += [wts["w_out_d"], wts["w_out_m"], wts["subln_g"], wts["lam"]]
    n_keys = sum(seg[0].shape[2] for seg in seg_list)
    return pl.pallas_call(
        functools.partial(_attn_ab_kernel, len(seg_list), lambda_init),
        grid=(b, s // tq),
        in_specs=in_specs,
        out_specs=pl.BlockSpec((None, tq, d), lambda bi, t: (bi, t, 0)),
        out_shape=jax.ShapeDtypeStruct((b, s, d), F32),
        scratch_shapes=[pltpu.VMEM((2, tq, n_keys), F32) for _ in range(n_bufs)],
        compiler_params=_params(2),
        name="attn_ab",
    )(*args)


def _proj_swa_kernel(rope, emit_cache, n_prev, fused, x_ref, mod_ref, g_ref, w_in_ref, *rest):
    if rope:
        rope_d = rest[0]
        rest = rest[1:]
    x = x_ref[...]
    if fused:
        x = _moe_residual(x, *rest[:4])
        rest[-1][...] = x
        rest = rest[4:]
    rest = rest[n_prev:]
    q_o, k_o, v_o = rest[:3]
    mod = mod_ref[0]
    h = _modulate(x, g_ref[...], mod[0:1], mod[1:2]).astype(BF16)
    big = _dot(h, w_in_ref[...])
    q, k, v = big[:, :1024], big[:, 1024:1280], big[:, 1280:1536]
    if emit_cache:
        k32_o, v32_o = rest[3:5]
        k32_o[...] = k
        v32_o[...] = v
    if rope:
        q, k = _rope(q, rope_d), _rope(k, rope_d)
    q = q * (SWA_HEAD_DIM ** -0.5 * LOG2E)
    lo = lax.broadcasted_iota(jnp.int32, (q.shape[0], LANES), 1) < SWA_HEAD_DIM
    for pair in range(2):
        for grp in range(SWA_GROUP):
            ca, cb = (2 * pair) * 2 + grp // 2, (2 * pair + 1) * 2 + grp // 2
            a = q[:, ca * LANES:(ca + 1) * LANES]
            bb = q[:, cb * LANES:(cb + 1) * LANES]
            if grp % 2 == 0:
                bb = pltpu.roll(bb, SWA_HEAD_DIM, 1)
            else:
                a = pltpu.roll(a, SWA_HEAD_DIM, 1)
            q_o[pair * SWA_GROUP + grp] = jnp.where(lo, a, bb).astype(BF16)
    for kvh in range(SWA_KV_HEADS):
        sl = slice((kvh // 2) * LANES, (kvh // 2 + 1) * LANES)
        keep = lo if kvh % 2 == 0 else jnp.logical_not(lo)
        k_o[kvh] = jnp.where(keep, k[:, sl], 0.0).astype(BF16)
        v_o[kvh] = jnp.where(keep, v[:, sl], 0.0).astype(BF16)


def _proj_swa(x, mods, row_fn, g, w_in, rope_tab, cache, tm, resid=None):
    b, s, d = x.shape
    rope = rope_tab is not None
    tok = lambda bi, t: (bi, t, 0)
    head = lambda bi, t: (bi, 0, t, 0)
    in_specs = [
        pl.BlockSpec((None, tm, d), tok),
        pl.BlockSpec((1, 6, d), lambda bi, t: (row_fn(bi), 0, 0)),
        _full((1, d)), _full((d, 1536)),
    ]
    args = [x, mods, g, w_in]
    if rope:
        in_specs.append(pl.BlockSpec((3, tm, LANES), lambda bi, t: (0, t, 0)))
        args.append(rope_tab)
    if resid is not None:
        _residual_specs(resid, row_fn, tm, d, in_specs, args)

    def hm(nh):
        return jax.ShapeDtypeStruct((b, nh, s, LANES), BF16), pl.BlockSpec((None, nh, tm, LANES), head)

    outs = [hm(8), hm(4), hm(4)]
    cache_outs, aliases, n_prev = _cache_outputs(cache, (256, 256), b, s, tm, in_specs, args, len(outs))
    outs += cache_outs
    if resid is not None:
        outs.append((jax.ShapeDtypeStruct((b, s, d), F32), pl.BlockSpec((None, tm, d), tok)))
    return pl.pallas_call(
        functools.partial(_proj_swa_kernel, rope, cache is not None, n_prev, resid is not None),
        grid=(b, s // tm),
        in_specs=in_specs,
        out_specs=[o[1] for o in outs],
        out_shape=[o[0] for o in outs],
        input_output_aliases=aliases,
        compiler_params=_params(2),
        name="proj_swa",
    )(*args)


def _attn_swa_kernel(windowed, band_w, x_ref, mod_ref, q_ref, k_ref, v_ref, *rest):
    if windowed:
        ck_ref, cv_ref = rest[:2]
        rest = rest[2:]
    sink_ref, w_ref, o_ref = rest[:3]
    bufs = rest[3:]
    tq, d = x_ref.shape
    n_keys = k_ref.shape[1]
    lo_k = lax.broadcasted_iota(jnp.int32, (1, LANES), 1) < SWA_HEAD_DIM
    if windowed:
        start = pl.program_id(1) * tq
        bstart = pl.multiple_of(jnp.clip(start - WINDOW, 0, n_keys - band_w), LANES)
        qpos = start + lax.broadcasted_iota(jnp.int32, (tq, band_w), 0)
        kpos = bstart + lax.broadcasted_iota(jnp.int32, (tq, band_w), 1)
        bias = jnp.where(jnp.abs(qpos - kpos) <= WINDOW, 0.0, NEG_INF)

    def keys_values(kvh):
        if not windowed:
            return [k_ref[kvh]], [v_ref[kvh]]
        keep = lo_k if kvh % 2 == 0 else jnp.logical_not(lo_k)
        sl = slice((kvh // 2) * LANES, (kvh // 2 + 1) * LANES)
        kc = jnp.where(keep, ck_ref[:, sl], 0.0).astype(BF16)
        vc = jnp.where(keep, cv_ref[:, sl], 0.0).astype(BF16)
        return ([k_ref[kvh, pl.ds(bstart, band_w), :], kc], [v_ref[kvh, pl.ds(bstart, band_w), :], vc])

    units = [(kvh, g) for kvh in range(SWA_KV_HEADS) for g in range(SWA_GROUP)]

    def scores(u, buf):
        kvh, g = units[u]
        q = q_ref[(kvh // 2) * SWA_GROUP + g]
        sink = sink_ref[kvh * SWA_GROUP + g] * LOG2E
        folds, off = [], 0
        for i, k in enumerate(keys_values(kvh)[0]):
            s = _dot_nt(q, k)
            if windowed and i == 0:
                s = s + bias
            buf[:, off:off + s.shape[1]] = s
            folds.append(_lane_fold(jnp.maximum, s))
            off += s.shape[1]
        m = jnp.maximum(jnp.max(_tree(jnp.maximum, folds), axis=-1, keepdims=True), sink)
        return m, sink

    def consume(u, buf, m, sink):
        vs = keys_values(units[u][0])[1]
        e, off = [], 0
        for v in vs:
            e.append(jnp.exp2(buf[:, off:off + v.shape[0]] - m))
            off += v.shape[0]
        l = jnp.sum(_tree(jnp.add, [_lane_fold(jnp.add, x) for x in e]), axis=-1, keepdims=True) + jnp.exp2(sink - m)
        return _tree(jnp.add, [_dot(x.astype(BF16), v) for x, v in zip(e, vs)]) / l

    acc = jnp.zeros((tq, d), F32)
    nxt = scores(0, bufs[0])
    prev = None
    for u, (kvh, g) in enumerate(units):
        cur = nxt
        if u + 1 < len(units):
            nxt = scores(u + 1, bufs[(u + 1) % len(bufs)])
        o = consume(u, bufs[u % len(bufs)], *cur)
        if g % 2 == 0:
            prev = o
            continue
        if kvh % 2 == 0:
            slab = prev + pltpu.roll(o, SWA_HEAD_DIM, 1)
        else:
            slab = pltpu.roll(prev, SWA_HEAD_DIM, 1) + o
        acc = acc + _dot(slab.astype(BF16), w_ref[kvh * 2 + g // 2])
    o_ref[...] = x_ref[...] + mod_ref[0][2:3] * acc


def _attn_swa(x, mods, row_fn, q, k, v, ctx, sink, w_out, tq, n_bufs):
    b, s, d = x.shape
    windowed = ctx is not None
    band_w = min(tq + 2 * WINDOW, s)
    kv_spec = pl.BlockSpec((None, SWA_KV_HEADS, s, LANES), lambda bi, t: (bi, 0, 0, 0))
    in_specs = [
        pl.BlockSpec((None, tq, d), lambda bi, t: (bi, t, 0)),
        pl.BlockSpec((1, 6, d), lambda bi, t: (row_fn(bi), 0, 0)),
        pl.BlockSpec((None, 8, tq, LANES), lambda bi, t: (bi, 0, t, 0)),
        kv_spec, kv_spec,
    ]
    args = [x, mods, q, k, v]
    if windowed:
        ck, cv, j = ctx
        p = ck.shape[2]
        spec = pl.BlockSpec((None, None, p, 256), lambda bi, t: (bi, j, 0, 0))
        in_specs += [spec, spec]
        args += [ck, cv]
    in_specs += [pl.BlockSpec(memory_space=pltpu.SMEM), _full((8, LANES, d))]
    args += [sink, w_out]
    n_keys = band_w + ctx[0].shape[2] if windowed else s
    return pl.pallas_call(
        functools.partial(_attn_swa_kernel, windowed, band_w),
        grid=(b, s // tq),
        in_specs=in_specs,
        out_specs=pl.BlockSpec((None, tq, d), lambda bi, t: (bi, t, 0)),
        out_shape=jax.ShapeDtypeStruct((b, s, d), F32),
        scratch_shapes=[pltpu.VMEM((tq, n_keys), F32) for _ in range(n_bufs)],
        compiler_params=_params(2),
        name="attn_swa",
    )(*args)


def _route(scores_t, bias):
    sel_t = scores_t + bias
    sel = [sel_t[e:e + 1] for e in range(N_EXPERTS)]
    raw = [scores_t[e:e + 1] for e in range(N_EXPERTS)]
    gscore = []
    for g in range(N_GROUPS):
        r = sel[g * 4:g * 4 + 4]
        pairs = [r[i] + r[j] for i in range(4) for j in range(i + 1, 4)]
        gscore.append(functools.reduce(jnp.maximum, pairs))
    best, gidx = gscore[0], jnp.zeros_like(gscore[0], dtype=jnp.int32)
    for g in range(1, N_GROUPS):
        take = gscore[g] > best
        best = jnp.where(take, gscore[g], best)
        gidx = jnp.where(take, g, gidx)
    vals = []
    for k in range(EXPERTS_PER_GROUP):
        v = sel[k]
        for g in range(1, N_GROUPS):
            v = jnp.where(gidx == g, sel[g * 4 + k], v)
        vals.append(v)

    def argmax4(vs):
        m, idx = vs[0], jnp.zeros_like(gidx)
        for k in range(1, 4):
            take = vs[k] > m
            m = jnp.where(take, vs[k], m)
            idx = jnp.where(take, k, idx)
        return idx

    i1 = argmax4(vals)
    i2 = argmax4([jnp.where(i1 == k, -jnp.inf, vals[k]) for k in range(4)])
    e1, e2 = gidx * 4 + i1, gidx * 4 + i2
    w1 = functools.reduce(lambda a, b: a + b, [jnp.where(e1 == e, raw[e], 0.0) for e in range(N_EXPERTS)])
    w2 = functools.reduce(lambda a, b: a + b, [jnp.where(e2 == e, raw[e], 0.0) for e in range(N_EXPERTS)])
    den = w1 + w2
    return e1, e2, w1 / den, w2 / den


SC_WINDOW = 128
SC_ROW = 128
N_PLANES = (D_MODEL // 2) // SC_ROW
MOE_ROW_BLOCK = 512
HALF_WORD = -65536


def _to_planes(packed, out_ref):
    for j in range(N_PLANES):
        out_ref[j] = packed[:, j * SC_ROW:(j + 1) * SC_ROW]


def _from_planes(ref):
    return jnp.concatenate([ref[j] for j in range(N_PLANES)], axis=1)


def _pack_bf16_pairs(x):
    n = x.shape[1] // 2
    lo = pltpu.bitcast(x[:, :n].astype(BF16).astype(F32), jnp.int32)
    hi = pltpu.bitcast(x[:, n:].astype(BF16).astype(F32), jnp.int32)
    return lax.shift_right_logical(lo, jnp.int32(16)) | (hi & jnp.int32(HALF_WORD))


def _unpack_bf16_pairs(p):
    lo = pltpu.bitcast(lax.shift_left(p, jnp.int32(16)), F32)
    hi = pltpu.bitcast(p & jnp.int32(HALF_WORD), F32)
    return jnp.concatenate([lo, hi], axis=1)


def _pick(idx, rows):
    return _tree(jnp.add, [jnp.where(idx == e, rows[e], 0.0) for e in range(N_EXPERTS)])


def _moe_route_kernel(x_ref, mod_ref, g_ref, rw_ref, rb_ref, tri_ref, hp_o, eidx_o, rank_o, gcol_o, cnt_o, seen):
    @pl.when(jnp.logical_and(pl.program_id(0) == 0, pl.program_id(1) == 0))
    def _():
        seen[...] = jnp.zeros_like(seen)

    tm = x_ref.shape[0]
    mod = mod_ref[0]
    h = _modulate(x_ref[...], g_ref[...], mod[3:4], mod[4:5])
    h16 = h.astype(BF16)
    _to_planes(_pack_bf16_pairs(h), hp_o)
    h_lo = (h - h16.astype(F32)).astype(BF16)
    part = _dot_nt(rw_ref[...], h16)
    logits_t = part[:N_EXPERTS] + part[N_EXPERTS:] + _dot_nt(rw_ref[:N_EXPERTS], h_lo)
    e1, e2, g1, g2 = _route(_sigmoid(logits_t), rb_ref[...])
    hot = jnp.concatenate([jnp.where(jnp.logical_or(e1 == e, e2 == e), 1.0, 0.0) for e in range(N_EXPERTS)], axis=0)
    before = _dot(hot.astype(BF16), tri_ref[...]) + seen[...]
    rows = [before[e:e + 1] for e in range(N_EXPERTS)]
    eidx_o[0], eidx_o[1] = e1, e2
    rank_o[0], rank_o[1] = _pick(e1, rows).astype(jnp.int32), _pick(e2, rows).astype(jnp.int32)
    gcol_o[...] = jnp.concatenate([g1, g2, jnp.zeros((LANES - 2, tm), F32)], axis=0).T
    seen[...] = seen[...] + jnp.sum(hot, axis=-1, keepdims=True)
    cnt_o[...] = seen[...]


def _moe_slots_kernel(mb, n_rows, eidx_ref, rank_ref, cnt_ref, idx_o, be_o):
    cnt = cnt_ref[...]
    padded = jnp.ceil(cnt / mb) * mb
    starts, ends, run = [], [], jnp.zeros((1, 1), F32)
    for e in range(N_EXPERTS):
        starts.append(run)
        run = run + padded[e:e + 1]
        ends.append(run)
    n_tiles, per_tile = eidx_ref.shape[1], eidx_ref.shape[3] // SC_WINDOW
    for k in range(2):
        for i in range(n_tiles):
            slot = rank_ref[k, i] + _pick(eidx_ref[k, i], starts).astype(jnp.int32)
            for j in range(N_PLANES):
                for q in range(per_tile):
                    row = ((k * N_PLANES + j) * n_tiles + i) * per_tile + q
                    idx_o[row:row + 1, :] = slot[:, q * SC_WINDOW:(q + 1) * SC_WINDOW] + j * n_rows
    first_row = lax.broadcasted_iota(jnp.int32, be_o.shape, 1).astype(F32) * mb
    be = _tree(jnp.add, [jnp.where(ends[e] <= first_row, 1.0, 0.0) for e in range(N_EXPERTS)])
    lane = lax.broadcasted_iota(jnp.int32, be_o.shape, 1)
    be = jnp.where(lane == be_o.shape[1] - 1, ends[N_EXPERTS - 1] / mb, jnp.minimum(be, N_EXPERTS - 1.0))
    be_o[...] = be.astype(jnp.int32)


def _moe_ffn_kernel(be_ref, hs_ref, wg_ref, wu_ref, wd_ref, ys_o, wgu16, wd16):
    i = pl.program_id(0)
    live = i < be_ref[be_ref.shape[0] - 1]

    @pl.when(jnp.logical_and(live, jnp.logical_or(i == 0, be_ref[i] != be_ref[jnp.maximum(i - 1, 0)])))
    def _():
        wgu16[:, :D_EXPERT] = wg_ref[...].astype(BF16)
        wgu16[:, D_EXPERT:] = wu_ref[...].astype(BF16)
        wd16[...] = wd_ref[...].astype(BF16)

    @pl.when(live)
    def _():
        h = _unpack_bf16_pairs(_from_planes(hs_ref)).astype(BF16)
        gu = _dot(h, wgu16[...])
        g, u = gu[:, :D_EXPERT], gu[:, D_EXPERT:]
        a = g * _sigmoid(g) * u
        _to_planes(_pack_bf16_pairs(_dot(a.astype(BF16), wd16[...])), ys_o)


def _moe_residual(x, z0_ref, z1_ref, gcol_ref, mod_ref):
    gc = gcol_ref[...]
    lane = lax.broadcasted_iota(jnp.int32, gc.shape, 1)
    g1 = jnp.sum(jnp.where(lane == 0, gc, 0.0), axis=-1, keepdims=True)
    g2 = jnp.sum(jnp.where(lane == 1, gc, 0.0), axis=-1, keepdims=True)
    y = g1 * _unpack_bf16_pairs(_from_planes(z0_ref)) + g2 * _unpack_bf16_pairs(_from_planes(z1_ref))
    return x + mod_ref[0][5:6] * y


def _moe_combine_kernel(x_ref, z0_ref, z1_ref, gcol_ref, mod_ref, fg_ref, o_ref):
    o_ref[...] = _rms(_moe_residual(x_ref[...], z0_ref, z1_ref, gcol_ref, mod_ref), fg_ref[...])


def _residual_specs(resid, row_fn, tm, d, in_specs, args):
    (z, gcol), mods_prev = resid
    for k in range(2):
        in_specs.append(pl.BlockSpec((None, N_PLANES, None, tm, SC_ROW), lambda bi, t, k=k: (k, 0, bi, t, 0)))
        args.append(z)
    in_specs += [pl.BlockSpec((None, tm, LANES), lambda bi, t: (bi, t, 0)),
                 pl.BlockSpec((1, 6, d), lambda bi, t: (row_fn(bi), 0, 0))]
    args += [gcol, mods_prev]


def _sc_mesh():
    return plsc.VectorSubcoreMesh(core_axis_name="core", subcore_axis_name="subcore")


def _sc_scatter_rows(rows, idx, n_out):
    n, w = rows.shape
    steps = n // SC_WINDOW

    @pl.kernel(out_type=jax.ShapeDtypeStruct((n_out, w), rows.dtype), mesh=_sc_mesh(), scratch_types=[])
    def scatter(x_hbm, i0_hbm, i1_hbm, o_hbm):
        def body(x_vmem, i0_vmem, i1_vmem):
            pltpu.sync_copy(x_vmem, o_hbm.at[i0_vmem.at[0]])
            pltpu.sync_copy(x_vmem, o_hbm.at[i1_vmem.at[0]])

        pltpu.emit_pipeline(
            body,
            grid=(steps,),
            in_specs=[pl.BlockSpec((SC_WINDOW, w), lambda i: (i, 0)),
                      pl.BlockSpec((1, SC_WINDOW), lambda i: (i, 0)),
                      pl.BlockSpec((1, SC_WINDOW), lambda i: (i + steps, 0))],
            out_specs=[],
            core_axis_name=("core", "subcore"),
            dimension_semantics=(pltpu.PARALLEL,),
        )(x_hbm, i0_hbm, i1_hbm)

    return scatter(rows, idx, idx)


def _sc_gather_rows(rows, idx):
    steps, w = idx.shape[0], rows.shape[1]

    @pl.kernel(out_type=jax.ShapeDtypeStruct((steps * SC_WINDOW, w), rows.dtype), mesh=_sc_mesh(),
               scratch_types=[])
    def gather(x_hbm, i_hbm, o_hbm):
        def body(i_vmem, o_vmem):
            pltpu.sync_copy(x_hbm.at[i_vmem.at[0]], o_vmem)

        pltpu.emit_pipeline(
            body,
            grid=(steps,),
            in_specs=[pl.BlockSpec((1, SC_WINDOW), lambda i: (i, 0))],
            out_specs=[pl.BlockSpec((SC_WINDOW, w), lambda i: (i, 0))],
            core_axis_name=("core", "subcore"),
            dimension_semantics=(pltpu.PARALLEL,),
        )(i_hbm, o_hbm)

    return gather(rows, idx)


def _moe_bucket(x, mods, row_fn, g, rw_t, rb, tm, mb):
    b, s, d = x.shape
    n, nt = b * s, (b * s) // tm
    per_b = s // tm
    tile = lambda bi, t: (0, bi * per_b + t, 0, 0)
    tri = jnp.asarray(np.triu(np.ones((tm, tm), np.float32), 1), BF16)
    hp, eidx, rank, gcol, cnt = pl.pallas_call(
        _moe_route_kernel,
        grid=(b, per_b),
        in_specs=[
            pl.BlockSpec((None, tm, d), lambda bi, t: (bi, t, 0)),
            pl.BlockSpec((1, 6, d), lambda bi, t: (row_fn(bi), 0, 0)),
            _full((1, d)), _full((2 * N_EXPERTS, d)), _full((N_EXPERTS, 1)), _full((tm, tm)),
        ],
        out_specs=[
            pl.BlockSpec((N_PLANES, tm, SC_ROW), lambda bi, t: (0, bi * per_b + t, 0)),
            pl.BlockSpec((2, None, 1, tm), tile), pl.BlockSpec((2, None, 1, tm), tile),
            pl.BlockSpec((None, tm, LANES), lambda bi, t: (bi, t, 0)),
            _full((N_EXPERTS, 1)),
        ],
        out_shape=[
            jax.ShapeDtypeStruct((N_PLANES, n, SC_ROW), jnp.int32),
            jax.ShapeDtypeStruct((2, nt, 1, tm), jnp.int32), jax.ShapeDtypeStruct((2, nt, 1, tm), jnp.int32),
            jax.ShapeDtypeStruct((b, s, LANES), F32),
            jax.ShapeDtypeStruct((N_EXPERTS, 1), F32),
        ],
        scratch_shapes=[pltpu.VMEM((N_EXPERTS, 1), F32)],
        compiler_params=_params(2),
        name="moe_route",
    )(x, mods, g, rw_t, rb, tri)

    n_rows = 2 * n + N_EXPERTS * mb
    nb = n_rows // mb
    nb_pad = -(-(nb + 1) // LANES) * LANES
    idx, be = pl.pallas_call(
        functools.partial(_moe_slots_kernel, mb, n_rows),
        out_shape=[jax.ShapeDtypeStruct((2 * N_PLANES * n // SC_WINDOW, SC_WINDOW), jnp.int32),
                   jax.ShapeDtypeStruct((1, nb_pad), jnp.int32)],
        name="moe_slots",
    )(eidx, rank, cnt)

    hs = _sc_scatter_rows(hp.reshape(N_PLANES * n, SC_ROW), idx, N_PLANES * n_rows)
    return hs.reshape(N_PLANES, n_rows, SC_ROW), idx, be.reshape(nb_pad), gcol


def _moe_experts(bucket, layer, w_gate, w_up, w_down, mb, out_dims):
    hs, idx, be, gcol = bucket
    n_rows, d = hs.shape[1], w_gate.shape[2]
    last = be.shape[0] - 1
    blk = lambda i, be_r: jnp.minimum(i, be_r[last] - 1)
    plane_block = pl.BlockSpec((N_PLANES, mb, SC_ROW), lambda i, be_r: (0, blk(i, be_r), 0))
    weight = lambda i, be_r: (layer, be_r[blk(i, be_r)], 0, 0)
    ys = pl.pallas_call(
        _moe_ffn_kernel,
        grid_spec=pltpu.PrefetchScalarGridSpec(
            num_scalar_prefetch=1,
            grid=(n_rows // mb,),
            in_specs=[
                plane_block,
                pl.BlockSpec((None, None, d, D_EXPERT), weight),
                pl.BlockSpec((None, None, d, D_EXPERT), weight),
                pl.BlockSpec((None, None, D_EXPERT, d), weight),
            ],
            out_specs=plane_block,
            scratch_shapes=[pltpu.VMEM((d, 2 * D_EXPERT), BF16), pltpu.VMEM((D_EXPERT, d), BF16)],
        ),
        out_shape=jax.ShapeDtypeStruct((N_PLANES, n_rows, SC_ROW), jnp.int32),
        compiler_params=_params(1),
        name="moe_ffn",
    )(be, hs, w_gate, w_up, w_down)
    z = _sc_gather_rows(ys.reshape(N_PLANES * n_rows, SC_ROW), idx)
    return z.reshape(2, N_PLANES, *out_dims, SC_ROW), gcol.reshape(*out_dims, LANES)


def _moe_combine(x, mods, row_fn, gathered, final_g, tm):
    b, s, d = x.shape
    in_specs = [pl.BlockSpec((None, tm, d), lambda bi, t: (bi, t, 0))]
    args = [x]
    _residual_specs((gathered, mods), row_fn, tm, d, in_specs, args)
    in_specs.append(_full((1, d)))
    args.append(final_g)
    return pl.pallas_call(
        _moe_combine_kernel,
        grid=(b, s // tm),
        in_specs=in_specs,
        out_specs=pl.BlockSpec((None, tm, d), lambda bi, t: (bi, t, 0)),
        out_shape=jax.ShapeDtypeStruct((b, s, d), F32),
        compiler_params=_params(2),
        name="moe_combine",
    )(*args)


def _rope_tables(n_tokens, rot_dim, offset):
    n_rows = n_tokens // GRID_W
    rows = np.repeat(np.arange(n_rows), GRID_W)
    cols = np.tile(np.arange(GRID_W), n_rows)
    n_freq = rot_dim // 4
    inv = jnp.asarray(ROPE_THETA, F32) ** (-jnp.arange(n_freq, dtype=F32) / n_freq)
    ang = jnp.concatenate([jnp.asarray(rows, F32)[:, None] * inv, jnp.asarray(cols, F32)[:, None] * inv], axis=-1)
    cos, sin = jnp.repeat(jnp.cos(ang), 2, axis=-1), jnp.repeat(jnp.sin(ang), 2, axis=-1)
    odd = jnp.asarray(np.arange(rot_dim) % 2 == 1)
    parts = [cos, jnp.where(odd, sin, 0.0), jnp.where(odd, 0.0, -sin)]
    period = 64 if rot_dim == 64 else LANES
    fill = [1.0, 0.0, 0.0]
    out = []
    for p, f in zip(parts, fill):
        slot = jnp.full((n_tokens, period), f, F32).at[:, offset:offset + rot_dim].set(p)
        out.append(jnp.tile(slot, (1, LANES // period)))
    return jnp.stack(out)


def _pad_heads(w, n_heads, lo, hi):
    k = w.shape[0]
    w = w.reshape(k, n_heads, -1)[:, :, lo:hi]
    return jnp.pad(w, ((0, 0), (0, 0), (0, LANES - (hi - lo)))).reshape(k, n_heads * LANES)


def _ab_weights(j, ab_w_in, diff_lambda, diff_subln_g, mla_q_norm_g, mla_w_qb, mla_kv_norm_g, mla_w_kvb, ab_w_out):
    place = np.zeros((LANES, MLA_HEADS * LANES), np.float32)
    for hd in range(MLA_HEADS):
        for r in range(MLA_ROPE_DIM):
            place[r, hd * LANES + MLA_NOPE_DIM + r] = 1.0
    w_out_m = ab_w_out[j][512:].reshape(MLA_HEADS, MLA_V_DIM, D_MODEL)
    return {
        "w_in": jnp.pad(ab_w_in[j], ((0, 0), (0, AB_IN_PAD - AB_IN))).astype(BF16),
        "qn_g": mla_q_norm_g[j][None, :],
        "w_qb": _pad_heads(mla_w_qb[j], MLA_HEADS, 0, MLA_NOPE_DIM + MLA_ROPE_DIM).astype(BF16),
        "kvn_g": mla_kv_norm_g[j][None, :],
        "wk": _pad_heads(mla_w_kvb[j], MLA_HEADS, 0, MLA_NOPE_DIM).astype(BF16),
        "wv": _pad_heads(mla_w_kvb[j], MLA_HEADS, MLA_NOPE_DIM, MLA_NOPE_DIM + MLA_V_DIM).astype(BF16),
        "e": jnp.asarray(place, BF16),
        "w_out_d": ab_w_out[j][:512].reshape(DIFF_HEADS, LANES, D_MODEL).astype(BF16),
        "w_out_m": jnp.pad(w_out_m, ((0, 0), (0, LANES - MLA_V_DIM), (0, 0))).astype(BF16),
        "subln_g": diff_subln_g[j][None, :],
        "lam": diff_lambda[j],
    }


def kernel(x_prompt, x_sample, cache_diff_k, cache_diff_v, cache_mla_ckv, cache_mla_krope, cache_swa_k, cache_swa_v,
           c, c_ctx, ada_w, ada_b, norm1_g, norm2_g, final_norm_g, ab_w_in, diff_lambda, diff_subln_g, mla_q_norm_g,
           mla_w_qb, mla_kv_norm_g, mla_w_kvb, ab_w_out, swa_w_in, swa_sink, swa_w_out, router_w, router_bias,
           moe_w_gate, moe_w_up, moe_w_down):
    bp, sp, d = x_prompt.shape
    bs, ss, _ = x_sample.shape
    depth = ada_w.shape[0]
    past = cache_diff_k.shape[2]
    n_ab = cache_diff_k.shape[1]
    n_swa = cache_swa_k.shape[1]
    assert bs + 1 <= COND_ROWS and d == D_MODEL

    cond = jnp.zeros((COND_ROWS, d), F32).at[:bs].set(c).at[bs].set(c_ctx)
    mods = _ada_all(cond, ada_w, ada_b).reshape(depth, COND_ROWS, 6, d)
    row_s = lambda bi: bi
    row_p = lambda bi: bi * 0 + bs

    rope_d = _rope_tables(ss, DIFF_QK_DIM, 0)
    rope_q = _rope_tables(ss, MLA_ROPE_DIM, MLA_NOPE_DIM)
    rope_k = _rope_tables(ss, MLA_ROPE_DIM, 0)

    cdk = cache_diff_k.reshape(bs, n_ab, past, 512)
    cdv = cache_diff_v.reshape(bs, n_ab, past, 512)
    ckr = jnp.pad(cache_mla_krope, ((0, 0), (0, 0), (0, 0), (0, LANES - MLA_ROPE_DIM)))
    csk = cache_swa_k.reshape(bs, n_swa, past, 256)
    csv = cache_swa_v.reshape(bs, n_swa, past, 256)
    rw_hi = router_w.T.astype(BF16)
    rw_t = jnp.concatenate([rw_hi, (router_w.T - rw_hi.astype(F32)).astype(BF16)], axis=0)
    rb = router_bias[:, None]

    tp = min(256, sp)
    ts = min(256, ss)
    tm_s = min(512, ss)
    tm_p = min(512, bp * sp)

    xp, xs = x_prompt, x_sample
    new_ab, new_swa = None, None
    pend_p = pend_s = None

    def project(fn, x, pend, *a):
        outs = list(fn(x, *a, pend))
        return (outs.pop() if pend is not None else x), outs

    for l in range(depth):
        j = l // 2
        ml = mods[l]
        g1, g2 = norm1_g[l][None, :], norm2_g[l][None, :]
        if l % 2 == 0:
            lambda_init = 0.8 - 0.6 * math.exp(-0.3 * l)
            wts = _ab_weights(j, ab_w_in, diff_lambda, diff_subln_g, mla_q_norm_g, mla_w_qb, mla_kv_norm_g,
                              mla_w_kvb, ab_w_out)
            xp, (qa, ka, va, qb, km, vm, *new_ab) = project(_proj_ab, xp, pend_p, ml, row_p, g1, wts, None,
                                                             (j, n_ab, new_ab), tp)
            xp = _attn_ab(xp, ml, row_p, (qa, qb), [(ka, va, km, vm)], wts, lambda_init, tp, DIFF_HEADS + MLA_HEADS)
            xs, (qa, ka, va, qb, km, vm) = project(_proj_ab, xs, pend_s, ml, row_s, g1, wts,
                                                   (rope_d, rope_q, rope_k), None, ts)
            cache_seg = _cache_ab(cdk, cdv, cache_mla_ckv, ckr, j, wts)
            xs = _attn_ab(xs, ml, row_s, (qa, qb), [(ka, va, km, vm), tuple(cache_seg)], wts, lambda_init, ts, 2)
        else:
            w_in = swa_w_in[j].astype(BF16)
            w_out = swa_w_out[j].reshape(8, LANES, d).astype(BF16)
            xp, (q, k, v, *new_swa) = project(_proj_swa, xp, pend_p, ml, row_p, g1, w_in, None,
                                              (j, n_swa, new_swa), tp)
            xp = _attn_swa(xp, ml, row_p, q, k, v, None, swa_sink[j], w_out, tp, SWA_HEADS)
            xs, (q, k, v) = project(_proj_swa, xs, pend_s, ml, row_s, g1, w_in, rope_d, None, ts)
            xs = _attn_swa(xs, ml, row_s, q, k, v, (csk, csv, j), swa_sink[j], w_out, ts, 2)
        bucket_p = _moe_bucket(xp.reshape(1, bp * sp, d), ml, row_p, g2, rw_t, rb, tm_p, MOE_ROW_BLOCK)
        bucket_s = _moe_bucket(xs, ml, row_s, g2, rw_t, rb, tm_s, MOE_ROW_BLOCK)
        got_p = _moe_experts(bucket_p, l, moe_w_gate, moe_w_up, moe_w_down, MOE_ROW_BLOCK, (bp, sp))
        got_s = _moe_experts(bucket_s, l, moe_w_gate, moe_w_up, moe_w_down, MOE_ROW_BLOCK, (bs, ss))
        pend_p, pend_s = (got_p, ml), (got_s, ml)

    xp = _moe_combine(xp, pend_p[1], row_p, pend_p[0], final_norm_g[None, :], tp)
    xs = _moe_combine(xs, pend_s[1], row_s, pend_s[0], final_norm_g[None, :], ts)

    new_diff_k = new_ab[0].reshape(bp, n_ab, sp, DIFF_HEADS, 2, DIFF_QK_DIM)
    new_diff_v = new_ab[1].reshape(bp, n_ab, sp, DIFF_HEADS, DIFF_V_DIM)
    new_mla_ckv, new_mla_krope = new_ab[2], new_ab[3]
    new_swa_k = new_swa[0].reshape(bp, n_swa, sp, SWA_KV_HEADS, SWA_HEAD_DIM)
    new_swa_v = new_swa[1].reshape(bp, n_swa, sp, SWA_KV_HEADS, SWA_HEAD_DIM)
    return (xp, xs, new_diff_k, new_diff_v, new_mla_ckv, new_mla_krope, new_swa_k, new_swa_v)
```

```python
import functools
import math

import jax
import jax.numpy as jnp
import numpy as np
from jax import lax
from jax.experimental import pallas as pl
from jax.experimental.pallas import tpu as pltpu
from jax.experimental.pallas import tpu_sc as plsc

F32 = jnp.float32
BF16 = jnp.bfloat16

D_MODEL = 1024
GRID_W = 64
ROPE_THETA = 10000.0
NORM_EPS = 1e-6
NEG_INF = -1e30
LOG2E = math.log2(math.e)
LANES = 128

DIFF_HEADS = 4
DIFF_QK_DIM = 64
DIFF_V_DIM = 128
MLA_HEADS = 8
MLA_Q_RANK = 384
MLA_KV_RANK = 256
MLA_NOPE_DIM = 64
MLA_ROPE_DIM = 32
MLA_V_DIM = 64
AB_IN = 3 * 512 + MLA_Q_RANK + MLA_KV_RANK + MLA_ROPE_DIM
AB_IN_PAD = 2304
SWA_HEADS = 16
SWA_KV_HEADS = 4
SWA_GROUP = 4
SWA_HEAD_DIM = 64
WINDOW = 128
N_EXPERTS = 16
N_GROUPS = 4
EXPERTS_PER_GROUP = 4
D_EXPERT = 256
COND_ROWS = 16

VMEM_LIMIT = 56 * 1024 * 1024


def _full(shape):
    n = len(shape)
    return pl.BlockSpec(shape, lambda *_: (0,) * n)


def _params(n_axes):
    return pltpu.CompilerParams(dimension_semantics=("arbitrary",) * n_axes, vmem_limit_bytes=VMEM_LIMIT)


def _sigmoid(x):
    return 1.0 / (1.0 + jnp.exp(-x))


def _rms(x, g):
    return x * lax.rsqrt(jnp.mean(x * x, axis=-1, keepdims=True) + NORM_EPS) * g


def _modulate(x, g, shift, scale):
    return _rms(x, g) * (1.0 + scale) + shift


def _dot(a, b):
    return jnp.dot(a, b, preferred_element_type=F32)


def _dot_nt(a, b):
    return lax.dot_general(a, b, (((1,), (1,)), ((), ())), preferred_element_type=F32)


def _rope(x, tab_ref):
    c, s_odd, s_even = tab_ref[0], tab_ref[1], tab_ref[2]
    out = []
    for i in range(x.shape[1] // LANES):
        xi = x[:, i * LANES:(i + 1) * LANES]
        out.append(xi * c + pltpu.roll(xi, 1, 1) * s_odd + pltpu.roll(xi, LANES - 1, 1) * s_even)
    return out[0] if len(out) == 1 else jnp.concatenate(out, axis=1)


def _tree(op, xs):
    xs = list(xs)
    while len(xs) > 1:
        xs = [op(xs[i], xs[i + 1]) if i + 1 < len(xs) else xs[i] for i in range(0, len(xs), 2)]
    return xs[0]


def _lane_fold(op, x):
    return _tree(op, [x[:, i * LANES:(i + 1) * LANES] for i in range(x.shape[1] // LANES)])


def _ada_kernel(cond_ref, w_ref, b_ref, o_ref):
    c = cond_ref[...]
    a = (c * _sigmoid(c)).astype(BF16)
    o_ref[...] = _dot(a, w_ref[...].astype(BF16)) + b_ref[...]


def _ada_all(cond, ada_w, ada_b):
    depth, d, n = ada_w.shape
    tn = 1536
    return pl.pallas_call(
        _ada_kernel,
        grid=(depth, n // tn),
        in_specs=[
            _full((COND_ROWS, d)),
            pl.BlockSpec((None, d, tn), lambda l, j: (l, 0, j)),
            pl.BlockSpec((None, 1, tn), lambda l, j: (l, 0, j)),
        ],
        out_specs=pl.BlockSpec((None, COND_ROWS, tn), lambda l, j: (l, 0, j)),
        out_shape=jax.ShapeDtypeStruct((depth, COND_ROWS, n), F32),
        compiler_params=_params(2),
        name="ada_mod",
    )(cond, ada_w, ada_b.reshape(depth, 1, n))


def _proj_ab_kernel(rope, emit_cache, n_prev, fused, x_ref, mod_ref, g_ref, w_in_ref, qn_ref, wqb_ref, kvn_ref,
                    wk_ref, wv_ref, e_ref, *rest):
    if rope:
        rope_d, rope_q, rope_k = rest[:3]
        rest = rest[3:]
    x = x_ref[...]
    if fused:
        x = _moe_residual(x, *rest[:4])
        rest[-1][...] = x
        rest = rest[4:]
    rest = rest[n_prev:]
    qa_o, ka_o, va_o, qb_o, km_o, vm_o = rest[:6]
    mod = mod_ref[0]
    h = _modulate(x, g_ref[...], mod[0:1], mod[1:2]).astype(BF16)
    big = _dot(h, w_in_ref[...])
    qa, ka, va = big[:, 0:512], big[:, 512:1024], big[:, 1024:1536]
    q_lat, ckv, kr = big[:, 1536:1920], big[:, 1920:2176], big[:, 2176:2304]
    qb = _dot(_rms(q_lat, qn_ref[...]).astype(BF16), wqb_ref[...])
    ckv_n = _rms(ckv, kvn_ref[...])
    if emit_cache:
        ka32_o, va32_o, ckv32_o, kr32_o = rest[6:10]
        ka32_o[...] = ka
        va32_o[...] = va
        ckv32_o[...] = ckv_n
        kr32_o[...] = kr[:, :MLA_ROPE_DIM]
    if rope:
        qa, ka = _rope(qa, rope_d), _rope(ka, rope_d)
        qb, kr = _rope(qb, rope_q), _rope(kr, rope_k)
    qa = qa * (DIFF_QK_DIM ** -0.5 * LOG2E)
    qb = qb * ((MLA_NOPE_DIM + MLA_ROPE_DIM) ** -0.5 * LOG2E)
    ckv16 = ckv_n.astype(BF16)
    km = _dot(ckv16, wk_ref[...]) + _dot(kr.astype(BF16), e_ref[...])
    vm = _dot(ckv16, wv_ref[...])
    lo = lax.broadcasted_iota(jnp.int32, (qa.shape[0], LANES), 1) < DIFF_QK_DIM
    for hd in range(DIFF_HEADS):
        sl = slice(hd * LANES, (hd + 1) * LANES)
        qa_o[2 * hd] = jnp.where(lo, qa[:, sl], 0.0).astype(BF16)
        qa_o[2 * hd + 1] = jnp.where(lo, 0.0, qa[:, sl]).astype(BF16)
        ka_o[hd] = ka[:, sl].astype(BF16)
        va_o[hd] = va[:, sl].astype(BF16)
    for hd in range(MLA_HEADS):
        sl = slice(hd * LANES, (hd + 1) * LANES)
        qb_o[hd] = qb[:, sl].astype(BF16)
        km_o[hd] = km[:, sl].astype(BF16)
        vm_o[hd] = vm[:, sl].astype(BF16)


def _cache_outputs(cache, widths, b, s, tm, in_specs, args, n_fixed_outs):
    if cache is None:
        return [], {}, 0
    j, n_layers, prev = cache
    outs = [(jax.ShapeDtypeStruct((b, n_layers, s, w), F32),
             pl.BlockSpec((None, None, tm, w), lambda bi, t: (bi, j, t, 0))) for w in widths]
    aliases = {}
    if prev is not None:
        for i, arr in enumerate(prev):
            aliases[len(args)] = n_fixed_outs + i
            in_specs.append(pl.BlockSpec(memory_space=pl.ANY))
            args.append(arr)
    return outs, aliases, len(aliases)


def _proj_ab(x, mods, row_fn, g, wts, rope_tabs, cache, tm, resid=None):
    b, s, d = x.shape
    rope = rope_tabs is not None
    tok = lambda bi, t: (bi, t, 0)
    head = lambda bi, t: (bi, 0, t, 0)
    in_specs = [
        pl.BlockSpec((None, tm, d), tok),
        pl.BlockSpec((1, 6, d), lambda bi, t: (row_fn(bi), 0, 0)),
        _full((1, d)),
        _full((d, AB_IN_PAD)), _full((1, MLA_Q_RANK)), _full((MLA_Q_RANK, 1024)), _full((1, MLA_KV_RANK)),
        _full((MLA_KV_RANK, 1024)), _full((MLA_KV_RANK, 1024)), _full((LANES, 1024)),
    ]
    args = [x, mods, g, wts["w_in"], wts["qn_g"], wts["w_qb"], wts["kvn_g"], wts["wk"], wts["wv"], wts["e"]]
    if rope:
        in_specs += [pl.BlockSpec((3, tm, LANES), lambda bi, t: (0, t, 0))] * 3
        args += list(rope_tabs)
    if resid is not None:
        _residual_specs(resid, row_fn, tm, d, in_specs, args)

    def hm(nh):
        return jax.ShapeDtypeStruct((b, nh, s, LANES), BF16), pl.BlockSpec((None, nh, tm, LANES), head)

    outs = [hm(8), hm(4), hm(4), hm(8), hm(8), hm(8)]
    cache_outs, aliases, n_prev = _cache_outputs(cache, (512, 512, MLA_KV_RANK, MLA_ROPE_DIM), b, s, tm, in_specs,
                                                 args, len(outs))
    outs += cache_outs
    if resid is not None:
        outs.append((jax.ShapeDtypeStruct((b, s, d), F32), pl.BlockSpec((None, tm, d), tok)))
    return pl.pallas_call(
        functools.partial(_proj_ab_kernel, rope, cache is not None, n_prev, resid is not None),
        grid=(b, s // tm),
        in_specs=in_specs,
        out_specs=[o[1] for o in outs],
        out_shape=[o[0] for o in outs],
        input_output_aliases=aliases,
        compiler_params=_params(2),
        name="proj_ab",
    )(*args)


def _cache_ab_kernel(dk_ref, dv_ref, ckv_ref, kr_ref, wk_ref, wv_ref, e_ref, ck_o, cv_o, km_o, vm_o):
    ckv16 = ckv_ref[...].astype(BF16)
    km = _dot(ckv16, wk_ref[...]) + _dot(kr_ref[...].astype(BF16), e_ref[...])
    vm = _dot(ckv16, wv_ref[...])
    for hd in range(DIFF_HEADS):
        sl = slice(hd * LANES, (hd + 1) * LANES)
        ck_o[hd] = dk_ref[:, sl].astype(BF16)
        cv_o[hd] = dv_ref[:, sl].astype(BF16)
    for hd in range(MLA_HEADS):
        sl = slice(hd * LANES, (hd + 1) * LANES)
        km_o[hd] = km[:, sl].astype(BF16)
        vm_o[hd] = vm[:, sl].astype(BF16)


def _cache_ab(cdk, cdv, cckv, ckr, j, wts):
    b, _, p, _ = cdk.shape
    lay = lambda bi: (bi, j, 0, 0)

    def hm(nh):
        return (jax.ShapeDtypeStruct((b, nh, p, LANES), BF16),
                pl.BlockSpec((None, nh, p, LANES), lambda bi: (bi, 0, 0, 0)))

    outs = [hm(4), hm(4), hm(8), hm(8)]
    return pl.pallas_call(
        _cache_ab_kernel,
        grid=(b,),
        in_specs=[
            pl.BlockSpec((None, None, p, 512), lay), pl.BlockSpec((None, None, p, 512), lay),
            pl.BlockSpec((None, None, p, MLA_KV_RANK), lay), pl.BlockSpec((None, None, p, LANES), lay),
            _full((MLA_KV_RANK, 1024)), _full((MLA_KV_RANK, 1024)), _full((LANES, 1024)),
        ],
        out_specs=[o[1] for o in outs],
        out_shape=[o[0] for o in outs],
        compiler_params=_params(1),
        name="cache_ab",
    )(cdk, cdv, cckv, ckr, wts["wk"], wts["wv"], wts["e"])


def _attn_ab_kernel(n_seg, lambda_init, x_ref, mod_ref, qa_ref, qb_ref, *rest):
    segs = [rest[4 * i:4 * i + 4] for i in range(n_seg)]
    wd_ref, wm_ref, sg_ref, lam_ref, o_ref = rest[4 * n_seg:4 * n_seg + 5]
    bufs = rest[4 * n_seg + 5:]
    lp = lam_ref[...]
    lam = (jnp.exp(jnp.sum(lp[0:1] * lp[1:2], axis=-1, keepdims=True))
           - jnp.exp(jnp.sum(lp[2:3] * lp[3:4], axis=-1, keepdims=True)) + lambda_init)
    tq, d = x_ref.shape
    widths = [sg[0].shape[1] for sg in segs]
    offs = [sum(widths[:i]) for i in range(n_seg)]
    units = [("diff", hd) for hd in range(DIFF_HEADS)] + [("mla", hd) for hd in range(MLA_HEADS)]

    def scores(u, buf):
        kind, hd = units[u]
        qs = [qa_ref[2 * hd], qa_ref[2 * hd + 1]] if kind == "diff" else [qb_ref[hd]]
        ks = [sg[0 if kind == "diff" else 2][hd] for sg in segs]
        ms = []
        for i, q in enumerate(qs):
            folds = []
            for k, off, w in zip(ks, offs, widths):
                s = _dot_nt(q, k)
                buf[i, :, off:off + w] = s
                folds.append(_lane_fold(jnp.maximum, s))
            ms.append(jnp.max(_tree(jnp.maximum, folds), axis=-1, keepdims=True))
        return ms

    def exps(buf, i, m):
        e = [jnp.exp2(buf[i, :, off:off + w] - m) for off, w in zip(offs, widths)]
        l = jnp.sum(_tree(jnp.add, [_lane_fold(jnp.add, x) for x in e]), axis=-1, keepdims=True)
        return e, l

    def consume(u, buf, ms, acc):
        kind, hd = units[u]
        if kind == "diff":
            vs = [sg[1][hd] for sg in segs]
            (e1, l1), (e2, l2) = exps(buf, 0, ms[0]), exps(buf, 1, ms[1])
            c1, c2 = 1.0 / l1, lam / l2
            o = _tree(jnp.add, [_dot((a * c1 - b * c2).astype(BF16), v) for a, b, v in zip(e1, e2, vs)])
            od = _rms(o, sg_ref[...]) * (1.0 - lambda_init)
            return acc + _dot(od.astype(BF16), wd_ref[hd])
        vs = [sg[3][hd] for sg in segs]
        e, l = exps(buf, 0, ms[0])
        o = _tree(jnp.add, [_dot(x.astype(BF16), v) for x, v in zip(e, vs)]) / l
        return acc + _dot(o.astype(BF16), wm_ref[hd])

    acc = jnp.zeros((tq, d), F32)
    ahead, pending = len(bufs) - 1, {}
    for u in range(min(ahead, len(units))):
        pending[u] = scores(u, bufs[u % len(bufs)])
    for u in range(len(units)):
        if u + ahead < len(units):
            pending[u + ahead] = scores(u + ahead, bufs[(u + ahead) % len(bufs)])
        acc = consume(u, bufs[u % len(bufs)], pending.pop(u), acc)
    o_ref[...] = x_ref[...] + mod_ref[0][2:3] * acc


def _attn_ab(x, mods, row_fn, q_parts, seg_list, wts, lambda_init, tq, n_bufs):
    b, s, d = x.shape
    qa, qb = q_parts
    in_specs = [
        pl.BlockSpec((None, tq, d), lambda bi, t: (bi, t, 0)),
        pl.BlockSpec((1, 6, d), lambda bi, t: (row_fn(bi), 0, 0)),
        pl.BlockSpec((None, 8, tq, LANES), lambda bi, t: (bi, 0, t, 0)),
        pl.BlockSpec((None, 8, tq, LANES), lambda bi, t: (bi, 0, t, 0)),
    ]
    args = [x, mods, qa, qb]
    kv_mode = dict(pipeline_mode=pl.Buffered(1)) if s // tq > 1 else {}
    for seg in seg_list:
        for arr in seg:
            nh, nk = arr.shape[1], arr.shape[2]
            in_specs.append(pl.BlockSpec((None, nh, nk, LANES), lambda bi, t: (bi, 0, 0, 0), **kv_mode))
            args.append(arr)
    in_specs += [_full((DIFF_HEADS, LANES, d)), _full((MLA_HEADS, LANES, d)), _full((1, LANES)),
                 _full((4, DIFF_QK_DIM))]
    args += [wts["w_out_d"], wts["w_out_m"], wts["subln_g"], wts["lam"]]
    n_keys = sum(seg[0].shape[2] for seg in seg_list)
    return pl.pallas_call(
        functools.partial(_attn_ab_kernel, len(seg_list), lambda_init),
        grid=(b, s // tq),
        in_specs=in_specs,
        out_specs=pl.BlockSpec((None, tq, d), lambda bi, t: (bi, t, 0)),
        out_shape=jax.ShapeDtypeStruct((b, s, d), F32),
        scratch_shapes=[pltpu.VMEM((2, tq, n_keys), F32) for _ in range(n_bufs)],
        compiler_params=_params(2),
        name="attn_ab",
    )(*args)


def _proj_swa_kernel(rope, emit_cache, n_prev, fused, x_ref, mod_ref, g_ref, w_in_ref, *rest):
    if rope:
        rope_d = rest[0]
        rest = rest[1:]
    x = x_ref[...]
    if fused:
        x = _moe_residual(x, *rest[:4])
        rest[-1][...] = x
        rest = rest[4:]
    rest = rest[n_prev:]
    q_o, k_o, v_o = rest[:3]
    mod = mod_ref[0]
    h = _modulate(x, g_ref[...], mod[0:1], mod[1:2]).astype(BF16)
    big = _dot(h, w_in_ref[...])
    q, k, v = big[:, :1024], big[:, 1024:1280], big[:, 1280:1536]
    if emit_cache:
        k32_o, v32_o = rest[3:5]
        k32_o[...] = k
        v32_o[...] = v
    if rope:
        q, k = _rope(q, rope_d), _rope(k, rope_d)
    q = q * (SWA_HEAD_DIM ** -0.5 * LOG2E)
    lo = lax.broadcasted_iota(jnp.int32, (q.shape[0], LANES), 1) < SWA_HEAD_DIM
    for pair in range(2):
        for grp in range(SWA_GROUP):
            ca, cb = (2 * pair) * 2 + grp // 2, (2 * pair + 1) * 2 + grp // 2
            a = q[:, ca * LANES:(ca + 1) * LANES]
            bb = q[:, cb * LANES:(cb + 1) * LANES]
            if grp % 2 == 0:
                bb = pltpu.roll(bb, SWA_HEAD_DIM, 1)
            else:
                a = pltpu.roll(a, SWA_HEAD_DIM, 1)
            q_o[pair * SWA_GROUP + grp] = jnp.where(lo, a, bb).astype(BF16)
    for kvh in range(SWA_KV_HEADS):
        sl = slice((kvh // 2) * LANES, (kvh // 2 + 1) * LANES)
        keep = lo if kvh % 2 == 0 else jnp.logical_not(lo)
        k_o[kvh] = jnp.where(keep, k[:, sl], 0.0).astype(BF16)
        v_o[kvh] = jnp.where(keep, v[:, sl], 0.0).astype(BF16)


def _proj_swa(x, mods, row_fn, g, w_in, rope_tab, cache, tm, resid=None):
    b, s, d = x.shape
    rope = rope_tab is not None
    tok = lambda bi, t: (bi, t, 0)
    head = lambda bi, t: (bi, 0, t, 0)
    in_specs = [
        pl.BlockSpec((None, tm, d), tok),
        pl.BlockSpec((1, 6, d), lambda bi, t: (row_fn(bi), 0, 0)),
        _full((1, d)), _full((d, 1536)),
    ]
    args = [x, mods, g, w_in]
    if rope:
        in_specs.append(pl.BlockSpec((3, tm, LANES), lambda bi, t: (0, t, 0)))
        args.append(rope_tab)
    if resid is not None:
        _residual_specs(resid, row_fn, tm, d, in_specs, args)

    def hm(nh):
        return jax.ShapeDtypeStruct((b, nh, s, LANES), BF16), pl.BlockSpec((None, nh, tm, LANES), head)

    outs = [hm(8), hm(4), hm(4)]
    cache_outs, aliases, n_prev = _cache_outputs(cache, (256, 256), b, s, tm, in_specs, args, len(outs))
    outs += cache_outs
    if resid is not None:
        outs.append((jax.ShapeDtypeStruct((b, s, d), F32), pl.BlockSpec((None, tm, d), tok)))
    return pl.pallas_call(
        functools.partial(_proj_swa_kernel, rope, cache is not None, n_prev, resid is not None),
        grid=(b, s // tm),
        in_specs=in_specs,
        out_specs=[o[1] for o in outs],
        out_shape=[o[0] for o in outs],
        input_output_aliases=aliases,
        compiler_params=_params(2),
        name="proj_swa",
    )(*args)


def _attn_swa_kernel(windowed, band_w, x_ref, mod_ref, q_ref, k_ref, v_ref, *rest):
    if windowed:
        ck_ref, cv_ref = rest[:2]
        rest = rest[2:]
    sink_ref, w_ref, o_ref = rest[:3]
    bufs = rest[3:]
    tq, d = x_ref.shape
    n_keys = k_ref.shape[1]
    lo_k = lax.broadcasted_iota(jnp.int32, (1, LANES), 1) < SWA_HEAD_DIM
    if windowed:
        start = pl.program_id(1) * tq
        bstart = pl.multiple_of(jnp.clip(start - WINDOW, 0, n_keys - band_w), LANES)
        qpos = start + lax.broadcasted_iota(jnp.int32, (tq, band_w), 0)
        kpos = bstart + lax.broadcasted_iota(jnp.int32, (tq, band_w), 1)
        bias = jnp.where(jnp.abs(qpos - kpos) <= WINDOW, 0.0, NEG_INF)

    def keys_values(kvh):
        if not windowed:
            return [k_ref[kvh]], [v_ref[kvh]]
        keep = lo_k if kvh % 2 == 0 else jnp.logical_not(lo_k)
        sl = slice((kvh // 2) * LANES, (kvh // 2 + 1) * LANES)
        kc = jnp.where(keep, ck_ref[:, sl], 0.0).astype(BF16)
        vc = jnp.where(keep, cv_ref[:, sl], 0.0).astype(BF16)
        return ([k_ref[kvh, pl.ds(bstart, band_w), :], kc], [v_ref[kvh, pl.ds(bstart, band_w), :], vc])

    units = [(kvh, g) for kvh in range(SWA_KV_HEADS) for g in range(SWA_GROUP)]

    def scores(u, buf):
        kvh, g = units[u]
        q = q_ref[(kvh // 2) * SWA_GROUP + g]
        sink = sink_ref[kvh * SWA_GROUP + g] * LOG2E
        folds, off = [], 0
        for i, k in enumerate(keys_values(kvh)[0]):
            s = _dot_nt(q, k)
            if windowed and i == 0:
                s = s + bias
            buf[:, off:off + s.shape[1]] = s
            folds.append(_lane_fold(jnp.maximum, s))
            off += s.shape[1]
        m = jnp.maximum(jnp.max(_tree(jnp.maximum, folds), axis=-1, keepdims=True), sink)
        return m, sink

    def consume(u, buf, m, sink):
        vs = keys_values(units[u][0])[1]
        e, off = [], 0
        for v in vs:
            e.append(jnp.exp2(buf[:, off:off + v.shape[0]] - m))
            off += v.shape[0]
        l = jnp.sum(_tree(jnp.add, [_lane_fold(jnp.add, x) for x in e]), axis=-1, keepdims=True) + jnp.exp2(sink - m)
        return _tree(jnp.add, [_dot(x.astype(BF16), v) for x, v in zip(e, vs)]) / l

    acc = jnp.zeros((tq, d), F32)
    ahead, pending = len(bufs) - 1, {}
    for u in range(min(ahead, len(units))):
        pending[u] = scores(u, bufs[u % len(bufs)])
    prev = None
    for u, (kvh, g) in enumerate(units):
        if u + ahead < len(units):
            pending[u + ahead] = scores(u + ahead, bufs[(u + ahead) % len(bufs)])
        o = consume(u, bufs[u % len(bufs)], *pending.pop(u))
        if g % 2 == 0:
            prev = o
            continue
        if kvh % 2 == 0:
            slab = prev + pltpu.roll(o, SWA_HEAD_DIM, 1)
        else:
            slab = pltpu.roll(prev, SWA_HEAD_DIM, 1) + o
        acc = acc + _dot(slab.astype(BF16), w_ref[kvh * 2 + g // 2])
    o_ref[...] = x_ref[...] + mod_ref[0][2:3] * acc


def _attn_swa(x, mods, row_fn, q, k, v, ctx, sink, w_out, tq, n_bufs):
    b, s, d = x.shape
    windowed = ctx is not None
    band_w = min(tq + 2 * WINDOW, s)
    kv_spec = pl.BlockSpec((None, SWA_KV_HEADS, s, LANES), lambda bi, t: (bi, 0, 0, 0))
    in_specs = [
        pl.BlockSpec((None, tq, d), lambda bi, t: (bi, t, 0)),
        pl.BlockSpec((1, 6, d), lambda bi, t: (row_fn(bi), 0, 0)),
        pl.BlockSpec((None, 8, tq, LANES), lambda bi, t: (bi, 0, t, 0)),
        kv_spec, kv_spec,
    ]
    args = [x, mods, q, k, v]
    if windowed:
        ck, cv, j = ctx
        p = ck.shape[2]
        spec = pl.BlockSpec((None, None, p, 256), lambda bi, t: (bi, j, 0, 0))
        in_specs += [spec, spec]
        args += [ck, cv]
    in_specs += [pl.BlockSpec(memory_space=pltpu.SMEM), _full((8, LANES, d))]
    args += [sink, w_out]
    n_keys = band_w + ctx[0].shape[2] if windowed else s
    return pl.pallas_call(
        functools.partial(_attn_swa_kernel, windowed, band_w),
        grid=(b, s // tq),
        in_specs=in_specs,
        out_specs=pl.BlockSpec((None, tq, d), lambda bi, t: (bi, t, 0)),
        out_shape=jax.ShapeDtypeStruct((b, s, d), F32),
        scratch_shapes=[pltpu.VMEM((tq, n_keys), F32) for _ in range(n_bufs)],
        compiler_params=_params(2),
        name="attn_swa",
    )(*args)


def _route(scores_t, bias):
    sel_t = scores_t + bias
    sel = [sel_t[e:e + 1] for e in range(N_EXPERTS)]
    raw = [scores_t[e:e + 1] for e in range(N_EXPERTS)]
    gscore = []
    for g in range(N_GROUPS):
        r = sel[g * 4:g * 4 + 4]
        pairs = [r[i] + r[j] for i in range(4) for j in range(i + 1, 4)]
        gscore.append(functools.reduce(jnp.maximum, pairs))
    best, gidx = gscore[0], jnp.zeros_like(gscore[0], dtype=jnp.int32)
    for g in range(1, N_GROUPS):
        take = gscore[g] > best
        best = jnp.where(take, gscore[g], best)
        gidx = jnp.where(take, g, gidx)
    vals = []
    for k in range(EXPERTS_PER_GROUP):
        v = sel[k]
        for g in range(1, N_GROUPS):
            v = jnp.where(gidx == g, sel[g * 4 + k], v)
        vals.append(v)

    def argmax4(vs):
        m, idx = vs[0], jnp.zeros_like(gidx)
        for k in range(1, 4):
            take = vs[k] > m
            m = jnp.where(take, vs[k], m)
            idx = jnp.where(take, k, idx)
        return idx

    i1 = argmax4(vals)
    i2 = argmax4([jnp.where(i1 == k, -jnp.inf, vals[k]) for k in range(4)])
    e1, e2 = gidx * 4 + i1, gidx * 4 + i2
    w1 = functools.reduce(lambda a, b: a + b, [jnp.where(e1 == e, raw[e], 0.0) for e in range(N_EXPERTS)])
    w2 = functools.reduce(lambda a, b: a + b, [jnp.where(e2 == e, raw[e], 0.0) for e in range(N_EXPERTS)])
    den = w1 + w2
    return e1, e2, w1 / den, w2 / den


SC_WINDOW = 128
SC_ROW = 128
N_PLANES = (D_MODEL // 2) // SC_ROW
MOE_ROW_BLOCK = 512
PROMPT_BUFS = 8
HALF_WORD = -65536


def _to_planes(packed, out_ref):
    for j in range(N_PLANES):
        out_ref[j] = packed[:, j * SC_ROW:(j + 1) * SC_ROW]


def _from_planes(ref):
    return jnp.concatenate([ref[j] for j in range(N_PLANES)], axis=1)


def _pack_bf16_pairs(x):
    n = x.shape[1] // 2
    lo = pltpu.bitcast(x[:, :n].astype(BF16).astype(F32), jnp.int32)
    hi = pltpu.bitcast(x[:, n:].astype(BF16).astype(F32), jnp.int32)
    return lax.shift_right_logical(lo, jnp.int32(16)) | (hi & jnp.int32(HALF_WORD))


def _unpack_bf16_pairs(p):
    lo = pltpu.bitcast(lax.shift_left(p, jnp.int32(16)), F32)
    hi = pltpu.bitcast(p & jnp.int32(HALF_WORD), F32)
    return jnp.concatenate([lo, hi], axis=1)


def _pick(idx, rows):
    return _tree(jnp.add, [jnp.where(idx == e, rows[e], 0.0) for e in range(N_EXPERTS)])


def _moe_route_kernel(x_ref, mod_ref, g_ref, rw_ref, rb_ref, tri_ref, hp_o, eidx_o, rank_o, gcol_o, cnt_o, seen):
    @pl.when(jnp.logical_and(pl.program_id(0) == 0, pl.program_id(1) == 0))
    def _():
        seen[...] = jnp.zeros_like(seen)

    tm = x_ref.shape[0]
    mod = mod_ref[0]
    h = _modulate(x_ref[...], g_ref[...], mod[3:4], mod[4:5])
    h16 = h.astype(BF16)
    _to_planes(_pack_bf16_pairs(h), hp_o)
    h_lo = (h - h16.astype(F32)).astype(BF16)
    part = _dot_nt(rw_ref[...], h16)
    logits_t = part[:N_EXPERTS] + part[N_EXPERTS:] + _dot_nt(rw_ref[:N_EXPERTS], h_lo)
    e1, e2, g1, g2 = _route(_sigmoid(logits_t), rb_ref[...])
    hot = jnp.concatenate([jnp.where(jnp.logical_or(e1 == e, e2 == e), 1.0, 0.0) for e in range(N_EXPERTS)], axis=0)
    before = _dot(hot.astype(BF16), tri_ref[...]) + seen[...]
    rows = [before[e:e + 1] for e in range(N_EXPERTS)]
    eidx_o[0], eidx_o[1] = e1, e2
    rank_o[0], rank_o[1] = _pick(e1, rows).astype(jnp.int32), _pick(e2, rows).astype(jnp.int32)
    gcol_o[...] = jnp.concatenate([g1, g2, jnp.zeros((LANES - 2, tm), F32)], axis=0).T
    seen[...] = seen[...] + jnp.sum(hot, axis=-1, keepdims=True)
    cnt_o[...] = seen[...]


def _moe_slots_kernel(mb, n_rows, eidx_ref, rank_ref, cnt_ref, idx_o, be_o):
    cnt = cnt_ref[...]
    padded = jnp.ceil(cnt / mb) * mb
    starts, ends, run = [], [], jnp.zeros((1, 1), F32)
    for e in range(N_EXPERTS):
        starts.append(run)
        run = run + padded[e:e + 1]
        ends.append(run)
    n_tiles, per_tile = eidx_ref.shape[1], eidx_ref.shape[3] // SC_WINDOW
    for k in range(2):
        for i in range(n_tiles):
            slot = rank_ref[k, i] + _pick(eidx_ref[k, i], starts).astype(jnp.int32)
            for j in range(N_PLANES):
                for q in range(per_tile):
                    row = ((k * N_PLANES + j) * n_tiles + i) * per_tile + q
                    idx_o[row:row + 1, :] = slot[:, q * SC_WINDOW:(q + 1) * SC_WINDOW] + j * n_rows
    first_row = lax.broadcasted_iota(jnp.int32, be_o.shape, 1).astype(F32) * mb
    be = _tree(jnp.add, [jnp.where(ends[e] <= first_row, 1.0, 0.0) for e in range(N_EXPERTS)])
    lane = lax.broadcasted_iota(jnp.int32, be_o.shape, 1)
    be = jnp.where(lane == be_o.shape[1] - 1, ends[N_EXPERTS - 1] / mb, jnp.minimum(be, N_EXPERTS - 1.0))
    be_o[...] = be.astype(jnp.int32)


def _moe_ffn_kernel(be_ref, hs_ref, wg_ref, wu_ref, wd_ref, ys_o, wgu16, wd16):
    i = pl.program_id(0)
    live = i < be_ref[be_ref.shape[0] - 1]

    @pl.when(jnp.logical_and(live, jnp.logical_or(i == 0, be_ref[i] != be_ref[jnp.maximum(i - 1, 0)])))
    def _():
        wgu16[:, :D_EXPERT] = wg_ref[...].astype(BF16)
        wgu16[:, D_EXPERT:] = wu_ref[...].astype(BF16)
        wd16[...] = wd_ref[...].astype(BF16)

    @pl.when(live)
    def _():
        h = _unpack_bf16_pairs(_from_planes(hs_ref)).astype(BF16)
        gu = _dot(h, wgu16[...])
        g, u = gu[:, :D_EXPERT], gu[:, D_EXPERT:]
        a = g * _sigmoid(g) * u
        _to_planes(_pack_bf16_pairs(_dot(a.astype(BF16), wd16[...])), ys_o)


def _moe_residual(x, z0_ref, z1_ref, gcol_ref, mod_ref):
    gc = gcol_ref[...]
    lane = lax.broadcasted_iota(jnp.int32, gc.shape, 1)
    g1 = jnp.sum(jnp.where(lane == 0, gc, 0.0), axis=-1, keepdims=True)
    g2 = jnp.sum(jnp.where(lane == 1, gc, 0.0), axis=-1, keepdims=True)
    y = g1 * _unpack_bf16_pairs(_from_planes(z0_ref)) + g2 * _unpack_bf16_pairs(_from_planes(z1_ref))
    return x + mod_ref[0][5:6] * y


def _moe_combine_kernel(x_ref, z0_ref, z1_ref, gcol_ref, mod_ref, fg_ref, o_ref):
    o_ref[...] = _rms(_moe_residual(x_ref[...], z0_ref, z1_ref, gcol_ref, mod_ref), fg_ref[...])


def _residual_specs(resid, row_fn, tm, d, in_specs, args):
    (z, gcol), mods_prev = resid
    for k in range(2):
        in_specs.append(pl.BlockSpec((None, N_PLANES, None, tm, SC_ROW), lambda bi, t, k=k: (k, 0, bi, t, 0)))
        args.append(z)
    in_specs += [pl.BlockSpec((None, tm, LANES), lambda bi, t: (bi, t, 0)),
                 pl.BlockSpec((1, 6, d), lambda bi, t: (row_fn(bi), 0, 0))]
    args += [gcol, mods_prev]


def _sc_mesh():
    return plsc.VectorSubcoreMesh(core_axis_name="core", subcore_axis_name="subcore")


def _sc_scatter_rows(rows, idx, n_out):
    n, w = rows.shape
    steps = n // SC_WINDOW

    @pl.kernel(out_type=jax.ShapeDtypeStruct((n_out, w), rows.dtype), mesh=_sc_mesh(), scratch_types=[])
    def scatter(x_hbm, i0_hbm, i1_hbm, o_hbm):
        def body(x_vmem, i0_vmem, i1_vmem):
            pltpu.sync_copy(x_vmem, o_hbm.at[i0_vmem.at[0]])
            pltpu.sync_copy(x_vmem, o_hbm.at[i1_vmem.at[0]])

        pltpu.emit_pipeline(
            body,
            grid=(steps,),
            in_specs=[pl.BlockSpec((SC_WINDOW, w), lambda i: (i, 0)),
                      pl.BlockSpec((1, SC_WINDOW), lambda i: (i, 0)),
                      pl.BlockSpec((1, SC_WINDOW), lambda i: (i + steps, 0))],
            out_specs=[],
            core_axis_name=("core", "subcore"),
            dimension_semantics=(pltpu.PARALLEL,),
        )(x_hbm, i0_hbm, i1_hbm)

    return scatter(rows, idx, idx)


def _sc_gather_rows(rows, idx):
    steps, w = idx.shape[0], rows.shape[1]

    @pl.kernel(out_type=jax.ShapeDtypeStruct((steps * SC_WINDOW, w), rows.dtype), mesh=_sc_mesh(),
               scratch_types=[])
    def gather(x_hbm, i_hbm, o_hbm):
        def body(i_vmem, o_vmem):
            pltpu.sync_copy(x_hbm.at[i_vmem.at[0]], o_vmem)

        pltpu.emit_pipeline(
            body,
            grid=(steps,),
            in_specs=[pl.BlockSpec((1, SC_WINDOW), lambda i: (i, 0))],
            out_specs=[pl.BlockSpec((SC_WINDOW, w), lambda i: (i, 0))],
            core_axis_name=("core", "subcore"),
            dimension_semantics=(pltpu.PARALLEL,),
        )(i_hbm, o_hbm)

    return gather(rows, idx)


def _moe_bucket(x, mods, row_fn, g, rw_t, rb, tm, mb):
    b, s, d = x.shape
    n, nt = b * s, (b * s) // tm
    per_b = s // tm
    tile = lambda bi, t: (0, bi * per_b + t, 0, 0)
    tri = jnp.asarray(np.triu(np.ones((tm, tm), np.float32), 1), BF16)
    hp, eidx, rank, gcol, cnt = pl.pallas_call(
        _moe_route_kernel,
        grid=(b, per_b),
        in_specs=[
            pl.BlockSpec((None, tm, d), lambda bi, t: (bi, t, 0)),
            pl.BlockSpec((1, 6, d), lambda bi, t: (row_fn(bi), 0, 0)),
            _full((1, d)), _full((2 * N_EXPERTS, d)), _full((N_EXPERTS, 1)), _full((tm, tm)),
        ],
        out_specs=[
            pl.BlockSpec((N_PLANES, tm, SC_ROW), lambda bi, t: (0, bi * per_b + t, 0)),
            pl.BlockSpec((2, None, 1, tm), tile), pl.BlockSpec((2, None, 1, tm), tile),
            pl.BlockSpec((None, tm, LANES), lambda bi, t: (bi, t, 0)),
            _full((N_EXPERTS, 1)),
        ],
        out_shape=[
            jax.ShapeDtypeStruct((N_PLANES, n, SC_ROW), jnp.int32),
            jax.ShapeDtypeStruct((2, nt, 1, tm), jnp.int32), jax.ShapeDtypeStruct((2, nt, 1, tm), jnp.int32),
            jax.ShapeDtypeStruct((b, s, LANES), F32),
            jax.ShapeDtypeStruct((N_EXPERTS, 1), F32),
        ],
        scratch_shapes=[pltpu.VMEM((N_EXPERTS, 1), F32)],
        compiler_params=_params(2),
        name="moe_route",
    )(x, mods, g, rw_t, rb, tri)

    n_rows = 2 * n + N_EXPERTS * mb
    nb = n_rows // mb
    nb_pad = -(-(nb + 1) // LANES) * LANES
    idx, be = pl.pallas_call(
        functools.partial(_moe_slots_kernel, mb, n_rows),
        out_shape=[jax.ShapeDtypeStruct((2 * N_PLANES * n // SC_WINDOW, SC_WINDOW), jnp.int32),
                   jax.ShapeDtypeStruct((1, nb_pad), jnp.int32)],
        name="moe_slots",
    )(eidx, rank, cnt)

    hs = _sc_scatter_rows(hp.reshape(N_PLANES * n, SC_ROW), idx, N_PLANES * n_rows)
    return hs.reshape(N_PLANES, n_rows, SC_ROW), idx, be.reshape(nb_pad), gcol


def _moe_experts(bucket, layer, w_gate, w_up, w_down, mb, out_dims):
    hs, idx, be, gcol = bucket
    n_rows, d = hs.shape[1], w_gate.shape[2]
    last = be.shape[0] - 1
    blk = lambda i, be_r: jnp.minimum(i, be_r[last] - 1)
    plane_block = pl.BlockSpec((N_PLANES, mb, SC_ROW), lambda i, be_r: (0, blk(i, be_r), 0))
    weight = lambda i, be_r: (layer, be_r[blk(i, be_r)], 0, 0)
    ys = pl.pallas_call(
        _moe_ffn_kernel,
        grid_spec=pltpu.PrefetchScalarGridSpec(
            num_scalar_prefetch=1,
            grid=(n_rows // mb,),
            in_specs=[
                plane_block,
                pl.BlockSpec((None, None, d, D_EXPERT), weight),
                pl.BlockSpec((None, None, d, D_EXPERT), weight),
                pl.BlockSpec((None, None, D_EXPERT, d), weight),
            ],
            out_specs=plane_block,
            scratch_shapes=[pltpu.VMEM((d, 2 * D_EXPERT), BF16), pltpu.VMEM((D_EXPERT, d), BF16)],
        ),
        out_shape=jax.ShapeDtypeStruct((N_PLANES, n_rows, SC_ROW), jnp.int32),
        compiler_params=_params(1),
        name="moe_ffn",
    )(be, hs, w_gate, w_up, w_down)
    z = _sc_gather_rows(ys.reshape(N_PLANES * n_rows, SC_ROW), idx)
    return z.reshape(2, N_PLANES, *out_dims, SC_ROW), gcol.reshape(*out_dims, LANES)


def _moe_combine(x, mods, row_fn, gathered, final_g, tm):
    b, s, d = x.shape
    in_specs = [pl.BlockSpec((None, tm, d), lambda bi, t: (bi, t, 0))]
    args = [x]
    _residual_specs((gathered, mods), row_fn, tm, d, in_specs, args)
    in_specs.append(_full((1, d)))
    args.append(final_g)
    return pl.pallas_call(
        _moe_combine_kernel,
        grid=(b, s // tm),
        in_specs=in_specs,
        out_specs=pl.BlockSpec((None, tm, d), lambda bi, t: (bi, t, 0)),
        out_shape=jax.ShapeDtypeStruct((b, s, d), F32),
        compiler_params=_params(2),
        name="moe_combine",
    )(*args)


def _rope_tables(n_tokens, rot_dim, offset):
    n_rows = n_tokens // GRID_W
    rows = np.repeat(np.arange(n_rows), GRID_W)
    cols = np.tile(np.arange(GRID_W), n_rows)
    n_freq = rot_dim // 4
    inv = jnp.asarray(ROPE_THETA, F32) ** (-jnp.arange(n_freq, dtype=F32) / n_freq)
    ang = jnp.concatenate([jnp.asarray(rows, F32)[:, None] * inv, jnp.asarray(cols, F32)[:, None] * inv], axis=-1)
    cos, sin = jnp.repeat(jnp.cos(ang), 2, axis=-1), jnp.repeat(jnp.sin(ang), 2, axis=-1)
    odd = jnp.asarray(np.arange(rot_dim) % 2 == 1)
    parts = [cos, jnp.where(odd, sin, 0.0), jnp.where(odd, 0.0, -sin)]
    period = 64 if rot_dim == 64 else LANES
    fill = [1.0, 0.0, 0.0]
    out = []
    for p, f in zip(parts, fill):
        slot = jnp.full((n_tokens, period), f, F32).at[:, offset:offset + rot_dim].set(p)
        out.append(jnp.tile(slot, (1, LANES // period)))
    return jnp.stack(out)


def _pad_heads(w, n_heads, lo, hi):
    k = w.shape[0]
    w = w.reshape(k, n_heads, -1)[:, :, lo:hi]
    return jnp.pad(w, ((0, 0), (0, 0), (0, LANES - (hi - lo)))).reshape(k, n_heads * LANES)


def _ab_weights(j, ab_w_in, diff_lambda, diff_subln_g, mla_q_norm_g, mla_w_qb, mla_kv_norm_g, mla_w_kvb, ab_w_out):
    place = np.zeros((LANES, MLA_HEADS * LANES), np.float32)
    for hd in range(MLA_HEADS):
        for r in range(MLA_ROPE_DIM):
            place[r, hd * LANES + MLA_NOPE_DIM + r] = 1.0
    w_out_m = ab_w_out[j][512:].reshape(MLA_HEADS, MLA_V_DIM, D_MODEL)
    return {
        "w_in": jnp.pad(ab_w_in[j], ((0, 0), (0, AB_IN_PAD - AB_IN))).astype(BF16),
        "qn_g": mla_q_norm_g[j][None, :],
        "w_qb": _pad_heads(mla_w_qb[j], MLA_HEADS, 0, MLA_NOPE_DIM + MLA_ROPE_DIM).astype(BF16),
        "kvn_g": mla_kv_norm_g[j][None, :],
        "wk": _pad_heads(mla_w_kvb[j], MLA_HEADS, 0, MLA_NOPE_DIM).astype(BF16),
        "wv": _pad_heads(mla_w_kvb[j], MLA_HEADS, MLA_NOPE_DIM, MLA_NOPE_DIM + MLA_V_DIM).astype(BF16),
        "e": jnp.asarray(place, BF16),
        "w_out_d": ab_w_out[j][:512].reshape(DIFF_HEADS, LANES, D_MODEL).astype(BF16),
        "w_out_m": jnp.pad(w_out_m, ((0, 0), (0, LANES - MLA_V_DIM), (0, 0))).astype(BF16),
        "subln_g": diff_subln_g[j][None, :],
        "lam": diff_lambda[j],
    }


def kernel(x_prompt, x_sample, cache_diff_k, cache_diff_v, cache_mla_ckv, cache_mla_krope, cache_swa_k, cache_swa_v,
           c, c_ctx, ada_w, ada_b, norm1_g, norm2_g, final_norm_g, ab_w_in, diff_lambda, diff_subln_g, mla_q_norm_g,
           mla_w_qb, mla_kv_norm_g, mla_w_kvb, ab_w_out, swa_w_in, swa_sink, swa_w_out, router_w, router_bias,
           moe_w_gate, moe_w_up, moe_w_down):
    bp, sp, d = x_prompt.shape
    bs, ss, _ = x_sample.shape
    depth = ada_w.shape[0]
    past = cache_diff_k.shape[2]
    n_ab = cache_diff_k.shape[1]
    n_swa = cache_swa_k.shape[1]
    assert bs + 1 <= COND_ROWS and d == D_MODEL

    cond = jnp.zeros((COND_ROWS, d), F32).at[:bs].set(c).at[bs].set(c_ctx)
    mods = _ada_all(cond, ada_w, ada_b).reshape(depth, COND_ROWS, 6, d)
    row_s = lambda bi: bi
    row_p = lambda bi: bi * 0 + bs

    rope_d = _rope_tables(ss, DIFF_QK_DIM, 0)
    rope_q = _rope_tables(ss, MLA_ROPE_DIM, MLA_NOPE_DIM)
    rope_k = _rope_tables(ss, MLA_ROPE_DIM, 0)

    cdk = cache_diff_k.reshape(bs, n_ab, past, 512)
    cdv = cache_diff_v.reshape(bs, n_ab, past, 512)
    ckr = jnp.pad(cache_mla_krope, ((0, 0), (0, 0), (0, 0), (0, LANES - MLA_ROPE_DIM)))
    csk = cache_swa_k.reshape(bs, n_swa, past, 256)
    csv = cache_swa_v.reshape(bs, n_swa, past, 256)
    rw_hi = router_w.T.astype(BF16)
    rw_t = jnp.concatenate([rw_hi, (router_w.T - rw_hi.astype(F32)).astype(BF16)], axis=0)
    rb = router_bias[:, None]

    tp = min(256, sp)
    ts = min(256, ss)
    tm_s = min(512, ss)
    tm_p = min(512, bp * sp)

    xp, xs = x_prompt, x_sample
    new_ab, new_swa = None, None
    pend_p = pend_s = None

    def project(fn, x, pend, *a):
        outs = list(fn(x, *a, pend))
        return (outs.pop() if pend is not None else x), outs

    for l in range(depth):
        j = l // 2
        ml = mods[l]
        g1, g2 = norm1_g[l][None, :], norm2_g[l][None, :]
        if l % 2 == 0:
            lambda_init = 0.8 - 0.6 * math.exp(-0.3 * l)
            wts = _ab_weights(j, ab_w_in, diff_lambda, diff_subln_g, mla_q_norm_g, mla_w_qb, mla_kv_norm_g,
                              mla_w_kvb, ab_w_out)
            xp, (qa, ka, va, qb, km, vm, *new_ab) = project(_proj_ab, xp, pend_p, ml, row_p, g1, wts, None,
                                                             (j, n_ab, new_ab), tp)
            xp = _attn_ab(xp, ml, row_p, (qa, qb), [(ka, va, km, vm)], wts, lambda_init, tp, PROMPT_BUFS)
            xs, (qa, ka, va, qb, km, vm) = project(_proj_ab, xs, pend_s, ml, row_s, g1, wts,
                                                   (rope_d, rope_q, rope_k), None, ts)
            cache_seg = _cache_ab(cdk, cdv, cache_mla_ckv, ckr, j, wts)
            xs = _attn_ab(xs, ml, row_s, (qa, qb), [(ka, va, km, vm), tuple(cache_seg)], wts, lambda_init, ts, 2)
        else:
            w_in = swa_w_in[j].astype(BF16)
            w_out = swa_w_out[j].reshape(8, LANES, d).astype(BF16)
            xp, (q, k, v, *new_swa) = project(_proj_swa, xp, pend_p, ml, row_p, g1, w_in, None,
                                              (j, n_swa, new_swa), tp)
            xp = _attn_swa(xp, ml, row_p, q, k, v, None, swa_sink[j], w_out, tp, PROMPT_BUFS)
            xs, (q, k, v) = project(_proj_swa, xs, pend_s, ml, row_s, g1, w_in, rope_d, None, ts)
            xs = _attn_swa(xs, ml, row_s, q, k, v, (csk, csv, j), swa_sink[j], w_out, ts, 2)
        bucket_p = _moe_bucket(xp.reshape(1, bp * sp, d), ml, row_p, g2, rw_t, rb, tm_p, MOE_ROW_BLOCK)
        bucket_s = _moe_bucket(xs, ml, row_s, g2, rw_t, rb, tm_s, MOE_ROW_BLOCK)
        got_p = _moe_experts(bucket_p, l, moe_w_gate, moe_w_up, moe_w_down, MOE_ROW_BLOCK, (bp, sp))
        got_s = _moe_experts(bucket_s, l, moe_w_gate, moe_w_up, moe_w_down, MOE_ROW_BLOCK, (bs, ss))
        pend_p, pend_s = (got_p, ml), (got_s, ml)

    xp = _moe_combine(xp, pend_p[1], row_p, pend_p[0], final_norm_g[None, :], tp)
    xs = _moe_combine(xs, pend_s[1], row_s, pend_s[0], final_norm_g[None, :], ts)

    new_diff_k = new_ab[0].reshape(bp, n_ab, sp, DIFF_HEADS, 2, DIFF_QK_DIM)
    new_diff_v = new_ab[1].reshape(bp, n_ab, sp, DIFF_HEADS, DIFF_V_DIM)
    new_mla_ckv, new_mla_krope = new_ab[2], new_ab[3]
    new_swa_k = new_swa[0].reshape(bp, n_swa, sp, SWA_KV_HEADS, SWA_HEAD_DIM)
    new_swa_v = new_swa[1].reshape(bp, n_swa, sp, SWA_KV_HEADS, SWA_HEAD_DIM)
    return (xp, xs, new_diff_k, new_diff_v, new_mla_ckv, new_mla_krope, new_swa_k, new_swa_v)
```

```python
import functools
import math

import jax
import jax.numpy as jnp
import numpy as np
from jax import lax
from jax.experimental import pallas as pl
from jax.experimental.pallas import tpu as pltpu
from jax.experimental.pallas import tpu_sc as plsc

F32 = jnp.float32
BF16 = jnp.bfloat16

D_MODEL = 1024
GRID_W = 64
ROPE_THETA = 10000.0
NORM_EPS = 1e-6
NEG_INF = -1e30
LOG2E = math.log2(math.e)
LANES = 128

DIFF_HEADS = 4
DIFF_QK_DIM = 64
DIFF_V_DIM = 128
MLA_HEADS = 8
MLA_Q_RANK = 384
MLA_KV_RANK = 256
MLA_NOPE_DIM = 64
MLA_ROPE_DIM = 32
MLA_V_DIM = 64
AB_IN = 3 * 512 + MLA_Q_RANK + MLA_KV_RANK + MLA_ROPE_DIM
AB_IN_PAD = 2304
SWA_HEADS = 16
SWA_KV_HEADS = 4
SWA_GROUP = 4
SWA_HEAD_DIM = 64
WINDOW = 128
N_EXPERTS = 16
N_GROUPS = 4
EXPERTS_PER_GROUP = 4
D_EXPERT = 256
COND_ROWS = 16

VMEM_LIMIT = 56 * 1024 * 1024


def _full(shape):
    n = len(shape)
    return pl.BlockSpec(shape, lambda *_: (0,) * n)


def _params(n_axes):
    return pltpu.CompilerParams(dimension_semantics=("arbitrary",) * n_axes, vmem_limit_bytes=VMEM_LIMIT)


def _sigmoid(x):
    return 1.0 / (1.0 + jnp.exp(-x))


def _rms(x, g):
    return x * lax.rsqrt(jnp.mean(x * x, axis=-1, keepdims=True) + NORM_EPS) * g


def _modulate(x, g, shift, scale):
    return _rms(x, g) * (1.0 + scale) + shift


def _dot(a, b):
    return jnp.dot(a, b, preferred_element_type=F32)


def _dot_nt(a, b):
    return lax.dot_general(a, b, (((1,), (1,)), ((), ())), preferred_element_type=F32)


def _rope(x, tab_ref):
    c, s_odd, s_even = tab_ref[0], tab_ref[1], tab_ref[2]
    out = []
    for i in range(x.shape[1] // LANES):
        xi = x[:, i * LANES:(i + 1) * LANES]
        out.append(xi * c + pltpu.roll(xi, 1, 1) * s_odd + pltpu.roll(xi, LANES - 1, 1) * s_even)
    return out[0] if len(out) == 1 else jnp.concatenate(out, axis=1)


def _tree(op, xs):
    xs = list(xs)
    while len(xs) > 1:
        xs = [op(xs[i], xs[i + 1]) if i + 1 < len(xs) else xs[i] for i in range(0, len(xs), 2)]
    return xs[0]


def _lane_fold(op, x):
    return _tree(op, [x[:, i * LANES:(i + 1) * LANES] for i in range(x.shape[1] // LANES)])


def _ada_kernel(cond_ref, w_ref, b_ref, o_ref):
    c = cond_ref[...]
    a = (c * _sigmoid(c)).astype(BF16)
    o_ref[...] = _dot(a, w_ref[...].astype(BF16)) + b_ref[...]


def _ada_all(cond, ada_w, ada_b):
    depth, d, n = ada_w.shape
    tn = 1536
    return pl.pallas_call(
        _ada_kernel,
        grid=(depth, n // tn),
        in_specs=[
            _full((COND_ROWS, d)),
            pl.BlockSpec((None, d, tn), lambda l, j: (l, 0, j)),
            pl.BlockSpec((None, 1, tn), lambda l, j: (l, 0, j)),
        ],
        out_specs=pl.BlockSpec((None, COND_ROWS, tn), lambda l, j: (l, 0, j)),
        out_shape=jax.ShapeDtypeStruct((depth, COND_ROWS, n), F32),
        compiler_params=_params(2),
        name="ada_mod",
    )(cond, ada_w, ada_b.reshape(depth, 1, n))


def _proj_ab_kernel(rope, emit_cache, n_prev, fused, x_ref, mod_ref, g_ref, w_in_ref, qn_ref, wqb_ref, kvn_ref,
                    wk_ref, wv_ref, e_ref, *rest):
    if rope:
        rope_d, rope_q, rope_k = rest[:3]
        rest = rest[3:]
    x = x_ref[...]
    if fused:
        x = _moe_residual(x, *rest[:4])
        rest[-1][...] = x
        rest = rest[4:]
    rest = rest[n_prev:]
    qa_o, ka_o, va_o, qb_o, km_o, vm_o = rest[:6]
    mod = mod_ref[0]
    h = _modulate(x, g_ref[...], mod[0:1], mod[1:2]).astype(BF16)
    big = _dot(h, w_in_ref[...])
    qa, ka, va = big[:, 0:512], big[:, 512:1024], big[:, 1024:1536]
    q_lat, ckv, kr = big[:, 1536:1920], big[:, 1920:2176], big[:, 2176:2304]
    qb = _dot(_rms(q_lat, qn_ref[...]).astype(BF16), wqb_ref[...])
    ckv_n = _rms(ckv, kvn_ref[...])
    if emit_cache:
        ka32_o, va32_o, ckv32_o, kr32_o = rest[6:10]
        ka32_o[...] = ka
        va32_o[...] = va
        ckv32_o[...] = ckv_n
        kr32_o[...] = kr[:, :MLA_ROPE_DIM]
    if rope:
        qa, ka = _rope(qa, rope_d), _rope(ka, rope_d)
        qb, kr = _rope(qb, rope_q), _rope(kr, rope_k)
    qa = qa * (DIFF_QK_DIM ** -0.5 * LOG2E)
    qb = qb * ((MLA_NOPE_DIM + MLA_ROPE_DIM) ** -0.5 * LOG2E)
    ckv16 = ckv_n.astype(BF16)
    km = _dot(ckv16, wk_ref[...]) + _dot(kr.astype(BF16), e_ref[...])
    vm = _dot(ckv16, wv_ref[...])
    lo = lax.broadcasted_iota(jnp.int32, (qa.shape[0], LANES), 1) < DIFF_QK_DIM
    for hd in range(DIFF_HEADS):
        sl = slice(hd * LANES, (hd + 1) * LANES)
        qa_o[2 * hd] = jnp.where(lo, qa[:, sl], 0.0).astype(BF16)
        qa_o[2 * hd + 1] = jnp.where(lo, 0.0, qa[:, sl]).astype(BF16)
        ka_o[hd] = ka[:, sl].astype(BF16)
        va_o[hd] = va[:, sl].astype(BF16)
    for hd in range(MLA_HEADS):
        sl = slice(hd * LANES, (hd + 1) * LANES)
        qb_o[hd] = qb[:, sl].astype(BF16)
        km_o[hd] = km[:, sl].astype(BF16)
        vm_o[hd] = vm[:, sl].astype(BF16)


def _cache_outputs(cache, widths, b, s, tm, in_specs, args, n_fixed_outs):
    if cache is None:
        return [], {}, 0
    j, n_layers, prev = cache
    outs = [(jax.ShapeDtypeStruct((b, n_layers, s, w), F32),
             pl.BlockSpec((None, None, tm, w), lambda bi, t: (bi, j, t, 0))) for w in widths]
    aliases = {}
    if prev is not None:
        for i, arr in enumerate(prev):
            aliases[len(args)] = n_fixed_outs + i
            in_specs.append(pl.BlockSpec(memory_space=pl.ANY))
            args.append(arr)
    return outs, aliases, len(aliases)


def _proj_ab(x, mods, row_fn, g, wts, rope_tabs, cache, tm, resid=None):
    b, s, d = x.shape
    rope = rope_tabs is not None
    tok = lambda bi, t: (bi, t, 0)
    head = lambda bi, t: (bi, 0, t, 0)
    in_specs = [
        pl.BlockSpec((None, tm, d), tok),
        pl.BlockSpec((1, 6, d), lambda bi, t: (row_fn(bi), 0, 0)),
        _full((1, d)),
        _full((d, AB_IN_PAD)), _full((1, MLA_Q_RANK)), _full((MLA_Q_RANK, 1024)), _full((1, MLA_KV_RANK)),
        _full((MLA_KV_RANK, 1024)), _full((MLA_KV_RANK, 1024)), _full((LANES, 1024)),
    ]
    args = [x, mods, g, wts["w_in"], wts["qn_g"], wts["w_qb"], wts["kvn_g"], wts["wk"], wts["wv"], wts["e"]]
    if rope:
        in_specs += [pl.BlockSpec((3, tm, LANES), lambda bi, t: (0, t, 0))] * 3
        args += list(rope_tabs)
    if resid is not None:
        _residual_specs(resid, row_fn, tm, d, in_specs, args)

    def hm(nh):
        return jax.ShapeDtypeStruct((b, nh, s, LANES), BF16), pl.BlockSpec((None, nh, tm, LANES), head)

    outs = [hm(8), hm(4), hm(4), hm(8), hm(8), hm(8)]
    cache_outs, aliases, n_prev = _cache_outputs(cache, (512, 512, MLA_KV_RANK, MLA_ROPE_DIM), b, s, tm, in_specs,
                                                 args, len(outs))
    outs += cache_outs
    if resid is not None:
        outs.append((jax.ShapeDtypeStruct((b, s, d), F32), pl.BlockSpec((None, tm, d), tok)))
    return pl.pallas_call(
        functools.partial(_proj_ab_kernel, rope, cache is not None, n_prev, resid is not None),
        grid=(b, s // tm),
        in_specs=in_specs,
        out_specs=[o[1] for o in outs],
        out_shape=[o[0] for o in outs],
        input_output_aliases=aliases,
        compiler_params=_params(2),
        name="proj_ab",
    )(*args)


def _cache_ab_kernel(dk_ref, dv_ref, ckv_ref, kr_ref, wk_ref, wv_ref, e_ref, ck_o, cv_o, km_o, vm_o):
    ckv16 = ckv_ref[...].astype(BF16)
    km = _dot(ckv16, wk_ref[...]) + _dot(kr_ref[...].astype(BF16), e_ref[...])
    vm = _dot(ckv16, wv_ref[...])
    for hd in range(DIFF_HEADS):
        sl = slice(hd * LANES, (hd + 1) * LANES)
        ck_o[hd] = dk_ref[:, sl].astype(BF16)
        cv_o[hd] = dv_ref[:, sl].astype(BF16)
    for hd in range(MLA_HEADS):
        sl = slice(hd * LANES, (hd + 1) * LANES)
        km_o[hd] = km[:, sl].astype(BF16)
        vm_o[hd] = vm[:, sl].astype(BF16)


def _cache_ab(cdk, cdv, cckv, ckr, j, wts):
    b, _, p, _ = cdk.shape
    lay = lambda bi: (bi, j, 0, 0)

    def hm(nh):
        return (jax.ShapeDtypeStruct((b, nh, p, LANES), BF16),
                pl.BlockSpec((None, nh, p, LANES), lambda bi: (bi, 0, 0, 0)))

    outs = [hm(4), hm(4), hm(8), hm(8)]
    return pl.pallas_call(
        _cache_ab_kernel,
        grid=(b,),
        in_specs=[
            pl.BlockSpec((None, None, p, 512), lay), pl.BlockSpec((None, None, p, 512), lay),
            pl.BlockSpec((None, None, p, MLA_KV_RANK), lay), pl.BlockSpec((None, None, p, LANES), lay),
            _full((MLA_KV_RANK, 1024)), _full((MLA_KV_RANK, 1024)), _full((LANES, 1024)),
        ],
        out_specs=[o[1] for o in outs],
        out_shape=[o[0] for o in outs],
        compiler_params=_params(1),
        name="cache_ab",
    )(cdk, cdv, cckv, ckr, wts["wk"], wts["wv"], wts["e"])


def _attn_ab_kernel(n_seg, lambda_init, x_ref, mod_ref, qa_ref, qb_ref, *rest):
    segs = [rest[4 * i:4 * i + 4] for i in range(n_seg)]
    wd_ref, wm_ref, sg_ref, lam_ref, o_ref = rest[4 * n_seg:4 * n_seg + 5]
    bufs = rest[4 * n_seg + 5:]
    lp = lam_ref[...]
    lam = (jnp.exp(jnp.sum(lp[0:1] * lp[1:2], axis=-1, keepdims=True))
           - jnp.exp(jnp.sum(lp[2:3] * lp[3:4], axis=-1, keepdims=True)) + lambda_init)
    tq, d = x_ref.shape
    widths = [sg[0].shape[1] for sg in segs]
    offs = [sum(widths[:i]) for i in range(n_seg)]
    units = []
    for hd in range(DIFF_HEADS):
        units += [("diff", hd), ("mla", 2 * hd), ("mla", 2 * hd + 1)]

    def scores(u, buf):
        kind, hd = units[u]
        qs = [qa_ref[2 * hd], qa_ref[2 * hd + 1]] if kind == "diff" else [qb_ref[hd]]
        ks = [sg[0 if kind == "diff" else 2][hd] for sg in segs]
        ms = []
        for i, q in enumerate(qs):
            folds = []
            for k, off, w in zip(ks, offs, widths):
                s = _dot_nt(q, k)
                buf[i, :, off:off + w] = s
                folds.append(_lane_fold(jnp.maximum, s))
            ms.append(jnp.max(_tree(jnp.maximum, folds), axis=-1, keepdims=True))
        return ms

    def exps(buf, i, m):
        e = [jnp.exp2(buf[i, :, off:off + w] - m) for off, w in zip(offs, widths)]
        l = jnp.sum(_tree(jnp.add, [_lane_fold(jnp.add, x) for x in e]), axis=-1, keepdims=True)
        return e, l

    def consume(u, buf, ms, acc):
        kind, hd = units[u]
        if kind == "diff":
            vs = [sg[1][hd] for sg in segs]
            (e1, l1), (e2, l2) = exps(buf, 0, ms[0]), exps(buf, 1, ms[1])
            c1, c2 = 1.0 / l1, lam / l2
            o = _tree(jnp.add, [_dot((a * c1 - b * c2).astype(BF16), v) for a, b, v in zip(e1, e2, vs)])
            od = _rms(o, sg_ref[...]) * (1.0 - lambda_init)
            return acc + _dot(od.astype(BF16), wd_ref[hd])
        vs = [sg[3][hd] for sg in segs]
        e, l = exps(buf, 0, ms[0])
        o = _tree(jnp.add, [_dot(x.astype(BF16), v) for x, v in zip(e, vs)]) / l
        return acc + _dot(o.astype(BF16), wm_ref[hd])

    acc = jnp.zeros((tq, d), F32)
    ahead, pending = len(bufs) - 1, {}
    for u in range(min(ahead, len(units))):
        pending[u] = scores(u, bufs[u % len(bufs)])
    for u in range(len(units)):
        if u + ahead < len(units):
            pending[u + ahead] = scores(u + ahead, bufs[(u + ahead) % len(bufs)])
        acc = consume(u, bufs[u % len(bufs)], pending.pop(u), acc)
    o_ref[...] = x_ref[...] + mod_ref[0][2:3] * acc


def _attn_ab(x, mods, row_fn, q_parts, seg_list, wts, lambda_init, tq, n_bufs):
    b, s, d = x.shape
    qa, qb = q_parts
    in_specs = [
        pl.BlockSpec((None, tq, d), lambda bi, t: (bi, t, 0)),
        pl.BlockSpec((1, 6, d), lambda bi, t: (row_fn(bi), 0, 0)),
        pl.BlockSpec((None, 8, tq, LANES), lambda bi, t: (bi, 0, t, 0)),
        pl.BlockSpec((None, 8, tq, LANES), lambda bi, t: (bi, 0, t, 0)),
    ]
    args = [x, mods, qa, qb]
    kv_mode = dict(pipeline_mode=pl.Buffered(1)) if s // tq > 1 else {}
    for seg in seg_list:
        for arr in seg:
            nh, nk = arr.shape[1], arr.shape[2]
            in_specs.append(pl.BlockSpec((None, nh, nk, LANES), lambda bi, t: (bi, 0, 0, 0), **kv_mode))
            args.append(arr)
    in_specs += [_full((DIFF_HEADS, LANES, d)), _full((MLA_HEADS, LANES, d)), _full((1, LANES)),
                 _full((4, DIFF_QK_DIM))]
    args += [wts["w_out_d"], wts["w_out_m"], wts["subln_g"], wts["lam"]]
    n_keys = sum(seg[0].shape[2] for seg in seg_list)
    return pl.pallas_call(
        functools.partial(_attn_ab_kernel, len(seg_list), lambda_init),
        grid=(b, s // tq),
        in_specs=in_specs,
        out_specs=pl.BlockSpec((None, tq, d), lambda bi, t: (bi, t, 0)),
        out_shape=jax.ShapeDtypeStruct((b, s, d), F32),
        scratch_shapes=[pltpu.VMEM((2, tq, n_keys), F32) for _ in range(n_bufs)],
        compiler_params=_params(2),
        name="attn_ab",
    )(*args)


def _proj_swa_kernel(rope, emit_cache, n_prev, fused, x_ref, mod_ref, g_ref, w_in_ref, *rest):
    if rope:
        rope_d = rest[0]
        rest = rest[1:]
    x = x_ref[...]
    if fused:
        x = _moe_residual(x, *rest[:4])
        rest[-1][...] = x
        rest = rest[4:]
    rest = rest[n_prev:]
    q_o, k_o, v_o = rest[:3]
    mod = mod_ref[0]
    h = _modulate(x, g_ref[...], mod[0:1], mod[1:2]).astype(BF16)
    big = _dot(h, w_in_ref[...])
    q, k, v = big[:, :1024], big[:, 1024:1280], big[:, 1280:1536]
    if emit_cache:
        k32_o, v32_o = rest[3:5]
        k32_o[...] = k
        v32_o[...] = v
    if rope:
        q, k = _rope(q, rope_d), _rope(k, rope_d)
    q = q * (SWA_HEAD_DIM ** -0.5 * LOG2E)
    lo = lax.broadcasted_iota(jnp.int32, (q.shape[0], LANES), 1) < SWA_HEAD_DIM
    for pair in range(2):
        for grp in range(SWA_GROUP):
            ca, cb = (2 * pair) * 2 + grp // 2, (2 * pair + 1) * 2 + grp // 2
            a = q[:, ca * LANES:(ca + 1) * LANES]
            bb = q[:, cb * LANES:(cb + 1) * LANES]
            if grp % 2 == 0:
                bb = pltpu.roll(bb, SWA_HEAD_DIM, 1)
            else:
                a = pltpu.roll(a, SWA_HEAD_DIM, 1)
            q_o[pair * SWA_GROUP + grp] = jnp.where(lo, a, bb).astype(BF16)
    for kvh in range(SWA_KV_HEADS):
        sl = slice((kvh // 2) * LANES, (kvh // 2 + 1) * LANES)
        keep = lo if kvh % 2 == 0 else jnp.logical_not(lo)
        k_o[kvh] = jnp.where(keep, k[:, sl], 0.0).astype(BF16)
        v_o[kvh] = jnp.where(keep, v[:, sl], 0.0).astype(BF16)


def _proj_swa(x, mods, row_fn, g, w_in, rope_tab, cache, tm, resid=None):
    b, s, d = x.shape
    rope = rope_tab is not None
    tok = lambda bi, t: (bi, t, 0)
    head = lambda bi, t: (bi, 0, t, 0)
    in_specs = [
        pl.BlockSpec((None, tm, d), tok),
        pl.BlockSpec((1, 6, d), lambda bi, t: (row_fn(bi), 0, 0)),
        _full((1, d)), _full((d, 1536)),
    ]
    args = [x, mods, g, w_in]
    if rope:
        in_specs.append(pl.BlockSpec((3, tm, LANES), lambda bi, t: (0, t, 0)))
        args.append(rope_tab)
    if resid is not None:
        _residual_specs(resid, row_fn, tm, d, in_specs, args)

    def hm(nh):
        return jax.ShapeDtypeStruct((b, nh, s, LANES), BF16), pl.BlockSpec((None, nh, tm, LANES), head)

    outs = [hm(8), hm(4), hm(4)]
    cache_outs, aliases, n_prev = _cache_outputs(cache, (256, 256), b, s, tm, in_specs, args, len(outs))
    outs += cache_outs
    if resid is not None:
        outs.append((jax.ShapeDtypeStruct((b, s, d), F32), pl.BlockSpec((None, tm, d), tok)))
    return pl.pallas_call(
        functools.partial(_proj_swa_kernel, rope, cache is not None, n_prev, resid is not None),
        grid=(b, s // tm),
        in_specs=in_specs,
        out_specs=[o[1] for o in outs],
        out_shape=[o[0] for o in outs],
        input_output_aliases=aliases,
        compiler_params=_params(2),
        name="proj_swa",
    )(*args)


def _attn_swa_kernel(windowed, band_w, x_ref, mod_ref, q_ref, k_ref, v_ref, *rest):
    if windowed:
        ck_ref, cv_ref = rest[:2]
        rest = rest[2:]
    sink_ref, w_ref, o_ref = rest[:3]
    bufs = rest[3:]
    tq, d = x_ref.shape
    n_keys = k_ref.shape[1]
    lo_k = lax.broadcasted_iota(jnp.int32, (1, LANES), 1) < SWA_HEAD_DIM
    if windowed:
        start = pl.program_id(1) * tq
        bstart = pl.multiple_of(jnp.clip(start - WINDOW, 0, n_keys - band_w), LANES)
        qpos = start + lax.broadcasted_iota(jnp.int32, (tq, band_w), 0)
        kpos = bstart + lax.broadcasted_iota(jnp.int32, (tq, band_w), 1)
        bias = jnp.where(jnp.abs(qpos - kpos) <= WINDOW, 0.0, NEG_INF)

    def keys_values(kvh):
        if not windowed:
            return [k_ref[kvh]], [v_ref[kvh]]
        keep = lo_k if kvh % 2 == 0 else jnp.logical_not(lo_k)
        sl = slice((kvh // 2) * LANES, (kvh // 2 + 1) * LANES)
        kc = jnp.where(keep, ck_ref[:, sl], 0.0).astype(BF16)
        vc = jnp.where(keep, cv_ref[:, sl], 0.0).astype(BF16)
        return ([k_ref[kvh, pl.ds(bstart, band_w), :], kc], [v_ref[kvh, pl.ds(bstart, band_w), :], vc])

    units = [(kvh, g) for kvh in range(SWA_KV_HEADS) for g in range(SWA_GROUP)]

    def scores(u, buf):
        kvh, g = units[u]
        q = q_ref[(kvh // 2) * SWA_GROUP + g]
        sink = sink_ref[kvh * SWA_GROUP + g] * LOG2E
        folds, off = [], 0
        for i, k in enumerate(keys_values(kvh)[0]):
            s = _dot_nt(q, k)
            if windowed and i == 0:
                s = s + bias
            buf[:, off:off + s.shape[1]] = s
            folds.append(_lane_fold(jnp.maximum, s))
            off += s.shape[1]
        m = jnp.maximum(jnp.max(_tree(jnp.maximum, folds), axis=-1, keepdims=True), sink)
        return m, sink

    def consume(u, buf, m, sink):
        vs = keys_values(units[u][0])[1]
        e, off = [], 0
        for v in vs:
            e.append(jnp.exp2(buf[:, off:off + v.shape[0]] - m))
            off += v.shape[0]
        l = jnp.sum(_tree(jnp.add, [_lane_fold(jnp.add, x) for x in e]), axis=-1, keepdims=True) + jnp.exp2(sink - m)
        return _tree(jnp.add, [_dot(x.astype(BF16), v) for x, v in zip(e, vs)]) / l

    acc = jnp.zeros((tq, d), F32)
    ahead, pending = len(bufs) - 1, {}
    for u in range(min(ahead, len(units))):
        pending[u] = scores(u, bufs[u % len(bufs)])
    prev = None
    for u, (kvh, g) in enumerate(units):
        if u + ahead < len(units):
            pending[u + ahead] = scores(u + ahead, bufs[(u + ahead) % len(bufs)])
        o = consume(u, bufs[u % len(bufs)], *pending.pop(u))
        if g % 2 == 0:
            prev = o
            continue
        if kvh % 2 == 0:
            slab = prev + pltpu.roll(o, SWA_HEAD_DIM, 1)
        else:
            slab = pltpu.roll(prev, SWA_HEAD_DIM, 1) + o
        acc = acc + _dot(slab.astype(BF16), w_ref[kvh * 2 + g // 2])
    o_ref[...] = x_ref[...] + mod_ref[0][2:3] * acc


def _attn_swa(x, mods, row_fn, q, k, v, ctx, sink, w_out, tq, n_bufs):
    b, s, d = x.shape
    windowed = ctx is not None
    band_w = min(tq + 2 * WINDOW, s)
    kv_spec = pl.BlockSpec((None, SWA_KV_HEADS, s, LANES), lambda bi, t: (bi, 0, 0, 0))
    in_specs = [
        pl.BlockSpec((None, tq, d), lambda bi, t: (bi, t, 0)),
        pl.BlockSpec((1, 6, d), lambda bi, t: (row_fn(bi), 0, 0)),
        pl.BlockSpec((None, 8, tq, LANES), lambda bi, t: (bi, 0, t, 0)),
        kv_spec, kv_spec,
    ]
    args = [x, mods, q, k, v]
    if windowed:
        ck, cv, j = ctx
        p = ck.shape[2]
        spec = pl.BlockSpec((None, None, p, 256), lambda bi, t: (bi, j, 0, 0))
        in_specs += [spec, spec]
        args += [ck, cv]
    in_specs += [pl.BlockSpec(memory_space=pltpu.SMEM), _full((8, LANES, d))]
    args += [sink, w_out]
    n_keys = band_w + ctx[0].shape[2] if windowed else s
    return pl.pallas_call(
        functools.partial(_attn_swa_kernel, windowed, band_w),
        grid=(b, s // tq),
        in_specs=in_specs,
        out_specs=pl.BlockSpec((None, tq, d), lambda bi, t: (bi, t, 0)),
        out_shape=jax.ShapeDtypeStruct((b, s, d), F32),
        scratch_shapes=[pltpu.VMEM((tq, n_keys), F32) for _ in range(n_bufs)],
        compiler_params=_params(2),
        name="attn_swa",
    )(*args)


def _route(scores_t, bias):
    sel_t = scores_t + bias
    sel = [sel_t[e:e + 1] for e in range(N_EXPERTS)]
    raw = [scores_t[e:e + 1] for e in range(N_EXPERTS)]
    gscore = []
    for g in range(N_GROUPS):
        r = sel[g * 4:g * 4 + 4]
        pairs = [r[i] + r[j] for i in range(4) for j in range(i + 1, 4)]
        gscore.append(functools.reduce(jnp.maximum, pairs))
    best, gidx = gscore[0], jnp.zeros_like(gscore[0], dtype=jnp.int32)
    for g in range(1, N_GROUPS):
        take = gscore[g] > best
        best = jnp.where(take, gscore[g], best)
        gidx = jnp.where(take, g, gidx)
    vals = []
    for k in range(EXPERTS_PER_GROUP):
        v = sel[k]
        for g in range(1, N_GROUPS):
            v = jnp.where(gidx == g, sel[g * 4 + k], v)
        vals.append(v)

    def argmax4(vs):
        m, idx = vs[0], jnp.zeros_like(gidx)
        for k in range(1, 4):
            take = vs[k] > m
            m = jnp.where(take, vs[k], m)
            idx = jnp.where(take, k, idx)
        return idx

    i1 = argmax4(vals)
    i2 = argmax4([jnp.where(i1 == k, -jnp.inf, vals[k]) for k in range(4)])
    e1, e2 = gidx * 4 + i1, gidx * 4 + i2
    w1 = functools.reduce(lambda a, b: a + b, [jnp.where(e1 == e, raw[e], 0.0) for e in range(N_EXPERTS)])
    w2 = functools.reduce(lambda a, b: a + b, [jnp.where(e2 == e, raw[e], 0.0) for e in range(N_EXPERTS)])
    den = w1 + w2
    return e1, e2, w1 / den, w2 / den


SC_WINDOW = 128
SC_ROW = 128
N_PLANES = (D_MODEL // 2) // SC_ROW
MOE_ROW_BLOCK = 512
PROMPT_BUFS = 8
HALF_WORD = -65536


def _to_planes(packed, out_ref):
    for j in range(N_PLANES):
        out_ref[j] = packed[:, j * SC_ROW:(j + 1) * SC_ROW]


def _from_planes(ref):
    return jnp.concatenate([ref[j] for j in range(N_PLANES)], axis=1)


def _pack_bf16_pairs(x):
    n = x.shape[1] // 2
    lo = pltpu.bitcast(x[:, :n].astype(BF16).astype(F32), jnp.int32)
    hi = pltpu.bitcast(x[:, n:].astype(BF16).astype(F32), jnp.int32)
    return lax.shift_right_logical(lo, jnp.int32(16)) | (hi & jnp.int32(HALF_WORD))


def _unpack_bf16_pairs(p):
    lo = pltpu.bitcast(lax.shift_left(p, jnp.int32(16)), F32)
    hi = pltpu.bitcast(p & jnp.int32(HALF_WORD), F32)
    return jnp.concatenate([lo, hi], axis=1)


def _pick(idx, rows):
    return _tree(jnp.add, [jnp.where(idx == e, rows[e], 0.0) for e in range(N_EXPERTS)])


def _moe_route_kernel(x_ref, mod_ref, g_ref, rw_ref, rb_ref, tri_ref, hp_o, eidx_o, rank_o, gcol_o, cnt_o, seen):
    @pl.when(jnp.logical_and(pl.program_id(0) == 0, pl.program_id(1) == 0))
    def _():
        seen[...] = jnp.zeros_like(seen)

    tm = x_ref.shape[0]
    mod = mod_ref[0]
    h = _modulate(x_ref[...], g_ref[...], mod[3:4], mod[4:5])
    h16 = h.astype(BF16)
    _to_planes(_pack_bf16_pairs(h), hp_o)
    h_lo = (h - h16.astype(F32)).astype(BF16)
    part = _dot_nt(rw_ref[...], h16)
    logits_t = part[:N_EXPERTS] + part[N_EXPERTS:] + _dot_nt(rw_ref[:N_EXPERTS], h_lo)
    e1, e2, g1, g2 = _route(_sigmoid(logits_t), rb_ref[...])
    hot = jnp.concatenate([jnp.where(jnp.logical_or(e1 == e, e2 == e), 1.0, 0.0) for e in range(N_EXPERTS)], axis=0)
    before = _dot(hot.astype(BF16), tri_ref[...]) + seen[...]
    rows = [before[e:e + 1] for e in range(N_EXPERTS)]
    eidx_o[0], eidx_o[1] = e1, e2
    rank_o[0], rank_o[1] = _pick(e1, rows).astype(jnp.int32), _pick(e2, rows).astype(jnp.int32)
    gcol_o[...] = jnp.concatenate([g1, g2, jnp.zeros((LANES - 2, tm), F32)], axis=0).T
    seen[...] = seen[...] + jnp.sum(hot, axis=-1, keepdims=True)
    cnt_o[...] = seen[...]


def _moe_slots_kernel(mb, n_rows, eidx_ref, rank_ref, cnt_ref, idx_o, be_o):
    cnt = cnt_ref[...]
    padded = jnp.ceil(cnt / mb) * mb
    starts, ends, run = [], [], jnp.zeros((1, 1), F32)
    for e in range(N_EXPERTS):
        starts.append(run)
        run = run + padded[e:e + 1]
        ends.append(run)
    n_tiles, per_tile = eidx_ref.shape[1], eidx_ref.shape[3] // SC_WINDOW
    for k in range(2):
        for i in range(n_tiles):
            slot = rank_ref[k, i] + _pick(eidx_ref[k, i], starts).astype(jnp.int32)
            for j in range(N_PLANES):
                for q in range(per_tile):
                    row = ((k * N_PLANES + j) * n_tiles + i) * per_tile + q
                    idx_o[row:row + 1, :] = slot[:, q * SC_WINDOW:(q + 1) * SC_WINDOW] + j * n_rows
    first_row = lax.broadcasted_iota(jnp.int32, be_o.shape, 1).astype(F32) * mb
    be = _tree(jnp.add, [jnp.where(ends[e] <= first_row, 1.0, 0.0) for e in range(N_EXPERTS)])
    lane = lax.broadcasted_iota(jnp.int32, be_o.shape, 1)
    be = jnp.where(lane == be_o.shape[1] - 1, ends[N_EXPERTS - 1] / mb, jnp.minimum(be, N_EXPERTS - 1.0))
    be_o[...] = be.astype(jnp.int32)


def _moe_ffn_kernel(be_ref, hs_ref, wg_ref, wu_ref, wd_ref, ys_o, wgu16, wd16):
    i = pl.program_id(0)
    live = i < be_ref[be_ref.shape[0] - 1]

    @pl.when(jnp.logical_and(live, jnp.logical_or(i == 0, be_ref[i] != be_ref[jnp.maximum(i - 1, 0)])))
    def _():
        wgu16[:, :D_EXPERT] = wg_ref[...].astype(BF16)
        wgu16[:, D_EXPERT:] = wu_ref[...].astype(BF16)
        wd16[...] = wd_ref[...].astype(BF16)

    @pl.when(live)
    def _():
        h = _unpack_bf16_pairs(_from_planes(hs_ref)).astype(BF16)
        gu = _dot(h, wgu16[...])
        g, u = gu[:, :D_EXPERT], gu[:, D_EXPERT:]
        a = g * _sigmoid(g) * u
        _to_planes(_pack_bf16_pairs(_dot(a.astype(BF16), wd16[...])), ys_o)


def _moe_residual(x, z0_ref, z1_ref, gcol_ref, mod_ref):
    gc = gcol_ref[...]
    lane = lax.broadcasted_iota(jnp.int32, gc.shape, 1)
    g1 = jnp.sum(jnp.where(lane == 0, gc, 0.0), axis=-1, keepdims=True)
    g2 = jnp.sum(jnp.where(lane == 1, gc, 0.0), axis=-1, keepdims=True)
    y = g1 * _unpack_bf16_pairs(_from_planes(z0_ref)) + g2 * _unpack_bf16_pairs(_from_planes(z1_ref))
    return x + mod_ref[0][5:6] * y


def _moe_combine_kernel(x_ref, z0_ref, z1_ref, gcol_ref, mod_ref, fg_ref, o_ref):
    o_ref[...] = _rms(_moe_residual(x_ref[...], z0_ref, z1_ref, gcol_ref, mod_ref), fg_ref[...])


def _residual_specs(resid, row_fn, tm, d, in_specs, args):
    (z, gcol), mods_prev = resid
    for k in range(2):
        in_specs.append(pl.BlockSpec((None, N_PLANES, None, tm, SC_ROW), lambda bi, t, k=k: (k, 0, bi, t, 0)))
        args.append(z)
    in_specs += [pl.BlockSpec((None, tm, LANES), lambda bi, t: (bi, t, 0)),
                 pl.BlockSpec((1, 6, d), lambda bi, t: (row_fn(bi), 0, 0))]
    args += [gcol, mods_prev]


def _sc_mesh():
    return plsc.VectorSubcoreMesh(core_axis_name="core", subcore_axis_name="subcore")


def _sc_scatter_rows(rows, idx, n_out):
    n, w = rows.shape
    steps = n // SC_WINDOW

    @pl.kernel(out_type=jax.ShapeDtypeStruct((n_out, w), rows.dtype), mesh=_sc_mesh(), scratch_types=[])
    def scatter(x_hbm, i0_hbm, i1_hbm, o_hbm):
        def body(x_vmem, i0_vmem, i1_vmem):
            pltpu.sync_copy(x_vmem, o_hbm.at[i0_vmem.at[0]])
            pltpu.sync_copy(x_vmem, o_hbm.at[i1_vmem.at[0]])

        pltpu.emit_pipeline(
            body,
            grid=(steps,),
            in_specs=[pl.BlockSpec((SC_WINDOW, w), lambda i: (i, 0)),
                      pl.BlockSpec((1, SC_WINDOW), lambda i: (i, 0)),
                      pl.BlockSpec((1, SC_WINDOW), lambda i: (i + steps, 0))],
            out_specs=[],
            core_axis_name=("core", "subcore"),
            dimension_semantics=(pltpu.PARALLEL,),
        )(x_hbm, i0_hbm, i1_hbm)

    return scatter(rows, idx, idx)


def _sc_gather_rows(rows, idx):
    steps, w = idx.shape[0], rows.shape[1]

    @pl.kernel(out_type=jax.ShapeDtypeStruct((steps * SC_WINDOW, w), rows.dtype), mesh=_sc_mesh(),
               scratch_types=[])
    def gather(x_hbm, i_hbm, o_hbm):
        def body(i_vmem, o_vmem):
            pltpu.sync_copy(x_hbm.at[i_vmem.at[0]], o_vmem)

        pltpu.emit_pipeline(
            body,
            grid=(steps,),
            in_specs=[pl.BlockSpec((1, SC_WINDOW), lambda i: (i, 0))],
            out_specs=[pl.BlockSpec((SC_WINDOW, w), lambda i: (i, 0))],
            core_axis_name=("core", "subcore"),
            dimension_semantics=(pltpu.PARALLEL,),
        )(i_hbm, o_hbm)

    return gather(rows, idx)


def _moe_bucket(x, mods, row_fn, g, rw_t, rb, tm, mb):
    b, s, d = x.shape
    n, nt = b * s, (b * s) // tm
    per_b = s // tm
    tile = lambda bi, t: (0, bi * per_b + t, 0, 0)
    tri = jnp.asarray(np.triu(np.ones((tm, tm), np.float32), 1), BF16)
    hp, eidx, rank, gcol, cnt = pl.pallas_call(
        _moe_route_kernel,
        grid=(b, per_b),
        in_specs=[
            pl.BlockSpec((None, tm, d), lambda bi, t: (bi, t, 0)),
            pl.BlockSpec((1, 6, d), lambda bi, t: (row_fn(bi), 0, 0)),
            _full((1, d)), _full((2 * N_EXPERTS, d)), _full((N_EXPERTS, 1)), _full((tm, tm)),
        ],
        out_specs=[
            pl.BlockSpec((N_PLANES, tm, SC_ROW), lambda bi, t: (0, bi * per_b + t, 0)),
            pl.BlockSpec((2, None, 1, tm), tile), pl.BlockSpec((2, None, 1, tm), tile),
            pl.BlockSpec((None, tm, LANES), lambda bi, t: (bi, t, 0)),
            _full((N_EXPERTS, 1)),
        ],
        out_shape=[
            jax.ShapeDtypeStruct((N_PLANES, n, SC_ROW), jnp.int32),
            jax.ShapeDtypeStruct((2, nt, 1, tm), jnp.int32), jax.ShapeDtypeStruct((2, nt, 1, tm), jnp.int32),
            jax.ShapeDtypeStruct((b, s, LANES), F32),
            jax.ShapeDtypeStruct((N_EXPERTS, 1), F32),
        ],
        scratch_shapes=[pltpu.VMEM((N_EXPERTS, 1), F32)],
        compiler_params=_params(2),
        name="moe_route",
    )(x, mods, g, rw_t, rb, tri)

    n_rows = 2 * n + N_EXPERTS * mb
    nb = n_rows // mb
    nb_pad = -(-(nb + 1) // LANES) * LANES
    idx, be = pl.pallas_call(
        functools.partial(_moe_slots_kernel, mb, n_rows),
        out_shape=[jax.ShapeDtypeStruct((2 * N_PLANES * n // SC_WINDOW, SC_WINDOW), jnp.int32),
                   jax.ShapeDtypeStruct((1, nb_pad), jnp.int32)],
        name="moe_slots",
    )(eidx, rank, cnt)

    hs = _sc_scatter_rows(hp.reshape(N_PLANES * n, SC_ROW), idx, N_PLANES * n_rows)
    return hs.reshape(N_PLANES, n_rows, SC_ROW), idx, be.reshape(nb_pad), gcol


def _moe_experts(bucket, layer, w_gate, w_up, w_down, mb, out_dims):
    hs, idx, be, gcol = bucket
    n_rows, d = hs.shape[1], w_gate.shape[2]
    last = be.shape[0] - 1
    blk = lambda i, be_r: jnp.minimum(i, be_r[last] - 1)
    plane_block = pl.BlockSpec((N_PLANES, mb, SC_ROW), lambda i, be_r: (0, blk(i, be_r), 0))
    weight = lambda i, be_r: (layer, be_r[blk(i, be_r)], 0, 0)
    ys = pl.pallas_call(
        _moe_ffn_kernel,
        grid_spec=pltpu.PrefetchScalarGridSpec(
            num_scalar_prefetch=1,
            grid=(n_rows // mb,),
            in_specs=[
                plane_block,
                pl.BlockSpec((None, None, d, D_EXPERT), weight),
                pl.BlockSpec((None, None, d, D_EXPERT), weight),
                pl.BlockSpec((None, None, D_EXPERT, d), weight),
            ],
            out_specs=plane_block,
            scratch_shapes=[pltpu.VMEM((d, 2 * D_EXPERT), BF16), pltpu.VMEM((D_EXPERT, d), BF16)],
        ),
        out_shape=jax.ShapeDtypeStruct((N_PLANES, n_rows, SC_ROW), jnp.int32),
        compiler_params=_params(1),
        name="moe_ffn",
    )(be, hs, w_gate, w_up, w_down)
    z = _sc_gather_rows(ys.reshape(N_PLANES * n_rows, SC_ROW), idx)
    return z.reshape(2, N_PLANES, *out_dims, SC_ROW), gcol.reshape(*out_dims, LANES)


def _moe_combine(x, mods, row_fn, gathered, final_g, tm):
    b, s, d = x.shape
    in_specs = [pl.BlockSpec((None, tm, d), lambda bi, t: (bi, t, 0))]
    args = [x]
    _residual_specs((gathered, mods), row_fn, tm, d, in_specs, args)
    in_specs.append(_full((1, d)))
    args.append(final_g)
    return pl.pallas_call(
        _moe_combine_kernel,
        grid=(b, s // tm),
        in_specs=in_specs,
        out_specs=pl.BlockSpec((None, tm, d), lambda bi, t: (bi, t, 0)),
        out_shape=jax.ShapeDtypeStruct((b, s, d), F32),
        compiler_params=_params(2),
        name="moe_combine",
    )(*args)


def _rope_tables(n_tokens, rot_dim, offset):
    n_rows = n_tokens // GRID_W
    rows = np.repeat(np.arange(n_rows), GRID_W)
    cols = np.tile(np.arange(GRID_W), n_rows)
    n_freq = rot_dim // 4
    inv = jnp.asarray(ROPE_THETA, F32) ** (-jnp.arange(n_freq, dtype=F32) / n_freq)
    ang = jnp.concatenate([jnp.asarray(rows, F32)[:, None] * inv, jnp.asarray(cols, F32)[:, None] * inv], axis=-1)
    cos, sin = jnp.repeat(jnp.cos(ang), 2, axis=-1), jnp.repeat(jnp.sin(ang), 2, axis=-1)
    odd = jnp.asarray(np.arange(rot_dim) % 2 == 1)
    parts = [cos, jnp.where(odd, sin, 0.0), jnp.where(odd, 0.0, -sin)]
    period = 64 if rot_dim == 64 else LANES
    fill = [1.0, 0.0, 0.0]
    out = []
    for p, f in zip(parts, fill):
        slot = jnp.full((n_tokens, period), f, F32).at[:, offset:offset + rot_dim].set(p)
        out.append(jnp.tile(slot, (1, LANES // period)))
    return jnp.stack(out)


def _pad_heads(w, n_heads, lo, hi):
    k = w.shape[0]
    w = w.reshape(k, n_heads, -1)[:, :, lo:hi]
    return jnp.pad(w, ((0, 0), (0, 0), (0, LANES - (hi - lo)))).reshape(k, n_heads * LANES)


def _ab_weights(j, ab_w_in, diff_lambda, diff_subln_g, mla_q_norm_g, mla_w_qb, mla_kv_norm_g, mla_w_kvb, ab_w_out):
    place = np.zeros((LANES, MLA_HEADS * LANES), np.float32)
    for hd in range(MLA_HEADS):
        for r in range(MLA_ROPE_DIM):
            place[r, hd * LANES + MLA_NOPE_DIM + r] = 1.0
    w_out_m = ab_w_out[j][512:].reshape(MLA_HEADS, MLA_V_DIM, D_MODEL)
    return {
        "w_in": jnp.pad(ab_w_in[j], ((0, 0), (0, AB_IN_PAD - AB_IN))).astype(BF16),
        "qn_g": mla_q_norm_g[j][None, :],
        "w_qb": _pad_heads(mla_w_qb[j], MLA_HEADS, 0, MLA_NOPE_DIM + MLA_ROPE_DIM).astype(BF16),
        "kvn_g": mla_kv_norm_g[j][None, :],
        "wk": _pad_heads(mla_w_kvb[j], MLA_HEADS, 0, MLA_NOPE_DIM).astype(BF16),
        "wv": _pad_heads(mla_w_kvb[j], MLA_HEADS, MLA_NOPE_DIM, MLA_NOPE_DIM + MLA_V_DIM).astype(BF16),
        "e": jnp.asarray(place, BF16),
        "w_out_d": ab_w_out[j][:512].reshape(DIFF_HEADS, LANES, D_MODEL).astype(BF16),
        "w_out_m": jnp.pad(w_out_m, ((0, 0), (0, LANES - MLA_V_DIM), (0, 0))).astype(BF16),
        "subln_g": diff_subln_g[j][None, :],
        "lam": diff_lambda[j],
    }


def kernel(x_prompt, x_sample, cache_diff_k, cache_diff_v, cache_mla_ckv, cache_mla_krope, cache_swa_k, cache_swa_v,
           c, c_ctx, ada_w, ada_b, norm1_g, norm2_g, final_norm_g, ab_w_in, diff_lambda, diff_subln_g, mla_q_norm_g,
           mla_w_qb, mla_kv_norm_g, mla_w_kvb, ab_w_out, swa_w_in, swa_sink, swa_w_out, router_w, router_bias,
           moe_w_gate, moe_w_up, moe_w_down):
    bp, sp, d = x_prompt.shape
    bs, ss, _ = x_sample.shape
    depth = ada_w.shape[0]
    past = cache_diff_k.shape[2]
    n_ab = cache_diff_k.shape[1]
    n_swa = cache_swa_k.shape[1]
    assert bs + 1 <= COND_ROWS and d == D_MODEL

    cond = jnp.zeros((COND_ROWS, d), F32).at[:bs].set(c).at[bs].set(c_ctx)
    mods = _ada_all(cond, ada_w, ada_b).reshape(depth, COND_ROWS, 6, d)
    row_s = lambda bi: bi
    row_p = lambda bi: bi * 0 + bs

    rope_d = _rope_tables(ss, DIFF_QK_DIM, 0)
    rope_q = _rope_tables(ss, MLA_ROPE_DIM, MLA_NOPE_DIM)
    rope_k = _rope_tables(ss, MLA_ROPE_DIM, 0)

    cdk = cache_diff_k.reshape(bs, n_ab, past, 512)
    cdv = cache_diff_v.reshape(bs, n_ab, past, 512)
    ckr = jnp.pad(cache_mla_krope, ((0, 0), (0, 0), (0, 0), (0, LANES - MLA_ROPE_DIM)))
    csk = cache_swa_k.reshape(bs, n_swa, past, 256)
    csv = cache_swa_v.reshape(bs, n_swa, past, 256)
    rw_hi = router_w.T.astype(BF16)
    rw_t = jnp.concatenate([rw_hi, (router_w.T - rw_hi.astype(F32)).astype(BF16)], axis=0)
    rb = router_bias[:, None]

    tp = min(256, sp)
    ts = min(256, ss)
    tm_s = min(512, ss)
    tm_p = min(512, bp * sp)

    xp, xs = x_prompt, x_sample
    new_ab, new_swa = None, None
    pend_p = pend_s = None

    def project(fn, x, pend, *a):
        outs = list(fn(x, *a, pend))
        return (outs.pop() if pend is not None else x), outs

    for l in range(depth):
        j = l // 2
        ml = mods[l]
        g1, g2 = norm1_g[l][None, :], norm2_g[l][None, :]
        if l % 2 == 0:
            lambda_init = 0.8 - 0.6 * math.exp(-0.3 * l)
            wts = _ab_weights(j, ab_w_in, diff_lambda, diff_subln_g, mla_q_norm_g, mla_w_qb, mla_kv_norm_g,
                              mla_w_kvb, ab_w_out)
            xp, (qa, ka, va, qb, km, vm, *new_ab) = project(_proj_ab, xp, pend_p, ml, row_p, g1, wts, None,
                                                             (j, n_ab, new_ab), tp)
            xp = _attn_ab(xp, ml, row_p, (qa, qb), [(ka, va, km, vm)], wts, lambda_init, tp, PROMPT_BUFS)
            xs, (qa, ka, va, qb, km, vm) = project(_proj_ab, xs, pend_s, ml, row_s, g1, wts,
                                                   (rope_d, rope_q, rope_k), None, ts)
            cache_seg = _cache_ab(cdk, cdv, cache_mla_ckv, ckr, j, wts)
            xs = _attn_ab(xs, ml, row_s, (qa, qb), [(ka, va, km, vm), tuple(cache_seg)], wts, lambda_init, ts, 2)
        else:
            w_in = swa_w_in[j].astype(BF16)
            w_out = swa_w_out[j].reshape(8, LANES, d).astype(BF16)
            xp, (q, k, v, *new_swa) = project(_proj_swa, xp, pend_p, ml, row_p, g1, w_in, None,
                                              (j, n_swa, new_swa), tp)
            xp = _attn_swa(xp, ml, row_p, q, k, v, None, swa_sink[j], w_out, tp, PROMPT_BUFS)
            xs, (q, k, v) = project(_proj_swa, xs, pend_s, ml, row_s, g1, w_in, rope_d, None, ts)
            xs = _attn_swa(xs, ml, row_s, q, k, v, (csk, csv, j), swa_sink[j], w_out, ts, 2)
        bucket_p = _moe_bucket(xp.reshape(1, bp * sp, d), ml, row_p, g2, rw_t, rb, tm_p, MOE_ROW_BLOCK)
        bucket_s = _moe_bucket(xs, ml, row_s, g2, rw_t, rb, tm_s, MOE_ROW_BLOCK)
        got_p = _moe_experts(bucket_p, l, moe_w_gate, moe_w_up, moe_w_down, MOE_ROW_BLOCK, (bp, sp))
        got_s = _moe_experts(bucket_s, l, moe_w_gate, moe_w_up, moe_w_down, MOE_ROW_BLOCK, (bs, ss))
        pend_p, pend_s = (got_p, ml), (got_s, ml)

    xp = _moe_combine(xp, pend_p[1], row_p, pend_p[0], final_norm_g[None, :], tp)
    xs = _moe_combine(xs, pend_s[1], row_s, pend_s[0], final_norm_g[None, :], ts)

    new_diff_k = new_ab[0].reshape(bp, n_ab, sp, DIFF_HEADS, 2, DIFF_QK_DIM)
    new_diff_v = new_ab[1].reshape(bp, n_ab, sp, DIFF_HEADS, DIFF_V_DIM)
    new_mla_ckv, new_mla_krope = new_ab[2], new_ab[3]
    new_swa_k = new_swa[0].reshape(bp, n_swa, sp, SWA_KV_HEADS, SWA_HEAD_DIM)
    new_swa_v = new_swa[1].reshape(bp, n_swa, sp, SWA_KV_HEADS, SWA_HEAD_DIM)
    return (xp, xs, new_diff_k, new_diff_v, new_mla_ckv, new_mla_krope, new_swa_k, new_swa_v)
```

```python
import functools
import math

import jax
import jax.numpy as jnp
import numpy as np
from jax import lax
from jax.experimental import pallas as pl
from jax.experimental.pallas import tpu as pltpu
from jax.experimental.pallas import tpu_sc as plsc

F32 = jnp.float32
BF16 = jnp.bfloat16

D_MODEL = 1024
GRID_W = 64
ROPE_THETA = 10000.0
NORM_EPS = 1e-6
NEG_INF = -1e30
LOG2E = math.log2(math.e)
LANES = 128

DIFF_HEADS = 4
DIFF_QK_DIM = 64
DIFF_V_DIM = 128
MLA_HEADS = 8
MLA_Q_RANK = 384
MLA_KV_RANK = 256
MLA_NOPE_DIM = 64
MLA_ROPE_DIM = 32
MLA_V_DIM = 64
AB_IN = 3 * 512 + MLA_Q_RANK + MLA_KV_RANK + MLA_ROPE_DIM
AB_IN_PAD = 2304
SWA_HEADS = 16
SWA_KV_HEADS = 4
SWA_GROUP = 4
SWA_HEAD_DIM = 64
WINDOW = 128
N_EXPERTS = 16
N_GROUPS = 4
EXPERTS_PER_GROUP = 4
D_EXPERT = 256
COND_ROWS = 16

VMEM_LIMIT = 56 * 1024 * 1024


def _full(shape):
    n = len(shape)
    return pl.BlockSpec(shape, lambda *_: (0,) * n)


def _params(n_axes):
    return pltpu.CompilerParams(dimension_semantics=("arbitrary",) * n_axes, vmem_limit_bytes=VMEM_LIMIT)


def _sigmoid(x):
    return 1.0 / (1.0 + jnp.exp(-x))


def _rms(x, g):
    return x * lax.rsqrt(jnp.mean(x * x, axis=-1, keepdims=True) + NORM_EPS) * g


def _modulate(x, g, shift, scale):
    return _rms(x, g) * (1.0 + scale) + shift


def _dot(a, b):
    return jnp.dot(a, b, preferred_element_type=F32)


def _dot_nt(a, b):
    return lax.dot_general(a, b, (((1,), (1,)), ((), ())), preferred_element_type=F32)


def _rope(x, tab_ref):
    c, s_odd, s_even = tab_ref[0], tab_ref[1], tab_ref[2]
    out = []
    for i in range(x.shape[1] // LANES):
        xi = x[:, i * LANES:(i + 1) * LANES]
        out.append(xi * c + pltpu.roll(xi, 1, 1) * s_odd + pltpu.roll(xi, LANES - 1, 1) * s_even)
    return out[0] if len(out) == 1 else jnp.concatenate(out, axis=1)


def _tree(op, xs):
    xs = list(xs)
    while len(xs) > 1:
        xs = [op(xs[i], xs[i + 1]) if i + 1 < len(xs) else xs[i] for i in range(0, len(xs), 2)]
    return xs[0]


def _lane_fold(op, x):
    return _tree(op, [x[:, i * LANES:(i + 1) * LANES] for i in range(x.shape[1] // LANES)])


def _ada_kernel(cond_ref, w_ref, b_ref, o_ref):
    c = cond_ref[...]
    a = (c * _sigmoid(c)).astype(BF16)
    o_ref[...] = _dot(a, w_ref[...].astype(BF16)) + b_ref[...]


def _ada_all(cond, ada_w, ada_b):
    depth, d, n = ada_w.shape
    tn = 1536
    return pl.pallas_call(
        _ada_kernel,
        grid=(depth, n // tn),
        in_specs=[
            _full((COND_ROWS, d)),
            pl.BlockSpec((None, d, tn), lambda l, j: (l, 0, j)),
            pl.BlockSpec((None, 1, tn), lambda l, j: (l, 0, j)),
        ],
        out_specs=pl.BlockSpec((None, COND_ROWS, tn), lambda l, j: (l, 0, j)),
        out_shape=jax.ShapeDtypeStruct((depth, COND_ROWS, n), F32),
        compiler_params=_params(2),
        name="ada_mod",
    )(cond, ada_w, ada_b.reshape(depth, 1, n))


def _proj_ab_kernel(rope, emit_cache, n_prev, fused, x_ref, mod_ref, g_ref, w_in_ref, qn_ref, wqb_ref, kvn_ref,
                    wk_ref, wv_ref, e_ref, *rest):
    if rope:
        rope_d, rope_q, rope_k = rest[:3]
        rest = rest[3:]
    x = x_ref[...]
    if fused:
        x = _moe_residual(x, *rest[:4])
        rest[-1][...] = x
        rest = rest[4:]
    rest = rest[n_prev:]
    qa_o, ka_o, va_o, qb_o, km_o, vm_o = rest[:6]
    mod = mod_ref[0]
    h = _modulate(x, g_ref[...], mod[0:1], mod[1:2]).astype(BF16)
    big = _dot(h, w_in_ref[...])
    qa, ka, va = big[:, 0:512], big[:, 512:1024], big[:, 1024:1536]
    q_lat, ckv, kr = big[:, 1536:1920], big[:, 1920:2176], big[:, 2176:2304]
    qb = _dot(_rms(q_lat, qn_ref[...]).astype(BF16), wqb_ref[...])
    ckv_n = _rms(ckv, kvn_ref[...])
    if emit_cache:
        ka32_o, va32_o, ckv32_o, kr32_o = rest[6:10]
        ka32_o[...] = ka
        va32_o[...] = va
        ckv32_o[...] = ckv_n
        kr32_o[...] = kr[:, :MLA_ROPE_DIM]
    if rope:
        qa, ka = _rope(qa, rope_d), _rope(ka, rope_d)
        qb, kr = _rope(qb, rope_q), _rope(kr, rope_k)
    qa = qa * (DIFF_QK_DIM ** -0.5 * LOG2E)
    qb = qb * ((MLA_NOPE_DIM + MLA_ROPE_DIM) ** -0.5 * LOG2E)
    ckv16 = ckv_n.astype(BF16)
    km = _dot(ckv16, wk_ref[...]) + _dot(kr.astype(BF16), e_ref[...])
    vm = _dot(ckv16, wv_ref[...])
    lo = lax.broadcasted_iota(jnp.int32, (qa.shape[0], LANES), 1) < DIFF_QK_DIM
    for hd in range(DIFF_HEADS):
        sl = slice(hd * LANES, (hd + 1) * LANES)
        qa_o[2 * hd] = jnp.where(lo, qa[:, sl], 0.0).astype(BF16)
        qa_o[2 * hd + 1] = jnp.where(lo, 0.0, qa[:, sl]).astype(BF16)
        ka_o[hd] = ka[:, sl].astype(BF16)
        va_o[hd] = va[:, sl].astype(BF16)
    for hd in range(MLA_HEADS):
        sl = slice(hd * LANES, (hd + 1) * LANES)
        qb_o[hd] = qb[:, sl].astype(BF16)
        km_o[hd] = km[:, sl].astype(BF16)
        vm_o[hd] = vm[:, sl].astype(BF16)


def _cache_outputs(cache, widths, b, s, tm, in_specs, args, n_fixed_outs):
    if cache is None:
        return [], {}, 0
    j, n_layers, prev = cache
    outs = [(jax.ShapeDtypeStruct((b, n_layers, s, w), F32),
             pl.BlockSpec((None, None, tm, w), lambda bi, t: (bi, j, t, 0))) for w in widths]
    aliases = {}
    if prev is not None:
        for i, arr in enumerate(prev):
            aliases[len(args)] = n_fixed_outs + i
            in_specs.append(pl.BlockSpec(memory_space=pl.ANY))
            args.append(arr)
    return outs, aliases, len(aliases)


def _proj_ab(x, mods, row_fn, g, wts, rope_tabs, cache, tm, resid=None):
    b, s, d = x.shape
    rope = rope_tabs is not None
    tok = lambda bi, t: (bi, t, 0)
    head = lambda bi, t: (bi, 0, t, 0)
    in_specs = [
        pl.BlockSpec((None, tm, d), tok),
        pl.BlockSpec((1, 6, d), lambda bi, t: (row_fn(bi), 0, 0)),
        _full((1, d)),
        _full((d, AB_IN_PAD)), _full((1, MLA_Q_RANK)), _full((MLA_Q_RANK, 1024)), _full((1, MLA_KV_RANK)),
        _full((MLA_KV_RANK, 1024)), _full((MLA_KV_RANK, 1024)), _full((LANES, 1024)),
    ]
    args = [x, mods, g, wts["w_in"], wts["qn_g"], wts["w_qb"], wts["kvn_g"], wts["wk"], wts["wv"], wts["e"]]
    if rope:
        in_specs += [pl.BlockSpec((3, tm, LANES), lambda bi, t: (0, t, 0))] * 3
        args += list(rope_tabs)
    if resid is not None:
        _residual_specs(resid, row_fn, tm, d, in_specs, args)

    def hm(nh):
        return jax.ShapeDtypeStruct((b, nh, s, LANES), BF16), pl.BlockSpec((None, nh, tm, LANES), head)

    outs = [hm(8), hm(4), hm(4), hm(8), hm(8), hm(8)]
    cache_outs, aliases, n_prev = _cache_outputs(cache, (512, 512, MLA_KV_RANK, MLA_ROPE_DIM), b, s, tm, in_specs,
                                                 args, len(outs))
    outs += cache_outs
    if resid is not None:
        outs.append((jax.ShapeDtypeStruct((b, s, d), F32), pl.BlockSpec((None, tm, d), tok)))
    return pl.pallas_call(
        functools.partial(_proj_ab_kernel, rope, cache is not None, n_prev, resid is not None),
        grid=(b, s // tm),
        in_specs=in_specs,
        out_specs=[o[1] for o in outs],
        out_shape=[o[0] for o in outs],
        input_output_aliases=aliases,
        compiler_params=_params(2),
        name="proj_ab",
    )(*args)


def _cache_ab_kernel(dk_ref, dv_ref, ckv_ref, kr_ref, wk_ref, wv_ref, e_ref, ck_o, cv_o, km_o, vm_o):
    ckv16 = ckv_ref[...].astype(BF16)
    km = _dot(ckv16, wk_ref[...]) + _dot(kr_ref[...].astype(BF16), e_ref[...])
    vm = _dot(ckv16, wv_ref[...])
    for hd in range(DIFF_HEADS):
        sl = slice(hd * LANES, (hd + 1) * LANES)
        ck_o[hd] = dk_ref[:, sl].astype(BF16)
        cv_o[hd] = dv_ref[:, sl].astype(BF16)
    for hd in range(MLA_HEADS):
        sl = slice(hd * LANES, (hd + 1) * LANES)
        km_o[hd] = km[:, sl].astype(BF16)
        vm_o[hd] = vm[:, sl].astype(BF16)


def _cache_ab(cdk, cdv, cckv, ckr, j, wts):
    b, _, p, _ = cdk.shape
    lay = lambda bi: (bi, j, 0, 0)

    def hm(nh):
        return (jax.ShapeDtypeStruct((b, nh, p, LANES), BF16),
                pl.BlockSpec((None, nh, p, LANES), lambda bi: (bi, 0, 0, 0)))

    outs = [hm(4), hm(4), hm(8), hm(8)]
    return pl.pallas_call(
        _cache_ab_kernel,
        grid=(b,),
        in_specs=[
            pl.BlockSpec((None, None, p, 512), lay), pl.BlockSpec((None, None, p, 512), lay),
            pl.BlockSpec((None, None, p, MLA_KV_RANK), lay), pl.BlockSpec((None, None, p, LANES), lay),
            _full((MLA_KV_RANK, 1024)), _full((MLA_KV_RANK, 1024)), _full((LANES, 1024)),
        ],
        out_specs=[o[1] for o in outs],
        out_shape=[o[0] for o in outs],
        compiler_params=_params(1),
        name="cache_ab",
    )(cdk, cdv, cckv, ckr, wts["wk"], wts["wv"], wts["e"])


def _attn_ab_kernel(n_seg, lambda_init, x_ref, mod_ref, qa_ref, qb_ref, *rest):
    segs = [rest[4 * i:4 * i + 4] for i in range(n_seg)]
    wd_ref, wm_ref, sg_ref, lam_ref, o_ref = rest[4 * n_seg:4 * n_seg + 5]
    bufs = rest[4 * n_seg + 5:]
    lp = lam_ref[...]
    lam = (jnp.exp(jnp.sum(lp[0:1] * lp[1:2], axis=-1, keepdims=True))
           - jnp.exp(jnp.sum(lp[2:3] * lp[3:4], axis=-1, keepdims=True)) + lambda_init)
    tq, d = x_ref.shape
    widths = [sg[0].shape[1] for sg in segs]
    offs = [sum(widths[:i]) for i in range(n_seg)]
    units = [("diff", hd) for hd in range(DIFF_HEADS)] + [("mla", hd) for hd in range(MLA_HEADS)]

    def scores(u, buf):
        kind, hd = units[u]
        qs = [qa_ref[2 * hd], qa_ref[2 * hd + 1]] if kind == "diff" else [qb_ref[hd]]
        ks = [sg[0 if kind == "diff" else 2][hd] for sg in segs]
        ms = []
        for i, q in enumerate(qs):
            folds = []
            for k, off, w in zip(ks, offs, widths):
                s = _dot_nt(q, k)
                buf[i, :, off:off + w] = s
                folds.append(_lane_fold(jnp.maximum, s))
            ms.append(jnp.max(_tree(jnp.maximum, folds), axis=-1, keepdims=True))
        return ms

    def exps(buf, i, m):
        e = [jnp.exp2(buf[i, :, off:off + w] - m) for off, w in zip(offs, widths)]
        l = jnp.sum(_tree(jnp.add, [_lane_fold(jnp.add, x) for x in e]), axis=-1, keepdims=True)
        return e, l

    def consume(u, buf, ms, acc):
        kind, hd = units[u]
        if kind == "diff":
            vs = [sg[1][hd] for sg in segs]
            (e1, l1), (e2, l2) = exps(buf, 0, ms[0]), exps(buf, 1, ms[1])
            c1, c2 = 1.0 / l1, lam / l2
            o = _tree(jnp.add, [_dot((a * c1 - b * c2).astype(BF16), v) for a, b, v in zip(e1, e2, vs)])
            od = _rms(o, sg_ref[...]) * (1.0 - lambda_init)
            return acc + _dot(od.astype(BF16), wd_ref[hd])
        vs = [sg[3][hd] for sg in segs]
        e, l = exps(buf, 0, ms[0])
        o = _tree(jnp.add, [_dot(x.astype(BF16), v) for x, v in zip(e, vs)]) / l
        return acc + _dot(o.astype(BF16), wm_ref[hd])

    acc = jnp.zeros((tq, d), F32)
    ahead, pending = len(bufs) - 1, {}
    for u in range(min(ahead, len(units))):
        pending[u] = scores(u, bufs[u % len(bufs)])
    for u in range(len(units)):
        if u + ahead < len(units):
            pending[u + ahead] = scores(u + ahead, bufs[(u + ahead) % len(bufs)])
        acc = consume(u, bufs[u % len(bufs)], pending.pop(u), acc)
    o_ref[...] = x_ref[...] + mod_ref[0][2:3] * acc


def _attn_ab(x, mods, row_fn, q_parts, seg_list, wts, lambda_init, tq, n_bufs):
    b, s, d = x.shape
    qa, qb = q_parts
    in_specs = [
        pl.BlockSpec((None, tq, d), lambda bi, t: (bi, t, 0)),
        pl.BlockSpec((1, 6, d), lambda bi, t: (row_fn(bi), 0, 0)),
        pl.BlockSpec((None, 8, tq, LANES), lambda bi, t: (bi, 0, t, 0)),
        pl.BlockSpec((None, 8, tq, LANES), lambda bi, t: (bi, 0, t, 0)),
    ]
    args = [x, mods, qa, qb]
    kv_mode = dict(pipeline_mode=pl.Buffered(1)) if s // tq > 1 else {}
    for seg in seg_list:
        for arr in seg:
            nh, nk = arr.shape[1], arr.shape[2]
            in_specs.append(pl.BlockSpec((None, nh, nk, LANES), lambda bi, t: (bi, 0, 0, 0), **kv_mode))
            args.append(arr)
    in_specs += [_full((DIFF_HEADS, LANES, d)), _full((MLA_HEADS, LANES, d)), _full((1, LANES)),
                 _full((4, DIFF_QK_DIM))]
    args += [wts["w_out_d"], wts["w_out_m"], wts["subln_g"], wts["lam"]]
    n_keys = sum(seg[0].shape[2] for seg in seg_list)
    return pl.pallas_call(
        functools.partial(_attn_ab_kernel, len(seg_list), lambda_init),
        grid=(b, s // tq),
        in_specs=in_specs,
        out_specs=pl.BlockSpec((None, tq, d), lambda bi, t: (bi, t, 0)),
        out_shape=jax.ShapeDtypeStruct((b, s, d), F32),
        scratch_shapes=[pltpu.VMEM((2, tq, n_keys), F32) for _ in range(n_bufs)],
        compiler_params=_params(2),
        name="attn_ab",
    )(*args)


def _proj_swa_kernel(rope, emit_cache, n_prev, fused, x_ref, mod_ref, g_ref, w_in_ref, *rest):
    if rope:
        rope_d = rest[0]
        rest = rest[1:]
    x = x_ref[...]
    if fused:
        x = _moe_residual(x, *rest[:4])
        rest[-1][...] = x
        rest = rest[4:]
    rest = rest[n_prev:]
    q_o, k_o, v_o = rest[:3]
    mod = mod_ref[0]
    h = _modulate(x, g_ref[...], mod[0:1], mod[1:2]).astype(BF16)
    big = _dot(h, w_in_ref[...])
    q, k, v = big[:, :1024], big[:, 1024:1280], big[:, 1280:1536]
    if emit_cache:
        k32_o, v32_o = rest[3:5]
        k32_o[...] = k
        v32_o[...] = v
    if rope:
        q, k = _rope(q, rope_d), _rope(k, rope_d)
    q = q * (SWA_HEAD_DIM ** -0.5 * LOG2E)
    lo = lax.broadcasted_iota(jnp.int32, (q.shape[0], LANES), 1) < SWA_HEAD_DIM
    for pair in range(2):
        for grp in range(SWA_GROUP):
            ca, cb = (2 * pair) * 2 + grp // 2, (2 * pair + 1) * 2 + grp // 2
            a = q[:, ca * LANES:(ca + 1) * LANES]
            bb = q[:, cb * LANES:(cb + 1) * LANES]
            if grp % 2 == 0:
                bb = pltpu.roll(bb, SWA_HEAD_DIM, 1)
            else:
                a = pltpu.roll(a, SWA_HEAD_DIM, 1)
            q_o[pair * SWA_GROUP + grp] = jnp.where(lo, a, bb).astype(BF16)
    for kvh in range(SWA_KV_HEADS):
        sl = slice((kvh // 2) * LANES, (kvh // 2 + 1) * LANES)
        keep = lo if kvh % 2 == 0 else jnp.logical_not(lo)
        k_o[kvh] = jnp.where(keep, k[:, sl], 0.0).astype(BF16)
        v_o[kvh] = jnp.where(keep, v[:, sl], 0.0).astype(BF16)


def _proj_swa(x, mods, row_fn, g, w_in, rope_tab, cache, tm, resid=None):
    b, s, d = x.shape
    rope = rope_tab is not None
    tok = lambda bi, t: (bi, t, 0)
    head = lambda bi, t: (bi, 0, t, 0)
    in_specs = [
        pl.BlockSpec((None, tm, d), tok),
        pl.BlockSpec((1, 6, d), lambda bi, t: (row_fn(bi), 0, 0)),
        _full((1, d)), _full((d, 1536)),
    ]
    args = [x, mods, g, w_in]
    if rope:
        in_specs.append(pl.BlockSpec((3, tm, LANES), lambda bi, t: (0, t, 0)))
        args.append(rope_tab)
    if resid is not None:
        _residual_specs(resid, row_fn, tm, d, in_specs, args)

    def hm(nh):
        return jax.ShapeDtypeStruct((b, nh, s, LANES), BF16), pl.BlockSpec((None, nh, tm, LANES), head)

    outs = [hm(8), hm(4), hm(4)]
    cache_outs, aliases, n_prev = _cache_outputs(cache, (256, 256), b, s, tm, in_specs, args, len(outs))
    outs += cache_outs
    if resid is not None:
        outs.append((jax.ShapeDtypeStruct((b, s, d), F32), pl.BlockSpec((None, tm, d), tok)))
    return pl.pallas_call(
        functools.partial(_proj_swa_kernel, rope, cache is not None, n_prev, resid is not None),
        grid=(b, s // tm),
        in_specs=in_specs,
        out_specs=[o[1] for o in outs],
        out_shape=[o[0] for o in outs],
        input_output_aliases=aliases,
        compiler_params=_params(2),
        name="proj_swa",
    )(*args)


def _attn_swa_kernel(windowed, band_w, x_ref, mod_ref, q_ref, k_ref, v_ref, *rest):
    if windowed:
        ck_ref, cv_ref = rest[:2]
        rest = rest[2:]
    sink_ref, w_ref, o_ref = rest[:3]
    bufs = rest[3:]
    tq, d = x_ref.shape
    n_keys = k_ref.shape[1]
    lo_k = lax.broadcasted_iota(jnp.int32, (1, LANES), 1) < SWA_HEAD_DIM
    if windowed:
        start = pl.program_id(1) * tq
        bstart = pl.multiple_of(jnp.clip(start - WINDOW, 0, n_keys - band_w), LANES)
        qpos = start + lax.broadcasted_iota(jnp.int32, (tq, band_w), 0)
        kpos = bstart + lax.broadcasted_iota(jnp.int32, (tq, band_w), 1)
        bias = jnp.where(jnp.abs(qpos - kpos) <= WINDOW, 0.0, NEG_INF)

    def keys_values(kvh):
        if not windowed:
            return [k_ref[kvh]], [v_ref[kvh]]
        keep = lo_k if kvh % 2 == 0 else jnp.logical_not(lo_k)
        sl = slice((kvh // 2) * LANES, (kvh // 2 + 1) * LANES)
        kc = jnp.where(keep, ck_ref[:, sl], 0.0).astype(BF16)
        vc = jnp.where(keep, cv_ref[:, sl], 0.0).astype(BF16)
        return ([k_ref[kvh, pl.ds(bstart, band_w), :], kc], [v_ref[kvh, pl.ds(bstart, band_w), :], vc])

    units = [(kvh, g) for kvh in range(SWA_KV_HEADS) for g in range(SWA_GROUP)]

    def scores(u, buf):
        kvh, g = units[u]
        q = q_ref[(kvh // 2) * SWA_GROUP + g]
        sink = sink_ref[kvh * SWA_GROUP + g] * LOG2E
        folds, off = [], 0
        for i, k in enumerate(keys_values(kvh)[0]):
            s = _dot_nt(q, k)
            if windowed and i == 0:
                s = s + bias
            buf[:, off:off + s.shape[1]] = s
            folds.append(_lane_fold(jnp.maximum, s))
            off += s.shape[1]
        m = jnp.maximum(jnp.max(_tree(jnp.maximum, folds), axis=-1, keepdims=True), sink)
        return m, sink

    def consume(u, buf, m, sink):
        vs = keys_values(units[u][0])[1]
        e, off = [], 0
        for v in vs:
            e.append(jnp.exp2(buf[:, off:off + v.shape[0]] - m))
            off += v.shape[0]
        l = jnp.sum(_tree(jnp.add, [_lane_fold(jnp.add, x) for x in e]), axis=-1, keepdims=True) + jnp.exp2(sink - m)
        return _tree(jnp.add, [_dot(x.astype(BF16), v) for x, v in zip(e, vs)]) / l

    acc = jnp.zeros((tq, d), F32)
    ahead, pending = len(bufs) - 1, {}
    for u in range(min(ahead, len(units))):
        pending[u] = scores(u, bufs[u % len(bufs)])
    prev = None
    for u, (kvh, g) in enumerate(units):
        if u + ahead < len(units):
            pending[u + ahead] = scores(u + ahead, bufs[(u + ahead) % len(bufs)])
        o = consume(u, bufs[u % len(bufs)], *pending.pop(u))
        if g % 2 == 0:
            prev = o
            continue
        if kvh % 2 == 0:
            slab = prev + pltpu.roll(o, SWA_HEAD_DIM, 1)
        else:
            slab = pltpu.roll(prev, SWA_HEAD_DIM, 1) + o
        acc = acc + _dot(slab.astype(BF16), w_ref[kvh * 2 + g // 2])
    o_ref[...] = x_ref[...] + mod_ref[0][2:3] * acc


def _attn_swa(x, mods, row_fn, q, k, v, ctx, sink, w_out, tq, n_bufs):
    b, s, d = x.shape
    windowed = ctx is not None
    band_w = min(tq + 2 * WINDOW, s)
    kv_spec = pl.BlockSpec((None, SWA_KV_HEADS, s, LANES), lambda bi, t: (bi, 0, 0, 0))
    in_specs = [
        pl.BlockSpec((None, tq, d), lambda bi, t: (bi, t, 0)),
        pl.BlockSpec((1, 6, d), lambda bi, t: (row_fn(bi), 0, 0)),
        pl.BlockSpec((None, 8, tq, LANES), lambda bi, t: (bi, 0, t, 0)),
        kv_spec, kv_spec,
    ]
    args = [x, mods, q, k, v]
    if windowed:
        ck, cv, j = ctx
        p = ck.shape[2]
        spec = pl.BlockSpec((None, None, p, 256), lambda bi, t: (bi, j, 0, 0))
        in_specs += [spec, spec]
        args += [ck, cv]
    in_specs += [pl.BlockSpec(memory_space=pltpu.SMEM), _full((8, LANES, d))]
    args += [sink, w_out]
    n_keys = band_w + ctx[0].shape[2] if windowed else s
    return pl.pallas_call(
        functools.partial(_attn_swa_kernel, windowed, band_w),
        grid=(b, s // tq),
        in_specs=in_specs,
        out_specs=pl.BlockSpec((None, tq, d), lambda bi, t: (bi, t, 0)),
        out_shape=jax.ShapeDtypeStruct((b, s, d), F32),
        scratch_shapes=[pltpu.VMEM((tq, n_keys), F32) for _ in range(n_bufs)],
        compiler_params=_params(2),
        name="attn_swa",
    )(*args)


def _route(scores_t, bias):
    sel_t = scores_t + bias
    sel = [sel_t[e:e + 1] for e in range(N_EXPERTS)]
    raw = [scores_t[e:e + 1] for e in range(N_EXPERTS)]
    gscore = []
    for g in range(N_GROUPS):
        r = sel[g * 4:g * 4 + 4]
        pairs = [r[i] + r[j] for i in range(4) for j in range(i + 1, 4)]
        gscore.append(functools.reduce(jnp.maximum, pairs))
    best, gidx = gscore[0], jnp.zeros_like(gscore[0], dtype=jnp.int32)
    for g in range(1, N_GROUPS):
        take = gscore[g] > best
        best = jnp.where(take, gscore[g], best)
        gidx = jnp.where(take, g, gidx)
    vals = []
    for k in range(EXPERTS_PER_GROUP):
        v = sel[k]
        for g in range(1, N_GROUPS):
            v = jnp.where(gidx == g, sel[g * 4 + k], v)
        vals.append(v)

    def argmax4(vs):
        m, idx = vs[0], jnp.zeros_like(gidx)
        for k in range(1, 4):
            take = vs[k] > m
            m = jnp.where(take, vs[k], m)
            idx = jnp.where(take, k, idx)
        return idx

    i1 = argmax4(vals)
    i2 = argmax4([jnp.where(i1 == k, -jnp.inf, vals[k]) for k in range(4)])
    e1, e2 = gidx * 4 + i1, gidx * 4 + i2
    w1 = functools.reduce(lambda a, b: a + b, [jnp.where(e1 == e, raw[e], 0.0) for e in range(N_EXPERTS)])
    w2 = functools.reduce(lambda a, b: a + b, [jnp.where(e2 == e, raw[e], 0.0) for e in range(N_EXPERTS)])
    den = w1 + w2
    return e1, e2, w1 / den, w2 / den


SC_WINDOW = 128
SC_ROW = 128
N_PLANES = (D_MODEL // 2) // SC_ROW
MOE_ROW_BLOCK = 1024
PROMPT_BUFS = 8
HALF_WORD = -65536


def _to_planes(packed, out_ref):
    for j in range(N_PLANES):
        out_ref[j] = packed[:, j * SC_ROW:(j + 1) * SC_ROW]


def _from_planes(ref):
    return jnp.concatenate([ref[j] for j in range(N_PLANES)], axis=1)


def _pack_bf16_pairs(x):
    n = x.shape[1] // 2
    lo = pltpu.bitcast(x[:, :n].astype(BF16).astype(F32), jnp.int32)
    hi = pltpu.bitcast(x[:, n:].astype(BF16).astype(F32), jnp.int32)
    return lax.shift_right_logical(lo, jnp.int32(16)) | (hi & jnp.int32(HALF_WORD))


def _unpack_bf16_pairs(p):
    lo = pltpu.bitcast(lax.shift_left(p, jnp.int32(16)), F32)
    hi = pltpu.bitcast(p & jnp.int32(HALF_WORD), F32)
    return jnp.concatenate([lo, hi], axis=1)


def _pick(idx, rows):
    return _tree(jnp.add, [jnp.where(idx == e, rows[e], 0.0) for e in range(N_EXPERTS)])


def _moe_route_kernel(x_ref, mod_ref, g_ref, rw_ref, rb_ref, tri_ref, hp_o, eidx_o, rank_o, gcol_o, cnt_o, seen):
    @pl.when(jnp.logical_and(pl.program_id(0) == 0, pl.program_id(1) == 0))
    def _():
        seen[...] = jnp.zeros_like(seen)

    tm = x_ref.shape[0]
    mod = mod_ref[0]
    h = _modulate(x_ref[...], g_ref[...], mod[3:4], mod[4:5])
    h16 = h.astype(BF16)
    _to_planes(_pack_bf16_pairs(h), hp_o)
    h_lo = (h - h16.astype(F32)).astype(BF16)
    part = _dot_nt(rw_ref[...], h16)
    logits_t = part[:N_EXPERTS] + part[N_EXPERTS:] + _dot_nt(rw_ref[:N_EXPERTS], h_lo)
    e1, e2, g1, g2 = _route(_sigmoid(logits_t), rb_ref[...])
    hot = jnp.concatenate([jnp.where(jnp.logical_or(e1 == e, e2 == e), 1.0, 0.0) for e in range(N_EXPERTS)], axis=0)
    before = _dot(hot.astype(BF16), tri_ref[...]) + seen[...]
    rows = [before[e:e + 1] for e in range(N_EXPERTS)]
    eidx_o[0], eidx_o[1] = e1, e2
    rank_o[0], rank_o[1] = _pick(e1, rows).astype(jnp.int32), _pick(e2, rows).astype(jnp.int32)
    gcol_o[...] = jnp.concatenate([g1, g2, jnp.zeros((LANES - 2, tm), F32)], axis=0).T
    seen[...] = seen[...] + jnp.sum(hot, axis=-1, keepdims=True)
    cnt_o[...] = seen[...]


def _moe_slots_kernel(mb, n_rows, eidx_ref, rank_ref, cnt_ref, idx_o, be_o):
    cnt = cnt_ref[...]
    padded = jnp.ceil(cnt / mb) * mb
    starts, ends, run = [], [], jnp.zeros((1, 1), F32)
    for e in range(N_EXPERTS):
        starts.append(run)
        run = run + padded[e:e + 1]
        ends.append(run)
    n_tiles, per_tile = eidx_ref.shape[1], eidx_ref.shape[3] // SC_WINDOW
    for k in range(2):
        for i in range(n_tiles):
            slot = rank_ref[k, i] + _pick(eidx_ref[k, i], starts).astype(jnp.int32)
            for j in range(N_PLANES):
                for q in range(per_tile):
                    row = ((k * N_PLANES + j) * n_tiles + i) * per_tile + q
                    idx_o[row:row + 1, :] = slot[:, q * SC_WINDOW:(q + 1) * SC_WINDOW] + j * n_rows
    first_row = lax.broadcasted_iota(jnp.int32, be_o.shape, 1).astype(F32) * mb
    be = _tree(jnp.add, [jnp.where(ends[e] <= first_row, 1.0, 0.0) for e in range(N_EXPERTS)])
    lane = lax.broadcasted_iota(jnp.int32, be_o.shape, 1)
    be = jnp.where(lane == be_o.shape[1] - 1, ends[N_EXPERTS - 1] / mb, jnp.minimum(be, N_EXPERTS - 1.0))
    be_o[...] = be.astype(jnp.int32)


def _moe_ffn_kernel(be_ref, hs_ref, wg_ref, wu_ref, wd_ref, ys_o, wgu16, wd16):
    i = pl.program_id(0)
    live = i < be_ref[be_ref.shape[0] - 1]

    @pl.when(jnp.logical_and(live, jnp.logical_or(i == 0, be_ref[i] != be_ref[jnp.maximum(i - 1, 0)])))
    def _():
        wgu16[:, :D_EXPERT] = wg_ref[...].astype(BF16)
        wgu16[:, D_EXPERT:] = wu_ref[...].astype(BF16)
        wd16[...] = wd_ref[...].astype(BF16)

    @pl.when(live)
    def _():
        h = _unpack_bf16_pairs(_from_planes(hs_ref)).astype(BF16)
        gu = _dot(h, wgu16[...])
        g, u = gu[:, :D_EXPERT], gu[:, D_EXPERT:]
        a = g * _sigmoid(g) * u
        _to_planes(_pack_bf16_pairs(_dot(a.astype(BF16), wd16[...])), ys_o)


def _moe_residual(x, z0_ref, z1_ref, gcol_ref, mod_ref):
    gc = gcol_ref[...]
    lane = lax.broadcasted_iota(jnp.int32, gc.shape, 1)
    g1 = jnp.sum(jnp.where(lane == 0, gc, 0.0), axis=-1, keepdims=True)
    g2 = jnp.sum(jnp.where(lane == 1, gc, 0.0), axis=-1, keepdims=True)
    y = g1 * _unpack_bf16_pairs(_from_planes(z0_ref)) + g2 * _unpack_bf16_pairs(_from_planes(z1_ref))
    return x + mod_ref[0][5:6] * y


def _moe_combine_kernel(x_ref, z0_ref, z1_ref, gcol_ref, mod_ref, fg_ref, o_ref):
    o_ref[...] = _rms(_moe_residual(x_ref[...], z0_ref, z1_ref, gcol_ref, mod_ref), fg_ref[...])


def _residual_specs(resid, row_fn, tm, d, in_specs, args):
    (z, gcol), mods_prev = resid
    for k in range(2):
        in_specs.append(pl.BlockSpec((None, N_PLANES, None, tm, SC_ROW), lambda bi, t, k=k: (k, 0, bi, t, 0)))
        args.append(z)
    in_specs += [pl.BlockSpec((None, tm, LANES), lambda bi, t: (bi, t, 0)),
                 pl.BlockSpec((1, 6, d), lambda bi, t: (row_fn(bi), 0, 0))]
    args += [gcol, mods_prev]


def _sc_mesh():
    return plsc.VectorSubcoreMesh(core_axis_name="core", subcore_axis_name="subcore")


def _sc_scatter_rows(rows, idx, n_out):
    n, w = rows.shape
    steps = n // SC_WINDOW

    @pl.kernel(out_type=jax.ShapeDtypeStruct((n_out, w), rows.dtype), mesh=_sc_mesh(), scratch_types=[])
    def scatter(x_hbm, i0_hbm, i1_hbm, o_hbm):
        def body(x_vmem, i0_vmem, i1_vmem):
            pltpu.sync_copy(x_vmem, o_hbm.at[i0_vmem.at[0]])
            pltpu.sync_copy(x_vmem, o_hbm.at[i1_vmem.at[0]])

        pltpu.emit_pipeline(
            body,
            grid=(steps,),
            in_specs=[pl.BlockSpec((SC_WINDOW, w), lambda i: (i, 0)),
                      pl.BlockSpec((1, SC_WINDOW), lambda i: (i, 0)),
                      pl.BlockSpec((1, SC_WINDOW), lambda i: (i + steps, 0))],
            out_specs=[],
            core_axis_name=("core", "subcore"),
            dimension_semantics=(pltpu.PARALLEL,),
        )(x_hbm, i0_hbm, i1_hbm)

    return scatter(rows, idx, idx)


def _sc_gather_rows(rows, idx):
    steps, w = idx.shape[0], rows.shape[1]

    @pl.kernel(out_type=jax.ShapeDtypeStruct((steps * SC_WINDOW, w), rows.dtype), mesh=_sc_mesh(),
               scratch_types=[])
    def gather(x_hbm, i_hbm, o_hbm):
        def body(i_vmem, o_vmem):
            pltpu.sync_copy(x_hbm.at[i_vmem.at[0]], o_vmem)

        pltpu.emit_pipeline(
            body,
            grid=(steps,),
            in_specs=[pl.BlockSpec((1, SC_WINDOW), lambda i: (i, 0))],
            out_specs=[pl.BlockSpec((SC_WINDOW, w), lambda i: (i, 0))],
            core_axis_name=("core", "subcore"),
            dimension_semantics=(pltpu.PARALLEL,),
        )(i_hbm, o_hbm)

    return gather(rows, idx)


def _moe_bucket(x, mods, row_fn, g, rw_t, rb, tm, mb):
    b, s, d = x.shape
    n, nt = b * s, (b * s) // tm
    per_b = s // tm
    tile = lambda bi, t: (0, bi * per_b + t, 0, 0)
    tri = jnp.asarray(np.triu(np.ones((tm, tm), np.float32), 1), BF16)
    hp, eidx, rank, gcol, cnt = pl.pallas_call(
        _moe_route_kernel,
        grid=(b, per_b),
        in_specs=[
            pl.BlockSpec((None, tm, d), lambda bi, t: (bi, t, 0)),
            pl.BlockSpec((1, 6, d), lambda bi, t: (row_fn(bi), 0, 0)),
            _full((1, d)), _full((2 * N_EXPERTS, d)), _full((N_EXPERTS, 1)), _full((tm, tm)),
        ],
        out_specs=[
            pl.BlockSpec((N_PLANES, tm, SC_ROW), lambda bi, t: (0, bi * per_b + t, 0)),
            pl.BlockSpec((2, None, 1, tm), tile), pl.BlockSpec((2, None, 1, tm), tile),
            pl.BlockSpec((None, tm, LANES), lambda bi, t: (bi, t, 0)),
            _full((N_EXPERTS, 1)),
        ],
        out_shape=[
            jax.ShapeDtypeStruct((N_PLANES, n, SC_ROW), jnp.int32),
            jax.ShapeDtypeStruct((2, nt, 1, tm), jnp.int32), jax.ShapeDtypeStruct((2, nt, 1, tm), jnp.int32),
            jax.ShapeDtypeStruct((b, s, LANES), F32),
            jax.ShapeDtypeStruct((N_EXPERTS, 1), F32),
        ],
        scratch_shapes=[pltpu.VMEM((N_EXPERTS, 1), F32)],
        compiler_params=_params(2),
        name="moe_route",
    )(x, mods, g, rw_t, rb, tri)

    n_rows = 2 * n + N_EXPERTS * mb
    nb = n_rows // mb
    nb_pad = -(-(nb + 1) // LANES) * LANES
    idx, be = pl.pallas_call(
        functools.partial(_moe_slots_kernel, mb, n_rows),
        out_shape=[jax.ShapeDtypeStruct((2 * N_PLANES * n // SC_WINDOW, SC_WINDOW), jnp.int32),
                   jax.ShapeDtypeStruct((1, nb_pad), jnp.int32)],
        name="moe_slots",
    )(eidx, rank, cnt)

    hs = _sc_scatter_rows(hp.reshape(N_PLANES * n, SC_ROW), idx, N_PLANES * n_rows)
    return hs.reshape(N_PLANES, n_rows, SC_ROW), idx, be.reshape(nb_pad), gcol


def _moe_experts(bucket, layer, w_gate, w_up, w_down, mb, out_dims):
    hs, idx, be, gcol = bucket
    n_rows, d = hs.shape[1], w_gate.shape[2]
    last = be.shape[0] - 1
    blk = lambda i, be_r: jnp.minimum(i, be_r[last] - 1)
    plane_block = pl.BlockSpec((N_PLANES, mb, SC_ROW), lambda i, be_r: (0, blk(i, be_r), 0))
    weight = lambda i, be_r: (layer, be_r[blk(i, be_r)], 0, 0)
    ys = pl.pallas_call(
        _moe_ffn_kernel,
        grid_spec=pltpu.PrefetchScalarGridSpec(
            num_scalar_prefetch=1,
            grid=(n_rows // mb,),
            in_specs=[
                plane_block,
                pl.BlockSpec((None, None, d, D_EXPERT), weight),
                pl.BlockSpec((None, None, d, D_EXPERT), weight),
                pl.BlockSpec((None, None, D_EXPERT, d), weight),
            ],
            out_specs=plane_block,
            scratch_shapes=[pltpu.VMEM((d, 2 * D_EXPERT), BF16), pltpu.VMEM((D_EXPERT, d), BF16)],
        ),
        out_shape=jax.ShapeDtypeStruct((N_PLANES, n_rows, SC_ROW), jnp.int32),
        compiler_params=_params(1),
        name="moe_ffn",
    )(be, hs, w_gate, w_up, w_down)
    z = _sc_gather_rows(ys.reshape(N_PLANES * n_rows, SC_ROW), idx)
    return z.reshape(2, N_PLANES, *out_dims, SC_ROW), gcol.reshape(*out_dims, LANES)


def _moe_combine(x, mods, row_fn, gathered, final_g, tm):
    b, s, d = x.shape
    in_specs = [pl.BlockSpec((None, tm, d), lambda bi, t: (bi, t, 0))]
    args = [x]
    _residual_specs((gathered, mods), row_fn, tm, d, in_specs, args)
    in_specs.append(_full((1, d)))
    args.append(final_g)
    return pl.pallas_call(
        _moe_combine_kernel,
        grid=(b, s // tm),
        in_specs=in_specs,
        out_specs=pl.BlockSpec((None, tm, d), lambda bi, t: (bi, t, 0)),
        out_shape=jax.ShapeDtypeStruct((b, s, d), F32),
        compiler_params=_params(2),
        name="moe_combine",
    )(*args)


def _rope_tables(n_tokens, rot_dim, offset):
    n_rows = n_tokens // GRID_W
    rows = np.repeat(np.arange(n_rows), GRID_W)
    cols = np.tile(np.arange(GRID_W), n_rows)
    n_freq = rot_dim // 4
    inv = jnp.asarray(ROPE_THETA, F32) ** (-jnp.arange(n_freq, dtype=F32) / n_freq)
    ang = jnp.concatenate([jnp.asarray(rows, F32)[:, None] * inv, jnp.asarray(cols, F32)[:, None] * inv], axis=-1)
    cos, sin = jnp.repeat(jnp.cos(ang), 2, axis=-1), jnp.repeat(jnp.sin(ang), 2, axis=-1)
    odd = jnp.asarray(np.arange(rot_dim) % 2 == 1)
    parts = [cos, jnp.where(odd, sin, 0.0), jnp.where(odd, 0.0, -sin)]
    period = 64 if rot_dim == 64 else LANES
    fill = [1.0, 0.0, 0.0]
    out = []
    for p, f in zip(parts, fill):
        slot = jnp.full((n_tokens, period), f, F32).at[:, offset:offset + rot_dim].set(p)
        out.append(jnp.tile(slot, (1, LANES // period)))
    return jnp.stack(out)


def _pad_heads(w, n_heads, lo, hi):
    k = w.shape[0]
    w = w.reshape(k, n_heads, -1)[:, :, lo:hi]
    return jnp.pad(w, ((0, 0), (0, 0), (0, LANES - (hi - lo)))).reshape(k, n_heads * LANES)


def _ab_weights(j, ab_w_in, diff_lambda, diff_subln_g, mla_q_norm_g, mla_w_qb, mla_kv_norm_g, mla_w_kvb, ab_w_out):
    place = np.zeros((LANES, MLA_HEADS * LANES), np.float32)
    for hd in range(MLA_HEADS):
        for r in range(MLA_ROPE_DIM):
            place[r, hd * LANES + MLA_NOPE_DIM + r] = 1.0
    w_out_m = ab_w_out[j][512:].reshape(MLA_HEADS, MLA_V_DIM, D_MODEL)
    return {
        "w_in": jnp.pad(ab_w_in[j], ((0, 0), (0, AB_IN_PAD - AB_IN))).astype(BF16),
        "qn_g": mla_q_norm_g[j][None, :],
        "w_qb": _pad_heads(mla_w_qb[j], MLA_HEADS, 0, MLA_NOPE_DIM + MLA_ROPE_DIM).astype(BF16),
        "kvn_g": mla_kv_norm_g[j][None, :],
        "wk": _pad_heads(mla_w_kvb[j], MLA_HEADS, 0, MLA_NOPE_DIM).astype(BF16),
        "wv": _pad_heads(mla_w_kvb[j], MLA_HEADS, MLA_NOPE_DIM, MLA_NOPE_DIM + MLA_V_DIM).astype(BF16),
        "e": jnp.asarray(place, BF16),
        "w_out_d": ab_w_out[j][:512].reshape(DIFF_HEADS, LANES, D_MODEL).astype(BF16),
        "w_out_m": jnp.pad(w_out_m, ((0, 0), (0, LANES - MLA_V_DIM), (0, 0))).astype(BF16),
        "subln_g": diff_subln_g[j][None, :],
        "lam": diff_lambda[j],
    }


def kernel(x_prompt, x_sample, cache_diff_k, cache_diff_v, cache_mla_ckv, cache_mla_krope, cache_swa_k, cache_swa_v,
           c, c_ctx, ada_w, ada_b, norm1_g, norm2_g, final_norm_g, ab_w_in, diff_lambda, diff_subln_g, mla_q_norm_g,
           mla_w_qb, mla_kv_norm_g, mla_w_kvb, ab_w_out, swa_w_in, swa_sink, swa_w_out, router_w, router_bias,
           moe_w_gate, moe_w_up, moe_w_down):
    bp, sp, d = x_prompt.shape
    bs, ss, _ = x_sample.shape
    depth = ada_w.shape[0]
    past = cache_diff_k.shape[2]
    n_ab = cache_diff_k.shape[1]
    n_swa = cache_swa_k.shape[1]
    assert bs + 1 <= COND_ROWS and d == D_MODEL

    cond = jnp.zeros((COND_ROWS, d), F32).at[:bs].set(c).at[bs].set(c_ctx)
    mods = _ada_all(cond, ada_w, ada_b).reshape(depth, COND_ROWS, 6, d)
    row_s = lambda bi: bi
    row_p = lambda bi: bi * 0 + bs

    rope_d = _rope_tables(ss, DIFF_QK_DIM, 0)
    rope_q = _rope_tables(ss, MLA_ROPE_DIM, MLA_NOPE_DIM)
    rope_k = _rope_tables(ss, MLA_ROPE_DIM, 0)

    cdk = cache_diff_k.reshape(bs, n_ab, past, 512)
    cdv = cache_diff_v.reshape(bs, n_ab, past, 512)
    ckr = jnp.pad(cache_mla_krope, ((0, 0), (0, 0), (0, 0), (0, LANES - MLA_ROPE_DIM)))
    csk = cache_swa_k.reshape(bs, n_swa, past, 256)
    csv = cache_swa_v.reshape(bs, n_swa, past, 256)
    rw_hi = router_w.T.astype(BF16)
    rw_t = jnp.concatenate([rw_hi, (router_w.T - rw_hi.astype(F32)).astype(BF16)], axis=0)
    rb = router_bias[:, None]

    tp = min(256, sp)
    ts = min(256, ss)
    tm_s = min(512, ss)
    tm_p = min(512, bp * sp)

    xp, xs = x_prompt, x_sample
    new_ab, new_swa = None, None
    pend_p = pend_s = None

    def project(fn, x, pend, *a):
        outs = list(fn(x, *a, pend))
        return (outs.pop() if pend is not None else x), outs

    for l in range(depth):
        j = l // 2
        ml = mods[l]
        g1, g2 = norm1_g[l][None, :], norm2_g[l][None, :]
        if l % 2 == 0:
            lambda_init = 0.8 - 0.6 * math.exp(-0.3 * l)
            wts = _ab_weights(j, ab_w_in, diff_lambda, diff_subln_g, mla_q_norm_g, mla_w_qb, mla_kv_norm_g,
                              mla_w_kvb, ab_w_out)
            xp, (qa, ka, va, qb, km, vm, *new_ab) = project(_proj_ab, xp, pend_p, ml, row_p, g1, wts, None,
                                                             (j, n_ab, new_ab), tp)
            xp = _attn_ab(xp, ml, row_p, (qa, qb), [(ka, va, km, vm)], wts, lambda_init, tp, PROMPT_BUFS)
            xs, (qa, ka, va, qb, km, vm) = project(_proj_ab, xs, pend_s, ml, row_s, g1, wts,
                                                   (rope_d, rope_q, rope_k), None, ts)
            cache_seg = _cache_ab(cdk, cdv, cache_mla_ckv, ckr, j, wts)
            xs = _attn_ab(xs, ml, row_s, (qa, qb), [(ka, va, km, vm), tuple(cache_seg)], wts, lambda_init, ts, 2)
        else:
            w_in = swa_w_in[j].astype(BF16)
            w_out = swa_w_out[j].reshape(8, LANES, d).astype(BF16)
            xp, (q, k, v, *new_swa) = project(_proj_swa, xp, pend_p, ml, row_p, g1, w_in, None,
                                              (j, n_swa, new_swa), tp)
            xp = _attn_swa(xp, ml, row_p, q, k, v, None, swa_sink[j], w_out, tp, PROMPT_BUFS)
            xs, (q, k, v) = project(_proj_swa, xs, pend_s, ml, row_s, g1, w_in, rope_d, None, ts)
            xs = _attn_swa(xs, ml, row_s, q, k, v, (csk, csv, j), swa_sink[j], w_out, ts, 2)
        bucket_p = _moe_bucket(xp.reshape(1, bp * sp, d), ml, row_p, g2, rw_t, rb, tm_p, MOE_ROW_BLOCK)
        bucket_s = _moe_bucket(xs, ml, row_s, g2, rw_t, rb, tm_s, MOE_ROW_BLOCK)
        got_p = _moe_experts(bucket_p, l, moe_w_gate, moe_w_up, moe_w_down, MOE_ROW_BLOCK, (bp, sp))
        got_s = _moe_experts(bucket_s, l, moe_w_gate, moe_w_up, moe_w_down, MOE_ROW_BLOCK, (bs, ss))
        pend_p, pend_s = (got_p, ml), (got_s, ml)

    xp = _moe_combine(xp, pend_p[1], row_p, pend_p[0], final_norm_g[None, :], tp)
    xs = _moe_combine(xs, pend_s[1], row_s, pend_s[0], final_norm_g[None, :], ts)

    new_diff_k = new_ab[0].reshape(bp, n_ab, sp, DIFF_HEADS, 2, DIFF_QK_DIM)
    new_diff_v = new_ab[1].reshape(bp, n_ab, sp, DIFF_HEADS, DIFF_V_DIM)
    new_mla_ckv, new_mla_krope = new_ab[2], new_ab[3]
    new_swa_k = new_swa[0].reshape(bp, n_swa, sp, SWA_KV_HEADS, SWA_HEAD_DIM)
    new_swa_v = new_swa[1].reshape(bp, n_swa, sp, SWA_KV_HEADS, SWA_HEAD_DIM)
    return (xp, xs, new_diff_k, new_diff_v, new_mla_ckv, new_mla_krope, new_swa_k, new_swa_v)
```

```python
import functools
import math

import jax
import jax.numpy as jnp
import numpy as np
from jax import lax
from jax.experimental import pallas as pl
from jax.experimental.pallas import tpu as pltpu
from jax.experimental.pallas import tpu_sc as plsc

F32 = jnp.float32
BF16 = jnp.bfloat16

D_MODEL = 1024
GRID_W = 64
ROPE_THETA = 10000.0
NORM_EPS = 1e-6
NEG_INF = -1e30
LOG2E = math.log2(math.e)
LANES = 128

DIFF_HEADS = 4
DIFF_QK_DIM = 64
DIFF_V_DIM = 128
MLA_HEADS = 8
MLA_Q_RANK = 384
MLA_KV_RANK = 256
MLA_NOPE_DIM = 64
MLA_ROPE_DIM = 32
MLA_V_DIM = 64
AB_IN = 3 * 512 + MLA_Q_RANK + MLA_KV_RANK + MLA_ROPE_DIM
AB_IN_PAD = 2304
SWA_HEADS = 16
SWA_KV_HEADS = 4
SWA_GROUP = 4
SWA_HEAD_DIM = 64
WINDOW = 128
N_EXPERTS = 16
N_GROUPS = 4
EXPERTS_PER_GROUP = 4
D_EXPERT = 256
COND_ROWS = 16

VMEM_LIMIT = 56 * 1024 * 1024


def _full(shape):
    n = len(shape)
    return pl.BlockSpec(shape, lambda *_: (0,) * n)


def _params(n_axes):
    return pltpu.CompilerParams(dimension_semantics=("arbitrary",) * n_axes, vmem_limit_bytes=VMEM_LIMIT)


def _sigmoid(x):
    return 1.0 / (1.0 + jnp.exp(-x))


def _rms(x, g):
    return x * lax.rsqrt(jnp.mean(x * x, axis=-1, keepdims=True) + NORM_EPS) * g


def _modulate(x, g, shift, scale):
    return _rms(x, g) * (1.0 + scale) + shift


def _dot(a, b):
    return jnp.dot(a, b, preferred_element_type=F32)


def _dot_nt(a, b):
    return lax.dot_general(a, b, (((1,), (1,)), ((), ())), preferred_element_type=F32)


def _rope(x, tab_ref):
    c, s_odd, s_even = tab_ref[0], tab_ref[1], tab_ref[2]
    out = []
    for i in range(x.shape[1] // LANES):
        xi = x[:, i * LANES:(i + 1) * LANES]
        out.append(xi * c + pltpu.roll(xi, 1, 1) * s_odd + pltpu.roll(xi, LANES - 1, 1) * s_even)
    return out[0] if len(out) == 1 else jnp.concatenate(out, axis=1)


def _tree(op, xs):
    xs = list(xs)
    while len(xs) > 1:
        xs = [op(xs[i], xs[i + 1]) if i + 1 < len(xs) else xs[i] for i in range(0, len(xs), 2)]
    return xs[0]


def _lane_fold(op, x):
    return _tree(op, [x[:, i * LANES:(i + 1) * LANES] for i in range(x.shape[1] // LANES)])


def _ada_kernel(cond_ref, w_ref, b_ref, o_ref):
    c = cond_ref[...]
    a = (c * _sigmoid(c)).astype(BF16)
    o_ref[...] = _dot(a, w_ref[...].astype(BF16)) + b_ref[...]


def _ada_all(cond, ada_w, ada_b):
    depth, d, n = ada_w.shape
    tn = 1536
    return pl.pallas_call(
        _ada_kernel,
        grid=(depth, n // tn),
        in_specs=[
            _full((COND_ROWS, d)),
            pl.BlockSpec((None, d, tn), lambda l, j: (l, 0, j)),
            pl.BlockSpec((None, 1, tn), lambda l, j: (l, 0, j)),
        ],
        out_specs=pl.BlockSpec((None, COND_ROWS, tn), lambda l, j: (l, 0, j)),
        out_shape=jax.ShapeDtypeStruct((depth, COND_ROWS, n), F32),
        compiler_params=_params(2),
        name="ada_mod",
    )(cond, ada_w, ada_b.reshape(depth, 1, n))


def _proj_ab_kernel(rope, emit_cache, n_prev, fused, x_ref, mod_ref, g_ref, w_in_ref, qn_ref, wqb_ref, kvn_ref,
                    wk_ref, wv_ref, e_ref, *rest):
    if rope:
        rope_d, rope_q, rope_k = rest[:3]
        rest = rest[3:]
    x = x_ref[...]
    if fused:
        x = _moe_residual(x, *rest[:4])
        rest[-1][...] = x
        rest = rest[4:]
    rest = rest[n_prev:]
    qa_o, ka_o, va_o, qb_o, km_o, vm_o = rest[:6]
    mod = mod_ref[0]
    h = _modulate(x, g_ref[...], mod[0:1], mod[1:2]).astype(BF16)
    big = _dot(h, w_in_ref[...])
    qa, ka, va = big[:, 0:512], big[:, 512:1024], big[:, 1024:1536]
    q_lat, ckv, kr = big[:, 1536:1920], big[:, 1920:2176], big[:, 2176:2304]
    qb = _dot(_rms(q_lat, qn_ref[...]).astype(BF16), wqb_ref[...])
    ckv_n = _rms(ckv, kvn_ref[...])
    if emit_cache:
        ka32_o, va32_o, ckv32_o, kr32_o = rest[6:10]
        ka32_o[...] = ka
        va32_o[...] = va
        ckv32_o[...] = ckv_n
        kr32_o[...] = kr[:, :MLA_ROPE_DIM]
    if rope:
        qa, ka = _rope(qa, rope_d), _rope(ka, rope_d)
        qb, kr = _rope(qb, rope_q), _rope(kr, rope_k)
    qa = qa * (DIFF_QK_DIM ** -0.5 * LOG2E)
    qb = qb * ((MLA_NOPE_DIM + MLA_ROPE_DIM) ** -0.5 * LOG2E)
    ckv16 = ckv_n.astype(BF16)
    km = _dot(ckv16, wk_ref[...]) + _dot(kr.astype(BF16), e_ref[...])
    vm = _dot(ckv16, wv_ref[...])
    lo = lax.broadcasted_iota(jnp.int32, (qa.shape[0], LANES), 1) < DIFF_QK_DIM
    for hd in range(DIFF_HEADS):
        sl = slice(hd * LANES, (hd + 1) * LANES)
        qa_o[2 * hd] = jnp.where(lo, qa[:, sl], 0.0).astype(BF16)
        qa_o[2 * hd + 1] = jnp.where(lo, 0.0, qa[:, sl]).astype(BF16)
        ka_o[hd] = ka[:, sl].astype(BF16)
        va_o[hd] = va[:, sl].astype(BF16)
    for hd in range(MLA_HEADS):
        sl = slice(hd * LANES, (hd + 1) * LANES)
        qb_o[hd] = qb[:, sl].astype(BF16)
        km_o[hd] = km[:, sl].astype(BF16)
        vm_o[hd] = vm[:, sl].astype(BF16)


def _cache_outputs(cache, widths, b, s, tm, in_specs, args, n_fixed_outs):
    if cache is None:
        return [], {}, 0
    j, n_layers, prev = cache
    outs = [(jax.ShapeDtypeStruct((b, n_layers, s, w), F32),
             pl.BlockSpec((None, None, tm, w), lambda bi, t: (bi, j, t, 0))) for w in widths]
    aliases = {}
    if prev is not None:
        for i, arr in enumerate(prev):
            aliases[len(args)] = n_fixed_outs + i
            in_specs.append(pl.BlockSpec(memory_space=pl.ANY))
            args.append(arr)
    return outs, aliases, len(aliases)


def _proj_ab(x, mods, row_fn, g, wts, rope_tabs, cache, tm, resid=None):
    b, s, d = x.shape
    rope = rope_tabs is not None
    tok = lambda bi, t: (bi, t, 0)
    head = lambda bi, t: (bi, 0, t, 0)
    in_specs = [
        pl.BlockSpec((None, tm, d), tok),
        pl.BlockSpec((1, 6, d), lambda bi, t: (row_fn(bi), 0, 0)),
        _full((1, d)),
        _full((d, AB_IN_PAD)), _full((1, MLA_Q_RANK)), _full((MLA_Q_RANK, 1024)), _full((1, MLA_KV_RANK)),
        _full((MLA_KV_RANK, 1024)), _full((MLA_KV_RANK, 1024)), _full((LANES, 1024)),
    ]
    args = [x, mods, g, wts["w_in"], wts["qn_g"], wts["w_qb"], wts["kvn_g"], wts["wk"], wts["wv"], wts["e"]]
    if rope:
        in_specs += [pl.BlockSpec((3, tm, LANES), lambda bi, t: (0, t, 0))] * 3
        args += list(rope_tabs)
    if resid is not None:
        _residual_specs(resid, row_fn, tm, d, in_specs, args)

    def hm(nh):
        return jax.ShapeDtypeStruct((b, nh, s, LANES), BF16), pl.BlockSpec((None, nh, tm, LANES), head)

    outs = [hm(8), hm(4), hm(4), hm(8), hm(8), hm(8)]
    cache_outs, aliases, n_prev = _cache_outputs(cache, (512, 512, MLA_KV_RANK, MLA_ROPE_DIM), b, s, tm, in_specs,
                                                 args, len(outs))
    outs += cache_outs
    if resid is not None:
        outs.append((jax.ShapeDtypeStruct((b, s, d), F32), pl.BlockSpec((None, tm, d), tok)))
    return pl.pallas_call(
        functools.partial(_proj_ab_kernel, rope, cache is not None, n_prev, resid is not None),
        grid=(b, s // tm),
        in_specs=in_specs,
        out_specs=[o[1] for o in outs],
        out_shape=[o[0] for o in outs],
        input_output_aliases=aliases,
        compiler_params=_params(2),
        name="proj_ab",
    )(*args)


def _cache_ab_kernel(dk_ref, dv_ref, ckv_ref, kr_ref, wk_ref, wv_ref, e_ref, ck_o, cv_o, km_o, vm_o):
    ckv16 = ckv_ref[...].astype(BF16)
    km = _dot(ckv16, wk_ref[...]) + _dot(kr_ref[...].astype(BF16), e_ref[...])
    vm = _dot(ckv16, wv_ref[...])
    for hd in range(DIFF_HEADS):
        sl = slice(hd * LANES, (hd + 1) * LANES)
        ck_o[hd] = dk_ref[:, sl].astype(BF16)
        cv_o[hd] = dv_ref[:, sl].astype(BF16)
    for hd in range(MLA_HEADS):
        sl = slice(hd * LANES, (hd + 1) * LANES)
        km_o[hd] = km[:, sl].astype(BF16)
        vm_o[hd] = vm[:, sl].astype(BF16)


def _cache_ab(cdk, cdv, cckv, ckr, j, wts):
    b, _, p, _ = cdk.shape
    lay = lambda bi: (bi, j, 0, 0)

    def hm(nh):
        return (jax.ShapeDtypeStruct((b, nh, p, LANES), BF16),
                pl.BlockSpec((None, nh, p, LANES), lambda bi: (bi, 0, 0, 0)))

    outs = [hm(4), hm(4), hm(8), hm(8)]
    return pl.pallas_call(
        _cache_ab_kernel,
        grid=(b,),
        in_specs=[
            pl.BlockSpec((None, None, p, 512), lay), pl.BlockSpec((None, None, p, 512), lay),
            pl.BlockSpec((None, None, p, MLA_KV_RANK), lay), pl.BlockSpec((None, None, p, LANES), lay),
            _full((MLA_KV_RANK, 1024)), _full((MLA_KV_RANK, 1024)), _full((LANES, 1024)),
        ],
        out_specs=[o[1] for o in outs],
        out_shape=[o[0] for o in outs],
        compiler_params=_params(1),
        name="cache_ab",
    )(cdk, cdv, cckv, ckr, wts["wk"], wts["wv"], wts["e"])


def _attn_ab_kernel(n_seg, lambda_init, x_ref, mod_ref, qa_ref, qb_ref, *rest):
    segs = [rest[4 * i:4 * i + 4] for i in range(n_seg)]
    wd_ref, wm_ref, sg_ref, lam_ref, o_ref = rest[4 * n_seg:4 * n_seg + 5]
    bufs = rest[4 * n_seg + 5:]
    lp = lam_ref[...]
    lam = (jnp.exp(jnp.sum(lp[0:1] * lp[1:2], axis=-1, keepdims=True))
           - jnp.exp(jnp.sum(lp[2:3] * lp[3:4], axis=-1, keepdims=True)) + lambda_init)
    tq, d = x_ref.shape
    widths = [sg[0].shape[1] for sg in segs]
    offs = [sum(widths[:i]) for i in range(n_seg)]
    units = [("diff", hd) for hd in range(DIFF_HEADS)] + [("mla", hd) for hd in range(MLA_HEADS)]

    def scores(u, buf):
        kind, hd = units[u]
        qs = [qa_ref[2 * hd], qa_ref[2 * hd + 1]] if kind == "diff" else [qb_ref[hd]]
        ks = [sg[0 if kind == "diff" else 2][hd] for sg in segs]
        ms = []
        for i, q in enumerate(qs):
            folds = []
            for k, off, w in zip(ks, offs, widths):
                s = _dot_nt(q, k)
                buf[i, :, off:off + w] = s
                folds.append(_lane_fold(jnp.maximum, s))
            ms.append(jnp.max(_tree(jnp.maximum, folds), axis=-1, keepdims=True))
        return ms

    def exps(buf, i, m):
        e = [jnp.exp2(buf[i, :, off:off + w] - m) for off, w in zip(offs, widths)]
        l = jnp.sum(_tree(jnp.add, [_lane_fold(jnp.add, x) for x in e]), axis=-1, keepdims=True)
        return e, l

    def consume(u, buf, ms, acc):
        kind, hd = units[u]
        if kind == "diff":
            vs = [sg[1][hd] for sg in segs]
            (e1, l1), (e2, l2) = exps(buf, 0, ms[0]), exps(buf, 1, ms[1])
            c1, c2 = 1.0 / l1, lam / l2
            o = _tree(jnp.add, [_dot((a * c1 - b * c2).astype(BF16), v) for a, b, v in zip(e1, e2, vs)])
            od = _rms(o, sg_ref[...]) * (1.0 - lambda_init)
            return acc + _dot(od.astype(BF16), wd_ref[hd])
        vs = [sg[3][hd] for sg in segs]
        e, l = exps(buf, 0, ms[0])
        o = _tree(jnp.add, [_dot(x.astype(BF16), v) for x, v in zip(e, vs)]) / l
        return acc + _dot(o.astype(BF16), wm_ref[hd])

    acc = jnp.zeros((tq, d), F32)
    ahead, pending = len(bufs) - 1, {}
    for u in range(min(ahead, len(units))):
        pending[u] = scores(u, bufs[u % len(bufs)])
    for u in range(len(units)):
        if u + ahead < len(units):
            pending[u + ahead] = scores(u + ahead, bufs[(u + ahead) % len(bufs)])
        acc = consume(u, bufs[u % len(bufs)], pending.pop(u), acc)
    o_ref[...] = x_ref[...] + mod_ref[0][2:3] * acc


def _attn_ab(x, mods, row_fn, q_parts, seg_list, wts, lambda_init, tq, n_bufs):
    b, s, d = x.shape
    qa, qb = q_parts
    in_specs = [
        pl.BlockSpec((None, tq, d), lambda bi, t: (bi, t, 0)),
        pl.BlockSpec((1, 6, d), lambda bi, t: (row_fn(bi), 0, 0)),
        pl.BlockSpec((None, 8, tq, LANES), lambda bi, t: (bi, 0, t, 0)),
        pl.BlockSpec((None, 8, tq, LANES), lambda bi, t: (bi, 0, t, 0)),
    ]
    args = [x, mods, qa, qb]
    kv_mode = dict(pipeline_mode=pl.Buffered(1)) if s // tq > 1 else {}
    for seg in seg_list:
        for arr in seg:
            nh, nk = arr.shape[1], arr.shape[2]
            in_specs.append(pl.BlockSpec((None, nh, nk, LANES), lambda bi, t: (bi, 0, 0, 0), **kv_mode))
            args.append(arr)
    in_specs += [_full((DIFF_HEADS, LANES, d)), _full((MLA_HEADS, LANES, d)), _full((1, LANES)),
                 _full((4, DIFF_QK_DIM))]
    args += [wts["w_out_d"], wts["w_out_m"], wts["subln_g"], wts["lam"]]
    n_keys = sum(seg[0].shape[2] for seg in seg_list)
    return pl.pallas_call(
        functools.partial(_attn_ab_kernel, len(seg_list), lambda_init),
        grid=(b, s // tq),
        in_specs=in_specs,
        out_specs=pl.BlockSpec((None, tq, d), lambda bi, t: (bi, t, 0)),
        out_shape=jax.ShapeDtypeStruct((b, s, d), F32),
        scratch_shapes=[pltpu.VMEM((2, tq, n_keys), F32) for _ in range(n_bufs)],
        compiler_params=_params(2),
        name="attn_ab",
    )(*args)


def _proj_swa_kernel(rope, emit_cache, n_prev, fused, x_ref, mod_ref, g_ref, w_in_ref, *rest):
    if rope:
        rope_d = rest[0]
        rest = rest[1:]
    x = x_ref[...]
    if fused:
        x = _moe_residual(x, *rest[:4])
        rest[-1][...] = x
        rest = rest[4:]
    rest = rest[n_prev:]
    q_o, k_o, v_o = rest[:3]
    mod = mod_ref[0]
    h = _modulate(x, g_ref[...], mod[0:1], mod[1:2]).astype(BF16)
    big = _dot(h, w_in_ref[...])
    q, k, v = big[:, :1024], big[:, 1024:1280], big[:, 1280:1536]
    if emit_cache:
        k32_o, v32_o = rest[3:5]
        k32_o[...] = k
        v32_o[...] = v
    if rope:
        q, k = _rope(q, rope_d), _rope(k, rope_d)
    q = q * (SWA_HEAD_DIM ** -0.5 * LOG2E)
    lo = lax.broadcasted_iota(jnp.int32, (q.shape[0], LANES), 1) < SWA_HEAD_DIM
    for pair in range(2):
        for grp in range(SWA_GROUP):
            ca, cb = (2 * pair) * 2 + grp // 2, (2 * pair + 1) * 2 + grp // 2
            a = q[:, ca * LANES:(ca + 1) * LANES]
            bb = q[:, cb * LANES:(cb + 1) * LANES]
            if grp % 2 == 0:
                bb = pltpu.roll(bb, SWA_HEAD_DIM, 1)
            else:
                a = pltpu.roll(a, SWA_HEAD_DIM, 1)
            q_o[pair * SWA_GROUP + grp] = jnp.where(lo, a, bb).astype(BF16)
    for kvh in range(SWA_KV_HEADS):
        sl = slice((kvh // 2) * LANES, (kvh // 2 + 1) * LANES)
        keep = lo if kvh % 2 == 0 else jnp.logical_not(lo)
        k_o[kvh] = jnp.where(keep, k[:, sl], 0.0).astype(BF16)
        v_o[kvh] = jnp.where(keep, v[:, sl], 0.0).astype(BF16)


def _proj_swa(x, mods, row_fn, g, w_in, rope_tab, cache, tm, resid=None):
    b, s, d = x.shape
    rope = rope_tab is not None
    tok = lambda bi, t: (bi, t, 0)
    head = lambda bi, t: (bi, 0, t, 0)
    in_specs = [
        pl.BlockSpec((None, tm, d), tok),
        pl.BlockSpec((1, 6, d), lambda bi, t: (row_fn(bi), 0, 0)),
        _full((1, d)), _full((d, 1536)),
    ]
    args = [x, mods, g, w_in]
    if rope:
        in_specs.append(pl.BlockSpec((3, tm, LANES), lambda bi, t: (0, t, 0)))
        args.append(rope_tab)
    if resid is not None:
        _residual_specs(resid, row_fn, tm, d, in_specs, args)

    def hm(nh):
        return jax.ShapeDtypeStruct((b, nh, s, LANES), BF16), pl.BlockSpec((None, nh, tm, LANES), head)

    outs = [hm(8), hm(4), hm(4)]
    cache_outs, aliases, n_prev = _cache_outputs(cache, (256, 256), b, s, tm, in_specs, args, len(outs))
    outs += cache_outs
    if resid is not None:
        outs.append((jax.ShapeDtypeStruct((b, s, d), F32), pl.BlockSpec((None, tm, d), tok)))
    return pl.pallas_call(
        functools.partial(_proj_swa_kernel, rope, cache is not None, n_prev, resid is not None),
        grid=(b, s // tm),
        in_specs=in_specs,
        out_specs=[o[1] for o in outs],
        out_shape=[o[0] for o in outs],
        input_output_aliases=aliases,
        compiler_params=_params(2),
        name="proj_swa",
    )(*args)


def _attn_swa_kernel(windowed, band_w, x_ref, mod_ref, q_ref, k_ref, v_ref, *rest):
    if windowed:
        ck_ref, cv_ref = rest[:2]
        rest = rest[2:]
    sink_ref, w_ref, o_ref = rest[:3]
    bufs = rest[3:]
    tq, d = x_ref.shape
    n_keys = k_ref.shape[1]
    lo_k = lax.broadcasted_iota(jnp.int32, (1, LANES), 1) < SWA_HEAD_DIM
    if windowed:
        start = pl.program_id(1) * tq
        bstart = pl.multiple_of(jnp.clip(start - WINDOW, 0, n_keys - band_w), LANES)
        qpos = start + lax.broadcasted_iota(jnp.int32, (tq, band_w), 0)
        kpos = bstart + lax.broadcasted_iota(jnp.int32, (tq, band_w), 1)
        bias = jnp.where(jnp.abs(qpos - kpos) <= WINDOW, 0.0, NEG_INF)

    def keys_values(kvh):
        if not windowed:
            return [k_ref[kvh]], [v_ref[kvh]]
        keep = lo_k if kvh % 2 == 0 else jnp.logical_not(lo_k)
        sl = slice((kvh // 2) * LANES, (kvh // 2 + 1) * LANES)
        kc = jnp.where(keep, ck_ref[:, sl], 0.0).astype(BF16)
        vc = jnp.where(keep, cv_ref[:, sl], 0.0).astype(BF16)
        return ([k_ref[kvh, pl.ds(bstart, band_w), :], kc], [v_ref[kvh, pl.ds(bstart, band_w), :], vc])

    units = [(kvh, g) for kvh in range(SWA_KV_HEADS) for g in range(SWA_GROUP)]

    def scores(u, buf):
        kvh, g = units[u]
        q = q_ref[(kvh // 2) * SWA_GROUP + g]
        sink = sink_ref[kvh * SWA_GROUP + g] * LOG2E
        folds, off = [], 0
        for i, k in enumerate(keys_values(kvh)[0]):
            s = _dot_nt(q, k)
            if windowed and i == 0:
                s = s + bias
            buf[:, off:off + s.shape[1]] = s
            folds.append(_lane_fold(jnp.maximum, s))
            off += s.shape[1]
        m = jnp.maximum(jnp.max(_tree(jnp.maximum, folds), axis=-1, keepdims=True), sink)
        return m, sink

    def consume(u, buf, m, sink):
        vs = keys_values(units[u][0])[1]
        e, off = [], 0
        for v in vs:
            e.append(jnp.exp2(buf[:, off:off + v.shape[0]] - m))
            off += v.shape[0]
        l = jnp.sum(_tree(jnp.add, [_lane_fold(jnp.add, x) for x in e]), axis=-1, keepdims=True) + jnp.exp2(sink - m)
        return _tree(jnp.add, [_dot(x.astype(BF16), v) for x, v in zip(e, vs)]) / l

    acc = jnp.zeros((tq, d), F32)
    ahead, pending = len(bufs) - 1, {}
    for u in range(min(ahead, len(units))):
        pending[u] = scores(u, bufs[u % len(bufs)])
    prev = None
    for u, (kvh, g) in enumerate(units):
        if u + ahead < len(units):
            pending[u + ahead] = scores(u + ahead, bufs[(u + ahead) % len(bufs)])
        o = consume(u, bufs[u % len(bufs)], *pending.pop(u))
        if g % 2 == 0:
            prev = o
            continue
        if kvh % 2 == 0:
            slab = prev + pltpu.roll(o, SWA_HEAD_DIM, 1)
        else:
            slab = pltpu.roll(prev, SWA_HEAD_DIM, 1) + o
        acc = acc + _dot(slab.astype(BF16), w_ref[kvh * 2 + g // 2])
    o_ref[...] = x_ref[...] + mod_ref[0][2:3] * acc


def _attn_swa(x, mods, row_fn, q, k, v, ctx, sink, w_out, tq, n_bufs):
    b, s, d = x.shape
    windowed = ctx is not None
    band_w = min(tq + 2 * WINDOW, s)
    kv_spec = pl.BlockSpec((None, SWA_KV_HEADS, s, LANES), lambda bi, t: (bi, 0, 0, 0))
    in_specs = [
        pl.BlockSpec((None, tq, d), lambda bi, t: (bi, t, 0)),
        pl.BlockSpec((1, 6, d), lambda bi, t: (row_fn(bi), 0, 0)),
        pl.BlockSpec((None, 8, tq, LANES), lambda bi, t: (bi, 0, t, 0)),
        kv_spec, kv_spec,
    ]
    args = [x, mods, q, k, v]
    if windowed:
        ck, cv, j = ctx
        p = ck.shape[2]
        spec = pl.BlockSpec((None, None, p, 256), lambda bi, t: (bi, j, 0, 0))
        in_specs += [spec, spec]
        args += [ck, cv]
    in_specs += [pl.BlockSpec(memory_space=pltpu.SMEM), _full((8, LANES, d))]
    args += [sink, w_out]
    n_keys = band_w + ctx[0].shape[2] if windowed else s
    return pl.pallas_call(
        functools.partial(_attn_swa_kernel, windowed, band_w),
        grid=(b, s // tq),
        in_specs=in_specs,
        out_specs=pl.BlockSpec((None, tq, d), lambda bi, t: (bi, t, 0)),
        out_shape=jax.ShapeDtypeStruct((b, s, d), F32),
        scratch_shapes=[pltpu.VMEM((tq, n_keys), F32) for _ in range(n_bufs)],
        compiler_params=_params(2),
        name="attn_swa",
    )(*args)


def _route(scores_t, bias):
    sel_t = scores_t + bias
    sel = [sel_t[e:e + 1] for e in range(N_EXPERTS)]
    raw = [scores_t[e:e + 1] for e in range(N_EXPERTS)]
    gscore = []
    for g in range(N_GROUPS):
        r = sel[g * 4:g * 4 + 4]
        pairs = [r[i] + r[j] for i in range(4) for j in range(i + 1, 4)]
        gscore.append(functools.reduce(jnp.maximum, pairs))
    best, gidx = gscore[0], jnp.zeros_like(gscore[0], dtype=jnp.int32)
    for g in range(1, N_GROUPS):
        take = gscore[g] > best
        best = jnp.where(take, gscore[g], best)
        gidx = jnp.where(take, g, gidx)
    vals = []
    for k in range(EXPERTS_PER_GROUP):
        v = sel[k]
        for g in range(1, N_GROUPS):
            v = jnp.where(gidx == g, sel[g * 4 + k], v)
        vals.append(v)

    def argmax4(vs):
        m, idx = vs[0], jnp.zeros_like(gidx)
        for k in range(1, 4):
            take = vs[k] > m
            m = jnp.where(take, vs[k], m)
            idx = jnp.where(take, k, idx)
        return idx

    i1 = argmax4(vals)
    i2 = argmax4([jnp.where(i1 == k, -jnp.inf, vals[k]) for k in range(4)])
    e1, e2 = gidx * 4 + i1, gidx * 4 + i2
    w1 = functools.reduce(lambda a, b: a + b, [jnp.where(e1 == e, raw[e], 0.0) for e in range(N_EXPERTS)])
    w2 = functools.reduce(lambda a, b: a + b, [jnp.where(e2 == e, raw[e], 0.0) for e in range(N_EXPERTS)])
    den = w1 + w2
    return e1, e2, w1 / den, w2 / den


SC_WINDOW = 128
SC_ROW = 128
N_PLANES = (D_MODEL // 2) // SC_ROW
MOE_ROW_BLOCK = 1024
PROMPT_BUFS = 8
HALF_WORD = -65536


def _to_planes(packed, out_ref):
    for j in range(N_PLANES):
        out_ref[j] = packed[:, j * SC_ROW:(j + 1) * SC_ROW]


def _from_planes(ref):
    return jnp.concatenate([ref[j] for j in range(N_PLANES)], axis=1)


def _pack_bf16_pairs(x):
    n = x.shape[1] // 2
    lo = pltpu.bitcast(x[:, :n].astype(BF16).astype(F32), jnp.int32)
    hi = pltpu.bitcast(x[:, n:].astype(BF16).astype(F32), jnp.int32)
    return lax.shift_right_logical(lo, jnp.int32(16)) | (hi & jnp.int32(HALF_WORD))


def _unpack_bf16_pairs(p):
    lo = pltpu.bitcast(lax.shift_left(p, jnp.int32(16)), F32)
    hi = pltpu.bitcast(p & jnp.int32(HALF_WORD), F32)
    return jnp.concatenate([lo, hi], axis=1)


def _pick(idx, rows):
    return _tree(jnp.add, [jnp.where(idx == e, rows[e], 0.0) for e in range(N_EXPERTS)])


def _moe_route_kernel(x_ref, mod_ref, g_ref, rw_ref, rb_ref, tri_ref, hp_o, eidx_o, rank_o, gcol_o, cnt_o, seen):
    @pl.when(jnp.logical_and(pl.program_id(0) == 0, pl.program_id(1) == 0))
    def _():
        seen[...] = jnp.zeros_like(seen)

    tm = x_ref.shape[0]
    mod = mod_ref[0]
    h = _modulate(x_ref[...], g_ref[...], mod[3:4], mod[4:5])
    h16 = h.astype(BF16)
    _to_planes(_pack_bf16_pairs(h), hp_o)
    h_lo = (h - h16.astype(F32)).astype(BF16)
    part = _dot_nt(rw_ref[...], h16)
    logits_t = part[:N_EXPERTS] + part[N_EXPERTS:] + _dot_nt(rw_ref[:N_EXPERTS], h_lo)
    e1, e2, g1, g2 = _route(_sigmoid(logits_t), rb_ref[...])
    hot = jnp.concatenate([jnp.where(jnp.logical_or(e1 == e, e2 == e), 1.0, 0.0) for e in range(N_EXPERTS)], axis=0)
    before = _dot(hot.astype(BF16), tri_ref[...]) + seen[...]
    rows = [before[e:e + 1] for e in range(N_EXPERTS)]
    eidx_o[0], eidx_o[1] = e1, e2
    rank_o[0], rank_o[1] = _pick(e1, rows).astype(jnp.int32), _pick(e2, rows).astype(jnp.int32)
    gcol_o[...] = jnp.concatenate([g1, g2, jnp.zeros((LANES - 2, tm), F32)], axis=0).T
    seen[...] = seen[...] + jnp.sum(hot, axis=-1, keepdims=True)
    cnt_o[...] = seen[...]


def _moe_slots_kernel(mb, n_rows, eidx_ref, rank_ref, cnt_ref, idx_o, be_o):
    cnt = cnt_ref[...]
    padded = jnp.ceil(cnt / mb) * mb
    starts, ends, run = [], [], jnp.zeros((1, 1), F32)
    for e in range(N_EXPERTS):
        starts.append(run)
        run = run + padded[e:e + 1]
        ends.append(run)
    n_tiles, per_tile = eidx_ref.shape[1], eidx_ref.shape[3] // SC_WINDOW
    for k in range(2):
        for i in range(n_tiles):
            slot = rank_ref[k, i] + _pick(eidx_ref[k, i], starts).astype(jnp.int32)
            for j in range(N_PLANES):
                for q in range(per_tile):
                    row = ((k * N_PLANES + j) * n_tiles + i) * per_tile + q
                    idx_o[row:row + 1, :] = slot[:, q * SC_WINDOW:(q + 1) * SC_WINDOW] + j * n_rows
    first_row = lax.broadcasted_iota(jnp.int32, be_o.shape, 1).astype(F32) * mb
    be = _tree(jnp.add, [jnp.where(ends[e] <= first_row, 1.0, 0.0) for e in range(N_EXPERTS)])
    lane = lax.broadcasted_iota(jnp.int32, be_o.shape, 1)
    be = jnp.where(lane == be_o.shape[1] - 1, ends[N_EXPERTS - 1] / mb, jnp.minimum(be, N_EXPERTS - 1.0))
    be_o[...] = be.astype(jnp.int32)


def _moe_ffn_kernel(be_ref, hs_ref, wg_ref, wu_ref, wd_ref, ys_o, wgu16, wd16):
    i = pl.program_id(0)
    live = i < be_ref[be_ref.shape[0] - 1]

    @pl.when(jnp.logical_and(live, jnp.logical_or(i == 0, be_ref[i] != be_ref[jnp.maximum(i - 1, 0)])))
    def _():
        wgu16[:, :D_EXPERT] = wg_ref[...].astype(BF16)
        wgu16[:, D_EXPERT:] = wu_ref[...].astype(BF16)
        wd16[...] = wd_ref[...].astype(BF16)

    @pl.when(live)
    def _():
        h = _unpack_bf16_pairs(_from_planes(hs_ref)).astype(BF16)
        gu = _dot(h, wgu16[...])
        g, u = gu[:, :D_EXPERT], gu[:, D_EXPERT:]
        a = g * _sigmoid(g) * u
        _to_planes(_pack_bf16_pairs(_dot(a.astype(BF16), wd16[...])), ys_o)


def _moe_residual(x, z0_ref, z1_ref, gcol_ref, mod_ref):
    gc = gcol_ref[...]
    lane = lax.broadcasted_iota(jnp.int32, gc.shape, 1)
    g1 = jnp.sum(jnp.where(lane == 0, gc, 0.0), axis=-1, keepdims=True)
    g2 = jnp.sum(jnp.where(lane == 1, gc, 0.0), axis=-1, keepdims=True)
    y = g1 * _unpack_bf16_pairs(_from_planes(z0_ref)) + g2 * _unpack_bf16_pairs(_from_planes(z1_ref))
    return x + mod_ref[0][5:6] * y


def _moe_combine_kernel(x_ref, z0_ref, z1_ref, gcol_ref, mod_ref, fg_ref, o_ref):
    o_ref[...] = _rms(_moe_residual(x_ref[...], z0_ref, z1_ref, gcol_ref, mod_ref), fg_ref[...])


def _residual_specs(resid, row_fn, tm, d, in_specs, args):
    (z, gcol), mods_prev = resid
    for k in range(2):
        in_specs.append(pl.BlockSpec((None, N_PLANES, None, tm, SC_ROW), lambda bi, t, k=k: (k, 0, bi, t, 0)))
        args.append(z)
    in_specs += [pl.BlockSpec((None, tm, LANES), lambda bi, t: (bi, t, 0)),
                 pl.BlockSpec((1, 6, d), lambda bi, t: (row_fn(bi), 0, 0))]
    args += [gcol, mods_prev]


def _sc_mesh():
    return plsc.VectorSubcoreMesh(core_axis_name="core", subcore_axis_name="subcore")


def _sc_scatter_rows(rows, idx, n_out):
    n, w = rows.shape
    steps = n // SC_WINDOW

    @pl.kernel(out_type=jax.ShapeDtypeStruct((n_out, w), rows.dtype), mesh=_sc_mesh(), scratch_types=[])
    def scatter(x_hbm, i0_hbm, i1_hbm, o_hbm):
        def body(x_vmem, i0_vmem, i1_vmem):
            pltpu.sync_copy(x_vmem, o_hbm.at[i0_vmem.at[0]])
            pltpu.sync_copy(x_vmem, o_hbm.at[i1_vmem.at[0]])

        pltpu.emit_pipeline(
            body,
            grid=(steps,),
            in_specs=[pl.BlockSpec((SC_WINDOW, w), lambda i: (i, 0)),
                      pl.BlockSpec((1, SC_WINDOW), lambda i: (i, 0)),
                      pl.BlockSpec((1, SC_WINDOW), lambda i: (i + steps, 0))],
            out_specs=[],
            core_axis_name=("core", "subcore"),
            dimension_semantics=(pltpu.PARALLEL,),
        )(x_hbm, i0_hbm, i1_hbm)

    return scatter(rows, idx, idx)


def _sc_gather_rows(rows, idx):
    steps, w = idx.shape[0], rows.shape[1]

    @pl.kernel(out_type=jax.ShapeDtypeStruct((steps * SC_WINDOW, w), rows.dtype), mesh=_sc_mesh(),
               scratch_types=[])
    def gather(x_hbm, i_hbm, o_hbm):
        def body(i_vmem, o_vmem):
            pltpu.sync_copy(x_hbm.at[i_vmem.at[0]], o_vmem)

        pltpu.emit_pipeline(
            body,
            grid=(steps,),
            in_specs=[pl.BlockSpec((1, SC_WINDOW), lambda i: (i, 0))],
            out_specs=[pl.BlockSpec((SC_WINDOW, w), lambda i: (i, 0))],
            core_axis_name=("core", "subcore"),
            dimension_semantics=(pltpu.PARALLEL,),
        )(i_hbm, o_hbm)

    return gather(rows, idx)


def _moe_bucket(x, mods, row_fn, g, rw_t, rb, tm, mb):
    b, s, d = x.shape
    n, nt = b * s, (b * s) // tm
    per_b = s // tm
    tile = lambda bi, t: (0, bi * per_b + t, 0, 0)
    tri = jnp.asarray(np.triu(np.ones((tm, tm), np.float32), 1), BF16)
    hp, eidx, rank, gcol, cnt = pl.pallas_call(
        _moe_route_kernel,
        grid=(b, per_b),
        in_specs=[
            pl.BlockSpec((None, tm, d), lambda bi, t: (bi, t, 0)),
            pl.BlockSpec((1, 6, d), lambda bi, t: (row_fn(bi), 0, 0)),
            _full((1, d)), _full((2 * N_EXPERTS, d)), _full((N_EXPERTS, 1)), _full((tm, tm)),
        ],
        out_specs=[
            pl.BlockSpec((N_PLANES, tm, SC_ROW), lambda bi, t: (0, bi * per_b + t, 0)),
            pl.BlockSpec((2, None, 1, tm), tile), pl.BlockSpec((2, None, 1, tm), tile),
            pl.BlockSpec((None, tm, LANES), lambda bi, t: (bi, t, 0)),
            _full((N_EXPERTS, 1)),
        ],
        out_shape=[
            jax.ShapeDtypeStruct((N_PLANES, n, SC_ROW), jnp.int32),
            jax.ShapeDtypeStruct((2, nt, 1, tm), jnp.int32), jax.ShapeDtypeStruct((2, nt, 1, tm), jnp.int32),
            jax.ShapeDtypeStruct((b, s, LANES), F32),
            jax.ShapeDtypeStruct((N_EXPERTS, 1), F32),
        ],
        scratch_shapes=[pltpu.VMEM((N_EXPERTS, 1), F32)],
        compiler_params=_params(2),
        name="moe_route",
    )(x, mods, g, rw_t, rb, tri)

    n_rows = 2 * n + N_EXPERTS * mb
    nb = n_rows // mb
    nb_pad = -(-(nb + 1) // LANES) * LANES
    idx, be = pl.pallas_call(
        functools.partial(_moe_slots_kernel, mb, n_rows),
        out_shape=[jax.ShapeDtypeStruct((2 * N_PLANES * n // SC_WINDOW, SC_WINDOW), jnp.int32),
                   jax.ShapeDtypeStruct((1, nb_pad), jnp.int32)],
        name="moe_slots",
    )(eidx, rank, cnt)

    hs = _sc_scatter_rows(hp.reshape(N_PLANES * n, SC_ROW), idx, N_PLANES * n_rows)
    return hs.reshape(N_PLANES, n_rows, SC_ROW), idx, be.reshape(nb_pad), gcol


def _moe_experts(bucket, layer, w_gate, w_up, w_down, mb, out_dims):
    hs, idx, be, gcol = bucket
    n_rows, d = hs.shape[1], w_gate.shape[2]
    last = be.shape[0] - 1
    blk = lambda i, be_r: jnp.minimum(i, be_r[last] - 1)
    plane_block = pl.BlockSpec((N_PLANES, mb, SC_ROW), lambda i, be_r: (0, blk(i, be_r), 0))
    weight = lambda i, be_r: (layer, be_r[blk(i, be_r)], 0, 0)
    ys = pl.pallas_call(
        _moe_ffn_kernel,
        grid_spec=pltpu.PrefetchScalarGridSpec(
            num_scalar_prefetch=1,
            grid=(n_rows // mb,),
            in_specs=[
                plane_block,
                pl.BlockSpec((None, None, d, D_EXPERT), weight),
                pl.BlockSpec((None, None, d, D_EXPERT), weight),
                pl.BlockSpec((None, None, D_EXPERT, d), weight),
            ],
            out_specs=plane_block,
            scratch_shapes=[pltpu.VMEM((d, 2 * D_EXPERT), BF16), pltpu.VMEM((D_EXPERT, d), BF16)],
        ),
        out_shape=jax.ShapeDtypeStruct((N_PLANES, n_rows, SC_ROW), jnp.int32),
        compiler_params=_params(1),
        name="moe_ffn",
    )(be, hs, w_gate, w_up, w_down)
    z = _sc_gather_rows(ys.reshape(N_PLANES * n_rows, SC_ROW), idx)
    return z.reshape(2, N_PLANES, *out_dims, SC_ROW), gcol.reshape(*out_dims, LANES)


def _moe_combine(x, mods, row_fn, gathered, final_g, tm):
    b, s, d = x.shape
    in_specs = [pl.BlockSpec((None, tm, d), lambda bi, t: (bi, t, 0))]
    args = [x]
    _residual_specs((gathered, mods), row_fn, tm, d, in_specs, args)
    in_specs.append(_full((1, d)))
    args.append(final_g)
    return pl.pallas_call(
        _moe_combine_kernel,
        grid=(b, s // tm),
        in_specs=in_specs,
        out_specs=pl.BlockSpec((None, tm, d), lambda bi, t: (bi, t, 0)),
        out_shape=jax.ShapeDtypeStruct((b, s, d), F32),
        compiler_params=_params(2),
        name="moe_combine",
    )(*args)


def _rope_tables(n_tokens, rot_dim, offset):
    n_rows = n_tokens // GRID_W
    rows = np.repeat(np.arange(n_rows), GRID_W)
    cols = np.tile(np.arange(GRID_W), n_rows)
    n_freq = rot_dim // 4
    inv = jnp.asarray(ROPE_THETA, F32) ** (-jnp.arange(n_freq, dtype=F32) / n_freq)
    ang = jnp.concatenate([jnp.asarray(rows, F32)[:, None] * inv, jnp.asarray(cols, F32)[:, None] * inv], axis=-1)
    cos, sin = jnp.repeat(jnp.cos(ang), 2, axis=-1), jnp.repeat(jnp.sin(ang), 2, axis=-1)
    odd = jnp.asarray(np.arange(rot_dim) % 2 == 1)
    parts = [cos, jnp.where(odd, sin, 0.0), jnp.where(odd, 0.0, -sin)]
    period = 64 if rot_dim == 64 else LANES
    fill = [1.0, 0.0, 0.0]
    out = []
    for p, f in zip(parts, fill):
        slot = jnp.full((n_tokens, period), f, F32).at[:, offset:offset + rot_dim].set(p)
        out.append(jnp.tile(slot, (1, LANES // period)))
    return jnp.stack(out)


def _pad_heads(w, n_heads, lo, hi):
    k = w.shape[0]
    w = w.reshape(k, n_heads, -1)[:, :, lo:hi]
    return jnp.pad(w, ((0, 0), (0, 0), (0, LANES - (hi - lo)))).reshape(k, n_heads * LANES)


def _ab_weights(j, ab_w_in, diff_lambda, diff_subln_g, mla_q_norm_g, mla_w_qb, mla_kv_norm_g, mla_w_kvb, ab_w_out):
    place = np.zeros((LANES, MLA_HEADS * LANES), np.float32)
    for hd in range(MLA_HEADS):
        for r in range(MLA_ROPE_DIM):
            place[r, hd * LANES + MLA_NOPE_DIM + r] = 1.0
    w_out_m = ab_w_out[j][512:].reshape(MLA_HEADS, MLA_V_DIM, D_MODEL)
    return {
        "w_in": jnp.pad(ab_w_in[j], ((0, 0), (0, AB_IN_PAD - AB_IN))).astype(BF16),
        "qn_g": mla_q_norm_g[j][None, :],
        "w_qb": _pad_heads(mla_w_qb[j], MLA_HEADS, 0, MLA_NOPE_DIM + MLA_ROPE_DIM).astype(BF16),
        "kvn_g": mla_kv_norm_g[j][None, :],
        "wk": _pad_heads(mla_w_kvb[j], MLA_HEADS, 0, MLA_NOPE_DIM).astype(BF16),
        "wv": _pad_heads(mla_w_kvb[j], MLA_HEADS, MLA_NOPE_DIM, MLA_NOPE_DIM + MLA_V_DIM).astype(BF16),
        "e": jnp.asarray(place, BF16),
        "w_out_d": ab_w_out[j][:512].reshape(DIFF_HEADS, LANES, D_MODEL).astype(BF16),
        "w_out_m": jnp.pad(w_out_m, ((0, 0), (0, LANES - MLA_V_DIM), (0, 0))).astype(BF16),
        "subln_g": diff_subln_g[j][None, :],
        "lam": diff_lambda[j],
    }


def kernel(x_prompt, x_sample, cache_diff_k, cache_diff_v, cache_mla_ckv, cache_mla_krope, cache_swa_k, cache_swa_v,
           c, c_ctx, ada_w, ada_b, norm1_g, norm2_g, final_norm_g, ab_w_in, diff_lambda, diff_subln_g, mla_q_norm_g,
           mla_w_qb, mla_kv_norm_g, mla_w_kvb, ab_w_out, swa_w_in, swa_sink, swa_w_out, router_w, router_bias,
           moe_w_gate, moe_w_up, moe_w_down):
    bp, sp, d = x_prompt.shape
    bs, ss, _ = x_sample.shape
    depth = ada_w.shape[0]
    past = cache_diff_k.shape[2]
    n_ab = cache_diff_k.shape[1]
    n_swa = cache_swa_k.shape[1]
    assert bs + 1 <= COND_ROWS and d == D_MODEL

    cond = jnp.zeros((COND_ROWS, d), F32).at[:bs].set(c).at[bs].set(c_ctx)
    mods = _ada_all(cond, ada_w, ada_b).reshape(depth, COND_ROWS, 6, d)
    row_s = lambda bi: bi
    row_p = lambda bi: bi * 0 + bs

    rope_d = _rope_tables(ss, DIFF_QK_DIM, 0)
    rope_q = _rope_tables(ss, MLA_ROPE_DIM, MLA_NOPE_DIM)
    rope_k = _rope_tables(ss, MLA_ROPE_DIM, 0)

    cdk = cache_diff_k.reshape(bs, n_ab, past, 512)
    cdv = cache_diff_v.reshape(bs, n_ab, past, 512)
    ckr = jnp.pad(cache_mla_krope, ((0, 0), (0, 0), (0, 0), (0, LANES - MLA_ROPE_DIM)))
    csk = cache_swa_k.reshape(bs, n_swa, past, 256)
    csv = cache_swa_v.reshape(bs, n_swa, past, 256)
    rw_hi = router_w.T.astype(BF16)
    rw_t = jnp.concatenate([rw_hi, (router_w.T - rw_hi.astype(F32)).astype(BF16)], axis=0)
    rb = router_bias[:, None]

    tp = min(256, sp)
    ts = min(256, ss)
    tm_s = min(512, ss)
    tm_p = min(512, bp * sp)

    xp, xs = x_prompt, x_sample
    new_ab, new_swa = None, None
    pend_p = pend_s = None

    def project(fn, x, pend, *a):
        outs = list(fn(x, *a, pend))
        return (outs.pop() if pend is not None else x), outs

    for l in range(depth):
        j = l // 2
        ml = mods[l]
        g1, g2 = norm1_g[l][None, :], norm2_g[l][None, :]
        if l % 2 == 0:
            lambda_init = 0.8 - 0.6 * math.exp(-0.3 * l)
            wts = _ab_weights(j, ab_w_in, diff_lambda, diff_subln_g, mla_q_norm_g, mla_w_qb, mla_kv_norm_g,
                              mla_w_kvb, ab_w_out)
            xp, (qa, ka, va, qb, km, vm, *new_ab) = project(_proj_ab, xp, pend_p, ml, row_p, g1, wts, None,
                                                             (j, n_ab, new_ab), tp)
            xp = _attn_ab(xp, ml, row_p, (qa, qb), [(ka, va, km, vm)], wts, lambda_init, tp, PROMPT_BUFS)
            xs, (qa, ka, va, qb, km, vm) = project(_proj_ab, xs, pend_s, ml, row_s, g1, wts,
                                                   (rope_d, rope_q, rope_k), None, tm_s)
            cache_seg = _cache_ab(cdk, cdv, cache_mla_ckv, ckr, j, wts)
            xs = _attn_ab(xs, ml, row_s, (qa, qb), [(ka, va, km, vm), tuple(cache_seg)], wts, lambda_init, ts, 2)
        else:
            w_in = swa_w_in[j].astype(BF16)
            w_out = swa_w_out[j].reshape(8, LANES, d).astype(BF16)
            xp, (q, k, v, *new_swa) = project(_proj_swa, xp, pend_p, ml, row_p, g1, w_in, None,
                                              (j, n_swa, new_swa), tp)
            xp = _attn_swa(xp, ml, row_p, q, k, v, None, swa_sink[j], w_out, tp, PROMPT_BUFS)
            xs, (q, k, v) = project(_proj_swa, xs, pend_s, ml, row_s, g1, w_in, rope_d, None, tm_s)
            xs = _attn_swa(xs, ml, row_s, q, k, v, (csk, csv, j), swa_sink[j], w_out, ts, 2)
        bucket_p = _moe_bucket(xp.reshape(1, bp * sp, d), ml, row_p, g2, rw_t, rb, tm_p, MOE_ROW_BLOCK)
        bucket_s = _moe_bucket(xs, ml, row_s, g2, rw_t, rb, tm_s, MOE_ROW_BLOCK)
        got_p = _moe_experts(bucket_p, l, moe_w_gate, moe_w_up, moe_w_down, MOE_ROW_BLOCK, (bp, sp))
        got_s = _moe_experts(bucket_s, l, moe_w_gate, moe_w_up, moe_w_down, MOE_ROW_BLOCK, (bs, ss))
        pend_p, pend_s = (got_p, ml), (got_s, ml)

    xp = _moe_combine(xp, pend_p[1], row_p, pend_p[0], final_norm_g[None, :], tp)
    xs = _moe_combine(xs, pend_s[1], row_s, pend_s[0], final_norm_g[None, :], ts)

    new_diff_k = new_ab[0].reshape(bp, n_ab, sp, DIFF_HEADS, 2, DIFF_QK_DIM)
    new_diff_v = new_ab[1].reshape(bp, n_ab, sp, DIFF_HEADS, DIFF_V_DIM)
    new_mla_ckv, new_mla_krope = new_ab[2], new_ab[3]
    new_swa_k = new_swa[0].reshape(bp, n_swa, sp, SWA_KV_HEADS, SWA_HEAD_DIM)
    new_swa_v = new_swa[1].reshape(bp, n_swa, sp, SWA_KV_HEADS, SWA_HEAD_DIM)
    return (xp, xs, new_diff_k, new_diff_v, new_mla_ckv, new_mla_krope, new_swa_k, new_swa_v)
```

```python
import functools
import math

import jax
import jax.numpy as jnp
import numpy as np
from jax import lax
from jax.experimental import pallas as pl
from jax.experimental.pallas import tpu as pltpu
from jax.experimental.pallas import tpu_sc as plsc

F32 = jnp.float32
BF16 = jnp.bfloat16

D_MODEL = 1024
GRID_W = 64
ROPE_THETA = 10000.0
NORM_EPS = 1e-6
NEG_INF = -1e30
LOG2E = math.log2(math.e)
LANES = 128

DIFF_HEADS = 4
DIFF_QK_DIM = 64
DIFF_V_DIM = 128
MLA_HEADS = 8
MLA_Q_RANK = 384
MLA_KV_RANK = 256
MLA_NOPE_DIM = 64
MLA_ROPE_DIM = 32
MLA_V_DIM = 64
AB_IN = 3 * 512 + MLA_Q_RANK + MLA_KV_RANK + MLA_ROPE_DIM
AB_IN_PAD = 2304
SWA_HEADS = 16
SWA_KV_HEADS = 4
SWA_GROUP = 4
SWA_HEAD_DIM = 64
WINDOW = 128
N_EXPERTS = 16
N_GROUPS = 4
EXPERTS_PER_GROUP = 4
D_EXPERT = 256
COND_ROWS = 16

VMEM_LIMIT = 56 * 1024 * 1024


def _full(shape):
    n = len(shape)
    return pl.BlockSpec(shape, lambda *_: (0,) * n)


def _params(n_axes):
    return pltpu.CompilerParams(dimension_semantics=("arbitrary",) * n_axes, vmem_limit_bytes=VMEM_LIMIT)


def _sigmoid(x):
    return 1.0 / (1.0 + jnp.exp(-x))


def _rms(x, g):
    return x * lax.rsqrt(jnp.mean(x * x, axis=-1, keepdims=True) + NORM_EPS) * g


def _modulate(x, g, shift, scale):
    return _rms(x, g) * (1.0 + scale) + shift


def _dot(a, b):
    return jnp.dot(a, b, preferred_element_type=F32)


def _dot_nt(a, b):
    return lax.dot_general(a, b, (((1,), (1,)), ((), ())), preferred_element_type=F32)


def _rope(x, tab_ref):
    c, s_odd, s_even = tab_ref[0], tab_ref[1], tab_ref[2]
    out = []
    for i in range(x.shape[1] // LANES):
        xi = x[:, i * LANES:(i + 1) * LANES]
        out.append(xi * c + pltpu.roll(xi, 1, 1) * s_odd + pltpu.roll(xi, LANES - 1, 1) * s_even)
    return out[0] if len(out) == 1 else jnp.concatenate(out, axis=1)


def _tree(op, xs):
    xs = list(xs)
    while len(xs) > 1:
        xs = [op(xs[i], xs[i + 1]) if i + 1 < len(xs) else xs[i] for i in range(0, len(xs), 2)]
    return xs[0]


def _lane_fold(op, x):
    return _tree(op, [x[:, i * LANES:(i + 1) * LANES] for i in range(x.shape[1] // LANES)])


def _ada_kernel(cond_ref, w_ref, b_ref, o_ref):
    c = cond_ref[...]
    a = (c * _sigmoid(c)).astype(BF16)
    o_ref[...] = _dot(a, w_ref[...].astype(BF16)) + b_ref[...]


def _ada_all(cond, ada_w, ada_b):
    depth, d, n = ada_w.shape
    tn = 1536
    return pl.pallas_call(
        _ada_kernel,
        grid=(depth, n // tn),
        in_specs=[
            _full((COND_ROWS, d)),
            pl.BlockSpec((None, d, tn), lambda l, j: (l, 0, j)),
            pl.BlockSpec((None, 1, tn), lambda l, j: (l, 0, j)),
        ],
        out_specs=pl.BlockSpec((None, COND_ROWS, tn), lambda l, j: (l, 0, j)),
        out_shape=jax.ShapeDtypeStruct((depth, COND_ROWS, n), F32),
        compiler_params=_params(2),
        name="ada_mod",
    )(cond, ada_w, ada_b.reshape(depth, 1, n))


def _proj_ab_kernel(rope, emit_cache, n_prev, fused, x_ref, mod_ref, g_ref, w_in_ref, qn_ref, wqb_ref, kvn_ref,
                    wk_ref, wv_ref, e_ref, *rest):
    if rope:
        rope_d, rope_q, rope_k = rest[:3]
        rest = rest[3:]
    x = x_ref[...]
    if fused:
        x = _moe_residual(x, *rest[:4])
        rest[-1][...] = x
        rest = rest[4:]
    rest = rest[n_prev:]
    qa_o, ka_o, va_o, qb_o, km_o, vm_o = rest[:6]
    mod = mod_ref[0]
    h = _modulate(x, g_ref[...], mod[0:1], mod[1:2]).astype(BF16)
    big = _dot(h, w_in_ref[...])
    qa, ka, va = big[:, 0:512], big[:, 512:1024], big[:, 1024:1536]
    q_lat, ckv, kr = big[:, 1536:1920], big[:, 1920:2176], big[:, 2176:2304]
    qb = _dot(_rms(q_lat, qn_ref[...]).astype(BF16), wqb_ref[...])
    ckv_n = _rms(ckv, kvn_ref[...])
    if emit_cache:
        ka32_o, va32_o, ckv32_o, kr32_o = rest[6:10]
        ka32_o[...] = ka
        va32_o[...] = va
        ckv32_o[...] = ckv_n
        kr32_o[...] = kr[:, :MLA_ROPE_DIM]
    if rope:
        qa, ka = _rope(qa, rope_d), _rope(ka, rope_d)
        qb, kr = _rope(qb, rope_q), _rope(kr, rope_k)
    qa = qa * (DIFF_QK_DIM ** -0.5 * LOG2E)
    qb = qb * ((MLA_NOPE_DIM + MLA_ROPE_DIM) ** -0.5 * LOG2E)
    ckv16 = ckv_n.astype(BF16)
    km = _dot(ckv16, wk_ref[...]) + _dot(kr.astype(BF16), e_ref[...])
    vm = _dot(ckv16, wv_ref[...])
    lo = lax.broadcasted_iota(jnp.int32, (qa.shape[0], LANES), 1) < DIFF_QK_DIM
    for hd in range(DIFF_HEADS):
        sl = slice(hd * LANES, (hd + 1) * LANES)
        qa_o[2 * hd] = jnp.where(lo, qa[:, sl], 0.0).astype(BF16)
        qa_o[2 * hd + 1] = jnp.where(lo, 0.0, qa[:, sl]).astype(BF16)
        ka_o[hd] = ka[:, sl].astype(BF16)
        va_o[hd] = va[:, sl].astype(BF16)
    for hd in range(MLA_HEADS):
        sl = slice(hd * LANES, (hd + 1) * LANES)
        qb_o[hd] = qb[:, sl].astype(BF16)
        km_o[hd] = km[:, sl].astype(BF16)
        vm_o[hd] = vm[:, sl].astype(BF16)


def _cache_outputs(cache, widths, b, s, tm, in_specs, args, n_fixed_outs):
    if cache is None:
        return [], {}, 0
    j, n_layers, prev = cache
    outs = [(jax.ShapeDtypeStruct((b, n_layers, s, w), F32),
             pl.BlockSpec((None, None, tm, w), lambda bi, t: (bi, j, t, 0))) for w in widths]
    aliases = {}
    if prev is not None:
        for i, arr in enumerate(prev):
            aliases[len(args)] = n_fixed_outs + i
            in_specs.append(pl.BlockSpec(memory_space=pl.ANY))
            args.append(arr)
    return outs, aliases, len(aliases)


def _proj_ab(x, mods, row_fn, g, wts, rope_tabs, cache, tm, resid=None):
    b, s, d = x.shape
    rope = rope_tabs is not None
    tok = lambda bi, t: (bi, t, 0)
    head = lambda bi, t: (bi, 0, t, 0)
    in_specs = [
        pl.BlockSpec((None, tm, d), tok),
        pl.BlockSpec((1, 6, d), lambda bi, t: (row_fn(bi), 0, 0)),
        _full((1, d)),
        _full((d, AB_IN_PAD)), _full((1, MLA_Q_RANK)), _full((MLA_Q_RANK, 1024)), _full((1, MLA_KV_RANK)),
        _full((MLA_KV_RANK, 1024)), _full((MLA_KV_RANK, 1024)), _full((LANES, 1024)),
    ]
    args = [x, mods, g, wts["w_in"], wts["qn_g"], wts["w_qb"], wts["kvn_g"], wts["wk"], wts["wv"], wts["e"]]
    if rope:
        in_specs += [pl.BlockSpec((3, tm, LANES), lambda bi, t: (0, t, 0))] * 3
        args += list(rope_tabs)
    if resid is not None:
        _residual_specs(resid, row_fn, tm, d, in_specs, args)

    def hm(nh):
        return jax.ShapeDtypeStruct((b, nh, s, LANES), BF16), pl.BlockSpec((None, nh, tm, LANES), head)

    outs = [hm(8), hm(4), hm(4), hm(8), hm(8), hm(8)]
    cache_outs, aliases, n_prev = _cache_outputs(cache, (512, 512, MLA_KV_RANK, MLA_ROPE_DIM), b, s, tm, in_specs,
                                                 args, len(outs))
    outs += cache_outs
    if resid is not None:
        outs.append((jax.ShapeDtypeStruct((b, s, d), F32), pl.BlockSpec((None, tm, d), tok)))
    return pl.pallas_call(
        functools.partial(_proj_ab_kernel, rope, cache is not None, n_prev, resid is not None),
        grid=(b, s // tm),
        in_specs=in_specs,
        out_specs=[o[1] for o in outs],
        out_shape=[o[0] for o in outs],
        input_output_aliases=aliases,
        compiler_params=_params(2),
        name="proj_ab",
    )(*args)


def _cache_ab_kernel(dk_ref, dv_ref, ckv_ref, kr_ref, wk_ref, wv_ref, e_ref, ck_o, cv_o, km_o, vm_o):
    ckv16 = ckv_ref[...].astype(BF16)
    km = _dot(ckv16, wk_ref[...]) + _dot(kr_ref[...].astype(BF16), e_ref[...])
    vm = _dot(ckv16, wv_ref[...])
    for hd in range(DIFF_HEADS):
        sl = slice(hd * LANES, (hd + 1) * LANES)
        ck_o[hd] = dk_ref[:, sl].astype(BF16)
        cv_o[hd] = dv_ref[:, sl].astype(BF16)
    for hd in range(MLA_HEADS):
        sl = slice(hd * LANES, (hd + 1) * LANES)
        km_o[hd] = km[:, sl].astype(BF16)
        vm_o[hd] = vm[:, sl].astype(BF16)


def _cache_ab(cdk, cdv, cckv, ckr, j, wts):
    b, _, p, _ = cdk.shape
    lay = lambda bi: (bi, j, 0, 0)

    def hm(nh):
        return (jax.ShapeDtypeStruct((b, nh, p, LANES), BF16),
                pl.BlockSpec((None, nh, p, LANES), lambda bi: (bi, 0, 0, 0)))

    outs = [hm(4), hm(4), hm(8), hm(8)]
    return pl.pallas_call(
        _cache_ab_kernel,
        grid=(b,),
        in_specs=[
            pl.BlockSpec((None, None, p, 512), lay), pl.BlockSpec((None, None, p, 512), lay),
            pl.BlockSpec((None, None, p, MLA_KV_RANK), lay), pl.BlockSpec((None, None, p, LANES), lay),
            _full((MLA_KV_RANK, 1024)), _full((MLA_KV_RANK, 1024)), _full((LANES, 1024)),
        ],
        out_specs=[o[1] for o in outs],
        out_shape=[o[0] for o in outs],
        compiler_params=_params(1),
        name="cache_ab",
    )(cdk, cdv, cckv, ckr, wts["wk"], wts["wv"], wts["e"])


def _attn_ab_kernel(n_seg, lambda_init, x_ref, mod_ref, qa_ref, qb_ref, *rest):
    segs = [rest[4 * i:4 * i + 4] for i in range(n_seg)]
    wd_ref, wm_ref, sg_ref, lam_ref, o_ref = rest[4 * n_seg:4 * n_seg + 5]
    bufs = rest[4 * n_seg + 5:]
    lp = lam_ref[...]
    lam = (jnp.exp(jnp.sum(lp[0:1] * lp[1:2], axis=-1, keepdims=True))
           - jnp.exp(jnp.sum(lp[2:3] * lp[3:4], axis=-1, keepdims=True)) + lambda_init)
    tq, d = x_ref.shape
    widths = [sg[0].shape[1] for sg in segs]
    offs = [sum(widths[:i]) for i in range(n_seg)]
    units = [("diff", hd) for hd in range(DIFF_HEADS)] + [("mla", hd) for hd in range(MLA_HEADS)]

    def scores(u, buf):
        kind, hd = units[u]
        qs = [qa_ref[2 * hd], qa_ref[2 * hd + 1]] if kind == "diff" else [qb_ref[hd]]
        ks = [sg[0 if kind == "diff" else 2][hd] for sg in segs]
        ms = []
        for i, q in enumerate(qs):
            folds = []
            for k, off, w in zip(ks, offs, widths):
                s = _dot_nt(q, k)
                buf[i, :, off:off + w] = s
                folds.append(_lane_fold(jnp.maximum, s))
            ms.append(jnp.max(_tree(jnp.maximum, folds), axis=-1, keepdims=True))
        return ms

    def exps(buf, i, m):
        e = [jnp.exp2(buf[i, :, off:off + w] - m) for off, w in zip(offs, widths)]
        l = jnp.sum(_tree(jnp.add, [_lane_fold(jnp.add, x) for x in e]), axis=-1, keepdims=True)
        return e, l

    def consume(u, buf, ms, acc):
        kind, hd = units[u]
        if kind == "diff":
            vs = [sg[1][hd] for sg in segs]
            (e1, l1), (e2, l2) = exps(buf, 0, ms[0]), exps(buf, 1, ms[1])
            c1, c2 = 1.0 / l1, lam / l2
            o = _tree(jnp.add, [_dot((a * c1 - b * c2).astype(BF16), v) for a, b, v in zip(e1, e2, vs)])
            od = _rms(o, sg_ref[...]) * (1.0 - lambda_init)
            return acc + _dot(od.astype(BF16), wd_ref[hd])
        vs = [sg[3][hd] for sg in segs]
        e, l = exps(buf, 0, ms[0])
        o = _tree(jnp.add, [_dot(x.astype(BF16), v) for x, v in zip(e, vs)]) / l
        return acc + _dot(o.astype(BF16), wm_ref[hd])

    acc = jnp.zeros((tq, d), F32)
    ahead, pending = len(bufs) - 1, {}
    for u in range(min(ahead, len(units))):
        pending[u] = scores(u, bufs[u % len(bufs)])
    for u in range(len(units)):
        if u + ahead < len(units):
            pending[u + ahead] = scores(u + ahead, bufs[(u + ahead) % len(bufs)])
        acc = consume(u, bufs[u % len(bufs)], pending.pop(u), acc)
    o_ref[...] = x_ref[...] + mod_ref[0][2:3] * acc


def _attn_ab(x, mods, row_fn, q_parts, seg_list, wts, lambda_init, tq, n_bufs):
    b, s, d = x.shape
    qa, qb = q_parts
    in_specs = [
        pl.BlockSpec((None, tq, d), lambda bi, t: (bi, t, 0)),
        pl.BlockSpec((1, 6, d), lambda bi, t: (row_fn(bi), 0, 0)),
        pl.BlockSpec((None, 8, tq, LANES), lambda bi, t: (bi, 0, t, 0)),
        pl.BlockSpec((None, 8, tq, LANES), lambda bi, t: (bi, 0, t, 0)),
    ]
    args = [x, mods, qa, qb]
    kv_mode = dict(pipeline_mode=pl.Buffered(1)) if s // tq > 1 else {}
    for seg in seg_list:
        for arr in seg:
            nh, nk = arr.shape[1], arr.shape[2]
            in_specs.append(pl.BlockSpec((None, nh, nk, LANES), lambda bi, t: (bi, 0, 0, 0), **kv_mode))
            args.append(arr)
    in_specs += [_full((DIFF_HEADS, LANES, d)), _full((MLA_HEADS, LANES, d)), _full((1, LANES)),
                 _full((4, DIFF_QK_DIM))]
    args += [wts["w_out_d"], wts["w_out_m"], wts["subln_g"], wts["lam"]]
    n_keys = sum(seg[0].shape[2] for seg in seg_list)
    return pl.pallas_call(
        functools.partial(_attn_ab_kernel, len(seg_list), lambda_init),
        grid=(b, s // tq),
        in_specs=in_specs,
        out_specs=pl.BlockSpec((None, tq, d), lambda bi, t: (bi, t, 0)),
        out_shape=jax.ShapeDtypeStruct((b, s, d), F32),
        scratch_shapes=[pltpu.VMEM((2, tq, n_keys), F32) for _ in range(n_bufs)],
        compiler_params=_params(2),
        name="attn_ab",
    )(*args)


def _proj_swa_kernel(rope, emit_cache, n_prev, fused, x_ref, mod_ref, g_ref, w_in_ref, *rest):
    if rope:
        rope_d = rest[0]
        rest = rest[1:]
    x = x_ref[...]
    if fused:
        x = _moe_residual(x, *rest[:4])
        rest[-1][...] = x
        rest = rest[4:]
    rest = rest[n_prev:]
    q_o, k_o, v_o = rest[:3]
    mod = mod_ref[0]
    h = _modulate(x, g_ref[...], mod[0:1], mod[1:2]).astype(BF16)
    big = _dot(h, w_in_ref[...])
    q, k, v = big[:, :1024], big[:, 1024:1280], big[:, 1280:1536]
    if emit_cache:
        k32_o, v32_o = rest[3:5]
        k32_o[...] = k
        v32_o[...] = v
    if rope:
        q, k = _rope(q, rope_d), _rope(k, rope_d)
    q = q * (SWA_HEAD_DIM ** -0.5 * LOG2E)
    lo = lax.broadcasted_iota(jnp.int32, (q.shape[0], LANES), 1) < SWA_HEAD_DIM
    for pair in range(2):
        for grp in range(SWA_GROUP):
            ca, cb = (2 * pair) * 2 + grp // 2, (2 * pair + 1) * 2 + grp // 2
            a = q[:, ca * LANES:(ca + 1) * LANES]
            bb = q[:, cb * LANES:(cb + 1) * LANES]
            if grp % 2 == 0:
                bb = pltpu.roll(bb, SWA_HEAD_DIM, 1)
            else:
                a = pltpu.roll(a, SWA_HEAD_DIM, 1)
            q_o[pair * SWA_GROUP + grp] = jnp.where(lo, a, bb).astype(BF16)
    for kvh in range(SWA_KV_HEADS):
        sl = slice((kvh // 2) * LANES, (kvh // 2 + 1) * LANES)
        keep = lo if kvh % 2 == 0 else jnp.logical_not(lo)
        k_o[kvh] = jnp.where(keep, k[:, sl], 0.0).astype(BF16)
        v_o[kvh] = jnp.where(keep, v[:, sl], 0.0).astype(BF16)


def _proj_swa(x, mods, row_fn, g, w_in, rope_tab, cache, tm, resid=None):
    b, s, d = x.shape
    rope = rope_tab is not None
    tok = lambda bi, t: (bi, t, 0)
    head = lambda bi, t: (bi, 0, t, 0)
    in_specs = [
        pl.BlockSpec((None, tm, d), tok),
        pl.BlockSpec((1, 6, d), lambda bi, t: (row_fn(bi), 0, 0)),
        _full((1, d)), _full((d, 1536)),
    ]
    args = [x, mods, g, w_in]
    if rope:
        in_specs.append(pl.BlockSpec((3, tm, LANES), lambda bi, t: (0, t, 0)))
        args.append(rope_tab)
    if resid is not None:
        _residual_specs(resid, row_fn, tm, d, in_specs, args)

    def hm(nh):
        return jax.ShapeDtypeStruct((b, nh, s, LANES), BF16), pl.BlockSpec((None, nh, tm, LANES), head)

    outs = [hm(8), hm(4), hm(4)]
    cache_outs, aliases, n_prev = _cache_outputs(cache, (256, 256), b, s, tm, in_specs, args, len(outs))
    outs += cache_outs
    if resid is not None:
        outs.append((jax.ShapeDtypeStruct((b, s, d), F32), pl.BlockSpec((None, tm, d), tok)))
    return pl.pallas_call(
        functools.partial(_proj_swa_kernel, rope, cache is not None, n_prev, resid is not None),
        grid=(b, s // tm),
        in_specs=in_specs,
        out_specs=[o[1] for o in outs],
        out_shape=[o[0] for o in outs],
        input_output_aliases=aliases,
        compiler_params=_params(2),
        name="proj_swa",
    )(*args)


def _attn_swa_kernel(windowed, band_w, x_ref, mod_ref, q_ref, k_ref, v_ref, *rest):
    if windowed:
        ck_ref, cv_ref = rest[:2]
        rest = rest[2:]
    sink_ref, w_ref, o_ref = rest[:3]
    bufs = rest[3:]
    tq, d = x_ref.shape
    n_keys = k_ref.shape[1]
    lo_k = lax.broadcasted_iota(jnp.int32, (1, LANES), 1) < SWA_HEAD_DIM
    if windowed:
        start = pl.program_id(1) * tq
        bstart = pl.multiple_of(jnp.clip(start - WINDOW, 0, n_keys - band_w), LANES)
        qpos = start + lax.broadcasted_iota(jnp.int32, (tq, band_w), 0)
        kpos = bstart + lax.broadcasted_iota(jnp.int32, (tq, band_w), 1)
        bias = jnp.where(jnp.abs(qpos - kpos) <= WINDOW, 0.0, NEG_INF)

    def keys_values(kvh):
        if not windowed:
            return [k_ref[kvh]], [v_ref[kvh]]
        keep = lo_k if kvh % 2 == 0 else jnp.logical_not(lo_k)
        sl = slice((kvh // 2) * LANES, (kvh // 2 + 1) * LANES)
        kc = jnp.where(keep, ck_ref[:, sl], 0.0).astype(BF16)
        vc = jnp.where(keep, cv_ref[:, sl], 0.0).astype(BF16)
        return ([k_ref[kvh, pl.ds(bstart, band_w), :], kc], [v_ref[kvh, pl.ds(bstart, band_w), :], vc])

    units = [(kvh, g) for kvh in range(SWA_KV_HEADS) for g in range(SWA_GROUP)]

    def scores(u, buf):
        kvh, g = units[u]
        q = q_ref[(kvh // 2) * SWA_GROUP + g]
        sink = sink_ref[kvh * SWA_GROUP + g] * LOG2E
        folds, off = [], 0
        for i, k in enumerate(keys_values(kvh)[0]):
            s = _dot_nt(q, k)
            if windowed and i == 0:
                s = s + bias
            buf[:, off:off + s.shape[1]] = s
            folds.append(_lane_fold(jnp.maximum, s))
            off += s.shape[1]
        m = jnp.maximum(jnp.max(_tree(jnp.maximum, folds), axis=-1, keepdims=True), sink)
        return m, sink

    def consume(u, buf, m, sink):
        vs = keys_values(units[u][0])[1]
        e, off = [], 0
        for v in vs:
            e.append(jnp.exp2(buf[:, off:off + v.shape[0]] - m))
            off += v.shape[0]
        l = jnp.sum(_tree(jnp.add, [_lane_fold(jnp.add, x) for x in e]), axis=-1, keepdims=True) + jnp.exp2(sink - m)
        return _tree(jnp.add, [_dot(x.astype(BF16), v) for x, v in zip(e, vs)]) / l

    acc = jnp.zeros((tq, d), F32)
    ahead, pending = len(bufs) - 1, {}
    for u in range(min(ahead, len(units))):
        pending[u] = scores(u, bufs[u % len(bufs)])
    prev = None
    for u, (kvh, g) in enumerate(units):
        if u + ahead < len(units):
            pending[u + ahead] = scores(u + ahead, bufs[(u + ahead) % len(bufs)])
        o = consume(u, bufs[u % len(bufs)], *pending.pop(u))
        if g % 2 == 0:
            prev = o
            continue
        if kvh % 2 == 0:
            slab = prev + pltpu.roll(o, SWA_HEAD_DIM, 1)
        else:
            slab = pltpu.roll(prev, SWA_HEAD_DIM, 1) + o
        acc = acc + _dot(slab.astype(BF16), w_ref[kvh * 2 + g // 2])
    o_ref[...] = x_ref[...] + mod_ref[0][2:3] * acc


def _attn_swa(x, mods, row_fn, q, k, v, ctx, sink, w_out, tq, n_bufs):
    b, s, d = x.shape
    windowed = ctx is not None
    band_w = min(tq + 2 * WINDOW, s)
    kv_spec = pl.BlockSpec((None, SWA_KV_HEADS, s, LANES), lambda bi, t: (bi, 0, 0, 0))
    in_specs = [
        pl.BlockSpec((None, tq, d), lambda bi, t: (bi, t, 0)),
        pl.BlockSpec((1, 6, d), lambda bi, t: (row_fn(bi), 0, 0)),
        pl.BlockSpec((None, 8, tq, LANES), lambda bi, t: (bi, 0, t, 0)),
        kv_spec, kv_spec,
    ]
    args = [x, mods, q, k, v]
    if windowed:
        ck, cv, j = ctx
        p = ck.shape[2]
        spec = pl.BlockSpec((None, None, p, 256), lambda bi, t: (bi, j, 0, 0))
        in_specs += [spec, spec]
        args += [ck, cv]
    in_specs += [pl.BlockSpec(memory_space=pltpu.SMEM), _full((8, LANES, d))]
    args += [sink, w_out]
    n_keys = band_w + ctx[0].shape[2] if windowed else s
    return pl.pallas_call(
        functools.partial(_attn_swa_kernel, windowed, band_w),
        grid=(b, s // tq),
        in_specs=in_specs,
        out_specs=pl.BlockSpec((None, tq, d), lambda bi, t: (bi, t, 0)),
        out_shape=jax.ShapeDtypeStruct((b, s, d), F32),
        scratch_shapes=[pltpu.VMEM((tq, n_keys), F32) for _ in range(n_bufs)],
        compiler_params=_params(2),
        name="attn_swa",
    )(*args)


def _route(scores_t, bias):
    sel_t = scores_t + bias
    sel = [sel_t[e:e + 1] for e in range(N_EXPERTS)]
    raw = [scores_t[e:e + 1] for e in range(N_EXPERTS)]
    gscore = []
    for g in range(N_GROUPS):
        r = sel[g * 4:g * 4 + 4]
        pairs = [r[i] + r[j] for i in range(4) for j in range(i + 1, 4)]
        gscore.append(functools.reduce(jnp.maximum, pairs))
    best, gidx = gscore[0], jnp.zeros_like(gscore[0], dtype=jnp.int32)
    for g in range(1, N_GROUPS):
        take = gscore[g] > best
        best = jnp.where(take, gscore[g], best)
        gidx = jnp.where(take, g, gidx)
    vals = []
    for k in range(EXPERTS_PER_GROUP):
        v = sel[k]
        for g in range(1, N_GROUPS):
            v = jnp.where(gidx == g, sel[g * 4 + k], v)
        vals.append(v)

    def argmax4(vs):
        m, idx = vs[0], jnp.zeros_like(gidx)
        for k in range(1, 4):
            take = vs[k] > m
            m = jnp.where(take, vs[k], m)
            idx = jnp.where(take, k, idx)
        return idx

    i1 = argmax4(vals)
    i2 = argmax4([jnp.where(i1 == k, -jnp.inf, vals[k]) for k in range(4)])
    e1, e2 = gidx * 4 + i1, gidx * 4 + i2
    w1 = functools.reduce(lambda a, b: a + b, [jnp.where(e1 == e, raw[e], 0.0) for e in range(N_EXPERTS)])
    w2 = functools.reduce(lambda a, b: a + b, [jnp.where(e2 == e, raw[e], 0.0) for e in range(N_EXPERTS)])
    den = w1 + w2
    return e1, e2, w1 / den, w2 / den


SC_WINDOW = 128
SC_ROW = 128
N_PLANES = (D_MODEL // 2) // SC_ROW
MOE_ROW_BLOCK = 1024
PROMPT_BUFS = 8
HALF_WORD = -65536


def _to_planes(packed, out_ref):
    for j in range(N_PLANES):
        out_ref[j] = packed[:, j * SC_ROW:(j + 1) * SC_ROW]


def _from_planes(ref):
    return jnp.concatenate([ref[j] for j in range(N_PLANES)], axis=1)


def _pack_bf16_pairs(x):
    n = x.shape[1] // 2
    lo = pltpu.bitcast(x[:, :n].astype(BF16).astype(F32), jnp.int32)
    hi = pltpu.bitcast(x[:, n:].astype(BF16).astype(F32), jnp.int32)
    return lax.shift_right_logical(lo, jnp.int32(16)) | (hi & jnp.int32(HALF_WORD))


def _unpack_bf16_pairs(p):
    lo = pltpu.bitcast(lax.shift_left(p, jnp.int32(16)), F32)
    hi = pltpu.bitcast(p & jnp.int32(HALF_WORD), F32)
    return jnp.concatenate([lo, hi], axis=1)


def _pick(idx, rows):
    return _tree(jnp.add, [jnp.where(idx == e, rows[e], 0.0) for e in range(N_EXPERTS)])


def _moe_route_kernel(x_ref, mod_ref, g_ref, rw_ref, rb_ref, tri_ref, hp_o, eidx_o, rank_o, gcol_o, cnt_o, seen):
    @pl.when(jnp.logical_and(pl.program_id(0) == 0, pl.program_id(1) == 0))
    def _():
        seen[...] = jnp.zeros_like(seen)

    tm = x_ref.shape[0]
    mod = mod_ref[0]
    h = _modulate(x_ref[...], g_ref[...], mod[3:4], mod[4:5])
    h16 = h.astype(BF16)
    _to_planes(_pack_bf16_pairs(h), hp_o)
    h_lo = (h - h16.astype(F32)).astype(BF16)
    part = _dot_nt(rw_ref[...], h16)
    logits_t = part[:N_EXPERTS] + part[N_EXPERTS:] + _dot_nt(rw_ref[:N_EXPERTS], h_lo)
    e1, e2, g1, g2 = _route(_sigmoid(logits_t), rb_ref[...])
    hot = jnp.concatenate([jnp.where(jnp.logical_or(e1 == e, e2 == e), 1.0, 0.0) for e in range(N_EXPERTS)], axis=0)
    before = _dot(hot.astype(BF16), tri_ref[...]) + seen[...]
    rows = [before[e:e + 1] for e in range(N_EXPERTS)]
    eidx_o[0], eidx_o[1] = e1, e2
    rank_o[0], rank_o[1] = _pick(e1, rows).astype(jnp.int32), _pick(e2, rows).astype(jnp.int32)
    gcol_o[...] = jnp.concatenate([g1, g2, jnp.zeros((LANES - 2, tm), F32)], axis=0).T
    seen[...] = seen[...] + jnp.sum(hot, axis=-1, keepdims=True)
    cnt_o[...] = seen[...]


def _moe_slots_kernel(mb, n_rows, eidx_ref, rank_ref, cnt_ref, idx_o, be_o):
    cnt = cnt_ref[...]
    padded = jnp.ceil(cnt / mb) * mb
    starts, ends, run = [], [], jnp.zeros((1, 1), F32)
    for e in range(N_EXPERTS):
        starts.append(run)
        run = run + padded[e:e + 1]
        ends.append(run)
    n_tiles, per_tile = eidx_ref.shape[1], eidx_ref.shape[3] // SC_WINDOW
    for k in range(2):
        for i in range(n_tiles):
            slot = rank_ref[k, i] + _pick(eidx_ref[k, i], starts).astype(jnp.int32)
            for j in range(N_PLANES):
                for q in range(per_tile):
                    row = ((k * N_PLANES + j) * n_tiles + i) * per_tile + q
                    idx_o[row:row + 1, :] = slot[:, q * SC_WINDOW:(q + 1) * SC_WINDOW] + j * n_rows
    first_row = lax.broadcasted_iota(jnp.int32, be_o.shape, 1).astype(F32) * mb
    be = _tree(jnp.add, [jnp.where(ends[e] <= first_row, 1.0, 0.0) for e in range(N_EXPERTS)])
    lane = lax.broadcasted_iota(jnp.int32, be_o.shape, 1)
    be = jnp.where(lane == be_o.shape[1] - 1, ends[N_EXPERTS - 1] / mb, jnp.minimum(be, N_EXPERTS - 1.0))
    be_o[...] = be.astype(jnp.int32)


def _moe_ffn_kernel(be_ref, hs_ref, wg_ref, wu_ref, wd_ref, ys_o, wgu16, wd16):
    i = pl.program_id(0)
    live = i < be_ref[be_ref.shape[0] - 1]

    @pl.when(jnp.logical_and(live, jnp.logical_or(i == 0, be_ref[i] != be_ref[jnp.maximum(i - 1, 0)])))
    def _():
        wgu16[:, :D_EXPERT] = wg_ref[...].astype(BF16)
        wgu16[:, D_EXPERT:] = wu_ref[...].astype(BF16)
        wd16[...] = wd_ref[...].astype(BF16)

    @pl.when(live)
    def _():
        h = _unpack_bf16_pairs(_from_planes(hs_ref)).astype(BF16)
        gu = _dot(h, wgu16[...])
        g, u = gu[:, :D_EXPERT], gu[:, D_EXPERT:]
        a = g * _sigmoid(g) * u
        _to_planes(_pack_bf16_pairs(_dot(a.astype(BF16), wd16[...])), ys_o)


def _moe_residual(x, z0_ref, z1_ref, gcol_ref, mod_ref):
    gc = gcol_ref[...]
    lane = lax.broadcasted_iota(jnp.int32, gc.shape, 1)
    g1 = jnp.sum(jnp.where(lane == 0, gc, 0.0), axis=-1, keepdims=True)
    g2 = jnp.sum(jnp.where(lane == 1, gc, 0.0), axis=-1, keepdims=True)
    y = g1 * _unpack_bf16_pairs(_from_planes(z0_ref)) + g2 * _unpack_bf16_pairs(_from_planes(z1_ref))
    return x + mod_ref[0][5:6] * y


def _moe_combine_kernel(x_ref, z0_ref, z1_ref, gcol_ref, mod_ref, fg_ref, o_ref):
    o_ref[...] = _rms(_moe_residual(x_ref[...], z0_ref, z1_ref, gcol_ref, mod_ref), fg_ref[...])


def _residual_specs(resid, row_fn, tm, d, in_specs, args):
    (z, gcol), mods_prev = resid
    for k in range(2):
        in_specs.append(pl.BlockSpec((None, N_PLANES, None, tm, SC_ROW), lambda bi, t, k=k: (k, 0, bi, t, 0)))
        args.append(z)
    in_specs += [pl.BlockSpec((None, tm, LANES), lambda bi, t: (bi, t, 0)),
                 pl.BlockSpec((1, 6, d), lambda bi, t: (row_fn(bi), 0, 0))]
    args += [gcol, mods_prev]


def _sc_mesh():
    return plsc.VectorSubcoreMesh(core_axis_name="core", subcore_axis_name="subcore")


def _sc_scatter_rows(rows, idx, n_out):
    n, w = rows.shape
    steps = n // SC_WINDOW

    @pl.kernel(out_type=jax.ShapeDtypeStruct((n_out, w), rows.dtype), mesh=_sc_mesh(), scratch_types=[])
    def scatter(x_hbm, i0_hbm, i1_hbm, o_hbm):
        def body(x_vmem, i0_vmem, i1_vmem):
            pltpu.sync_copy(x_vmem, o_hbm.at[i0_vmem.at[0]])
            pltpu.sync_copy(x_vmem, o_hbm.at[i1_vmem.at[0]])

        pltpu.emit_pipeline(
            body,
            grid=(steps,),
            in_specs=[pl.BlockSpec((SC_WINDOW, w), lambda i: (i, 0)),
                      pl.BlockSpec((1, SC_WINDOW), lambda i: (i, 0)),
                      pl.BlockSpec((1, SC_WINDOW), lambda i: (i + steps, 0))],
            out_specs=[],
            core_axis_name=("core", "subcore"),
            dimension_semantics=(pltpu.PARALLEL,),
        )(x_hbm, i0_hbm, i1_hbm)

    return scatter(rows, idx, idx)


def _sc_gather_rows(rows, idx):
    steps, w = idx.shape[0], rows.shape[1]

    @pl.kernel(out_type=jax.ShapeDtypeStruct((steps * SC_WINDOW, w), rows.dtype), mesh=_sc_mesh(),
               scratch_types=[])
    def gather(x_hbm, i_hbm, o_hbm):
        def body(i_vmem, o_vmem):
            pltpu.sync_copy(x_hbm.at[i_vmem.at[0]], o_vmem)

        pltpu.emit_pipeline(
            body,
            grid=(steps,),
            in_specs=[pl.BlockSpec((1, SC_WINDOW), lambda i: (i, 0))],
            out_specs=[pl.BlockSpec((SC_WINDOW, w), lambda i: (i, 0))],
            core_axis_name=("core", "subcore"),
            dimension_semantics=(pltpu.PARALLEL,),
        )(i_hbm, o_hbm)

    return gather(rows, idx)


def _moe_bucket(x, mods, row_fn, g, rw_t, rb, tm, mb):
    b, s, d = x.shape
    n, nt = b * s, (b * s) // tm
    per_b = s // tm
    tile = lambda bi, t: (0, bi * per_b + t, 0, 0)
    tri = jnp.asarray(np.triu(np.ones((tm, tm), np.float32), 1), BF16)
    hp, eidx, rank, gcol, cnt = pl.pallas_call(
        _moe_route_kernel,
        grid=(b, per_b),
        in_specs=[
            pl.BlockSpec((None, tm, d), lambda bi, t: (bi, t, 0)),
            pl.BlockSpec((1, 6, d), lambda bi, t: (row_fn(bi), 0, 0)),
            _full((1, d)), _full((2 * N_EXPERTS, d)), _full((N_EXPERTS, 1)), _full((tm, tm)),
        ],
        out_specs=[
            pl.BlockSpec((N_PLANES, tm, SC_ROW), lambda bi, t: (0, bi * per_b + t, 0)),
            pl.BlockSpec((2, None, 1, tm), tile), pl.BlockSpec((2, None, 1, tm), tile),
            pl.BlockSpec((None, tm, LANES), lambda bi, t: (bi, t, 0)),
            _full((N_EXPERTS, 1)),
        ],
        out_shape=[
            jax.ShapeDtypeStruct((N_PLANES, n, SC_ROW), jnp.int32),
            jax.ShapeDtypeStruct((2, nt, 1, tm), jnp.int32), jax.ShapeDtypeStruct((2, nt, 1, tm), jnp.int32),
            jax.ShapeDtypeStruct((b, s, LANES), F32),
            jax.ShapeDtypeStruct((N_EXPERTS, 1), F32),
        ],
        scratch_shapes=[pltpu.VMEM((N_EXPERTS, 1), F32)],
        compiler_params=_params(2),
        name="moe_route",
    )(x, mods, g, rw_t, rb, tri)

    n_rows = 2 * n + N_EXPERTS * mb
    nb = n_rows // mb
    nb_pad = -(-(nb + 1) // LANES) * LANES
    idx, be = pl.pallas_call(
        functools.partial(_moe_slots_kernel, mb, n_rows),
        out_shape=[jax.ShapeDtypeStruct((2 * N_PLANES * n // SC_WINDOW, SC_WINDOW), jnp.int32),
                   jax.ShapeDtypeStruct((1, nb_pad), jnp.int32)],
        name="moe_slots",
    )(eidx, rank, cnt)

    hs = _sc_scatter_rows(hp.reshape(N_PLANES * n, SC_ROW), idx, N_PLANES * n_rows)
    return hs.reshape(N_PLANES, n_rows, SC_ROW), idx, be.reshape(nb_pad), gcol


def _moe_experts(bucket, layer, w_gate, w_up, w_down, mb, out_dims):
    hs, idx, be, gcol = bucket
    n_rows, d = hs.shape[1], w_gate.shape[2]
    last = be.shape[0] - 1
    blk = lambda i, be_r: jnp.minimum(i, be_r[last] - 1)
    plane_block = pl.BlockSpec((N_PLANES, mb, SC_ROW), lambda i, be_r: (0, blk(i, be_r), 0))
    weight = lambda i, be_r: (layer, be_r[blk(i, be_r)], 0, 0)
    ys = pl.pallas_call(
        _moe_ffn_kernel,
        grid_spec=pltpu.PrefetchScalarGridSpec(
            num_scalar_prefetch=1,
            grid=(n_rows // mb,),
            in_specs=[
                plane_block,
                pl.BlockSpec((None, None, d, D_EXPERT), weight),
                pl.BlockSpec((None, None, d, D_EXPERT), weight),
                pl.BlockSpec((None, None, D_EXPERT, d), weight),
            ],
            out_specs=plane_block,
            scratch_shapes=[pltpu.VMEM((d, 2 * D_EXPERT), BF16), pltpu.VMEM((D_EXPERT, d), BF16)],
        ),
        out_shape=jax.ShapeDtypeStruct((N_PLANES, n_rows, SC_ROW), jnp.int32),
        compiler_params=_params(1),
        name="moe_ffn",
    )(be, hs, w_gate, w_up, w_down)
    z = _sc_gather_rows(ys.reshape(N_PLANES * n_rows, SC_ROW), idx)
    return z.reshape(2, N_PLANES, *out_dims, SC_ROW), gcol.reshape(*out_dims, LANES)


def _moe_combine(x, mods, row_fn, gathered, final_g, tm):
    b, s, d = x.shape
    in_specs = [pl.BlockSpec((None, tm, d), lambda bi, t: (bi, t, 0))]
    args = [x]
    _residual_specs((gathered, mods), row_fn, tm, d, in_specs, args)
    in_specs.append(_full((1, d)))
    args.append(final_g)
    return pl.pallas_call(
        _moe_combine_kernel,
        grid=(b, s // tm),
        in_specs=in_specs,
        out_specs=pl.BlockSpec((None, tm, d), lambda bi, t: (bi, t, 0)),
        out_shape=jax.ShapeDtypeStruct((b, s, d), F32),
        compiler_params=_params(2),
        name="moe_combine",
    )(*args)


def _rope_tables(n_tokens, rot_dim, offset):
    n_rows = n_tokens // GRID_W
    rows = np.repeat(np.arange(n_rows), GRID_W)
    cols = np.tile(np.arange(GRID_W), n_rows)
    n_freq = rot_dim // 4
    inv = jnp.asarray(ROPE_THETA, F32) ** (-jnp.arange(n_freq, dtype=F32) / n_freq)
    ang = jnp.concatenate([jnp.asarray(rows, F32)[:, None] * inv, jnp.asarray(cols, F32)[:, None] * inv], axis=-1)
    cos, sin = jnp.repeat(jnp.cos(ang), 2, axis=-1), jnp.repeat(jnp.sin(ang), 2, axis=-1)
    odd = jnp.asarray(np.arange(rot_dim) % 2 == 1)
    parts = [cos, jnp.where(odd, sin, 0.0), jnp.where(odd, 0.0, -sin)]
    period = 64 if rot_dim == 64 else LANES
    fill = [1.0, 0.0, 0.0]
    out = []
    for p, f in zip(parts, fill):
        slot = jnp.full((n_tokens, period), f, F32).at[:, offset:offset + rot_dim].set(p)
        out.append(jnp.tile(slot, (1, LANES // period)))
    return jnp.stack(out)


def _pad_heads(w, n_heads, lo, hi):
    k = w.shape[0]
    w = w.reshape(k, n_heads, -1)[:, :, lo:hi]
    return jnp.pad(w, ((0, 0), (0, 0), (0, LANES - (hi - lo)))).reshape(k, n_heads * LANES)


def _ab_weights(j, ab_w_in, diff_lambda, diff_subln_g, mla_q_norm_g, mla_w_qb, mla_kv_norm_g, mla_w_kvb, ab_w_out):
    place = np.zeros((LANES, MLA_HEADS * LANES), np.float32)
    for hd in range(MLA_HEADS):
        for r in range(MLA_ROPE_DIM):
            place[r, hd * LANES + MLA_NOPE_DIM + r] = 1.0
    w_out_m = ab_w_out[j][512:].reshape(MLA_HEADS, MLA_V_DIM, D_MODEL)
    return {
        "w_in": jnp.pad(ab_w_in[j], ((0, 0), (0, AB_IN_PAD - AB_IN))).astype(BF16),
        "qn_g": mla_q_norm_g[j][None, :],
        "w_qb": _pad_heads(mla_w_qb[j], MLA_HEADS, 0, MLA_NOPE_DIM + MLA_ROPE_DIM).astype(BF16),
        "kvn_g": mla_kv_norm_g[j][None, :],
        "wk": _pad_heads(mla_w_kvb[j], MLA_HEADS, 0, MLA_NOPE_DIM).astype(BF16),
        "wv": _pad_heads(mla_w_kvb[j], MLA_HEADS, MLA_NOPE_DIM, MLA_NOPE_DIM + MLA_V_DIM).astype(BF16),
        "e": jnp.asarray(place, BF16),
        "w_out_d": ab_w_out[j][:512].reshape(DIFF_HEADS, LANES, D_MODEL).astype(BF16),
        "w_out_m": jnp.pad(w_out_m, ((0, 0), (0, LANES - MLA_V_DIM), (0, 0))).astype(BF16),
        "subln_g": diff_subln_g[j][None, :],
        "lam": diff_lambda[j],
    }


def kernel(x_prompt, x_sample, cache_diff_k, cache_diff_v, cache_mla_ckv, cache_mla_krope, cache_swa_k, cache_swa_v,
           c, c_ctx, ada_w, ada_b, norm1_g, norm2_g, final_norm_g, ab_w_in, diff_lambda, diff_subln_g, mla_q_norm_g,
           mla_w_qb, mla_kv_norm_g, mla_w_kvb, ab_w_out, swa_w_in, swa_sink, swa_w_out, router_w, router_bias,
           moe_w_gate, moe_w_up, moe_w_down):
    bp, sp, d = x_prompt.shape
    bs, ss, _ = x_sample.shape
    depth = ada_w.shape[0]
    past = cache_diff_k.shape[2]
    n_ab = cache_diff_k.shape[1]
    n_swa = cache_swa_k.shape[1]
    assert bs + 1 <= COND_ROWS and d == D_MODEL

    cond = jnp.zeros((COND_ROWS, d), F32).at[:bs].set(c).at[bs].set(c_ctx)
    mods = _ada_all(cond, ada_w, ada_b).reshape(depth, COND_ROWS, 6, d)
    row_s = lambda bi: bi
    row_p = lambda bi: bi * 0 + bs

    rope_d = _rope_tables(ss, DIFF_QK_DIM, 0)
    rope_q = _rope_tables(ss, MLA_ROPE_DIM, MLA_NOPE_DIM)
    rope_k = _rope_tables(ss, MLA_ROPE_DIM, 0)

    cdk = cache_diff_k.reshape(bs, n_ab, past, 512)
    cdv = cache_diff_v.reshape(bs, n_ab, past, 512)
    ckr = jnp.pad(cache_mla_krope, ((0, 0), (0, 0), (0, 0), (0, LANES - MLA_ROPE_DIM)))
    csk = cache_swa_k.reshape(bs, n_swa, past, 256)
    csv = cache_swa_v.reshape(bs, n_swa, past, 256)
    rw_hi = router_w.T.astype(BF16)
    rw_t = jnp.concatenate([rw_hi, (router_w.T - rw_hi.astype(F32)).astype(BF16)], axis=0)
    rb = router_bias[:, None]

    tp = min(256, sp)
    ts = min(256, ss)
    tm_s = min(512, ss)
    tm_p = min(512, bp * sp)

    xp, xs = x_prompt, x_sample
    new_ab, new_swa = None, None
    pend_p = pend_s = None

    def project(fn, x, pend, *a):
        outs = list(fn(x, *a, pend))
        return (outs.pop() if pend is not None else x), outs

    for l in range(depth):
        j = l // 2
        ml = mods[l]
        g1, g2 = norm1_g[l][None, :], norm2_g[l][None, :]
        if l % 2 == 0:
            lambda_init = 0.8 - 0.6 * math.exp(-0.3 * l)
            wts = _ab_weights(j, ab_w_in, diff_lambda, diff_subln_g, mla_q_norm_g, mla_w_qb, mla_kv_norm_g,
                              mla_w_kvb, ab_w_out)
            xp, (qa, ka, va, qb, km, vm, *new_ab) = project(_proj_ab, xp, pend_p, ml, row_p, g1, wts, None,
                                                             (j, n_ab, new_ab), tp)
            xp = _attn_ab(xp, ml, row_p, (qa, qb), [(ka, va, km, vm)], wts, lambda_init, tp, PROMPT_BUFS)
            xs, (qa, ka, va, qb, km, vm) = project(_proj_ab, xs, pend_s, ml, row_s, g1, wts,
                                                   (rope_d, rope_q, rope_k), None, tm_s)
            cache_seg = _cache_ab(cdk, cdv, cache_mla_ckv, ckr, j, wts)
            xs = _attn_ab(xs, ml, row_s, (qa, qb), [(ka, va, km, vm), tuple(cache_seg)], wts, lambda_init, ts, 3)
        else:
            w_in = swa_w_in[j].astype(BF16)
            w_out = swa_w_out[j].reshape(8, LANES, d).astype(BF16)
            xp, (q, k, v, *new_swa) = project(_proj_swa, xp, pend_p, ml, row_p, g1, w_in, None,
                                              (j, n_swa, new_swa), tp)
            xp = _attn_swa(xp, ml, row_p, q, k, v, None, swa_sink[j], w_out, tp, PROMPT_BUFS)
            xs, (q, k, v) = project(_proj_swa, xs, pend_s, ml, row_s, g1, w_in, rope_d, None, tm_s)
            xs = _attn_swa(xs, ml, row_s, q, k, v, (csk, csv, j), swa_sink[j], w_out, ts, 2)
        bucket_p = _moe_bucket(xp.reshape(1, bp * sp, d), ml, row_p, g2, rw_t, rb, tm_p, MOE_ROW_BLOCK)
        bucket_s = _moe_bucket(xs, ml, row_s, g2, rw_t, rb, tm_s, MOE_ROW_BLOCK)
        got_p = _moe_experts(bucket_p, l, moe_w_gate, moe_w_up, moe_w_down, MOE_ROW_BLOCK, (bp, sp))
        got_s = _moe_experts(bucket_s, l, moe_w_gate, moe_w_up, moe_w_down, MOE_ROW_BLOCK, (bs, ss))
        pend_p, pend_s = (got_p, ml), (got_s, ml)

    xp = _moe_combine(xp, pend_p[1], row_p, pend_p[0], final_norm_g[None, :], tp)
    xs = _moe_combine(xs, pend_s[1], row_s, pend_s[0], final_norm_g[None, :], tm_s)

    new_diff_k = new_ab[0].reshape(bp, n_ab, sp, DIFF_HEADS, 2, DIFF_QK_DIM)
    new_diff_v = new_ab[1].reshape(bp, n_ab, sp, DIFF_HEADS, DIFF_V_DIM)
    new_mla_ckv, new_mla_krope = new_ab[2], new_ab[3]
    new_swa_k = new_swa[0].reshape(bp, n_swa, sp, SWA_KV_HEADS, SWA_HEAD_DIM)
    new_swa_v = new_swa[1].reshape(bp, n_swa, sp, SWA_KV_HEADS, SWA_HEAD_DIM)
    return (xp, xs, new_diff_k, new_diff_v, new_mla_ckv, new_mla_krope, new_swa_k, new_swa_v)
```

```python
import functools
import math

import jax
import jax.numpy as jnp
import numpy as np
from jax import lax
from jax.experimental import pallas as pl
from jax.experimental.pallas import tpu as pltpu
from jax.experimental.pallas import tpu_sc as plsc

F32 = jnp.float32
BF16 = jnp.bfloat16

D_MODEL = 1024
GRID_W = 64
ROPE_THETA = 10000.0
NORM_EPS = 1e-6
NEG_INF = -1e30
LOG2E = math.log2(math.e)
LANES = 128

DIFF_HEADS = 4
DIFF_QK_DIM = 64
DIFF_V_DIM = 128
MLA_HEADS = 8
MLA_Q_RANK = 384
MLA_KV_RANK = 256
MLA_NOPE_DIM = 64
MLA_ROPE_DIM = 32
MLA_V_DIM = 64
AB_IN = 3 * 512 + MLA_Q_RANK + MLA_KV_RANK + MLA_ROPE_DIM
AB_IN_PAD = 2304
SWA_HEADS = 16
SWA_KV_HEADS = 4
SWA_GROUP = 4
SWA_HEAD_DIM = 64
WINDOW = 128
N_EXPERTS = 16
N_GROUPS = 4
EXPERTS_PER_GROUP = 4
D_EXPERT = 256
COND_ROWS = 16

VMEM_LIMIT = 56 * 1024 * 1024


def _full(shape):
    n = len(shape)
    return pl.BlockSpec(shape, lambda *_: (0,) * n)


def _params(n_axes):
    return pltpu.CompilerParams(dimension_semantics=("arbitrary",) * n_axes, vmem_limit_bytes=VMEM_LIMIT)


def _sigmoid(x):
    return 1.0 / (1.0 + jnp.exp(-x))


def _rms(x, g):
    return x * lax.rsqrt(jnp.mean(x * x, axis=-1, keepdims=True) + NORM_EPS) * g


def _modulate(x, g, shift, scale):
    return _rms(x, g) * (1.0 + scale) + shift


def _dot(a, b):
    return jnp.dot(a, b, preferred_element_type=F32)


def _dot_nt(a, b):
    return lax.dot_general(a, b, (((1,), (1,)), ((), ())), preferred_element_type=F32)


def _ones_lane():
    lane = lax.broadcasted_iota(jnp.int32, (1, MLA_HEADS * LANES), 1)
    return jnp.where((lane & (LANES - 1)) == MLA_V_DIM, 1.0, 0.0)


def _rope(x, tab_ref):
    c, s_odd, s_even = tab_ref[0], tab_ref[1], tab_ref[2]
    out = []
    for i in range(x.shape[1] // LANES):
        xi = x[:, i * LANES:(i + 1) * LANES]
        out.append(xi * c + pltpu.roll(xi, 1, 1) * s_odd + pltpu.roll(xi, LANES - 1, 1) * s_even)
    return out[0] if len(out) == 1 else jnp.concatenate(out, axis=1)


def _tree(op, xs):
    xs = list(xs)
    while len(xs) > 1:
        xs = [op(xs[i], xs[i + 1]) if i + 1 < len(xs) else xs[i] for i in range(0, len(xs), 2)]
    return xs[0]


def _lane_fold(op, x):
    return _tree(op, [x[:, i * LANES:(i + 1) * LANES] for i in range(x.shape[1] // LANES)])


def _ada_kernel(cond_ref, w_ref, b_ref, o_ref):
    c = cond_ref[...]
    a = (c * _sigmoid(c)).astype(BF16)
    o_ref[...] = _dot(a, w_ref[...].astype(BF16)) + b_ref[...]


def _ada_all(cond, ada_w, ada_b):
    depth, d, n = ada_w.shape
    tn = 1536
    return pl.pallas_call(
        _ada_kernel,
        grid=(depth, n // tn),
        in_specs=[
            _full((COND_ROWS, d)),
            pl.BlockSpec((None, d, tn), lambda l, j: (l, 0, j)),
            pl.BlockSpec((None, 1, tn), lambda l, j: (l, 0, j)),
        ],
        out_specs=pl.BlockSpec((None, COND_ROWS, tn), lambda l, j: (l, 0, j)),
        out_shape=jax.ShapeDtypeStruct((depth, COND_ROWS, n), F32),
        compiler_params=_params(2),
        name="ada_mod",
    )(cond, ada_w, ada_b.reshape(depth, 1, n))


def _proj_ab_kernel(rope, emit_cache, n_prev, fused, x_ref, mod_ref, g_ref, w_in_ref, qn_ref, wqb_ref, kvn_ref,
                    wk_ref, wv_ref, e_ref, *rest):
    if rope:
        rope_d, rope_q, rope_k = rest[:3]
        rest = rest[3:]
    x = x_ref[...]
    if fused:
        x = _moe_residual(x, *rest[:4])
        rest[-1][...] = x
        rest = rest[4:]
    rest = rest[n_prev:]
    qa_o, ka_o, va_o, qb_o, km_o, vm_o = rest[:6]
    mod = mod_ref[0]
    h = _modulate(x, g_ref[...], mod[0:1], mod[1:2]).astype(BF16)
    big = _dot(h, w_in_ref[...])
    qa, ka, va = big[:, 0:512], big[:, 512:1024], big[:, 1024:1536]
    q_lat, ckv, kr = big[:, 1536:1920], big[:, 1920:2176], big[:, 2176:2304]
    qb = _dot(_rms(q_lat, qn_ref[...]).astype(BF16), wqb_ref[...])
    ckv_n = _rms(ckv, kvn_ref[...])
    if emit_cache:
        ka32_o, va32_o, ckv32_o, kr32_o = rest[6:10]
        ka32_o[...] = ka
        va32_o[...] = va
        ckv32_o[...] = ckv_n
        kr32_o[...] = kr[:, :MLA_ROPE_DIM]
    if rope:
        qa, ka = _rope(qa, rope_d), _rope(ka, rope_d)
        qb, kr = _rope(qb, rope_q), _rope(kr, rope_k)
    qa = qa * (DIFF_QK_DIM ** -0.5 * LOG2E)
    qb = qb * ((MLA_NOPE_DIM + MLA_ROPE_DIM) ** -0.5 * LOG2E)
    ckv16 = ckv_n.astype(BF16)
    km = _dot(ckv16, wk_ref[...]) + _dot(kr.astype(BF16), e_ref[...])
    vm = _dot(ckv16, wv_ref[...]) + _ones_lane()
    lo = lax.broadcasted_iota(jnp.int32, (qa.shape[0], LANES), 1) < DIFF_QK_DIM
    for hd in range(DIFF_HEADS):
        sl = slice(hd * LANES, (hd + 1) * LANES)
        qa_o[2 * hd] = jnp.where(lo, qa[:, sl], 0.0).astype(BF16)
        qa_o[2 * hd + 1] = jnp.where(lo, 0.0, qa[:, sl]).astype(BF16)
        ka_o[hd] = ka[:, sl].astype(BF16)
        va_o[hd] = va[:, sl].astype(BF16)
    for hd in range(MLA_HEADS):
        sl = slice(hd * LANES, (hd + 1) * LANES)
        qb_o[hd] = qb[:, sl].astype(BF16)
        km_o[hd] = km[:, sl].astype(BF16)
        vm_o[hd] = vm[:, sl].astype(BF16)


def _cache_outputs(cache, widths, b, s, tm, in_specs, args, n_fixed_outs):
    if cache is None:
        return [], {}, 0
    j, n_layers, prev = cache
    outs = [(jax.ShapeDtypeStruct((b, n_layers, s, w), F32),
             pl.BlockSpec((None, None, tm, w), lambda bi, t: (bi, j, t, 0))) for w in widths]
    aliases = {}
    if prev is not None:
        for i, arr in enumerate(prev):
            aliases[len(args)] = n_fixed_outs + i
            in_specs.append(pl.BlockSpec(memory_space=pl.ANY))
            args.append(arr)
    return outs, aliases, len(aliases)


def _proj_ab(x, mods, row_fn, g, wts, rope_tabs, cache, tm, resid=None):
    b, s, d = x.shape
    rope = rope_tabs is not None
    tok = lambda bi, t: (bi, t, 0)
    head = lambda bi, t: (bi, 0, t, 0)
    in_specs = [
        pl.BlockSpec((None, tm, d), tok),
        pl.BlockSpec((1, 6, d), lambda bi, t: (row_fn(bi), 0, 0)),
        _full((1, d)),
        _full((d, AB_IN_PAD)), _full((1, MLA_Q_RANK)), _full((MLA_Q_RANK, 1024)), _full((1, MLA_KV_RANK)),
        _full((MLA_KV_RANK, 1024)), _full((MLA_KV_RANK, 1024)), _full((LANES, 1024)),
    ]
    args = [x, mods, g, wts["w_in"], wts["qn_g"], wts["w_qb"], wts["kvn_g"], wts["wk"], wts["wv"], wts["e"]]
    if rope:
        in_specs += [pl.BlockSpec((3, tm, LANES), lambda bi, t: (0, t, 0))] * 3
        args += list(rope_tabs)
    if resid is not None:
        _residual_specs(resid, row_fn, tm, d, in_specs, args)

    def hm(nh):
        return jax.ShapeDtypeStruct((b, nh, s, LANES), BF16), pl.BlockSpec((None, nh, tm, LANES), head)

    outs = [hm(8), hm(4), hm(4), hm(8), hm(8), hm(8)]
    cache_outs, aliases, n_prev = _cache_outputs(cache, (512, 512, MLA_KV_RANK, MLA_ROPE_DIM), b, s, tm, in_specs,
                                                 args, len(outs))
    outs += cache_outs
    if resid is not None:
        outs.append((jax.ShapeDtypeStruct((b, s, d), F32), pl.BlockSpec((None, tm, d), tok)))
    return pl.pallas_call(
        functools.partial(_proj_ab_kernel, rope, cache is not None, n_prev, resid is not None),
        grid=(b, s // tm),
        in_specs=in_specs,
        out_specs=[o[1] for o in outs],
        out_shape=[o[0] for o in outs],
        input_output_aliases=aliases,
        compiler_params=_params(2),
        name="proj_ab",
    )(*args)


def _cache_ab_kernel(dk_ref, dv_ref, ckv_ref, kr_ref, wk_ref, wv_ref, e_ref, ck_o, cv_o, km_o, vm_o):
    ckv16 = ckv_ref[...].astype(BF16)
    km = _dot(ckv16, wk_ref[...]) + _dot(kr_ref[...].astype(BF16), e_ref[...])
    vm = _dot(ckv16, wv_ref[...]) + _ones_lane()
    for hd in range(DIFF_HEADS):
        sl = slice(hd * LANES, (hd + 1) * LANES)
        ck_o[hd] = dk_ref[:, sl].astype(BF16)
        cv_o[hd] = dv_ref[:, sl].astype(BF16)
    for hd in range(MLA_HEADS):
        sl = slice(hd * LANES, (hd + 1) * LANES)
        km_o[hd] = km[:, sl].astype(BF16)
        vm_o[hd] = vm[:, sl].astype(BF16)


def _cache_ab(cdk, cdv, cckv, ckr, j, wts):
    b, _, p, _ = cdk.shape
    lay = lambda bi: (bi, j, 0, 0)

    def hm(nh):
        return (jax.ShapeDtypeStruct((b, nh, p, LANES), BF16),
                pl.BlockSpec((None, nh, p, LANES), lambda bi: (bi, 0, 0, 0)))

    outs = [hm(4), hm(4), hm(8), hm(8)]
    return pl.pallas_call(
        _cache_ab_kernel,
        grid=(b,),
        in_specs=[
            pl.BlockSpec((None, None, p, 512), lay), pl.BlockSpec((None, None, p, 512), lay),
            pl.BlockSpec((None, None, p, MLA_KV_RANK), lay), pl.BlockSpec((None, None, p, LANES), lay),
            _full((MLA_KV_RANK, 1024)), _full((MLA_KV_RANK, 1024)), _full((LANES, 1024)),
        ],
        out_specs=[o[1] for o in outs],
        out_shape=[o[0] for o in outs],
        compiler_params=_params(1),
        name="cache_ab",
    )(cdk, cdv, cckv, ckr, wts["wk"], wts["wv"], wts["e"])


def _attn_ab_kernel(n_seg, lambda_init, x_ref, mod_ref, qa_ref, qb_ref, *rest):
    segs = [rest[4 * i:4 * i + 4] for i in range(n_seg)]
    wd_ref, wm_ref, sg_ref, lam_ref, o_ref = rest[4 * n_seg:4 * n_seg + 5]
    bufs = rest[4 * n_seg + 5:]
    lp = lam_ref[...]
    lam = (jnp.exp(jnp.sum(lp[0:1] * lp[1:2], axis=-1, keepdims=True))
           - jnp.exp(jnp.sum(lp[2:3] * lp[3:4], axis=-1, keepdims=True)) + lambda_init)
    tq, d = x_ref.shape
    widths = [sg[0].shape[1] for sg in segs]
    offs = [sum(widths[:i]) for i in range(n_seg)]
    units = [("diff", hd) for hd in range(DIFF_HEADS)] + [("mla", hd) for hd in range(MLA_HEADS)]

    def scores(u, buf):
        kind, hd = units[u]
        qs = [qa_ref[2 * hd], qa_ref[2 * hd + 1]] if kind == "diff" else [qb_ref[hd]]
        ks = [sg[0 if kind == "diff" else 2][hd] for sg in segs]
        ms = []
        for i, q in enumerate(qs):
            folds = []
            for k, off, w in zip(ks, offs, widths):
                s = _dot_nt(q, k)
                buf[i, :, off:off + w] = s
                folds.append(_lane_fold(jnp.maximum, s))
            ms.append(jnp.max(_tree(jnp.maximum, folds), axis=-1, keepdims=True))
        return ms

    def exps(buf, i, m):
        e = [jnp.exp2(buf[i, :, off:off + w] - m) for off, w in zip(offs, widths)]
        l = jnp.sum(_tree(jnp.add, [_lane_fold(jnp.add, x) for x in e]), axis=-1, keepdims=True)
        return e, l

    def consume(u, buf, ms, acc):
        kind, hd = units[u]
        if kind == "diff":
            vs = [sg[1][hd] for sg in segs]
            (e1, l1), (e2, l2) = exps(buf, 0, ms[0]), exps(buf, 1, ms[1])
            c1, c2 = 1.0 / l1, lam / l2
            o = _tree(jnp.add, [_dot((a * c1 - b * c2).astype(BF16), v) for a, b, v in zip(e1, e2, vs)])
            od = _rms(o, sg_ref[...]) * (1.0 - lambda_init)
            return acc + _dot(od.astype(BF16), wd_ref[hd])
        vs = [sg[3][hd] for sg in segs]
        e = [jnp.exp2(buf[0, :, off:off + w] - ms[0]) for off, w in zip(offs, widths)]
        o = _tree(jnp.add, [_dot(x.astype(BF16), v) for x, v in zip(e, vs)])
        lane = lax.broadcasted_iota(jnp.int32, o.shape, 1)
        o = o / jnp.sum(jnp.where(lane == MLA_V_DIM, o, 0.0), axis=-1, keepdims=True)
        return acc + _dot(o.astype(BF16), wm_ref[hd])

    acc = jnp.zeros((tq, d), F32)
    ahead, pending = len(bufs) - 1, {}
    for u in range(min(ahead, len(units))):
        pending[u] = scores(u, bufs[u % len(bufs)])
    for u in range(len(units)):
        if u + ahead < len(units):
            pending[u + ahead] = scores(u + ahead, bufs[(u + ahead) % len(bufs)])
        acc = consume(u, bufs[u % len(bufs)], pending.pop(u), acc)
    o_ref[...] = x_ref[...] + mod_ref[0][2:3] * acc


def _attn_ab(x, mods, row_fn, q_parts, seg_list, wts, lambda_init, tq, n_bufs):
    b, s, d = x.shape
    qa, qb = q_parts
    in_specs = [
        pl.BlockSpec((None, tq, d), lambda bi, t: (bi, t, 0)),
        pl.BlockSpec((1, 6, d), lambda bi, t: (row_fn(bi), 0, 0)),
        pl.BlockSpec((None, 8, tq, LANES), lambda bi, t: (bi, 0, t, 0)),
        pl.BlockSpec((None, 8, tq, LANES), lambda bi, t: (bi, 0, t, 0)),
    ]
    args = [x, mods, qa, qb]
    kv_mode = dict(pipeline_mode=pl.Buffered(1)) if s // tq > 1 else {}
    for seg in seg_list:
        for arr in seg:
            nh, nk = arr.shape[1], arr.shape[2]
            in_specs.append(pl.BlockSpec((None, nh, nk, LANES), lambda bi, t: (bi, 0, 0, 0), **kv_mode))
            args.append(arr)
    in_specs += [_full((DIFF_HEADS, LANES, d)), _full((MLA_HEADS, LANES, d)), _full((1, LANES)),
                 _full((4, DIFF_QK_DIM))]
    args += [wts["w_out_d"], wts["w_out_m"], wts["subln_g"], wts["lam"]]
    n_keys = sum(seg[0].shape[2] for seg in seg_list)
    return pl.pallas_call(
        functools.partial(_attn_ab_kernel, len(seg_list), lambda_init),
        grid=(b, s // tq),
        in_specs=in_specs,
        out_specs=pl.BlockSpec((None, tq, d), lambda bi, t: (bi, t, 0)),
        out_shape=jax.ShapeDtypeStruct((b, s, d), F32),
        scratch_shapes=[pltpu.VMEM((2, tq, n_keys), F32) for _ in range(n_bufs)],
        compiler_params=_params(2),
        name="attn_ab",
    )(*args)


def _proj_swa_kernel(rope, emit_cache, n_prev, fused, x_ref, mod_ref, g_ref, w_in_ref, *rest):
    if rope:
        rope_d = rest[0]
        rest = rest[1:]
    x = x_ref[...]
    if fused:
        x = _moe_residual(x, *rest[:4])
        rest[-1][...] = x
        rest = rest[4:]
    rest = rest[n_prev:]
    q_o, k_o, v_o = rest[:3]
    mod = mod_ref[0]
    h = _modulate(x, g_ref[...], mod[0:1], mod[1:2]).astype(BF16)
    big = _dot(h, w_in_ref[...])
    q, k, v = big[:, :1024], big[:, 1024:1280], big[:, 1280:1536]
    if emit_cache:
        k32_o, v32_o = rest[3:5]
        k32_o[...] = k
        v32_o[...] = v
    if rope:
        q, k = _rope(q, rope_d), _rope(k, rope_d)
    q = q * (SWA_HEAD_DIM ** -0.5 * LOG2E)
    lo = lax.broadcasted_iota(jnp.int32, (q.shape[0], LANES), 1) < SWA_HEAD_DIM
    for pair in range(2):
        for grp in range(SWA_GROUP):
            ca, cb = (2 * pair) * 2 + grp // 2, (2 * pair + 1) * 2 + grp // 2
            a = q[:, ca * LANES:(ca + 1) * LANES]
            bb = q[:, cb * LANES:(cb + 1) * LANES]
            if grp % 2 == 0:
                bb = pltpu.roll(bb, SWA_HEAD_DIM, 1)
            else:
                a = pltpu.roll(a, SWA_HEAD_DIM, 1)
            q_o[pair * SWA_GROUP + grp] = jnp.where(lo, a, bb).astype(BF16)
    for kvh in range(SWA_KV_HEADS):
        sl = slice((kvh // 2) * LANES, (kvh // 2 + 1) * LANES)
        keep = lo if kvh % 2 == 0 else jnp.logical_not(lo)
        k_o[kvh] = jnp.where(keep, k[:, sl], 0.0).astype(BF16)
        v_o[kvh] = jnp.where(keep, v[:, sl], 0.0).astype(BF16)


def _proj_swa(x, mods, row_fn, g, w_in, rope_tab, cache, tm, resid=None):
    b, s, d = x.shape
    rope = rope_tab is not None
    tok = lambda bi, t: (bi, t, 0)
    head = lambda bi, t: (bi, 0, t, 0)
    in_specs = [
        pl.BlockSpec((None, tm, d), tok),
        pl.BlockSpec((1, 6, d), lambda bi, t: (row_fn(bi), 0, 0)),
        _full((1, d)), _full((d, 1536)),
    ]
    args = [x, mods, g, w_in]
    if rope:
        in_specs.append(pl.BlockSpec((3, tm, LANES), lambda bi, t: (0, t, 0)))
        args.append(rope_tab)
    if resid is not None:
        _residual_specs(resid, row_fn, tm, d, in_specs, args)

    def hm(nh):
        return jax.ShapeDtypeStruct((b, nh, s, LANES), BF16), pl.BlockSpec((None, nh, tm, LANES), head)

    outs = [hm(8), hm(4), hm(4)]
    cache_outs, aliases, n_prev = _cache_outputs(cache, (256, 256), b, s, tm, in_specs, args, len(outs))
    outs += cache_outs
    if resid is not None:
        outs.append((jax.ShapeDtypeStruct((b, s, d), F32), pl.BlockSpec((None, tm, d), tok)))
    return pl.pallas_call(
        functools.partial(_proj_swa_kernel, rope, cache is not None, n_prev, resid is not None),
        grid=(b, s // tm),
        in_specs=in_specs,
        out_specs=[o[1] for o in outs],
        out_shape=[o[0] for o in outs],
        input_output_aliases=aliases,
        compiler_params=_params(2),
        name="proj_swa",
    )(*args)


def _attn_swa_kernel(windowed, band_w, x_ref, mod_ref, q_ref, k_ref, v_ref, *rest):
    if windowed:
        ck_ref, cv_ref = rest[:2]
        rest = rest[2:]
    sink_ref, w_ref, o_ref = rest[:3]
    bufs = rest[3:]
    tq, d = x_ref.shape
    n_keys = k_ref.shape[1]
    lo_k = lax.broadcasted_iota(jnp.int32, (1, LANES), 1) < SWA_HEAD_DIM
    if windowed:
        start = pl.program_id(1) * tq
        bstart = pl.multiple_of(jnp.clip(start - WINDOW, 0, n_keys - band_w), LANES)
        qpos = start + lax.broadcasted_iota(jnp.int32, (tq, band_w), 0)
        kpos = bstart + lax.broadcasted_iota(jnp.int32, (tq, band_w), 1)
        bias = jnp.where(jnp.abs(qpos - kpos) <= WINDOW, 0.0, NEG_INF)

    def keys_values(kvh):
        if not windowed:
            return [k_ref[kvh]], [v_ref[kvh]]
        keep = lo_k if kvh % 2 == 0 else jnp.logical_not(lo_k)
        sl = slice((kvh // 2) * LANES, (kvh // 2 + 1) * LANES)
        kc = jnp.where(keep, ck_ref[:, sl], 0.0).astype(BF16)
        vc = jnp.where(keep, cv_ref[:, sl], 0.0).astype(BF16)
        return ([k_ref[kvh, pl.ds(bstart, band_w), :], kc], [v_ref[kvh, pl.ds(bstart, band_w), :], vc])

    units = [(kvh, g) for kvh in range(SWA_KV_HEADS) for g in range(SWA_GROUP)]

    def scores(u, buf):
        kvh, g = units[u]
        q = q_ref[(kvh // 2) * SWA_GROUP + g]
        sink = sink_ref[kvh * SWA_GROUP + g] * LOG2E
        folds, off = [], 0
        for i, k in enumerate(keys_values(kvh)[0]):
            s = _dot_nt(q, k)
            if windowed and i == 0:
                s = s + bias
            buf[:, off:off + s.shape[1]] = s
            folds.append(_lane_fold(jnp.maximum, s))
            off += s.shape[1]
        m = jnp.maximum(jnp.max(_tree(jnp.maximum, folds), axis=-1, keepdims=True), sink)
        return m, sink

    def consume(u, buf, m, sink):
        vs = keys_values(units[u][0])[1]
        e, off = [], 0
        for v in vs:
            e.append(jnp.exp2(buf[:, off:off + v.shape[0]] - m))
            off += v.shape[0]
        l = jnp.sum(_tree(jnp.add, [_lane_fold(jnp.add, x) for x in e]), axis=-1, keepdims=True) + jnp.exp2(sink - m)
        return _tree(jnp.add, [_dot(x.astype(BF16), v) for x, v in zip(e, vs)]) / l

    acc = jnp.zeros((tq, d), F32)
    ahead, pending = len(bufs) - 1, {}
    for u in range(min(ahead, len(units))):
        pending[u] = scores(u, bufs[u % len(bufs)])
    prev = None
    for u, (kvh, g) in enumerate(units):
        if u + ahead < len(units):
            pending[u + ahead] = scores(u + ahead, bufs[(u + ahead) % len(bufs)])
        o = consume(u, bufs[u % len(bufs)], *pending.pop(u))
        if g % 2 == 0:
            prev = o
            continue
        if kvh % 2 == 0:
            slab = prev + pltpu.roll(o, SWA_HEAD_DIM, 1)
        else:
            slab = pltpu.roll(prev, SWA_HEAD_DIM, 1) + o
        acc = acc + _dot(slab.astype(BF16), w_ref[kvh * 2 + g // 2])
    o_ref[...] = x_ref[...] + mod_ref[0][2:3] * acc


def _attn_swa(x, mods, row_fn, q, k, v, ctx, sink, w_out, tq, n_bufs):
    b, s, d = x.shape
    windowed = ctx is not None
    band_w = min(tq + 2 * WINDOW, s)
    kv_spec = pl.BlockSpec((None, SWA_KV_HEADS, s, LANES), lambda bi, t: (bi, 0, 0, 0))
    in_specs = [
        pl.BlockSpec((None, tq, d), lambda bi, t: (bi, t, 0)),
        pl.BlockSpec((1, 6, d), lambda bi, t: (row_fn(bi), 0, 0)),
        pl.BlockSpec((None, 8, tq, LANES), lambda bi, t: (bi, 0, t, 0)),
        kv_spec, kv_spec,
    ]
    args = [x, mods, q, k, v]
    if windowed:
        ck, cv, j = ctx
        p = ck.shape[2]
        spec = pl.BlockSpec((None, None, p, 256), lambda bi, t: (bi, j, 0, 0))
        in_specs += [spec, spec]
        args += [ck, cv]
    in_specs += [pl.BlockSpec(memory_space=pltpu.SMEM), _full((8, LANES, d))]
    args += [sink, w_out]
    n_keys = band_w + ctx[0].shape[2] if windowed else s
    return pl.pallas_call(
        functools.partial(_attn_swa_kernel, windowed, band_w),
        grid=(b, s // tq),
        in_specs=in_specs,
        out_specs=pl.BlockSpec((None, tq, d), lambda bi, t: (bi, t, 0)),
        out_shape=jax.ShapeDtypeStruct((b, s, d), F32),
        scratch_shapes=[pltpu.VMEM((tq, n_keys), F32) for _ in range(n_bufs)],
        compiler_params=_params(2),
        name="attn_swa",
    )(*args)


def _route(scores_t, bias):
    sel_t = scores_t + bias
    sel = [sel_t[e:e + 1] for e in range(N_EXPERTS)]
    raw = [scores_t[e:e + 1] for e in range(N_EXPERTS)]
    gscore = []
    for g in range(N_GROUPS):
        r = sel[g * 4:g * 4 + 4]
        pairs = [r[i] + r[j] for i in range(4) for j in range(i + 1, 4)]
        gscore.append(functools.reduce(jnp.maximum, pairs))
    best, gidx = gscore[0], jnp.zeros_like(gscore[0], dtype=jnp.int32)
    for g in range(1, N_GROUPS):
        take = gscore[g] > best
        best = jnp.where(take, gscore[g], best)
        gidx = jnp.where(take, g, gidx)
    vals = []
    for k in range(EXPERTS_PER_GROUP):
        v = sel[k]
        for g in range(1, N_GROUPS):
            v = jnp.where(gidx == g, sel[g * 4 + k], v)
        vals.append(v)

    def argmax4(vs):
        m, idx = vs[0], jnp.zeros_like(gidx)
        for k in range(1, 4):
            take = vs[k] > m
            m = jnp.where(take, vs[k], m)
            idx = jnp.where(take, k, idx)
        return idx

    i1 = argmax4(vals)
    i2 = argmax4([jnp.where(i1 == k, -jnp.inf, vals[k]) for k in range(4)])
    e1, e2 = gidx * 4 + i1, gidx * 4 + i2
    w1 = functools.reduce(lambda a, b: a + b, [jnp.where(e1 == e, raw[e], 0.0) for e in range(N_EXPERTS)])
    w2 = functools.reduce(lambda a, b: a + b, [jnp.where(e2 == e, raw[e], 0.0) for e in range(N_EXPERTS)])
    den = w1 + w2
    return e1, e2, w1 / den, w2 / den


SC_WINDOW = 128
SC_ROW = 128
N_PLANES = (D_MODEL // 2) // SC_ROW
MOE_ROW_BLOCK = 1024
PROMPT_BUFS = 8
HALF_WORD = -65536


def _to_planes(packed, out_ref):
    for j in range(N_PLANES):
        out_ref[j] = packed[:, j * SC_ROW:(j + 1) * SC_ROW]


def _from_planes(ref):
    return jnp.concatenate([ref[j] for j in range(N_PLANES)], axis=1)


def _pack_bf16_pairs(x):
    n = x.shape[1] // 2
    lo = pltpu.bitcast(x[:, :n].astype(BF16).astype(F32), jnp.int32)
    hi = pltpu.bitcast(x[:, n:].astype(BF16).astype(F32), jnp.int32)
    return lax.shift_right_logical(lo, jnp.int32(16)) | (hi & jnp.int32(HALF_WORD))


def _unpack_bf16_pairs(p):
    lo = pltpu.bitcast(lax.shift_left(p, jnp.int32(16)), F32)
    hi = pltpu.bitcast(p & jnp.int32(HALF_WORD), F32)
    return jnp.concatenate([lo, hi], axis=1)


def _pick(idx, rows):
    return _tree(jnp.add, [jnp.where(idx == e, rows[e], 0.0) for e in range(N_EXPERTS)])


def _moe_route_kernel(x_ref, mod_ref, g_ref, rw_ref, rb_ref, tri_ref, hp_o, eidx_o, rank_o, gcol_o, cnt_o, seen):
    @pl.when(jnp.logical_and(pl.program_id(0) == 0, pl.program_id(1) == 0))
    def _():
        seen[...] = jnp.zeros_like(seen)

    tm = x_ref.shape[0]
    mod = mod_ref[0]
    h = _modulate(x_ref[...], g_ref[...], mod[3:4], mod[4:5])
    h16 = h.astype(BF16)
    _to_planes(_pack_bf16_pairs(h), hp_o)
    h_lo = (h - h16.astype(F32)).astype(BF16)
    part = _dot_nt(rw_ref[...], h16)
    logits_t = part[:N_EXPERTS] + part[N_EXPERTS:] + _dot_nt(rw_ref[:N_EXPERTS], h_lo)
    e1, e2, g1, g2 = _route(_sigmoid(logits_t), rb_ref[...])
    hot = jnp.concatenate([jnp.where(jnp.logical_or(e1 == e, e2 == e), 1.0, 0.0) for e in range(N_EXPERTS)], axis=0)
    before = _dot(hot.astype(BF16), tri_ref[...]) + seen[...]
    rows = [before[e:e + 1] for e in range(N_EXPERTS)]
    eidx_o[0], eidx_o[1] = e1, e2
    rank_o[0], rank_o[1] = _pick(e1, rows).astype(jnp.int32), _pick(e2, rows).astype(jnp.int32)
    gcol_o[...] = jnp.concatenate([g1, g2, jnp.zeros((LANES - 2, tm), F32)], axis=0).T
    seen[...] = seen[...] + jnp.sum(hot, axis=-1, keepdims=True)
    cnt_o[...] = seen[...]


def _moe_slots_kernel(mb, n_rows, eidx_ref, rank_ref, cnt_ref, idx_o, be_o):
    cnt = cnt_ref[...]
    padded = jnp.ceil(cnt / mb) * mb
    starts, ends, run = [], [], jnp.zeros((1, 1), F32)
    for e in range(N_EXPERTS):
        starts.append(run)
        run = run + padded[e:e + 1]
        ends.append(run)
    n_tiles, per_tile = eidx_ref.shape[1], eidx_ref.shape[3] // SC_WINDOW
    for k in range(2):
        for i in range(n_tiles):
            slot = rank_ref[k, i] + _pick(eidx_ref[k, i], starts).astype(jnp.int32)
            for j in range(N_PLANES):
                for q in range(per_tile):
                    row = ((k * N_PLANES + j) * n_tiles + i) * per_tile + q
                    idx_o[row:row + 1, :] = slot[:, q * SC_WINDOW:(q + 1) * SC_WINDOW] + j * n_rows
    first_row = lax.broadcasted_iota(jnp.int32, be_o.shape, 1).astype(F32) * mb
    be = _tree(jnp.add, [jnp.where(ends[e] <= first_row, 1.0, 0.0) for e in range(N_EXPERTS)])
    lane = lax.broadcasted_iota(jnp.int32, be_o.shape, 1)
    be = jnp.where(lane == be_o.shape[1] - 1, ends[N_EXPERTS - 1] / mb, jnp.minimum(be, N_EXPERTS - 1.0))
    be_o[...] = be.astype(jnp.int32)


def _moe_ffn_kernel(be_ref, hs_ref, wg_ref, wu_ref, wd_ref, ys_o, wgu16, wd16):
    i = pl.program_id(0)
    live = i < be_ref[be_ref.shape[0] - 1]

    @pl.when(jnp.logical_and(live, jnp.logical_or(i == 0, be_ref[i] != be_ref[jnp.maximum(i - 1, 0)])))
    def _():
        wgu16[:, :D_EXPERT] = wg_ref[...].astype(BF16)
        wgu16[:, D_EXPERT:] = wu_ref[...].astype(BF16)
        wd16[...] = wd_ref[...].astype(BF16)

    @pl.when(live)
    def _():
        h = _unpack_bf16_pairs(_from_planes(hs_ref)).astype(BF16)
        gu = _dot(h, wgu16[...])
        g, u = gu[:, :D_EXPERT], gu[:, D_EXPERT:]
        a = g * _sigmoid(g) * u
        _to_planes(_pack_bf16_pairs(_dot(a.astype(BF16), wd16[...])), ys_o)


def _moe_residual(x, z0_ref, z1_ref, gcol_ref, mod_ref):
    gc = gcol_ref[...]
    lane = lax.broadcasted_iota(jnp.int32, gc.shape, 1)
    g1 = jnp.sum(jnp.where(lane == 0, gc, 0.0), axis=-1, keepdims=True)
    g2 = jnp.sum(jnp.where(lane == 1, gc, 0.0), axis=-1, keepdims=True)
    y = g1 * _unpack_bf16_pairs(_from_planes(z0_ref)) + g2 * _unpack_bf16_pairs(_from_planes(z1_ref))
    return x + mod_ref[0][5:6] * y


def _moe_combine_kernel(x_ref, z0_ref, z1_ref, gcol_ref, mod_ref, fg_ref, o_ref):
    o_ref[...] = _rms(_moe_residual(x_ref[...], z0_ref, z1_ref, gcol_ref, mod_ref), fg_ref[...])


def _residual_specs(resid, row_fn, tm, d, in_specs, args):
    (z, gcol), mods_prev = resid
    for k in range(2):
        in_specs.append(pl.BlockSpec((None, N_PLANES, None, tm, SC_ROW), lambda bi, t, k=k: (k, 0, bi, t, 0)))
        args.append(z)
    in_specs += [pl.BlockSpec((None, tm, LANES), lambda bi, t: (bi, t, 0)),
                 pl.BlockSpec((1, 6, d), lambda bi, t: (row_fn(bi), 0, 0))]
    args += [gcol, mods_prev]


def _sc_mesh():
    return plsc.VectorSubcoreMesh(core_axis_name="core", subcore_axis_name="subcore")


def _sc_scatter_rows(rows, idx, n_out):
    n, w = rows.shape
    steps = n // SC_WINDOW

    @pl.kernel(out_type=jax.ShapeDtypeStruct((n_out, w), rows.dtype), mesh=_sc_mesh(), scratch_types=[])
    def scatter(x_hbm, i0_hbm, i1_hbm, o_hbm):
        def body(x_vmem, i0_vmem, i1_vmem):
            pltpu.sync_copy(x_vmem, o_hbm.at[i0_vmem.at[0]])
            pltpu.sync_copy(x_vmem, o_hbm.at[i1_vmem.at[0]])

        pltpu.emit_pipeline(
            body,
            grid=(steps,),
            in_specs=[pl.BlockSpec((SC_WINDOW, w), lambda i: (i, 0)),
                      pl.BlockSpec((1, SC_WINDOW), lambda i: (i, 0)),
                      pl.BlockSpec((1, SC_WINDOW), lambda i: (i + steps, 0))],
            out_specs=[],
            core_axis_name=("core", "subcore"),
            dimension_semantics=(pltpu.PARALLEL,),
        )(x_hbm, i0_hbm, i1_hbm)

    return scatter(rows, idx, idx)


def _sc_gather_rows(rows, idx):
    steps, w = idx.shape[0], rows.shape[1]

    @pl.kernel(out_type=jax.ShapeDtypeStruct((steps * SC_WINDOW, w), rows.dtype), mesh=_sc_mesh(),
               scratch_types=[])
    def gather(x_hbm, i_hbm, o_hbm):
        def body(i_vmem, o_vmem):
            pltpu.sync_copy(x_hbm.at[i_vmem.at[0]], o_vmem)

        pltpu.emit_pipeline(
            body,
            grid=(steps,),
            in_specs=[pl.BlockSpec((1, SC_WINDOW), lambda i: (i, 0))],
            out_specs=[pl.BlockSpec((SC_WINDOW, w), lambda i: (i, 0))],
            core_axis_name=("core", "subcore"),
            dimension_semantics=(pltpu.PARALLEL,),
        )(i_hbm, o_hbm)

    return gather(rows, idx)


def _moe_bucket(x, mods, row_fn, g, rw_t, rb, tm, mb):
    b, s, d = x.shape
    n, nt = b * s, (b * s) // tm
    per_b = s // tm
    tile = lambda bi, t: (0, bi * per_b + t, 0, 0)
    tri = jnp.asarray(np.triu(np.ones((tm, tm), np.float32), 1), BF16)
    hp, eidx, rank, gcol, cnt = pl.pallas_call(
        _moe_route_kernel,
        grid=(b, per_b),
        in_specs=[
            pl.BlockSpec((None, tm, d), lambda bi, t: (bi, t, 0)),
            pl.BlockSpec((1, 6, d), lambda bi, t: (row_fn(bi), 0, 0)),
            _full((1, d)), _full((2 * N_EXPERTS, d)), _full((N_EXPERTS, 1)), _full((tm, tm)),
        ],
        out_specs=[
            pl.BlockSpec((N_PLANES, tm, SC_ROW), lambda bi, t: (0, bi * per_b + t, 0)),
            pl.BlockSpec((2, None, 1, tm), tile), pl.BlockSpec((2, None, 1, tm), tile),
            pl.BlockSpec((None, tm, LANES), lambda bi, t: (bi, t, 0)),
            _full((N_EXPERTS, 1)),
        ],
        out_shape=[
            jax.ShapeDtypeStruct((N_PLANES, n, SC_ROW), jnp.int32),
            jax.ShapeDtypeStruct((2, nt, 1, tm), jnp.int32), jax.ShapeDtypeStruct((2, nt, 1, tm), jnp.int32),
            jax.ShapeDtypeStruct((b, s, LANES), F32),
            jax.ShapeDtypeStruct((N_EXPERTS, 1), F32),
        ],
        scratch_shapes=[pltpu.VMEM((N_EXPERTS, 1), F32)],
        compiler_params=_params(2),
        name="moe_route",
    )(x, mods, g, rw_t, rb, tri)

    n_rows = 2 * n + N_EXPERTS * mb
    nb = n_rows // mb
    nb_pad = -(-(nb + 1) // LANES) * LANES
    idx, be = pl.pallas_call(
        functools.partial(_moe_slots_kernel, mb, n_rows),
        out_shape=[jax.ShapeDtypeStruct((2 * N_PLANES * n // SC_WINDOW, SC_WINDOW), jnp.int32),
                   jax.ShapeDtypeStruct((1, nb_pad), jnp.int32)],
        name="moe_slots",
    )(eidx, rank, cnt)

    hs = _sc_scatter_rows(hp.reshape(N_PLANES * n, SC_ROW), idx, N_PLANES * n_rows)
    return hs.reshape(N_PLANES, n_rows, SC_ROW), idx, be.reshape(nb_pad), gcol


def _moe_experts(bucket, layer, w_gate, w_up, w_down, mb, out_dims):
    hs, idx, be, gcol = bucket
    n_rows, d = hs.shape[1], w_gate.shape[2]
    last = be.shape[0] - 1
    blk = lambda i, be_r: jnp.minimum(i, be_r[last] - 1)
    plane_block = pl.BlockSpec((N_PLANES, mb, SC_ROW), lambda i, be_r: (0, blk(i, be_r), 0))
    weight = lambda i, be_r: (layer, be_r[blk(i, be_r)], 0, 0)
    ys = pl.pallas_call(
        _moe_ffn_kernel,
        grid_spec=pltpu.PrefetchScalarGridSpec(
            num_scalar_prefetch=1,
            grid=(n_rows // mb,),
            in_specs=[
                plane_block,
                pl.BlockSpec((None, None, d, D_EXPERT), weight),
                pl.BlockSpec((None, None, d, D_EXPERT), weight),
                pl.BlockSpec((None, None, D_EXPERT, d), weight),
            ],
            out_specs=plane_block,
            scratch_shapes=[pltpu.VMEM((d, 2 * D_EXPERT), BF16), pltpu.VMEM((D_EXPERT, d), BF16)],
        ),
        out_shape=jax.ShapeDtypeStruct((N_PLANES, n_rows, SC_ROW), jnp.int32),
        compiler_params=_params(1),
        name="moe_ffn",
    )(be, hs, w_gate, w_up, w_down)
    z = _sc_gather_rows(ys.reshape(N_PLANES * n_rows, SC_ROW), idx)
    return z.reshape(2, N_PLANES, *out_dims, SC_ROW), gcol.reshape(*out_dims, LANES)


def _moe_combine(x, mods, row_fn, gathered, final_g, tm):
    b, s, d = x.shape
    in_specs = [pl.BlockSpec((None, tm, d), lambda bi, t: (bi, t, 0))]
    args = [x]
    _residual_specs((gathered, mods), row_fn, tm, d, in_specs, args)
    in_specs.append(_full((1, d)))
    args.append(final_g)
    return pl.pallas_call(
        _moe_combine_kernel,
        grid=(b, s // tm),
        in_specs=in_specs,
        out_specs=pl.BlockSpec((None, tm, d), lambda bi, t: (bi, t, 0)),
        out_shape=jax.ShapeDtypeStruct((b, s, d), F32),
        compiler_params=_params(2),
        name="moe_combine",
    )(*args)


def _rope_tables(n_tokens, rot_dim, offset):
    n_rows = n_tokens // GRID_W
    rows = np.repeat(np.arange(n_rows), GRID_W)
    cols = np.tile(np.arange(GRID_W), n_rows)
    n_freq = rot_dim // 4
    inv = jnp.asarray(ROPE_THETA, F32) ** (-jnp.arange(n_freq, dtype=F32) / n_freq)
    ang = jnp.concatenate([jnp.asarray(rows, F32)[:, None] * inv, jnp.asarray(cols, F32)[:, None] * inv], axis=-1)
    cos, sin = jnp.repeat(jnp.cos(ang), 2, axis=-1), jnp.repeat(jnp.sin(ang), 2, axis=-1)
    odd = jnp.asarray(np.arange(rot_dim) % 2 == 1)
    parts = [cos, jnp.where(odd, sin, 0.0), jnp.where(odd, 0.0, -sin)]
    period = 64 if rot_dim == 64 else LANES
    fill = [1.0, 0.0, 0.0]
    out = []
    for p, f in zip(parts, fill):
        slot = jnp.full((n_tokens, period), f, F32).at[:, offset:offset + rot_dim].set(p)
        out.append(jnp.tile(slot, (1, LANES // period)))
    return jnp.stack(out)


def _pad_heads(w, n_heads, lo, hi):
    k = w.shape[0]
    w = w.reshape(k, n_heads, -1)[:, :, lo:hi]
    return jnp.pad(w, ((0, 0), (0, 0), (0, LANES - (hi - lo)))).reshape(k, n_heads * LANES)


def _ab_weights(j, ab_w_in, diff_lambda, diff_subln_g, mla_q_norm_g, mla_w_qb, mla_kv_norm_g, mla_w_kvb, ab_w_out):
    place = np.zeros((LANES, MLA_HEADS * LANES), np.float32)
    for hd in range(MLA_HEADS):
        for r in range(MLA_ROPE_DIM):
            place[r, hd * LANES + MLA_NOPE_DIM + r] = 1.0
    w_out_m = ab_w_out[j][512:].reshape(MLA_HEADS, MLA_V_DIM, D_MODEL)
    return {
        "w_in": jnp.pad(ab_w_in[j], ((0, 0), (0, AB_IN_PAD - AB_IN))).astype(BF16),
        "qn_g": mla_q_norm_g[j][None, :],
        "w_qb": _pad_heads(mla_w_qb[j], MLA_HEADS, 0, MLA_NOPE_DIM + MLA_ROPE_DIM).astype(BF16),
        "kvn_g": mla_kv_norm_g[j][None, :],
        "wk": _pad_heads(mla_w_kvb[j], MLA_HEADS, 0, MLA_NOPE_DIM).astype(BF16),
        "wv": _pad_heads(mla_w_kvb[j], MLA_HEADS, MLA_NOPE_DIM, MLA_NOPE_DIM + MLA_V_DIM).astype(BF16),
        "e": jnp.asarray(place, BF16),
        "w_out_d": ab_w_out[j][:512].reshape(DIFF_HEADS, LANES, D_MODEL).astype(BF16),
        "w_out_m": jnp.pad(w_out_m, ((0, 0), (0, LANES - MLA_V_DIM), (0, 0))).astype(BF16),
        "subln_g": diff_subln_g[j][None, :],
        "lam": diff_lambda[j],
    }


def kernel(x_prompt, x_sample, cache_diff_k, cache_diff_v, cache_mla_ckv, cache_mla_krope, cache_swa_k, cache_swa_v,
           c, c_ctx, ada_w, ada_b, norm1_g, norm2_g, final_norm_g, ab_w_in, diff_lambda, diff_subln_g, mla_q_norm_g,
           mla_w_qb, mla_kv_norm_g, mla_w_kvb, ab_w_out, swa_w_in, swa_sink, swa_w_out, router_w, router_bias,
           moe_w_gate, moe_w_up, moe_w_down):
    bp, sp, d = x_prompt.shape
    bs, ss, _ = x_sample.shape
    depth = ada_w.shape[0]
    past = cache_diff_k.shape[2]
    n_ab = cache_diff_k.shape[1]
    n_swa = cache_swa_k.shape[1]
    assert bs + 1 <= COND_ROWS and d == D_MODEL

    cond = jnp.zeros((COND_ROWS, d), F32).at[:bs].set(c).at[bs].set(c_ctx)
    mods = _ada_all(cond, ada_w, ada_b).reshape(depth, COND_ROWS, 6, d)
    row_s = lambda bi: bi
    row_p = lambda bi: bi * 0 + bs

    rope_d = _rope_tables(ss, DIFF_QK_DIM, 0)
    rope_q = _rope_tables(ss, MLA_ROPE_DIM, MLA_NOPE_DIM)
    rope_k = _rope_tables(ss, MLA_ROPE_DIM, 0)

    cdk = cache_diff_k.reshape(bs, n_ab, past, 512)
    cdv = cache_diff_v.reshape(bs, n_ab, past, 512)
    ckr = jnp.pad(cache_mla_krope, ((0, 0), (0, 0), (0, 0), (0, LANES - MLA_ROPE_DIM)))
    csk = cache_swa_k.reshape(bs, n_swa, past, 256)
    csv = cache_swa_v.reshape(bs, n_swa, past, 256)
    rw_hi = router_w.T.astype(BF16)
    rw_t = jnp.concatenate([rw_hi, (router_w.T - rw_hi.astype(F32)).astype(BF16)], axis=0)
    rb = router_bias[:, None]

    tp = min(256, sp)
    ts = min(256, ss)
    tm_s = min(512, ss)
    tm_p = min(512, bp * sp)

    xp, xs = x_prompt, x_sample
    new_ab, new_swa = None, None
    pend_p = pend_s = None

    def project(fn, x, pend, *a):
        outs = list(fn(x, *a, pend))
        return (outs.pop() if pend is not None else x), outs

    for l in range(depth):
        j = l // 2
        ml = mods[l]
        g1, g2 = norm1_g[l][None, :], norm2_g[l][None, :]
        if l % 2 == 0:
            lambda_init = 0.8 - 0.6 * math.exp(-0.3 * l)
            wts = _ab_weights(j, ab_w_in, diff_lambda, diff_subln_g, mla_q_norm_g, mla_w_qb, mla_kv_norm_g,
                              mla_w_kvb, ab_w_out)
            xp, (qa, ka, va, qb, km, vm, *new_ab) = project(_proj_ab, xp, pend_p, ml, row_p, g1, wts, None,
                                                             (j, n_ab, new_ab), tp)
            xp = _attn_ab(xp, ml, row_p, (qa, qb), [(ka, va, km, vm)], wts, lambda_init, tp, PROMPT_BUFS)
            xs, (qa, ka, va, qb, km, vm) = project(_proj_ab, xs, pend_s, ml, row_s, g1, wts,
                                                   (rope_d, rope_q, rope_k), None, tm_s)
            cache_seg = _cache_ab(cdk, cdv, cache_mla_ckv, ckr, j, wts)
            xs = _attn_ab(xs, ml, row_s, (qa, qb), [(ka, va, km, vm), tuple(cache_seg)], wts, lambda_init, ts, 3)
        else:
            w_in = swa_w_in[j].astype(BF16)
            w_out = swa_w_out[j].reshape(8, LANES, d).astype(BF16)
            xp, (q, k, v, *new_swa) = project(_proj_swa, xp, pend_p, ml, row_p, g1, w_in, None,
                                              (j, n_swa, new_swa), tp)
            xp = _attn_swa(xp, ml, row_p, q, k, v, None, swa_sink[j], w_out, tp, PROMPT_BUFS)
            xs, (q, k, v) = project(_proj_swa, xs, pend_s, ml, row_s, g1, w_in, rope_d, None, tm_s)
            xs = _attn_swa(xs, ml, row_s, q, k, v, (csk, csv, j), swa_sink[j], w_out, ts, 2)
        bucket_p = _moe_bucket(xp.reshape(1, bp * sp, d), ml, row_p, g2, rw_t, rb, tm_p, MOE_ROW_BLOCK)
        bucket_s = _moe_bucket(xs, ml, row_s, g2, rw_t, rb, tm_s, MOE_ROW_BLOCK)
        got_p = _moe_experts(bucket_p, l, moe_w_gate, moe_w_up, moe_w_down, MOE_ROW_BLOCK, (bp, sp))
        got_s = _moe_experts(bucket_s, l, moe_w_gate, moe_w_up, moe_w_down, MOE_ROW_BLOCK, (bs, ss))
        pend_p, pend_s = (got_p, ml), (got_s, ml)

    xp = _moe_combine(xp, pend_p[1], row_p, pend_p[0], final_norm_g[None, :], tp)
    xs = _moe_combine(xs, pend_s[1], row_s, pend_s[0], final_norm_g[None, :], tm_s)

    new_diff_k = new_ab[0].reshape(bp, n_ab, sp, DIFF_HEADS, 2, DIFF_QK_DIM)
    new_diff_v = new_ab[1].reshape(bp, n_ab, sp, DIFF_HEADS, DIFF_V_DIM)
    new_mla_ckv, new_mla_krope = new_ab[2], new_ab[3]
    new_swa_k = new_swa[0].reshape(bp, n_swa, sp, SWA_KV_HEADS, SWA_HEAD_DIM)
    new_swa_v = new_swa[1].reshape(bp, n_swa, sp, SWA_KV_HEADS, SWA_HEAD_DIM)
    return (xp, xs, new_diff_k, new_diff_v, new_mla_ckv, new_mla_krope, new_swa_k, new_swa_v)
```

```python
import functools
import math

import jax
import jax.numpy as jnp
import numpy as np
from jax import lax
from jax.experimental import pallas as pl
from jax.experimental.pallas import tpu as pltpu
from jax.experimental.pallas import tpu_sc as plsc

F32 = jnp.float32
BF16 = jnp.bfloat16

D_MODEL = 1024
GRID_W = 64
ROPE_THETA = 10000.0
NORM_EPS = 1e-6
NEG_INF = -1e30
LOG2E = math.log2(math.e)
LANES = 128

DIFF_HEADS = 4
DIFF_QK_DIM = 64
DIFF_V_DIM = 128
MLA_HEADS = 8
MLA_Q_RANK = 384
MLA_KV_RANK = 256
MLA_NOPE_DIM = 64
MLA_ROPE_DIM = 32
MLA_V_DIM = 64
AB_IN = 3 * 512 + MLA_Q_RANK + MLA_KV_RANK + MLA_ROPE_DIM
AB_IN_PAD = 2304
SWA_HEADS = 16
SWA_KV_HEADS = 4
SWA_GROUP = 4
SWA_HEAD_DIM = 64
WINDOW = 128
N_EXPERTS = 16
N_GROUPS = 4
EXPERTS_PER_GROUP = 4
D_EXPERT = 256
COND_ROWS = 16

VMEM_LIMIT = 56 * 1024 * 1024


def _full(shape):
    n = len(shape)
    return pl.BlockSpec(shape, lambda *_: (0,) * n)


def _params(n_axes):
    return pltpu.CompilerParams(dimension_semantics=("arbitrary",) * n_axes, vmem_limit_bytes=VMEM_LIMIT)


def _sigmoid(x):
    return 1.0 / (1.0 + jnp.exp(-x))


def _rms(x, g):
    return x * lax.rsqrt(jnp.mean(x * x, axis=-1, keepdims=True) + NORM_EPS) * g


def _modulate(x, g, shift, scale):
    return _rms(x, g) * (1.0 + scale) + shift


def _dot(a, b):
    return jnp.dot(a, b, preferred_element_type=F32)


def _dot_nt(a, b):
    return lax.dot_general(a, b, (((1,), (1,)), ((), ())), preferred_element_type=F32)


def _ones_lane():
    lane = lax.broadcasted_iota(jnp.int32, (1, MLA_HEADS * LANES), 1)
    return jnp.where((lane & (LANES - 1)) == MLA_V_DIM, 1.0, 0.0)


def _rope(x, tab_ref):
    c, s_odd, s_even = tab_ref[0], tab_ref[1], tab_ref[2]
    out = []
    for i in range(x.shape[1] // LANES):
        xi = x[:, i * LANES:(i + 1) * LANES]
        out.append(xi * c + pltpu.roll(xi, 1, 1) * s_odd + pltpu.roll(xi, LANES - 1, 1) * s_even)
    return out[0] if len(out) == 1 else jnp.concatenate(out, axis=1)


def _tree(op, xs):
    xs = list(xs)
    while len(xs) > 1:
        xs = [op(xs[i], xs[i + 1]) if i + 1 < len(xs) else xs[i] for i in range(0, len(xs), 2)]
    return xs[0]


def _lane_fold(op, x):
    return _tree(op, [x[:, i * LANES:(i + 1) * LANES] for i in range(x.shape[1] // LANES)])


def _ada_kernel(cond_ref, w_ref, b_ref, o_ref):
    c = cond_ref[...]
    a = (c * _sigmoid(c)).astype(BF16)
    o_ref[...] = _dot(a, w_ref[...].astype(BF16)) + b_ref[...]


def _ada_all(cond, ada_w, ada_b):
    depth, d, n = ada_w.shape
    tn = 1536
    return pl.pallas_call(
        _ada_kernel,
        grid=(depth, n // tn),
        in_specs=[
            _full((COND_ROWS, d)),
            pl.BlockSpec((None, d, tn), lambda l, j: (l, 0, j)),
            pl.BlockSpec((None, 1, tn), lambda l, j: (l, 0, j)),
        ],
        out_specs=pl.BlockSpec((None, COND_ROWS, tn), lambda l, j: (l, 0, j)),
        out_shape=jax.ShapeDtypeStruct((depth, COND_ROWS, n), F32),
        compiler_params=_params(2),
        name="ada_mod",
    )(cond, ada_w, ada_b.reshape(depth, 1, n))


def _proj_ab_kernel(rope, emit_cache, n_prev, fused, x_ref, mod_ref, g_ref, w_in_ref, qn_ref, wqb_ref, kvn_ref,
                    wk_ref, wv_ref, e_ref, *rest):
    if rope:
        rope_d, rope_q, rope_k = rest[:3]
        rest = rest[3:]
    x = x_ref[...]
    if fused:
        x = _moe_residual(x, *rest[:4])
        rest[-1][...] = x
        rest = rest[4:]
    rest = rest[n_prev:]
    qa_o, ka_o, va_o, qb_o, km_o, vm_o = rest[:6]
    mod = mod_ref[0]
    h = _modulate(x, g_ref[...], mod[0:1], mod[1:2]).astype(BF16)
    big = _dot(h, w_in_ref[...])
    qa, ka, va = big[:, 0:512], big[:, 512:1024], big[:, 1024:1536]
    q_lat, ckv, kr = big[:, 1536:1920], big[:, 1920:2176], big[:, 2176:2304]
    qb = _dot(_rms(q_lat, qn_ref[...]).astype(BF16), wqb_ref[...])
    ckv_n = _rms(ckv, kvn_ref[...])
    if emit_cache:
        ka32_o, va32_o, ckv32_o, kr32_o = rest[6:10]
        ka32_o[...] = ka
        va32_o[...] = va
        ckv32_o[...] = ckv_n
        kr32_o[...] = kr[:, :MLA_ROPE_DIM]
    if rope:
        qa, ka = _rope(qa, rope_d), _rope(ka, rope_d)
        qb, kr = _rope(qb, rope_q), _rope(kr, rope_k)
    qa = qa * (DIFF_QK_DIM ** -0.5 * LOG2E)
    qb = qb * ((MLA_NOPE_DIM + MLA_ROPE_DIM) ** -0.5 * LOG2E)
    ckv16 = ckv_n.astype(BF16)
    km = _dot(ckv16, wk_ref[...]) + _dot(kr.astype(BF16), e_ref[...])
    vm = _dot(ckv16, wv_ref[...]) + _ones_lane()
    lo = lax.broadcasted_iota(jnp.int32, (qa.shape[0], LANES), 1) < DIFF_QK_DIM
    for hd in range(DIFF_HEADS):
        sl = slice(hd * LANES, (hd + 1) * LANES)
        qa_o[2 * hd] = jnp.where(lo, qa[:, sl], 0.0).astype(BF16)
        qa_o[2 * hd + 1] = jnp.where(lo, 0.0, qa[:, sl]).astype(BF16)
        ka_o[hd] = ka[:, sl].astype(BF16)
        va_o[hd] = va[:, sl].astype(BF16)
    for hd in range(MLA_HEADS):
        sl = slice(hd * LANES, (hd + 1) * LANES)
        qb_o[hd] = qb[:, sl].astype(BF16)
        km_o[hd] = km[:, sl].astype(BF16)
        vm_o[hd] = vm[:, sl].astype(BF16)


def _cache_outputs(cache, widths, b, s, tm, in_specs, args, n_fixed_outs):
    if cache is None:
        return [], {}, 0
    j, n_layers, prev = cache
    outs = [(jax.ShapeDtypeStruct((b, n_layers, s, w), F32),
             pl.BlockSpec((None, None, tm, w), lambda bi, t: (bi, j, t, 0))) for w in widths]
    aliases = {}
    if prev is not None:
        for i, arr in enumerate(prev):
            aliases[len(args)] = n_fixed_outs + i
            in_specs.append(pl.BlockSpec(memory_space=pl.ANY))
            args.append(arr)
    return outs, aliases, len(aliases)


def _proj_ab(x, mods, row_fn, g, wts, rope_tabs, cache, tm, resid=None):
    b, s, d = x.shape
    rope = rope_tabs is not None
    tok = lambda bi, t: (bi, t, 0)
    head = lambda bi, t: (bi, 0, t, 0)
    in_specs = [
        pl.BlockSpec((None, tm, d), tok),
        pl.BlockSpec((1, 6, d), lambda bi, t: (row_fn(bi), 0, 0)),
        _full((1, d)),
        _full((d, AB_IN_PAD)), _full((1, MLA_Q_RANK)), _full((MLA_Q_RANK, 1024)), _full((1, MLA_KV_RANK)),
        _full((MLA_KV_RANK, 1024)), _full((MLA_KV_RANK, 1024)), _full((LANES, 1024)),
    ]
    args = [x, mods, g, wts["w_in"], wts["qn_g"], wts["w_qb"], wts["kvn_g"], wts["wk"], wts["wv"], wts["e"]]
    if rope:
        in_specs += [pl.BlockSpec((3, tm, LANES), lambda bi, t: (0, t, 0))] * 3
        args += list(rope_tabs)
    if resid is not None:
        _residual_specs(resid, row_fn, tm, d, in_specs, args)

    def hm(nh):
        return jax.ShapeDtypeStruct((b, nh, s, LANES), BF16), pl.BlockSpec((None, nh, tm, LANES), head)

    outs = [hm(8), hm(4), hm(4), hm(8), hm(8), hm(8)]
    cache_outs, aliases, n_prev = _cache_outputs(cache, (512, 512, MLA_KV_RANK, MLA_ROPE_DIM), b, s, tm, in_specs,
                                                 args, len(outs))
    outs += cache_outs
    if resid is not None:
        outs.append((jax.ShapeDtypeStruct((b, s, d), F32), pl.BlockSpec((None, tm, d), tok)))
    return pl.pallas_call(
        functools.partial(_proj_ab_kernel, rope, cache is not None, n_prev, resid is not None),
        grid=(b, s // tm),
        in_specs=in_specs,
        out_specs=[o[1] for o in outs],
        out_shape=[o[0] for o in outs],
        input_output_aliases=aliases,
        compiler_params=_params(2),
        name="proj_ab",
    )(*args)


def _cache_ab_kernel(dk_ref, dv_ref, ckv_ref, kr_ref, wk_ref, wv_ref, e_ref, ck_o, cv_o, km_o, vm_o):
    ckv16 = ckv_ref[...].astype(BF16)
    km = _dot(ckv16, wk_ref[...]) + _dot(kr_ref[...].astype(BF16), e_ref[...])
    vm = _dot(ckv16, wv_ref[...]) + _ones_lane()
    for hd in range(DIFF_HEADS):
        sl = slice(hd * LANES, (hd + 1) * LANES)
        ck_o[hd] = dk_ref[:, sl].astype(BF16)
        cv_o[hd] = dv_ref[:, sl].astype(BF16)
    for hd in range(MLA_HEADS):
        sl = slice(hd * LANES, (hd + 1) * LANES)
        km_o[hd] = km[:, sl].astype(BF16)
        vm_o[hd] = vm[:, sl].astype(BF16)


def _cache_ab(cdk, cdv, cckv, ckr, j, wts):
    b, _, p, _ = cdk.shape
    lay = lambda bi: (bi, j, 0, 0)

    def hm(nh):
        return (jax.ShapeDtypeStruct((b, nh, p, LANES), BF16),
                pl.BlockSpec((None, nh, p, LANES), lambda bi: (bi, 0, 0, 0)))

    outs = [hm(4), hm(4), hm(8), hm(8)]
    return pl.pallas_call(
        _cache_ab_kernel,
        grid=(b,),
        in_specs=[
            pl.BlockSpec((None, None, p, 512), lay), pl.BlockSpec((None, None, p, 512), lay),
            pl.BlockSpec((None, None, p, MLA_KV_RANK), lay), pl.BlockSpec((None, None, p, LANES), lay),
            _full((MLA_KV_RANK, 1024)), _full((MLA_KV_RANK, 1024)), _full((LANES, 1024)),
        ],
        out_specs=[o[1] for o in outs],
        out_shape=[o[0] for o in outs],
        compiler_params=_params(1),
        name="cache_ab",
    )(cdk, cdv, cckv, ckr, wts["wk"], wts["wv"], wts["e"])


def _attn_ab_kernel(n_seg, lambda_init, x_ref, mod_ref, qa_ref, qb_ref, *rest):
    segs = [rest[4 * i:4 * i + 4] for i in range(n_seg)]
    wd_ref, wm_ref, sg_ref, lam_ref, o_ref = rest[4 * n_seg:4 * n_seg + 5]
    bufs = rest[4 * n_seg + 5:]
    lp = lam_ref[...]
    lam = (jnp.exp(jnp.sum(lp[0:1] * lp[1:2], axis=-1, keepdims=True))
           - jnp.exp(jnp.sum(lp[2:3] * lp[3:4], axis=-1, keepdims=True)) + lambda_init)
    tq, d = x_ref.shape
    widths = [sg[0].shape[1] for sg in segs]
    offs = [sum(widths[:i]) for i in range(n_seg)]
    units = [("diff", hd) for hd in range(DIFF_HEADS)] + [("mla", hd) for hd in range(MLA_HEADS)]

    def scores(u, buf):
        kind, hd = units[u]
        qs = [qa_ref[2 * hd], qa_ref[2 * hd + 1]] if kind == "diff" else [qb_ref[hd]]
        ks = [sg[0 if kind == "diff" else 2][hd] for sg in segs]
        ms = []
        for i, q in enumerate(qs):
            folds = []
            for k, off, w in zip(ks, offs, widths):
                s = _dot_nt(q, k)
                buf[i, :, off:off + w] = s
                folds.append(_lane_fold(jnp.maximum, s))
            ms.append(jnp.max(_tree(jnp.maximum, folds), axis=-1, keepdims=True))
        return ms

    def exps(buf, i, m):
        e = [jnp.exp2(buf[i, :, off:off + w] - m) for off, w in zip(offs, widths)]
        l = jnp.sum(_tree(jnp.add, [_lane_fold(jnp.add, x) for x in e]), axis=-1, keepdims=True)
        return e, l

    def consume(u, buf, ms, acc):
        kind, hd = units[u]
        if kind == "diff":
            vs = [sg[1][hd] for sg in segs]
            (e1, l1), (e2, l2) = exps(buf, 0, ms[0]), exps(buf, 1, ms[1])
            c1, c2 = 1.0 / l1, lam / l2
            o = _tree(jnp.add, [_dot((a * c1 - b * c2).astype(BF16), v) for a, b, v in zip(e1, e2, vs)])
            od = _rms(o, sg_ref[...]) * (1.0 - lambda_init)
            return acc + _dot(od.astype(BF16), wd_ref[hd])
        vs = [sg[3][hd] for sg in segs]
        if n_seg == 1:
            e, l = exps(buf, 0, ms[0])
            o = _tree(jnp.add, [_dot(x.astype(BF16), v) for x, v in zip(e, vs)]) / l
            return acc + _dot(o.astype(BF16), wm_ref[hd])
        e = [jnp.exp2(buf[0, :, off:off + w] - ms[0]) for off, w in zip(offs, widths)]
        o = _tree(jnp.add, [_dot(x.astype(BF16), v) for x, v in zip(e, vs)])
        lane = lax.broadcasted_iota(jnp.int32, o.shape, 1)
        o = o / jnp.sum(jnp.where(lane == MLA_V_DIM, o, 0.0), axis=-1, keepdims=True)
        return acc + _dot(o.astype(BF16), wm_ref[hd])

    acc = jnp.zeros((tq, d), F32)
    ahead, pending = len(bufs) - 1, {}
    for u in range(min(ahead, len(units))):
        pending[u] = scores(u, bufs[u % len(bufs)])
    for u in range(len(units)):
        if u + ahead < len(units):
            pending[u + ahead] = scores(u + ahead, bufs[(u + ahead) % len(bufs)])
        acc = consume(u, bufs[u % len(bufs)], pending.pop(u), acc)
    o_ref[...] = x_ref[...] + mod_ref[0][2:3] * acc


def _attn_ab(x, mods, row_fn, q_parts, seg_list, wts, lambda_init, tq, n_bufs):
    b, s, d = x.shape
    qa, qb = q_parts
    in_specs = [
        pl.BlockSpec((None, tq, d), lambda bi, t: (bi, t, 0)),
        pl.BlockSpec((1, 6, d), lambda bi, t: (row_fn(bi), 0, 0)),
        pl.BlockSpec((None, 8, tq, LANES), lambda bi, t: (bi, 0, t, 0)),
        pl.BlockSpec((None, 8, tq, LANES), lambda bi, t: (bi, 0, t, 0)),
    ]
    args = [x, mods, qa, qb]
    kv_mode = dict(pipeline_mode=pl.Buffered(1)) if s // tq > 1 else {}
    for seg in seg_list:
        for arr in seg:
            nh, nk = arr.shape[1], arr.shape[2]
            in_specs.append(pl.BlockSpec((None, nh, nk, LANES), lambda bi, t: (bi, 0, 0, 0), **kv_mode))
            args.append(arr)
    in_specs += [_full((DIFF_HEADS, LANES, d)), _full((MLA_HEADS, LANES, d)), _full((1, LANES)),
                 _full((4, DIFF_QK_DIM))]
    args += [wts["w_out_d"], wts["w_out_m"], wts["subln_g"], wts["lam"]]
    n_keys = sum(seg[0].shape[2] for seg in seg_list)
    return pl.pallas_call(
        functools.partial(_attn_ab_kernel, len(seg_list), lambda_init),
        grid=(b, s // tq),
        in_specs=in_specs,
        out_specs=pl.BlockSpec((None, tq, d), lambda bi, t: (bi, t, 0)),
        out_shape=jax.ShapeDtypeStruct((b, s, d), F32),
        scratch_shapes=[pltpu.VMEM((2, tq, n_keys), F32) for _ in range(n_bufs)],
        compiler_params=_params(2),
        name="attn_ab",
    )(*args)


def _proj_swa_kernel(rope, emit_cache, n_prev, fused, x_ref, mod_ref, g_ref, w_in_ref, *rest):
    if rope:
        rope_d = rest[0]
        rest = rest[1:]
    x = x_ref[...]
    if fused:
        x = _moe_residual(x, *rest[:4])
        rest[-1][...] = x
        rest = rest[4:]
    rest = rest[n_prev:]
    q_o, k_o, v_o = rest[:3]
    mod = mod_ref[0]
    h = _modulate(x, g_ref[...], mod[0:1], mod[1:2]).astype(BF16)
    big = _dot(h, w_in_ref[...])
    q, k, v = big[:, :1024], big[:, 1024:1280], big[:, 1280:1536]
    if emit_cache:
        k32_o, v32_o = rest[3:5]
        k32_o[...] = k
        v32_o[...] = v
    if rope:
        q, k = _rope(q, rope_d), _rope(k, rope_d)
    q = q * (SWA_HEAD_DIM ** -0.5 * LOG2E)
    lo = lax.broadcasted_iota(jnp.int32, (q.shape[0], LANES), 1) < SWA_HEAD_DIM
    for pair in range(2):
        for grp in range(SWA_GROUP):
            ca, cb = (2 * pair) * 2 + grp // 2, (2 * pair + 1) * 2 + grp // 2
            a = q[:, ca * LANES:(ca + 1) * LANES]
            bb = q[:, cb * LANES:(cb + 1) * LANES]
            if grp % 2 == 0:
                bb = pltpu.roll(bb, SWA_HEAD_DIM, 1)
            else:
                a = pltpu.roll(a, SWA_HEAD_DIM, 1)
            q_o[pair * SWA_GROUP + grp] = jnp.where(lo, a, bb).astype(BF16)
    for kvh in range(SWA_KV_HEADS):
        sl = slice((kvh // 2) * LANES, (kvh // 2 + 1) * LANES)
        keep = lo if kvh % 2 == 0 else jnp.logical_not(lo)
        k_o[kvh] = jnp.where(keep, k[:, sl], 0.0).astype(BF16)
        v_o[kvh] = jnp.where(keep, v[:, sl], 0.0).astype(BF16)


def _proj_swa(x, mods, row_fn, g, w_in, rope_tab, cache, tm, resid=None):
    b, s, d = x.shape
    rope = rope_tab is not None
    tok = lambda bi, t: (bi, t, 0)
    head = lambda bi, t: (bi, 0, t, 0)
    in_specs = [
        pl.BlockSpec((None, tm, d), tok),
        pl.BlockSpec((1, 6, d), lambda bi, t: (row_fn(bi), 0, 0)),
        _full((1, d)), _full((d, 1536)),
    ]
    args = [x, mods, g, w_in]
    if rope:
        in_specs.append(pl.BlockSpec((3, tm, LANES), lambda bi, t: (0, t, 0)))
        args.append(rope_tab)
    if resid is not None:
        _residual_specs(resid, row_fn, tm, d, in_specs, args)

    def hm(nh):
        return jax.ShapeDtypeStruct((b, nh, s, LANES), BF16), pl.BlockSpec((None, nh, tm, LANES), head)

    outs = [hm(8), hm(4), hm(4)]
    cache_outs, aliases, n_prev = _cache_outputs(cache, (256, 256), b, s, tm, in_specs, args, len(outs))
    outs += cache_outs
    if resid is not None:
        outs.append((jax.ShapeDtypeStruct((b, s, d), F32), pl.BlockSpec((None, tm, d), tok)))
    return pl.pallas_call(
        functools.partial(_proj_swa_kernel, rope, cache is not None, n_prev, resid is not None),
        grid=(b, s // tm),
        in_specs=in_specs,
        out_specs=[o[1] for o in outs],
        out_shape=[o[0] for o in outs],
        input_output_aliases=aliases,
        compiler_params=_params(2),
        name="proj_swa",
    )(*args)


def _attn_swa_kernel(windowed, band_w, x_ref, mod_ref, q_ref, k_ref, v_ref, *rest):
    if windowed:
        ck_ref, cv_ref = rest[:2]
        rest = rest[2:]
    sink_ref, w_ref, o_ref = rest[:3]
    bufs = rest[3:]
    tq, d = x_ref.shape
    n_keys = k_ref.shape[1]
    lo_k = lax.broadcasted_iota(jnp.int32, (1, LANES), 1) < SWA_HEAD_DIM
    if windowed:
        start = pl.program_id(1) * tq
        bstart = pl.multiple_of(jnp.clip(start - WINDOW, 0, n_keys - band_w), LANES)
        qpos = start + lax.broadcasted_iota(jnp.int32, (tq, band_w), 0)
        kpos = bstart + lax.broadcasted_iota(jnp.int32, (tq, band_w), 1)
        bias = jnp.where(jnp.abs(qpos - kpos) <= WINDOW, 0.0, NEG_INF)

    def keys_values(kvh):
        if not windowed:
            return [k_ref[kvh]], [v_ref[kvh]]
        keep = lo_k if kvh % 2 == 0 else jnp.logical_not(lo_k)
        sl = slice((kvh // 2) * LANES, (kvh // 2 + 1) * LANES)
        kc = jnp.where(keep, ck_ref[:, sl], 0.0).astype(BF16)
        vc = jnp.where(keep, cv_ref[:, sl], 0.0).astype(BF16)
        return ([k_ref[kvh, pl.ds(bstart, band_w), :], kc], [v_ref[kvh, pl.ds(bstart, band_w), :], vc])

    units = [(kvh, g) for kvh in range(SWA_KV_HEADS) for g in range(SWA_GROUP)]

    def scores(u, buf):
        kvh, g = units[u]
        q = q_ref[(kvh // 2) * SWA_GROUP + g]
        sink = sink_ref[kvh * SWA_GROUP + g] * LOG2E
        folds, off = [], 0
        for i, k in enumerate(keys_values(kvh)[0]):
            s = _dot_nt(q, k)
            if windowed and i == 0:
                s = s + bias
            buf[:, off:off + s.shape[1]] = s
            folds.append(_lane_fold(jnp.maximum, s))
            off += s.shape[1]
        m = jnp.maximum(jnp.max(_tree(jnp.maximum, folds), axis=-1, keepdims=True), sink)
        return m, sink

    def consume(u, buf, m, sink):
        vs = keys_values(units[u][0])[1]
        e, off = [], 0
        for v in vs:
            e.append(jnp.exp2(buf[:, off:off + v.shape[0]] - m))
            off += v.shape[0]
        l = jnp.sum(_tree(jnp.add, [_lane_fold(jnp.add, x) for x in e]), axis=-1, keepdims=True) + jnp.exp2(sink - m)
        return _tree(jnp.add, [_dot(x.astype(BF16), v) for x, v in zip(e, vs)]) / l

    acc = jnp.zeros((tq, d), F32)
    ahead, pending = len(bufs) - 1, {}
    for u in range(min(ahead, len(units))):
        pending[u] = scores(u, bufs[u % len(bufs)])
    prev = None
    for u, (kvh, g) in enumerate(units):
        if u + ahead < len(units):
            pending[u + ahead] = scores(u + ahead, bufs[(u + ahead) % len(bufs)])
        o = consume(u, bufs[u % len(bufs)], *pending.pop(u))
        if g % 2 == 0:
            prev = o
            continue
        if kvh % 2 == 0:
            slab = prev + pltpu.roll(o, SWA_HEAD_DIM, 1)
        else:
            slab = pltpu.roll(prev, SWA_HEAD_DIM, 1) + o
        acc = acc + _dot(slab.astype(BF16), w_ref[kvh * 2 + g // 2])
    o_ref[...] = x_ref[...] + mod_ref[0][2:3] * acc


def _attn_swa(x, mods, row_fn, q, k, v, ctx, sink, w_out, tq, n_bufs):
    b, s, d = x.shape
    windowed = ctx is not None
    band_w = min(tq + 2 * WINDOW, s)
    kv_spec = pl.BlockSpec((None, SWA_KV_HEADS, s, LANES), lambda bi, t: (bi, 0, 0, 0))
    in_specs = [
        pl.BlockSpec((None, tq, d), lambda bi, t: (bi, t, 0)),
        pl.BlockSpec((1, 6, d), lambda bi, t: (row_fn(bi), 0, 0)),
        pl.BlockSpec((None, 8, tq, LANES), lambda bi, t: (bi, 0, t, 0)),
        kv_spec, kv_spec,
    ]
    args = [x, mods, q, k, v]
    if windowed:
        ck, cv, j = ctx
        p = ck.shape[2]
        spec = pl.BlockSpec((None, None, p, 256), lambda bi, t: (bi, j, 0, 0))
        in_specs += [spec, spec]
        args += [ck, cv]
    in_specs += [pl.BlockSpec(memory_space=pltpu.SMEM), _full((8, LANES, d))]
    args += [sink, w_out]
    n_keys = band_w + ctx[0].shape[2] if windowed else s
    return pl.pallas_call(
        functools.partial(_attn_swa_kernel, windowed, band_w),
        grid=(b, s // tq),
        in_specs=in_specs,
        out_specs=pl.BlockSpec((None, tq, d), lambda bi, t: (bi, t, 0)),
        out_shape=jax.ShapeDtypeStruct((b, s, d), F32),
        scratch_shapes=[pltpu.VMEM((tq, n_keys), F32) for _ in range(n_bufs)],
        compiler_params=_params(2),
        name="attn_swa",
    )(*args)


def _route(scores_t, bias):
    sel_t = scores_t + bias
    sel = [sel_t[e:e + 1] for e in range(N_EXPERTS)]
    raw = [scores_t[e:e + 1] for e in range(N_EXPERTS)]
    gscore = []
    for g in range(N_GROUPS):
        r = sel[g * 4:g * 4 + 4]
        pairs = [r[i] + r[j] for i in range(4) for j in range(i + 1, 4)]
        gscore.append(functools.reduce(jnp.maximum, pairs))
    best, gidx = gscore[0], jnp.zeros_like(gscore[0], dtype=jnp.int32)
    for g in range(1, N_GROUPS):
        take = gscore[g] > best
        best = jnp.where(take, gscore[g], best)
        gidx = jnp.where(take, g, gidx)
    vals = []
    for k in range(EXPERTS_PER_GROUP):
        v = sel[k]
        for g in range(1, N_GROUPS):
            v = jnp.where(gidx == g, sel[g * 4 + k], v)
        vals.append(v)

    def argmax4(vs):
        m, idx = vs[0], jnp.zeros_like(gidx)
        for k in range(1, 4):
            take = vs[k] > m
            m = jnp.where(take, vs[k], m)
            idx = jnp.where(take, k, idx)
        return idx

    i1 = argmax4(vals)
    i2 = argmax4([jnp.where(i1 == k, -jnp.inf, vals[k]) for k in range(4)])
    e1, e2 = gidx * 4 + i1, gidx * 4 + i2
    w1 = functools.reduce(lambda a, b: a + b, [jnp.where(e1 == e, raw[e], 0.0) for e in range(N_EXPERTS)])
    w2 = functools.reduce(lambda a, b: a + b, [jnp.where(e2 == e, raw[e], 0.0) for e in range(N_EXPERTS)])
    den = w1 + w2
    return e1, e2, w1 / den, w2 / den


SC_WINDOW = 128
SC_ROW = 128
N_PLANES = (D_MODEL // 2) // SC_ROW
MOE_ROW_BLOCK = 1024
PROMPT_BUFS = 8
HALF_WORD = -65536


def _to_planes(packed, out_ref):
    for j in range(N_PLANES):
        out_ref[j] = packed[:, j * SC_ROW:(j + 1) * SC_ROW]


def _from_planes(ref):
    return jnp.concatenate([ref[j] for j in range(N_PLANES)], axis=1)


def _pack_bf16_pairs(x):
    n = x.shape[1] // 2
    lo = pltpu.bitcast(x[:, :n].astype(BF16).astype(F32), jnp.int32)
    hi = pltpu.bitcast(x[:, n:].astype(BF16).astype(F32), jnp.int32)
    return lax.shift_right_logical(lo, jnp.int32(16)) | (hi & jnp.int32(HALF_WORD))


def _unpack_bf16_pairs(p):
    lo = pltpu.bitcast(lax.shift_left(p, jnp.int32(16)), F32)
    hi = pltpu.bitcast(p & jnp.int32(HALF_WORD), F32)
    return jnp.concatenate([lo, hi], axis=1)


def _pick(idx, rows):
    return _tree(jnp.add, [jnp.where(idx == e, rows[e], 0.0) for e in range(N_EXPERTS)])


def _moe_route_kernel(x_ref, mod_ref, g_ref, rw_ref, rb_ref, tri_ref, hp_o, eidx_o, rank_o, gcol_o, cnt_o, seen):
    @pl.when(jnp.logical_and(pl.program_id(0) == 0, pl.program_id(1) == 0))
    def _():
        seen[...] = jnp.zeros_like(seen)

    tm = x_ref.shape[0]
    mod = mod_ref[0]
    h = _modulate(x_ref[...], g_ref[...], mod[3:4], mod[4:5])
    h16 = h.astype(BF16)
    _to_planes(_pack_bf16_pairs(h), hp_o)
    h_lo = (h - h16.astype(F32)).astype(BF16)
    part = _dot_nt(rw_ref[...], h16)
    logits_t = part[:N_EXPERTS] + part[N_EXPERTS:] + _dot_nt(rw_ref[:N_EXPERTS], h_lo)
    e1, e2, g1, g2 = _route(_sigmoid(logits_t), rb_ref[...])
    hot = jnp.concatenate([jnp.where(jnp.logical_or(e1 == e, e2 == e), 1.0, 0.0) for e in range(N_EXPERTS)], axis=0)
    before = _dot(hot.astype(BF16), tri_ref[...]) + seen[...]
    rows = [before[e:e + 1] for e in range(N_EXPERTS)]
    eidx_o[0], eidx_o[1] = e1, e2
    rank_o[0], rank_o[1] = _pick(e1, rows).astype(jnp.int32), _pick(e2, rows).astype(jnp.int32)
    gcol_o[...] = jnp.concatenate([g1, g2, jnp.zeros((LANES - 2, tm), F32)], axis=0).T
    seen[...] = seen[...] + jnp.sum(hot, axis=-1, keepdims=True)
    cnt_o[...] = seen[...]


def _moe_slots_kernel(mb, n_rows, eidx_ref, rank_ref, cnt_ref, idx_o, be_o):
    cnt = cnt_ref[...]
    padded = jnp.ceil(cnt / mb) * mb
    starts, ends, run = [], [], jnp.zeros((1, 1), F32)
    for e in range(N_EXPERTS):
        starts.append(run)
        run = run + padded[e:e + 1]
        ends.append(run)
    n_tiles, per_tile = eidx_ref.shape[1], eidx_ref.shape[3] // SC_WINDOW
    for k in range(2):
        for i in range(n_tiles):
            slot = rank_ref[k, i] + _pick(eidx_ref[k, i], starts).astype(jnp.int32)
            for j in range(N_PLANES):
                for q in range(per_tile):
                    row = ((k * N_PLANES + j) * n_tiles + i) * per_tile + q
                    idx_o[row:row + 1, :] = slot[:, q * SC_WINDOW:(q + 1) * SC_WINDOW] + j * n_rows
    first_row = lax.broadcasted_iota(jnp.int32, be_o.shape, 1).astype(F32) * mb
    be = _tree(jnp.add, [jnp.where(ends[e] <= first_row, 1.0, 0.0) for e in range(N_EXPERTS)])
    lane = lax.broadcasted_iota(jnp.int32, be_o.shape, 1)
    be = jnp.where(lane == be_o.shape[1] - 1, ends[N_EXPERTS - 1] / mb, jnp.minimum(be, N_EXPERTS - 1.0))
    be_o[...] = be.astype(jnp.int32)


def _moe_ffn_kernel(be_ref, hs_ref, wg_ref, wu_ref, wd_ref, ys_o, wgu16, wd16):
    i = pl.program_id(0)
    live = i < be_ref[be_ref.shape[0] - 1]

    @pl.when(jnp.logical_and(live, jnp.logical_or(i == 0, be_ref[i] != be_ref[jnp.maximum(i - 1, 0)])))
    def _():
        wgu16[:, :D_EXPERT] = wg_ref[...].astype(BF16)
        wgu16[:, D_EXPERT:] = wu_ref[...].astype(BF16)
        wd16[...] = wd_ref[...].astype(BF16)

    @pl.when(live)
    def _():
        h = _unpack_bf16_pairs(_from_planes(hs_ref)).astype(BF16)
        gu = _dot(h, wgu16[...])
        g, u = gu[:, :D_EXPERT], gu[:, D_EXPERT:]
        a = g * _sigmoid(g) * u
        _to_planes(_pack_bf16_pairs(_dot(a.astype(BF16), wd16[...])), ys_o)


def _moe_residual(x, z0_ref, z1_ref, gcol_ref, mod_ref):
    gc = gcol_ref[...]
    lane = lax.broadcasted_iota(jnp.int32, gc.shape, 1)
    g1 = jnp.sum(jnp.where(lane == 0, gc, 0.0), axis=-1, keepdims=True)
    g2 = jnp.sum(jnp.where(lane == 1, gc, 0.0), axis=-1, keepdims=True)
    y = g1 * _unpack_bf16_pairs(_from_planes(z0_ref)) + g2 * _unpack_bf16_pairs(_from_planes(z1_ref))
    return x + mod_ref[0][5:6] * y


def _moe_combine_kernel(x_ref, z0_ref, z1_ref, gcol_ref, mod_ref, fg_ref, o_ref):
    o_ref[...] = _rms(_moe_residual(x_ref[...], z0_ref, z1_ref, gcol_ref, mod_ref), fg_ref[...])


def _residual_specs(resid, row_fn, tm, d, in_specs, args):
    (z, gcol), mods_prev = resid
    for k in range(2):
        in_specs.append(pl.BlockSpec((None, N_PLANES, None, tm, SC_ROW), lambda bi, t, k=k: (k, 0, bi, t, 0)))
        args.append(z)
    in_specs += [pl.BlockSpec((None, tm, LANES), lambda bi, t: (bi, t, 0)),
                 pl.BlockSpec((1, 6, d), lambda bi, t: (row_fn(bi), 0, 0))]
    args += [gcol, mods_prev]


def _sc_mesh():
    return plsc.VectorSubcoreMesh(core_axis_name="core", subcore_axis_name="subcore")


def _sc_scatter_rows(rows, idx, n_out):
    n, w = rows.shape
    steps = n // SC_WINDOW

    @pl.kernel(out_type=jax.ShapeDtypeStruct((n_out, w), rows.dtype), mesh=_sc_mesh(), scratch_types=[])
    def scatter(x_hbm, i0_hbm, i1_hbm, o_hbm):
        def body(x_vmem, i0_vmem, i1_vmem):
            pltpu.sync_copy(x_vmem, o_hbm.at[i0_vmem.at[0]])
            pltpu.sync_copy(x_vmem, o_hbm.at[i1_vmem.at[0]])

        pltpu.emit_pipeline(
            body,
            grid=(steps,),
            in_specs=[pl.BlockSpec((SC_WINDOW, w), lambda i: (i, 0)),
                      pl.BlockSpec((1, SC_WINDOW), lambda i: (i, 0)),
                      pl.BlockSpec((1, SC_WINDOW), lambda i: (i + steps, 0))],
            out_specs=[],
            core_axis_name=("core", "subcore"),
            dimension_semantics=(pltpu.PARALLEL,),
        )(x_hbm, i0_hbm, i1_hbm)

    return scatter(rows, idx, idx)


def _sc_gather_rows(rows, idx):
    steps, w = idx.shape[0], rows.shape[1]

    @pl.kernel(out_type=jax.ShapeDtypeStruct((steps * SC_WINDOW, w), rows.dtype), mesh=_sc_mesh(),
               scratch_types=[])
    def gather(x_hbm, i_hbm, o_hbm):
        def body(i_vmem, o_vmem):
            pltpu.sync_copy(x_hbm.at[i_vmem.at[0]], o_vmem)

        pltpu.emit_pipeline(
            body,
            grid=(steps,),
            in_specs=[pl.BlockSpec((1, SC_WINDOW), lambda i: (i, 0))],
            out_specs=[pl.BlockSpec((SC_WINDOW, w), lambda i: (i, 0))],
            core_axis_name=("core", "subcore"),
            dimension_semantics=(pltpu.PARALLEL,),
        )(i_hbm, o_hbm)

    return gather(rows, idx)


def _moe_bucket(x, mods, row_fn, g, rw_t, rb, tm, mb):
    b, s, d = x.shape
    n, nt = b * s, (b * s) // tm
    per_b = s // tm
    tile = lambda bi, t: (0, bi * per_b + t, 0, 0)
    tri = jnp.asarray(np.triu(np.ones((tm, tm), np.float32), 1), BF16)
    hp, eidx, rank, gcol, cnt = pl.pallas_call(
        _moe_route_kernel,
        grid=(b, per_b),
        in_specs=[
            pl.BlockSpec((None, tm, d), lambda bi, t: (bi, t, 0)),
            pl.BlockSpec((1, 6, d), lambda bi, t: (row_fn(bi), 0, 0)),
            _full((1, d)), _full((2 * N_EXPERTS, d)), _full((N_EXPERTS, 1)), _full((tm, tm)),
        ],
        out_specs=[
            pl.BlockSpec((N_PLANES, tm, SC_ROW), lambda bi, t: (0, bi * per_b + t, 0)),
            pl.BlockSpec((2, None, 1, tm), tile), pl.BlockSpec((2, None, 1, tm), tile),
            pl.BlockSpec((None, tm, LANES), lambda bi, t: (bi, t, 0)),
            _full((N_EXPERTS, 1)),
        ],
        out_shape=[
            jax.ShapeDtypeStruct((N_PLANES, n, SC_ROW), jnp.int32),
            jax.ShapeDtypeStruct((2, nt, 1, tm), jnp.int32), jax.ShapeDtypeStruct((2, nt, 1, tm), jnp.int32),
            jax.ShapeDtypeStruct((b, s, LANES), F32),
            jax.ShapeDtypeStruct((N_EXPERTS, 1), F32),
        ],
        scratch_shapes=[pltpu.VMEM((N_EXPERTS, 1), F32)],
        compiler_params=_params(2),
        name="moe_route",
    )(x, mods, g, rw_t, rb, tri)

    n_rows = 2 * n + N_EXPERTS * mb
    nb = n_rows // mb
    nb_pad = -(-(nb + 1) // LANES) * LANES
    idx, be = pl.pallas_call(
        functools.partial(_moe_slots_kernel, mb, n_rows),
        out_shape=[jax.ShapeDtypeStruct((2 * N_PLANES * n // SC_WINDOW, SC_WINDOW), jnp.int32),
                   jax.ShapeDtypeStruct((1, nb_pad), jnp.int32)],
        name="moe_slots",
    )(eidx, rank, cnt)

    hs = _sc_scatter_rows(hp.reshape(N_PLANES * n, SC_ROW), idx, N_PLANES * n_rows)
    return hs.reshape(N_PLANES, n_rows, SC_ROW), idx, be.reshape(nb_pad), gcol


def _moe_experts(bucket, layer, w_gate, w_up, w_down, mb, out_dims):
    hs, idx, be, gcol = bucket
    n_rows, d = hs.shape[1], w_gate.shape[2]
    last = be.shape[0] - 1
    blk = lambda i, be_r: jnp.minimum(i, be_r[last] - 1)
    plane_block = pl.BlockSpec((N_PLANES, mb, SC_ROW), lambda i, be_r: (0, blk(i, be_r), 0))
    weight = lambda i, be_r: (layer, be_r[blk(i, be_r)], 0, 0)
    ys = pl.pallas_call(
        _moe_ffn_kernel,
        grid_spec=pltpu.PrefetchScalarGridSpec(
            num_scalar_prefetch=1,
            grid=(n_rows // mb,),
            in_specs=[
                plane_block,
                pl.BlockSpec((None, None, d, D_EXPERT), weight),
                pl.BlockSpec((None, None, d, D_EXPERT), weight),
                pl.BlockSpec((None, None, D_EXPERT, d), weight),
            ],
            out_specs=plane_block,
            scratch_shapes=[pltpu.VMEM((d, 2 * D_EXPERT), BF16), pltpu.VMEM((D_EXPERT, d), BF16)],
        ),
        out_shape=jax.ShapeDtypeStruct((N_PLANES, n_rows, SC_ROW), jnp.int32),
        compiler_params=_params(1),
        name="moe_ffn",
    )(be, hs, w_gate, w_up, w_down)
    z = _sc_gather_rows(ys.reshape(N_PLANES * n_rows, SC_ROW), idx)
    return z.reshape(2, N_PLANES, *out_dims, SC_ROW), gcol.reshape(*out_dims, LANES)


def _moe_combine(x, mods, row_fn, gathered, final_g, tm):
    b, s, d = x.shape
    in_specs = [pl.BlockSpec((None, tm, d), lambda bi, t: (bi, t, 0))]
    args = [x]
    _residual_specs((gathered, mods), row_fn, tm, d, in_specs, args)
    in_specs.append(_full((1, d)))
    args.append(final_g)
    return pl.pallas_call(
        _moe_combine_kernel,
        grid=(b, s // tm),
        in_specs=in_specs,
        out_specs=pl.BlockSpec((None, tm, d), lambda bi, t: (bi, t, 0)),
        out_shape=jax.ShapeDtypeStruct((b, s, d), F32),
        compiler_params=_params(2),
        name="moe_combine",
    )(*args)


def _rope_tables(n_tokens, rot_dim, offset):
    n_rows = n_tokens // GRID_W
    rows = np.repeat(np.arange(n_rows), GRID_W)
    cols = np.tile(np.arange(GRID_W), n_rows)
    n_freq = rot_dim // 4
    inv = jnp.asarray(ROPE_THETA, F32) ** (-jnp.arange(n_freq, dtype=F32) / n_freq)
    ang = jnp.concatenate([jnp.asarray(rows, F32)[:, None] * inv, jnp.asarray(cols, F32)[:, None] * inv], axis=-1)
    cos, sin = jnp.repeat(jnp.cos(ang), 2, axis=-1), jnp.repeat(jnp.sin(ang), 2, axis=-1)
    odd = jnp.asarray(np.arange(rot_dim) % 2 == 1)
    parts = [cos, jnp.where(odd, sin, 0.0), jnp.where(odd, 0.0, -sin)]
    period = 64 if rot_dim == 64 else LANES
    fill = [1.0, 0.0, 0.0]
    out = []
    for p, f in zip(parts, fill):
        slot = jnp.full((n_tokens, period), f, F32).at[:, offset:offset + rot_dim].set(p)
        out.append(jnp.tile(slot, (1, LANES // period)))
    return jnp.stack(out)


def _pad_heads(w, n_heads, lo, hi):
    k = w.shape[0]
    w = w.reshape(k, n_heads, -1)[:, :, lo:hi]
    return jnp.pad(w, ((0, 0), (0, 0), (0, LANES - (hi - lo)))).reshape(k, n_heads * LANES)


def _ab_weights(j, ab_w_in, diff_lambda, diff_subln_g, mla_q_norm_g, mla_w_qb, mla_kv_norm_g, mla_w_kvb, ab_w_out):
    place = np.zeros((LANES, MLA_HEADS * LANES), np.float32)
    for hd in range(MLA_HEADS):
        for r in range(MLA_ROPE_DIM):
            place[r, hd * LANES + MLA_NOPE_DIM + r] = 1.0
    w_out_m = ab_w_out[j][512:].reshape(MLA_HEADS, MLA_V_DIM, D_MODEL)
    return {
        "w_in": jnp.pad(ab_w_in[j], ((0, 0), (0, AB_IN_PAD - AB_IN))).astype(BF16),
        "qn_g": mla_q_norm_g[j][None, :],
        "w_qb": _pad_heads(mla_w_qb[j], MLA_HEADS, 0, MLA_NOPE_DIM + MLA_ROPE_DIM).astype(BF16),
        "kvn_g": mla_kv_norm_g[j][None, :],
        "wk": _pad_heads(mla_w_kvb[j], MLA_HEADS, 0, MLA_NOPE_DIM).astype(BF16),
        "wv": _pad_heads(mla_w_kvb[j], MLA_HEADS, MLA_NOPE_DIM, MLA_NOPE_DIM + MLA_V_DIM).astype(BF16),
        "e": jnp.asarray(place, BF16),
        "w_out_d": ab_w_out[j][:512].reshape(DIFF_HEADS, LANES, D_MODEL).astype(BF16),
        "w_out_m": jnp.pad(w_out_m, ((0, 0), (0, LANES - MLA_V_DIM), (0, 0))).astype(BF16),
        "subln_g": diff_subln_g[j][None, :],
        "lam": diff_lambda[j],
    }


def kernel(x_prompt, x_sample, cache_diff_k, cache_diff_v, cache_mla_ckv, cache_mla_krope, cache_swa_k, cache_swa_v,
           c, c_ctx, ada_w, ada_b, norm1_g, norm2_g, final_norm_g, ab_w_in, diff_lambda, diff_subln_g, mla_q_norm_g,
           mla_w_qb, mla_kv_norm_g, mla_w_kvb, ab_w_out, swa_w_in, swa_sink, swa_w_out, router_w, router_bias,
           moe_w_gate, moe_w_up, moe_w_down):
    bp, sp, d = x_prompt.shape
    bs, ss, _ = x_sample.shape
    depth = ada_w.shape[0]
    past = cache_diff_k.shape[2]
    n_ab = cache_diff_k.shape[1]
    n_swa = cache_swa_k.shape[1]
    assert bs + 1 <= COND_ROWS and d == D_MODEL

    cond = jnp.zeros((COND_ROWS, d), F32).at[:bs].set(c).at[bs].set(c_ctx)
    mods = _ada_all(cond, ada_w, ada_b).reshape(depth, COND_ROWS, 6, d)
    row_s = lambda bi: bi
    row_p = lambda bi: bi * 0 + bs

    rope_d = _rope_tables(ss, DIFF_QK_DIM, 0)
    rope_q = _rope_tables(ss, MLA_ROPE_DIM, MLA_NOPE_DIM)
    rope_k = _rope_tables(ss, MLA_ROPE_DIM, 0)

    cdk = cache_diff_k.reshape(bs, n_ab, past, 512)
    cdv = cache_diff_v.reshape(bs, n_ab, past, 512)
    ckr = jnp.pad(cache_mla_krope, ((0, 0), (0, 0), (0, 0), (0, LANES - MLA_ROPE_DIM)))
    csk = cache_swa_k.reshape(bs, n_swa, past, 256)
    csv = cache_swa_v.reshape(bs, n_swa, past, 256)
    rw_hi = router_w.T.astype(BF16)
    rw_t = jnp.concatenate([rw_hi, (router_w.T - rw_hi.astype(F32)).astype(BF16)], axis=0)
    rb = router_bias[:, None]

    tp = min(256, sp)
    ts = min(256, ss)
    tm_s = min(512, ss)
    tm_p = min(512, bp * sp)

    xp, xs = x_prompt, x_sample
    new_ab, new_swa = None, None
    pend_p = pend_s = None

    def project(fn, x, pend, *a):
        outs = list(fn(x, *a, pend))
        return (outs.pop() if pend is not None else x), outs

    for l in range(depth):
        j = l // 2
        ml = mods[l]
        g1, g2 = norm1_g[l][None, :], norm2_g[l][None, :]
        if l % 2 == 0:
            lambda_init = 0.8 - 0.6 * math.exp(-0.3 * l)
            wts = _ab_weights(j, ab_w_in, diff_lambda, diff_subln_g, mla_q_norm_g, mla_w_qb, mla_kv_norm_g,
                              mla_w_kvb, ab_w_out)
            xp, (qa, ka, va, qb, km, vm, *new_ab) = project(_proj_ab, xp, pend_p, ml, row_p, g1, wts, None,
                                                             (j, n_ab, new_ab), tp)
            xp = _attn_ab(xp, ml, row_p, (qa, qb), [(ka, va, km, vm)], wts, lambda_init, tp, PROMPT_BUFS)
            xs, (qa, ka, va, qb, km, vm) = project(_proj_ab, xs, pend_s, ml, row_s, g1, wts,
                                                   (rope_d, rope_q, rope_k), None, tm_s)
            cache_seg = _cache_ab(cdk, cdv, cache_mla_ckv, ckr, j, wts)
            xs = _attn_ab(xs, ml, row_s, (qa, qb), [(ka, va, km, vm), tuple(cache_seg)], wts, lambda_init, ts, 3)
        else:
            w_in = swa_w_in[j].astype(BF16)
            w_out = swa_w_out[j].reshape(8, LANES, d).astype(BF16)
            xp, (q, k, v, *new_swa) = project(_proj_swa, xp, pend_p, ml, row_p, g1, w_in, None,
                                              (j, n_swa, new_swa), tp)
            xp = _attn_swa(xp, ml, row_p, q, k, v, None, swa_sink[j], w_out, tp, PROMPT_BUFS)
            xs, (q, k, v) = project(_proj_swa, xs, pend_s, ml, row_s, g1, w_in, rope_d, None, tm_s)
            xs = _attn_swa(xs, ml, row_s, q, k, v, (csk, csv, j), swa_sink[j], w_out, ts, 2)
        bucket_p = _moe_bucket(xp.reshape(1, bp * sp, d), ml, row_p, g2, rw_t, rb, tm_p, MOE_ROW_BLOCK)
        bucket_s = _moe_bucket(xs, ml, row_s, g2, rw_t, rb, tm_s, MOE_ROW_BLOCK)
        got_p = _moe_experts(bucket_p, l, moe_w_gate, moe_w_up, moe_w_down, MOE_ROW_BLOCK, (bp, sp))
        got_s = _moe_experts(bucket_s, l, moe_w_gate, moe_w_up, moe_w_down, MOE_ROW_BLOCK, (bs, ss))
        pend_p, pend_s = (got_p, ml), (got_s, ml)

    xp = _moe_combine(xp, pend_p[1], row_p, pend_p[0], final_norm_g[None, :], tp)
    xs = _moe_combine(xs, pend_s[1], row_s, pend_s[0], final_norm_g[None, :], tm_s)

    new_diff_k = new_ab[0].reshape(bp, n_ab, sp, DIFF_HEADS, 2, DIFF_QK_DIM)
    new_diff_v = new_ab[1].reshape(bp, n_ab, sp, DIFF_HEADS, DIFF_V_DIM)
    new_mla_ckv, new_mla_krope = new_ab[2], new_ab[3]
    new_swa_k = new_swa[0].reshape(bp, n_swa, sp, SWA_KV_HEADS, SWA_HEAD_DIM)
    new_swa_v = new_swa[1].reshape(bp, n_swa, sp, SWA_KV_HEADS, SWA_HEAD_DIM)
    return (xp, xs, new_diff_k, new_diff_v, new_mla_ckv, new_mla_krope, new_swa_k, new_swa_v)
```
